```python
import jax, jax.numpy as jnp
from jax import lax
import numpy as np

D_MODEL = 2048
BATCH = 8
SEQ = 2048
DEPTH = 1

MEM_LEN = 256
NORM_EPS = 1e-6

MIX_WIDTH = D_MODEL
POOL_WIDTH = MIX_WIDTH // 2
POOL_WINDOWS = (2, 4, 8, 16)
POOL_GROUP = POOL_WIDTH // len(POOL_WINDOWS)
RWKV_WIDTH = MIX_WIDTH - POOL_WIDTH
RWKV_HEAD = 64
RWKV_HEADS = RWKV_WIDTH // RWKV_HEAD
GN_EPS = 64e-5
DECAY_LORA = max(32, int(round(1.8 * RWKV_WIDTH ** 0.5 / 32)) * 32)
AAA_LORA = max(32, int(round(1.8 * RWKV_WIDTH ** 0.5 / 32)) * 32)
GATE_LORA = max(32, int(round(0.6 * RWKV_WIDTH ** 0.8 / 32)) * 32)
SHIFT_COLS = 3 * RWKV_WIDTH + DECAY_LORA + AAA_LORA + GATE_LORA
IN_COLS = POOL_WIDTH + SHIFT_COLS

XATTN_HEADS = 4
XATTN_HEAD_DIM = D_MODEL // XATTN_HEADS

MOE_GROUPS = 4
MOE_EXPERTS_PER_GROUP = 4
MOE_EXPERTS = MOE_GROUPS * MOE_EXPERTS_PER_GROUP
MOE_TOPK = 2
MOE_FF = D_MODEL // 4

kernel_name = "hybrid_pool_rwkv7_memxattn_hmoe"


def _rmsnorm(x, g):
    xf = x.astype(jnp.float32)
    y = xf * lax.rsqrt(jnp.mean(xf * xf, axis=-1, keepdims=True) + NORM_EPS)
    return (y * g.astype(jnp.float32)).astype(x.dtype)


def _token_shift(u):
    return jnp.pad(u, ((0, 0), (1, 0), (0, 0)))[:, :-1]


def _pool_mixer(u, w_pool, pool_scale):
    B, S, _ = u.shape
    uf = u.astype(jnp.float32)
    c = jnp.cumsum(uf, axis=1)
    maxw = max(POOL_WINDOWS)
    c_pad = jnp.pad(c, ((0, 0), (maxw, 0), (0, 0)))
    pos = jnp.arange(S)
    outs = []
    for gi, w in enumerate(POOL_WINDOWS):
        lo, hi = gi * POOL_GROUP, (gi + 1) * POOL_GROUP
        c_prev = c_pad[:, maxw - w: maxw - w + S, lo:hi]
        cnt = jnp.minimum(pos + 1, w).astype(jnp.float32)[None, :, None]
        outs.append((c[:, :, lo:hi] - c_prev) / cnt - uf[:, :, lo:hi])
    pooled = jnp.stack(outs, axis=2).astype(u.dtype)
    mixed = jnp.einsum('bsgc,gcd->bsgd', pooled, w_pool)
    return (mixed.reshape(B, S, POOL_WIDTH) * pool_scale).astype(u.dtype)


def _rwkv7_scan(r, decay, k, v, a_vec, b_vec):
    B, S, H, N = r.shape

    def step(state, inp):
        r_t, w_t, k_t, v_t, a_t, b_t = inp
        sa = jnp.einsum('bhij,bhj->bhi', state, a_t)
        state = (state * w_t[:, :, None, :] + sa[..., None] * b_t[:, :, None, :]
                 + v_t[..., None] * k_t[:, :, None, :])
        y_t = jnp.einsum('bhij,bhj->bhi', state, r_t)
        return state, y_t

    xs = tuple(jnp.moveaxis(t, 1, 0) for t in (r, decay, k, v, a_vec, b_vec))
    state0 = jnp.zeros((B, H, N, N), jnp.float32)
    _, ys = lax.scan(step, state0, xs)
    return jnp.moveaxis(ys, 0, 1)


def _rwkv7_mixer(p, mu, w0, w2, a0, a2, g2, k_k, k_a, r_k, ln_w, ln_b):
    B, S, _ = p.shape
    H, N, C = RWKV_HEADS, RWKV_HEAD, RWKV_WIDTH
    f32 = jnp.float32
    pf = p.astype(f32)
    z = pf + (_token_shift(pf) - pf) * mu.astype(f32)
    o1, o2 = 3 * C + DECAY_LORA, 3 * C + DECAY_LORA + AAA_LORA
    r, k, v = z[..., :C], z[..., C:2 * C], z[..., 2 * C:3 * C]
    w_lo, a_lo, g_lo = z[..., 3 * C:o1], z[..., o1:o2], z[..., o2:]
    w_log = -jax.nn.softplus(-(w0.astype(f32) + jnp.tanh(w_lo) @ w2.astype(f32))) - 0.5
    decay = jnp.exp(-jnp.exp(w_log))
    a = jax.nn.sigmoid(a0.astype(f32) + a_lo @ a2.astype(f32))
    g = jax.nn.sigmoid(g_lo) @ g2.astype(f32)
    heads = lambda t: t.reshape(B, S, H, N)
    kk = heads(k * k_k.astype(f32))
    kk = kk * lax.rsqrt(jnp.maximum(jnp.sum(kk * kk, axis=-1, keepdims=True), 1e-24))
    k = k * (1.0 + (a - 1.0) * k_a.astype(f32))
    r4, k4, v4, a4 = heads(r), heads(k), heads(v), heads(a)
    y = _rwkv7_scan(r4, heads(decay), k4, v4, -kk, kk * a4)
    mean = jnp.mean(y, axis=-1, keepdims=True)
    var = jnp.mean(jnp.square(y - mean), axis=-1, keepdims=True)
    y = ((y - mean) * lax.rsqrt(var + GN_EPS)).reshape(B, S, C) * ln_w.astype(f32) + ln_b.astype(f32)
    bonus = jnp.sum(r4 * k4 * r_k.astype(f32), axis=-1, keepdims=True) * v4
    y = y + bonus.reshape(B, S, C)
    return (y * g).astype(p.dtype)


def _memory_xattn(hn, mem_n, w_q, w_kv, w_o):
    B, S, D = hn.shape
    M = mem_n.shape[1]
    q = (hn @ w_q).reshape(B, S, XATTN_HEADS, XATTN_HEAD_DIM)
    kv = (mem_n @ w_kv).reshape(B, M, 2, XATTN_HEADS, XATTN_HEAD_DIM)
    k, v = kv[:, :, 0], kv[:, :, 1]
    s = jnp.einsum('bshd,bmhd->bhsm', q, k).astype(jnp.float32) * (XATTN_HEAD_DIM ** -0.5)
    prob = jax.nn.softmax(s, axis=-1).astype(hn.dtype)
    o = jnp.einsum('bhsm,bmhd->bshd', prob, v).reshape(B, S, D)
    return o @ w_o


def _hier_moe(hn, w_group, b_group, w_expert, b_expert, w_gate, w_up, w_down):
    B, S, D = hn.shape
    t = hn.reshape(B * S, D)
    f32 = jnp.float32
    g_logits = (t @ w_group).astype(f32) + b_group.astype(f32)
    g_prob = jax.nn.softmax(g_logits, axis=-1)
    g_idx = jnp.argmax(g_logits, axis=-1)
    g_w = jnp.take_along_axis(g_prob, g_idx[:, None], axis=-1)
    e_logits = ((t @ w_expert).astype(f32) + b_expert.astype(f32)).reshape(-1, MOE_GROUPS, MOE_EXPERTS_PER_GROUP)
    e_logits = jnp.take_along_axis(e_logits, g_idx[:, None, None], axis=1)[:, 0]
    top_v, top_i = lax.top_k(e_logits, MOE_TOPK)
    top_p = jax.nn.softmax(top_v, axis=-1) * g_w
    expert_id = g_idx[:, None] * MOE_EXPERTS_PER_GROUP + top_i
    gates = jnp.sum(jax.nn.one_hot(expert_id, MOE_EXPERTS, dtype=f32) * top_p[..., None], axis=1)
    gates = gates.astype(t.dtype)
    out = jnp.zeros_like(t)
    for e in range(MOE_EXPERTS):
        hid = jax.nn.silu(t @ w_gate[e]) * (t @ w_up[e])
        out = out + gates[:, e:e + 1] * (hid @ w_down[e])
    return out.reshape(B, S, D)


def setup_inputs(seed: int = 0) -> dict:
    key = jax.random.key(seed)
    ks = iter(jax.random.split(key, 40))
    f32 = jnp.float32
    L, D, C = DEPTH, D_MODEL, RWKV_WIDTH

    def nrm(shape, scale):
        return jax.random.normal(next(ks), shape, f32) * scale

    def gain(shape, base=1.0):
        return base + 0.02 * jax.random.normal(next(ks), shape, f32)

    def unif(shape, lo, hi):
        return jax.random.uniform(next(ks), shape, f32, lo, hi)

    return {
        'x': nrm((BATCH, SEQ, D), 1.0),
        'mem': nrm((BATCH, MEM_LEN, D), 1.0),
        'norm_mix_g': gain((L, D)),
        'w_in': nrm((L, D, IN_COLS), D ** -0.5),
        'pool_w': nrm((L, len(POOL_WINDOWS), POOL_GROUP, POOL_GROUP), POOL_GROUP ** -0.5),
        'pool_scale': gain((L, POOL_WIDTH)),
        'rwkv_mu': unif((L, SHIFT_COLS), 0.0, 1.0),
        'rwkv_w0': unif((L, C), -6.5, -1.5),
        'rwkv_w2': nrm((L, DECAY_LORA, C), 0.5 * DECAY_LORA ** -0.5),
        'rwkv_a0': nrm((L, C), 0.1),
        'rwkv_a2': nrm((L, AAA_LORA, C), AAA_LORA ** -0.5),
        'rwkv_g2': nrm((L, GATE_LORA, C), GATE_LORA ** -0.5),
        'rwkv_k_k': gain((L, C), 0.85),
        'rwkv_k_a': gain((L, C)),
        'rwkv_r_k': nrm((L, RWKV_HEADS, RWKV_HEAD), 0.1),
        'rwkv_ln_w': gain((L, C)),
        'rwkv_ln_b': nrm((L, C), 0.02),
        'w_out': nrm((L, MIX_WIDTH, D), MIX_WIDTH ** -0.5),
        'norm_xattn_g': gain((L, D)),
        'norm_mem_g': gain((L, D)),
        'xattn_w_q': nrm((L, D, D), D ** -0.5),
        'xattn_w_kv': nrm((L, D, 2 * D), D ** -0.5),
        'xattn_w_o': nrm((L, D, D), D ** -0.5),
        'norm_ffn_g': gain((L, D)),
        'moe_w_group': nrm((L, D, MOE_GROUPS), D ** -0.5),
        'moe_b_group': nrm((L, MOE_GROUPS), 0.01),
        'moe_w_expert': nrm((L, D, MOE_EXPERTS), D ** -0.5),
        'moe_b_expert': nrm((L, MOE_EXPERTS), 0.01),
        'moe_w_gate': nrm((L, MOE_EXPERTS, D, MOE_FF), D ** -0.5),
        'moe_w_up': nrm((L, MOE_EXPERTS, D, MOE_FF), D ** -0.5),
        'moe_w_down': nrm((L, MOE_EXPERTS, MOE_FF, D), MOE_FF ** -0.5),
        'norm_final_g': gain((D,)),
    }


def reference(x, mem, norm_mix_g, w_in, pool_w, pool_scale, rwkv_mu, rwkv_w0, rwkv_w2,
              rwkv_a0, rwkv_a2, rwkv_g2, rwkv_k_k, rwkv_k_a, rwkv_r_k, rwkv_ln_w, rwkv_ln_b,
              w_out, norm_xattn_g, norm_mem_g, xattn_w_q, xattn_w_kv, xattn_w_o,
              norm_ffn_g, moe_w_group, moe_b_group, moe_w_expert, moe_b_expert,
              moe_w_gate, moe_w_up, moe_w_down, norm_final_g):
    h = x
    for l in range(DEPTH):
        hn = _rmsnorm(h, norm_mix_g[l])
        proj = hn @ w_in[l]
        pool_out = _pool_mixer(proj[..., :POOL_WIDTH], pool_w[l], pool_scale[l])
        rwkv_out = _rwkv7_mixer(proj[..., POOL_WIDTH:], rwkv_mu[l], rwkv_w0[l], rwkv_w2[l],
                                rwkv_a0[l], rwkv_a2[l], rwkv_g2[l], rwkv_k_k[l], rwkv_k_a[l],
                                rwkv_r_k[l], rwkv_ln_w[l], rwkv_ln_b[l])
        mixed = jnp.concatenate([pool_out, rwkv_out], axis=-1)
        h = h + mixed @ w_out[l]
        hn = _rmsnorm(h, norm_xattn_g[l])
        mem_n = _rmsnorm(mem, norm_mem_g[l])
        h = h + _memory_xattn(hn, mem_n, xattn_w_q[l], xattn_w_kv[l], xattn_w_o[l])
        hn = _rmsnorm(h, norm_ffn_g[l])
        h = h + _hier_moe(hn, moe_w_group[l], moe_b_group[l], moe_w_expert[l], moe_b_expert[l],
                          moe_w_gate[l], moe_w_up[l], moe_w_down[l])
    return _rmsnorm(h, norm_final_g)
```

```python
import functools

import jax
import jax.numpy as jnp
from jax import lax
from jax.experimental import pallas as pl
from jax.experimental.pallas import tpu as pltpu

f32 = jnp.float32
bf16 = jnp.bfloat16

D_MODEL = 2048
MEM_LEN = 256
NORM_EPS = 1e-6

POOL_WIDTH = 1024
POOL_WINDOWS = (2, 4, 8, 16)
POOL_GROUP = 256
POOL_HALO = 16
RWKV_WIDTH = 1024
RWKV_HEAD = 64
RWKV_HEADS = 16
GN_EPS = 64e-5
DECAY_LORA = 64
AAA_LORA = 64
GATE_LORA = 160
LORA_PAD = 512
LORA_W_OFF, LORA_A_OFF, LORA_G_OFF = 0, 128, 256
MAIN_COLS = POOL_WIDTH + 3 * RWKV_WIDTH
PROJ_COLS = MAIN_COLS + LORA_PAD

RWKV_CHUNK = 64
HEADS_PER_STEP = 4
HG_LANES = HEADS_PER_STEP * RWKV_HEAD

XATTN_HEADS = 4
XATTN_HEAD_DIM = 512

MOE_GROUPS = 4
MOE_EXPERTS_PER_GROUP = 4
MOE_EXPERTS = 16
MOE_FF = 512
ROUTER_LANES = 128

VMEM_LIMIT = 56 * 1024 * 1024


def _cparams(sem):
    return pltpu.CompilerParams(dimension_semantics=sem, vmem_limit_bytes=VMEM_LIMIT)


def _split_dot(x, w_bf16):
    hi = x.astype(bf16)
    lo = (x - hi.astype(f32)).astype(bf16)
    return (jnp.dot(hi, w_bf16, preferred_element_type=f32)
            + jnp.dot(lo, w_bf16, preferred_element_type=f32))


def _norm_mm_kernel(x_ref, g_ref, w_ref, o_ref, xn_ref):
    @pl.when(pl.program_id(1) == 0)
    def _():
        x = x_ref[...]
        ms = jnp.mean(x * x, axis=-1, keepdims=True)
        xn_ref[...] = (x * lax.rsqrt(ms + NORM_EPS) * g_ref[...]).astype(bf16)

    o_ref[...] = jnp.dot(xn_ref[...], w_ref[...], preferred_element_type=f32).astype(o_ref.dtype)


def norm_matmul(x, g, w, *, tm, tn, out_dtype=bf16, name="norm_matmul"):
    M, K = x.shape
    N = w.shape[1]
    return pl.pallas_call(
        _norm_mm_kernel,
        out_shape=jax.ShapeDtypeStruct((M, N), out_dtype),
        grid=(M // tm, N // tn),
        in_specs=[
            pl.BlockSpec((tm, K), lambda i, j: (i, 0)),
            pl.BlockSpec((1, K), lambda i, j: (0, 0)),
            pl.BlockSpec((K, tn), lambda i, j: (0, j)),
        ],
        out_specs=pl.BlockSpec((tm, tn), lambda i, j: (i, j)),
        scratch_shapes=[pltpu.VMEM((tm, K), bf16)],
        compiler_params=_cparams(("parallel", "arbitrary")),
        name=name,
    )(x, g.reshape(1, K), w)


def _mm_res_kernel(*refs, n):
    a_refs, w_refs, res_ref, o_ref = refs[:n], refs[n:2 * n], refs[2 * n], refs[2 * n + 1]
    acc = res_ref[...]
    for a_ref, w_ref in zip(a_refs, w_refs):
        acc = acc + jnp.dot(a_ref[...], w_ref[...], preferred_element_type=f32)
    o_ref[...] = acc


def matmul_residual(a_list, w_list, res, *, tm, tn, name="matmul_residual"):
    M, N = res.shape
    n = len(a_list)
    return pl.pallas_call(
        functools.partial(_mm_res_kernel, n=n),
        out_shape=jax.ShapeDtypeStruct((M, N), f32),
        grid=(M // tm, N // tn),
        in_specs=(
            [pl.BlockSpec((tm, a.shape[1]), lambda i, j: (i, 0)) for a in a_list]
            + [pl.BlockSpec((w.shape[0], tn), lambda i, j: (0, j)) for w in w_list]
            + [pl.BlockSpec((tm, tn), lambda i, j: (i, j))]),
        out_specs=pl.BlockSpec((tm, tn), lambda i, j: (i, j)),
        compiler_params=_cparams(("parallel", "arbitrary")),
        name=name,
    )(*a_list, *w_list, res)


def _pool_kernel(u_ref, halo_ref, pw_ref, ps_ref, o_ref, *, ts):
    s = pl.program_id(1)
    u = u_ref[...].astype(f32)
    halo = jnp.where(s > 0, halo_ref[...].astype(f32), 0.0)
    ext = jnp.concatenate([halo, u], axis=0)
    pos = s * ts + lax.broadcasted_iota(jnp.int32, (ts, 1), 0)
    acc = ext
    shift = 1
    for gi, w in enumerate(POOL_WINDOWS):
        while shift < w:
            acc = acc + pltpu.roll(acc, shift, 0)
            shift *= 2
        lo, hi = gi * POOL_GROUP, (gi + 1) * POOL_GROUP
        cnt = jnp.minimum(pos + 1, w).astype(f32)
        pooled = acc[POOL_HALO:, lo:hi] / cnt - u[:, lo:hi]
        mixed = jnp.dot(pooled.astype(bf16), pw_ref[gi], preferred_element_type=f32)
        o_ref[:, lo:hi] = (mixed * ps_ref[:, lo:hi]).astype(o_ref.dtype)


def pool_mixer(proj, pool_w, pool_scale, *, batch, seq, ts):
    ns = seq // ts
    hb = ts // POOL_HALO
    return pl.pallas_call(
        functools.partial(_pool_kernel, ts=ts),
        out_shape=jax.ShapeDtypeStruct((batch * seq, POOL_WIDTH), bf16),
        grid=(batch, ns),
        in_specs=[
            pl.BlockSpec((ts, POOL_WIDTH), lambda b, s: (b * ns + s, 0)),
            pl.BlockSpec((POOL_HALO, POOL_WIDTH),
                         lambda b, s: (jnp.maximum((b * ns + s) * hb - 1, 0), 0)),
            pl.BlockSpec((len(POOL_WINDOWS), POOL_GROUP, POOL_GROUP), lambda b, s: (0, 0, 0)),
            pl.BlockSpec((1, POOL_WIDTH), lambda b, s: (0, 0)),
        ],
        out_specs=pl.BlockSpec((ts, POOL_WIDTH), lambda b, s: (b * ns + s, 0)),
        compiler_params=_cparams(("parallel", "arbitrary")),
        name="pool_mixer",
    )(proj, proj, pool_w, pool_scale.reshape(1, POOL_WIDTH))


def _head_ones():
    r = lax.broadcasted_iota(jnp.int32, (HG_LANES, HG_LANES), 0) // RWKV_HEAD
    c = lax.broadcasted_iota(jnp.int32, (HG_LANES, HG_LANES), 1) // RWKV_HEAD
    return jnp.where(r == c, 1.0, 0.0).astype(bf16)


def _head_sum(x, ones_bd):
    parts = [_split_dot(x[:, c:c + HG_LANES], ones_bd) for c in range(0, x.shape[1], HG_LANES)]
    return parts[0] if len(parts) == 1 else jnp.concatenate(parts, axis=1)


def _sigmoid(x):
    return 1.0 / (1.0 + jnp.exp(-x))


def _rwkv_prep_kernel(pr_ref, pk_ref, pv_ref, pl_ref, hr_ref, hk_ref, hv_ref, hl_ref,
                      mu_ref, mul_ref, w0_ref, w2_ref, a0_ref, a2_ref, g2_ref, kk_ref, ka_ref,
                      r_out, lw_out, k_out, v_out, kk_out, kka_out, g_out, *, ts):
    s = pl.program_id(1)
    row0 = lax.broadcasted_iota(jnp.int32, (ts, 1), 0) == 0
    C = RWKV_WIDTH

    def shifted(p_ref, h_ref, mu):
        p = p_ref[...].astype(f32)
        last = h_ref[...].astype(f32)[POOL_HALO - 1:POOL_HALO, :]
        last = jnp.where(s > 0, last, 0.0)
        prev = jnp.where(row0, last, pltpu.roll(p, 1, 0))
        return p + (prev - p) * mu

    r = shifted(pr_ref, hr_ref, mu_ref[:, 0:C])
    k = shifted(pk_ref, hk_ref, mu_ref[:, C:2 * C])
    v = shifted(pv_ref, hv_ref, mu_ref[:, 2 * C:3 * C])
    lo = shifted(pl_ref, hl_ref, mul_ref[...])
    w_lo = lo[:, LORA_W_OFF:LORA_W_OFF + 128]
    a_lo = lo[:, LORA_A_OFF:LORA_A_OFF + 128]
    g_lo = lo[:, LORA_G_OFF:LORA_G_OFF + 256]

    wx = w0_ref[...] + jnp.dot(jnp.tanh(w_lo).astype(bf16), w2_ref[...], preferred_element_type=f32)
    nx = -wx
    softplus = jnp.maximum(nx, 0.0) + jnp.log(1.0 + jnp.exp(-jnp.abs(nx)))
    w_log = -softplus - 0.5
    lw_out[...] = -jnp.exp(w_log)
    a = _sigmoid(a0_ref[...] + jnp.dot(a_lo.astype(bf16), a2_ref[...], preferred_element_type=f32))
    g = jnp.dot(_sigmoid(g_lo).astype(bf16), g2_ref[...], preferred_element_type=f32)

    ones_bd = _head_ones()
    kk = k * kk_ref[...]
    kk = kk * lax.rsqrt(jnp.maximum(_head_sum(kk * kk, ones_bd), 1e-24))
    k = k * (1.0 + (a - 1.0) * ka_ref[...])

    r_out[...] = r.astype(r_out.dtype)
    k_out[...] = k.astype(k_out.dtype)
    v_out[...] = v.astype(v_out.dtype)
    kk_out[...] = kk.astype(kk_out.dtype)
    kka_out[...] = (kk * a).astype(kka_out.dtype)
    g_out[...] = g.astype(g_out.dtype)


def rwkv_prep(proj, mu_main, mu_lora, w0, w2p, a0, a2p, g2p, k_k, k_a, *, batch, seq, ts):
    ns = seq // ts
    hb = ts // POOL_HALO
    C = RWKV_WIDTH
    T = batch * seq
    lora_blk = MAIN_COLS // LORA_PAD

    def main_spec(col):
        return pl.BlockSpec((ts, C), lambda b, s: (b * ns + s, col))

    def halo_spec(width, col):
        return pl.BlockSpec((POOL_HALO, width),
                            lambda b, s: (jnp.maximum((b * ns + s) * hb - 1, 0), col))

    def full(shape):
        return pl.BlockSpec(shape, lambda b, s: (0,) * len(shape))

    row = lambda t: t.reshape(1, -1)
    out_sd = lambda dt: jax.ShapeDtypeStruct((T, C), dt)
    out_spec = pl.BlockSpec((ts, C), lambda b, s: (b * ns + s, 0))
    return pl.pallas_call(
        functools.partial(_rwkv_prep_kernel, ts=ts),
        out_shape=[out_sd(bf16), out_sd(f32), out_sd(bf16), out_sd(bf16),
                   out_sd(bf16), out_sd(bf16), out_sd(bf16)],
        grid=(batch, ns),
        in_specs=[
            main_spec(1), main_spec(2), main_spec(3),
            pl.BlockSpec((ts, LORA_PAD), lambda b, s: (b * ns + s, lora_blk)),
            halo_spec(C, 1), halo_spec(C, 2), halo_spec(C, 3), halo_spec(LORA_PAD, lora_blk),
            full((1, 3 * C)), full((1, LORA_PAD)), full((1, C)), full((128, C)),
            full((1, C)), full((128, C)), full((256, C)), full((1, C)), full((1, C)),
        ],
        out_specs=[out_spec] * 7,
        compiler_params=_cparams(("parallel", "arbitrary")),
        name="rwkv_prep",
    )(proj, proj, proj, proj, proj, proj, proj, proj,
      row(mu_main), row(mu_lora), row(w0), w2p, row(a0), a2p, g2p, row(k_k), row(k_a))


def _bd_mask():
    r = lax.broadcasted_iota(jnp.int32, (HG_LANES, HG_LANES), 0) // RWKV_CHUNK
    c = lax.broadcasted_iota(jnp.int32, (HG_LANES, HG_LANES), 1) // RWKV_HEAD
    return r == c


def _rwkv_scan_kernel(r_ref, lw_ref, k_ref, v_ref, kk_ref, kka_ref, g_ref,
                      lnw_ref, lnb_ref, rk_ref, o_ref, state_ref, *, n_chunks):
    L = RWKV_CHUNK
    W = HG_LANES

    @pl.when(pl.program_id(2) == 0)
    def _():
        state_ref[...] = jnp.zeros_like(state_ref)

    bd_mask = _bd_mask()
    ones_bd = _head_ones()

    def bd(x):
        tiled = jnp.concatenate([x] * HEADS_PER_STEP, axis=0)
        return jnp.where(bd_mask, tiled, 0.0).astype(bf16)

    def mm(a, b_bf16):
        return jnp.dot(a.astype(bf16), b_bf16, preferred_element_type=f32)

    def mm_nt(a, b_bf16):
        return lax.dot_general(a.astype(bf16), b_bf16, (((1,), (1,)), ((), ())),
                               preferred_element_type=f32)

    def mm_tn(a_bf16, b_bf16):
        return lax.dot_general(a_bf16, b_bf16, (((0,), (0,)), ((), ())),
                               preferred_element_type=f32)

    def split(x):
        hi = x.astype(bf16)
        return hi, (x - hi.astype(f32)).astype(bf16)

    t_idx = lax.broadcasted_iota(jnp.int32, (L, W), 0)
    s_idx = lax.broadcasted_iota(jnp.int32, (L, W), 1) % L
    strict = t_idx > s_idx
    incl = t_idx >= s_idx
    eye_all = jnp.where(t_idx == s_idx, 1.0, 0.0)
    tri = jnp.where(lax.broadcasted_iota(jnp.int32, (L, L), 0)
                    >= lax.broadcasted_iota(jnp.int32, (L, L), 1), 1.0, 0.0).astype(bf16)
    ones_lw = jnp.ones((L, W), bf16)
    blockdiag = (lax.broadcasted_iota(jnp.int32, (W, W), 0) // RWKV_HEAD
                 == lax.broadcasted_iota(jnp.int32, (W, W), 1) // RWKV_HEAD)

    state = state_ref[...]
    for c in range(n_chunks):
        rows = pl.ds(c * L, L)
        r = r_ref[rows, :].astype(f32)
        k = k_ref[rows, :].astype(f32)
        v = v_ref[rows, :].astype(f32)
        kk = kk_ref[rows, :].astype(f32)
        kka = kka_ref[rows, :].astype(f32)
        lw = lw_ref[rows, :]

        lw_hi, lw_lo = split(lw)
        cs = (jnp.dot(tri, lw_hi, preferred_element_type=f32)
              + jnp.dot(tri, lw_lo, preferred_element_type=f32))
        total_rows = mm_tn(lw_hi, ones_lw) + mm_tn(lw_lo, ones_lw)
        cs_last = cs[L - 1:L, :]
        g_inv = jnp.exp(-cs)
        to_end = jnp.exp(cs_last - cs)
        a_t = -kk * jnp.exp(cs - lw)
        r_t = r * jnp.exp(cs)
        b_t = kka * g_inv
        k_t = k * g_inv
        b_e = kka * to_end
        k_e = k * to_end
        v_bd = bd(v)

        ar = jnp.concatenate([a_t, r_t], axis=0)
        m_b = mm_nt(ar, bd(b_t))
        m_k = mm_nt(ar, bd(k_t))
        m_ab = jnp.where(strict, m_b[:L], 0.0)
        m_rb = jnp.where(incl, m_b[L:], 0.0)
        m_ak = jnp.where(strict, m_k[:L], 0.0)
        m_rk = jnp.where(incl, m_k[L:], 0.0)

        p = m_ab
        tinv = eye_all + p
        n = 2
        while n < L:
            p = mm(p, bd(p))
            tinv = tinv + mm(tinv, bd(p))
            n *= 2

        u_v = mm(tinv, bd(mm(m_ak, v_bd)))
        a_h = mm(tinv, bd(a_t))
        r_h = r_t + mm(m_rb, bd(a_h))
        y_v = mm(m_rb, bd(u_v)) + mm(m_rk, v_bd)
        p_st = jnp.where(blockdiag, mm_tn(b_e.astype(bf16), a_h.astype(bf16)), 0.0)
        g_st = jnp.where(blockdiag,
                         mm_tn(jnp.concatenate([b_e, k_e], axis=0).astype(bf16),
                               jnp.concatenate([u_v, v], axis=0).astype(bf16)), 0.0)

        state_b = state.astype(bf16)
        y = mm(r_h, state_b) + y_v
        state = jnp.exp(total_rows) * state + mm(p_st, state_b) + g_st

        mean = _split_dot(y, ones_bd) * (1.0 / RWKV_HEAD)
        yc = y - mean
        var = _split_dot(yc * yc, ones_bd) * (1.0 / RWKV_HEAD)
        yn = yc * lax.rsqrt(var + GN_EPS) * lnw_ref[...] + lnb_ref[...]
        bonus = _split_dot(r * k * rk_ref[...], ones_bd) * v
        o_ref[rows, :] = ((yn + bonus) * g_ref[rows, :].astype(f32)).astype(o_ref.dtype)

    state_ref[...] = state


def rwkv_scan(r, lw, k, v, kk, kka, g, ln_w, ln_b, r_k, *, batch, seq, rows):
    nb = seq // rows
    n_hg = RWKV_HEADS // HEADS_PER_STEP
    T = batch * seq
    blk = pl.BlockSpec((rows, HG_LANES), lambda b, h, c: (b * nb + c, h))
    par = pl.BlockSpec((1, HG_LANES), lambda b, h, c: (0, h))
    return pl.pallas_call(
        functools.partial(_rwkv_scan_kernel, n_chunks=rows // RWKV_CHUNK),
        out_shape=jax.ShapeDtypeStruct((T, RWKV_WIDTH), bf16),
        grid=(batch, n_hg, nb),
        in_specs=[blk] * 7 + [par] * 3,
        out_specs=blk,
        scratch_shapes=[pltpu.VMEM((HG_LANES, HG_LANES), f32)],
        compiler_params=_cparams(("parallel", "parallel", "arbitrary")),
        name="rwkv_scan",
    )(r, lw, k, v, kk, kka, g, ln_w.reshape(1, -1), ln_b.reshape(1, -1), r_k.reshape(1, -1))


def _xattn_kernel(q_ref, k_ref, v_ref, o_ref):
    s = lax.dot_general(q_ref[...], k_ref[...], (((1,), (1,)), ((), ())),
                        preferred_element_type=f32) * (XATTN_HEAD_DIM ** -0.5)
    m = jnp.max(s, axis=-1, keepdims=True)
    e = jnp.exp(s - m)
    l = jnp.sum(e, axis=-1, keepdims=True)
    o = jnp.dot(e.astype(bf16), v_ref[...], preferred_element_type=f32)
    o_ref[...] = (o / l).astype(o_ref.dtype)


def xattn_core(q, kv, *, batch, seq, ts):
    ns = seq // ts
    hd = XATTN_HEAD_DIM
    return pl.pallas_call(
        _xattn_kernel,
        out_shape=jax.ShapeDtypeStruct(q.shape, bf16),
        grid=(batch, ns, XATTN_HEADS),
        in_specs=[
            pl.BlockSpec((ts, hd), lambda b, s, h: (b * ns + s, h)),
            pl.BlockSpec((MEM_LEN, hd), lambda b, s, h: (b, h)),
            pl.BlockSpec((MEM_LEN, hd), lambda b, s, h: (b, XATTN_HEADS + h)),
        ],
        out_specs=pl.BlockSpec((ts, hd), lambda b, s, h: (b * ns + s, h)),
        compiler_params=_cparams(("parallel", "parallel", "arbitrary")),
        name="xattn_core",
    )(q, kv, kv)


def _router_kernel(h_ref, g_ref, wr_ref, br_ref, hn_ref, gates_ref):
    x = h_ref[...]
    ms = jnp.mean(x * x, axis=-1, keepdims=True)
    hn = x * lax.rsqrt(ms + NORM_EPS) * g_ref[...]
    hn_ref[...] = hn.astype(hn_ref.dtype)

    w = wr_ref[...]
    w_hi = w.astype(bf16)
    w_lo = (w - w_hi.astype(f32)).astype(bf16)
    h_hi = hn.astype(bf16)
    logits = (_split_dot(hn, w_hi) + jnp.dot(h_hi, w_lo, preferred_element_type=f32)) + br_ref[...]

    lane_i = lax.broadcasted_iota(jnp.int32, logits.shape, 1)
    lane = lane_i.astype(f32)
    neg = -jnp.inf
    big = float(ROUTER_LANES)
    is_g = lane_i < MOE_GROUPS
    gl = jnp.where(is_g, logits, neg)
    gmax = jnp.max(gl, axis=-1, keepdims=True)
    g_idx = jnp.min(jnp.where(gl == gmax, lane, big), axis=-1, keepdims=True)
    g_w = 1.0 / jnp.sum(jnp.exp(gl - gmax), axis=-1, keepdims=True)

    lo_lane = MOE_GROUPS + g_idx * MOE_EXPERTS_PER_GROUP
    sel = (lane >= lo_lane) & (lane < lo_lane + MOE_EXPERTS_PER_GROUP)
    el = jnp.where(sel, logits, neg)
    v1 = jnp.max(el, axis=-1, keepdims=True)
    i1 = jnp.min(jnp.where(el == v1, lane, big), axis=-1, keepdims=True)
    el2 = jnp.where(lane == i1, neg, el)
    v2 = jnp.max(el2, axis=-1, keepdims=True)
    i2 = jnp.min(jnp.where(el2 == v2, lane, big), axis=-1, keepdims=True)
    e2 = jnp.exp(v2 - v1)
    p1 = 1.0 / (1.0 + e2)
    p2 = e2 / (1.0 + e2)
    gates_ref[...] = (jnp.where(lane == i1, p1 * g_w, 0.0)
                      + jnp.where(lane == i2, p2 * g_w, 0.0))


def moe_router(h, g, w_router, b_router, *, tm):
    T, D = h.shape
    return pl.pallas_call(
        _router_kernel,
        out_shape=[jax.ShapeDtypeStruct((T, D), bf16),
                   jax.ShapeDtypeStruct((T, ROUTER_LANES), f32)],
        grid=(T // tm,),
        in_specs=[
            pl.BlockSpec((tm, D), lambda i: (i, 0)),
            pl.BlockSpec((1, D), lambda i: (0, 0)),
            pl.BlockSpec((D, ROUTER_LANES), lambda i: (0, 0)),
            pl.BlockSpec((1, ROUTER_LANES), lambda i: (0, 0)),
        ],
        out_specs=[pl.BlockSpec((tm, D), lambda i: (i, 0)),
                   pl.BlockSpec((tm, ROUTER_LANES), lambda i: (i, 0))],
        compiler_params=_cparams(("parallel",)),
        name="moe_router",
    )(h, g.reshape(1, D), w_router, b_router)


def _moe_dense_kernel(hn_ref, gates_ref, wg_ref, wu_ref, wd_ref, h_ref, gf_ref, o_ref, acc_ref):
    e = pl.program_id(1)

    @pl.when(e == 0)
    def _():
        acc_ref[...] = jnp.zeros_like(acc_ref)

    t = hn_ref[...]
    gate = jnp.dot(t, wg_ref[0], preferred_element_type=f32)
    up = jnp.dot(t, wu_ref[0], preferred_element_type=f32)
    hid = gate * _sigmoid(gate) * up
    lane = lax.broadcasted_iota(jnp.int32, gates_ref.shape, 1)
    gcol = jnp.sum(jnp.where(lane == e + MOE_GROUPS, gates_ref[...], 0.0), axis=-1, keepdims=True)
    acc_ref[...] += gcol * jnp.dot(hid.astype(bf16), wd_ref[0], preferred_element_type=f32)

    @pl.when(e == MOE_EXPERTS - 1)
    def _():
        y = h_ref[...] + acc_ref[...]
        ms = jnp.mean(y * y, axis=-1, keepdims=True)
        o_ref[...] = y * lax.rsqrt(ms + NORM_EPS) * gf_ref[...]


def moe_dense(hn, gates, wg, wu, wd, h, g_final, *, tm):
    T, D = h.shape
    return pl.pallas_call(
        _moe_dense_kernel,
        out_shape=jax.ShapeDtypeStruct((T, D), f32),
        grid=(T // tm, MOE_EXPERTS),
        in_specs=[
            pl.BlockSpec((tm, D), lambda i, e: (i, 0)),
            pl.BlockSpec((tm, ROUTER_LANES), lambda i, e: (i, 0)),
            pl.BlockSpec((1, D, MOE_FF), lambda i, e: (e, 0, 0)),
            pl.BlockSpec((1, D, MOE_FF), lambda i, e: (e, 0, 0)),
            pl.BlockSpec((1, MOE_FF, D), lambda i, e: (e, 0, 0)),
            pl.BlockSpec((tm, D), lambda i, e: (i, 0)),
            pl.BlockSpec((1, D), lambda i, e: (0, 0)),
        ],
        out_specs=pl.BlockSpec((tm, D), lambda i, e: (i, 0)),
        scratch_shapes=[pltpu.VMEM((tm, D), f32)],
        compiler_params=_cparams(("parallel", "arbitrary")),
        name="moe_dense",
    )(hn, gates, wg, wu, wd, h, g_final.reshape(1, D))


def _pad_rows(w, rows):
    return jnp.pad(w, ((0, rows - w.shape[0]), (0, 0)))


def _pack_lora_cols(t):
    o1, o2 = DECAY_LORA, DECAY_LORA + AAA_LORA
    pad = lambda a, n: jnp.pad(a, [(0, 0)] * (a.ndim - 1) + [(0, n - a.shape[-1])])
    return jnp.concatenate([pad(t[..., :o1], 128), pad(t[..., o1:o2], 128),
                            pad(t[..., o2:], 256)], axis=-1)


def _layer(x2, mem2, p, *, batch, seq, cfg):
    D = D_MODEL
    w_in = p['w_in']
    w_in_p = jnp.concatenate([w_in[:, :MAIN_COLS], _pack_lora_cols(w_in[:, MAIN_COLS:])],
                             axis=1).astype(bf16)
    mu = p['rwkv_mu']
    mu_main = mu[:3 * RWKV_WIDTH]
    mu_lora = _pack_lora_cols(mu[3 * RWKV_WIDTH:])

    proj = norm_matmul(x2, p['norm_mix_g'], w_in_p, tm=cfg['tm'], tn=cfg['tn_in'], name="in_proj")
    pool_out = pool_mixer(proj, p['pool_w'].astype(bf16), p['pool_scale'],
                          batch=batch, seq=seq, ts=cfg['ts_pool'])
    r, lw, k, v, kk, kka, g = rwkv_prep(
        proj, mu_main, mu_lora, p['rwkv_w0'], _pad_rows(p['rwkv_w2'], 128).astype(bf16),
        p['rwkv_a0'], _pad_rows(p['rwkv_a2'], 128).astype(bf16),
        _pad_rows(p['rwkv_g2'], 256).astype(bf16), p['rwkv_k_k'], p['rwkv_k_a'],
        batch=batch, seq=seq, ts=cfg['ts_prep'])
    rwkv_out = rwkv_scan(r, lw, k, v, kk, kka, g, p['rwkv_ln_w'], p['rwkv_ln_b'], p['rwkv_r_k'],
                         batch=batch, seq=seq, rows=cfg['scan_rows'])
    w_out = p['w_out'].astype(bf16)
    h1 = matmul_residual([pool_out, rwkv_out], [w_out[:POOL_WIDTH], w_out[POOL_WIDTH:]], x2,
                         tm=cfg['tm'], tn=cfg['tn'], name="out_proj")

    q = norm_matmul(h1, p['norm_xattn_g'], p['xattn_w_q'].astype(bf16),
                    tm=cfg['tm'], tn=cfg['tn'], name="q_proj")
    kv = norm_matmul(mem2, p['norm_mem_g'], p['xattn_w_kv'].astype(bf16),
                     tm=min(cfg['tm'], mem2.shape[0]), tn=cfg['tn'], name="kv_proj")
    o = xattn_core(q, kv, batch=batch, seq=seq, ts=cfg['ts_attn'])
    h2 = matmul_residual([o], [p['xattn_w_o'].astype(bf16)], h1,
                         tm=cfg['tm'], tn=cfg['tn'], name="o_proj")

    w_router = jnp.pad(jnp.concatenate([p['moe_w_group'], p['moe_w_expert']], axis=1),
                       ((0, 0), (0, ROUTER_LANES - MOE_GROUPS - MOE_EXPERTS)))
    b_router = jnp.pad(jnp.concatenate([p['moe_b_group'], p['moe_b_expert']]),
                       (0, ROUTER_LANES - MOE_GROUPS - MOE_EXPERTS)).reshape(1, ROUTER_LANES)
    hn3, gates = moe_router(h2, p['norm_ffn_g'], w_router, b_router, tm=cfg['tm'])
    return moe_dense(hn3, gates, p['moe_w_gate'].astype(bf16), p['moe_w_up'].astype(bf16),
                     p['moe_w_down'].astype(bf16), h2, p['norm_final_g'], tm=cfg['tm_moe'])


_CFG = dict(tm=512, tn=1024, tn_in=1536, ts_pool=512, ts_prep=256, scan_rows=256,
            ts_attn=1024, tm_moe=512)


def kernel(x, mem, norm_mix_g, w_in, pool_w, pool_scale, rwkv_mu, rwkv_w0, rwkv_w2, rwkv_a0,
           rwkv_a2, rwkv_g2, rwkv_k_k, rwkv_k_a, rwkv_r_k, rwkv_ln_w, rwkv_ln_b, w_out,
           norm_xattn_g, norm_mem_g, xattn_w_q, xattn_w_kv, xattn_w_o, norm_ffn_g,
           moe_w_group, moe_b_group, moe_w_expert, moe_b_expert, moe_w_gate, moe_w_up,
           moe_w_down, norm_final_g):
    batch, seq, D = x.shape
    p = dict(norm_mix_g=norm_mix_g[0], w_in=w_in[0], pool_w=pool_w[0], pool_scale=pool_scale[0],
             rwkv_mu=rwkv_mu[0], rwkv_w0=rwkv_w0[0], rwkv_w2=rwkv_w2[0], rwkv_a0=rwkv_a0[0],
             rwkv_a2=rwkv_a2[0], rwkv_g2=rwkv_g2[0], rwkv_k_k=rwkv_k_k[0], rwkv_k_a=rwkv_k_a[0],
             rwkv_r_k=rwkv_r_k[0], rwkv_ln_w=rwkv_ln_w[0], rwkv_ln_b=rwkv_ln_b[0], w_out=w_out[0],
             norm_xattn_g=norm_xattn_g[0], norm_mem_g=norm_mem_g[0], xattn_w_q=xattn_w_q[0],
             xattn_w_kv=xattn_w_kv[0], xattn_w_o=xattn_w_o[0], norm_ffn_g=norm_ffn_g[0],
             moe_w_group=moe_w_group[0], moe_b_group=moe_b_group[0], moe_w_expert=moe_w_expert[0],
             moe_b_expert=moe_b_expert[0], moe_w_gate=moe_w_gate[0], moe_w_up=moe_w_up[0],
             moe_w_down=moe_w_down[0], norm_final_g=norm_final_g)
    out = _layer(x.reshape(batch * seq, D), mem.reshape(batch * MEM_LEN, D), p,
                 batch=batch, seq=seq, cfg=_CFG)
    return out.reshape(batch, seq, D)
```

```python
import functools

import jax
import jax.numpy as jnp
from jax import lax
from jax.experimental import pallas as pl
from jax.experimental.pallas import tpu as pltpu

f32 = jnp.float32
bf16 = jnp.bfloat16

D_MODEL = 2048
MEM_LEN = 256
NORM_EPS = 1e-6

POOL_WIDTH = 1024
POOL_WINDOWS = (2, 4, 8, 16)
POOL_GROUP = 256
POOL_HALO = 16
RWKV_WIDTH = 1024
RWKV_HEAD = 64
RWKV_HEADS = 16
GN_EPS = 64e-5
DECAY_LORA = 64
AAA_LORA = 64
GATE_LORA = 160
LORA_PAD = 512
LORA_W_OFF, LORA_A_OFF, LORA_G_OFF = 0, 128, 256
MAIN_COLS = POOL_WIDTH + 3 * RWKV_WIDTH
PROJ_COLS = MAIN_COLS + LORA_PAD

RWKV_CHUNK = 64
HEADS_PER_STEP = 4
HG_LANES = HEADS_PER_STEP * RWKV_HEAD

XATTN_HEADS = 4
XATTN_HEAD_DIM = 512

MOE_GROUPS = 4
MOE_EXPERTS_PER_GROUP = 4
MOE_EXPERTS = 16
MOE_FF = 512
ROUTER_LANES = 128

VMEM_LIMIT = 56 * 1024 * 1024


def _cparams(sem):
    return pltpu.CompilerParams(dimension_semantics=sem, vmem_limit_bytes=VMEM_LIMIT)


def _split_dot(x, w_bf16):
    hi = x.astype(bf16)
    lo = (x - hi.astype(f32)).astype(bf16)
    return (jnp.dot(hi, w_bf16, preferred_element_type=f32)
            + jnp.dot(lo, w_bf16, preferred_element_type=f32))


def _norm_mm_kernel(x_ref, g_ref, w_ref, o_ref, xn_ref):
    @pl.when(pl.program_id(1) == 0)
    def _():
        x = x_ref[...]
        ms = jnp.mean(x * x, axis=-1, keepdims=True)
        xn_ref[...] = (x * lax.rsqrt(ms + NORM_EPS) * g_ref[...]).astype(bf16)

    o_ref[...] = jnp.dot(xn_ref[...], w_ref[...], preferred_element_type=f32).astype(o_ref.dtype)


def norm_matmul(x, g, w, *, tm, tn, out_dtype=bf16, name="norm_matmul"):
    M, K = x.shape
    N = w.shape[1]
    return pl.pallas_call(
        _norm_mm_kernel,
        out_shape=jax.ShapeDtypeStruct((M, N), out_dtype),
        grid=(M // tm, N // tn),
        in_specs=[
            pl.BlockSpec((tm, K), lambda i, j: (i, 0)),
            pl.BlockSpec((1, K), lambda i, j: (0, 0)),
            pl.BlockSpec((K, tn), lambda i, j: (0, j)),
        ],
        out_specs=pl.BlockSpec((tm, tn), lambda i, j: (i, j)),
        scratch_shapes=[pltpu.VMEM((tm, K), bf16)],
        compiler_params=_cparams(("parallel", "arbitrary")),
        name=name,
    )(x, g.reshape(1, K), w)


def _mm_res_kernel(*refs, n):
    a_refs, w_refs, res_ref, o_ref = refs[:n], refs[n:2 * n], refs[2 * n], refs[2 * n + 1]
    acc = res_ref[...]
    for a_ref, w_ref in zip(a_refs, w_refs):
        acc = acc + jnp.dot(a_ref[...], w_ref[...], preferred_element_type=f32)
    o_ref[...] = acc


def matmul_residual(a_list, w_list, res, *, tm, tn, name="matmul_residual"):
    M, N = res.shape
    n = len(a_list)
    return pl.pallas_call(
        functools.partial(_mm_res_kernel, n=n),
        out_shape=jax.ShapeDtypeStruct((M, N), f32),
        grid=(M // tm, N // tn),
        in_specs=(
            [pl.BlockSpec((tm, a.shape[1]), lambda i, j: (i, 0)) for a in a_list]
            + [pl.BlockSpec((w.shape[0], tn), lambda i, j: (0, j)) for w in w_list]
            + [pl.BlockSpec((tm, tn), lambda i, j: (i, j))]),
        out_specs=pl.BlockSpec((tm, tn), lambda i, j: (i, j)),
        compiler_params=_cparams(("parallel", "arbitrary")),
        name=name,
    )(*a_list, *w_list, res)


def _pool_kernel(u_ref, halo_ref, pw_ref, ps_ref, o_ref, *, ts):
    s = pl.program_id(1)
    u = u_ref[...].astype(f32)
    halo = jnp.where(s > 0, halo_ref[...].astype(f32), 0.0)
    ext = jnp.concatenate([halo, u], axis=0)
    pos = s * ts + lax.broadcasted_iota(jnp.int32, (ts, 1), 0)
    acc = ext
    shift = 1
    for gi, w in enumerate(POOL_WINDOWS):
        while shift < w:
            acc = acc + pltpu.roll(acc, shift, 0)
            shift *= 2
        lo, hi = gi * POOL_GROUP, (gi + 1) * POOL_GROUP
        cnt = jnp.minimum(pos + 1, w).astype(f32)
        pooled = acc[POOL_HALO:, lo:hi] / cnt - u[:, lo:hi]
        mixed = jnp.dot(pooled.astype(bf16), pw_ref[gi], preferred_element_type=f32)
        o_ref[:, lo:hi] = (mixed * ps_ref[:, lo:hi]).astype(o_ref.dtype)


def pool_mixer(proj, pool_w, pool_scale, *, batch, seq, ts):
    ns = seq // ts
    hb = ts // POOL_HALO
    return pl.pallas_call(
        functools.partial(_pool_kernel, ts=ts),
        out_shape=jax.ShapeDtypeStruct((batch * seq, POOL_WIDTH), bf16),
        grid=(batch, ns),
        in_specs=[
            pl.BlockSpec((ts, POOL_WIDTH), lambda b, s: (b * ns + s, 0)),
            pl.BlockSpec((POOL_HALO, POOL_WIDTH),
                         lambda b, s: (jnp.maximum((b * ns + s) * hb - 1, 0), 0)),
            pl.BlockSpec((len(POOL_WINDOWS), POOL_GROUP, POOL_GROUP), lambda b, s: (0, 0, 0)),
            pl.BlockSpec((1, POOL_WIDTH), lambda b, s: (0, 0)),
        ],
        out_specs=pl.BlockSpec((ts, POOL_WIDTH), lambda b, s: (b * ns + s, 0)),
        compiler_params=_cparams(("parallel", "arbitrary")),
        name="pool_mixer",
    )(proj, proj, pool_w, pool_scale.reshape(1, POOL_WIDTH))


def _head_ones():
    r = lax.broadcasted_iota(jnp.int32, (HG_LANES, HG_LANES), 0) // RWKV_HEAD
    c = lax.broadcasted_iota(jnp.int32, (HG_LANES, HG_LANES), 1) // RWKV_HEAD
    return jnp.where(r == c, 1.0, 0.0).astype(bf16)


def _head_sum(x, ones_bd):
    parts = [_split_dot(x[:, c:c + HG_LANES], ones_bd) for c in range(0, x.shape[1], HG_LANES)]
    return parts[0] if len(parts) == 1 else jnp.concatenate(parts, axis=1)


def _sigmoid(x):
    return 1.0 / (1.0 + jnp.exp(-x))


def _rwkv_prep_kernel(pr_ref, pk_ref, pv_ref, pl_ref, hr_ref, hk_ref, hv_ref, hl_ref,
                      mu_ref, mul_ref, w0_ref, w2_ref, a0_ref, a2_ref, g2_ref, kk_ref, ka_ref,
                      r_out, lw_out, k_out, v_out, kk_out, kka_out, g_out, *, ts):
    s = pl.program_id(1)
    row0 = lax.broadcasted_iota(jnp.int32, (ts, 1), 0) == 0
    C = RWKV_WIDTH

    def shifted(p_ref, h_ref, mu):
        p = p_ref[...].astype(f32)
        last = h_ref[...].astype(f32)[POOL_HALO - 1:POOL_HALO, :]
        last = jnp.where(s > 0, last, 0.0)
        prev = jnp.where(row0, last, pltpu.roll(p, 1, 0))
        return p + (prev - p) * mu

    r = shifted(pr_ref, hr_ref, mu_ref[:, 0:C])
    k = shifted(pk_ref, hk_ref, mu_ref[:, C:2 * C])
    v = shifted(pv_ref, hv_ref, mu_ref[:, 2 * C:3 * C])
    lo = shifted(pl_ref, hl_ref, mul_ref[...])
    w_lo = lo[:, LORA_W_OFF:LORA_W_OFF + 128]
    a_lo = lo[:, LORA_A_OFF:LORA_A_OFF + 128]
    g_lo = lo[:, LORA_G_OFF:LORA_G_OFF + 256]

    wx = w0_ref[...] + jnp.dot(jnp.tanh(w_lo).astype(bf16), w2_ref[...], preferred_element_type=f32)
    nx = -wx
    softplus = jnp.maximum(nx, 0.0) + jnp.log(1.0 + jnp.exp(-jnp.abs(nx)))
    w_log = -softplus - 0.5
    lw_out[...] = -jnp.exp(w_log)
    a = _sigmoid(a0_ref[...] + jnp.dot(a_lo.astype(bf16), a2_ref[...], preferred_element_type=f32))
    g = jnp.dot(_sigmoid(g_lo).astype(bf16), g2_ref[...], preferred_element_type=f32)

    ones_bd = _head_ones()
    kk = k * kk_ref[...]
    kk = kk * lax.rsqrt(jnp.maximum(_head_sum(kk * kk, ones_bd), 1e-24))
    k = k * (1.0 + (a - 1.0) * ka_ref[...])

    r_out[...] = r.astype(r_out.dtype)
    k_out[...] = k.astype(k_out.dtype)
    v_out[...] = v.astype(v_out.dtype)
    kk_out[...] = kk.astype(kk_out.dtype)
    kka_out[...] = (kk * a).astype(kka_out.dtype)
    g_out[...] = g.astype(g_out.dtype)


def rwkv_prep(proj, mu_main, mu_lora, w0, w2p, a0, a2p, g2p, k_k, k_a, *, batch, seq, ts):
    ns = seq // ts
    hb = ts // POOL_HALO
    C = RWKV_WIDTH
    T = batch * seq
    lora_blk = MAIN_COLS // LORA_PAD

    def main_spec(col):
        return pl.BlockSpec((ts, C), lambda b, s: (b * ns + s, col))

    def halo_spec(width, col):
        return pl.BlockSpec((POOL_HALO, width),
                            lambda b, s: (jnp.maximum((b * ns + s) * hb - 1, 0), col))

    def full(shape):
        return pl.BlockSpec(shape, lambda b, s: (0,) * len(shape))

    row = lambda t: t.reshape(1, -1)
    out_sd = lambda dt: jax.ShapeDtypeStruct((T, C), dt)
    out_spec = pl.BlockSpec((ts, C), lambda b, s: (b * ns + s, 0))
    return pl.pallas_call(
        functools.partial(_rwkv_prep_kernel, ts=ts),
        out_shape=[out_sd(bf16), out_sd(f32), out_sd(bf16), out_sd(bf16),
                   out_sd(bf16), out_sd(bf16), out_sd(bf16)],
        grid=(batch, ns),
        in_specs=[
            main_spec(1), main_spec(2), main_spec(3),
            pl.BlockSpec((ts, LORA_PAD), lambda b, s: (b * ns + s, lora_blk)),
            halo_spec(C, 1), halo_spec(C, 2), halo_spec(C, 3), halo_spec(LORA_PAD, lora_blk),
            full((1, 3 * C)), full((1, LORA_PAD)), full((1, C)), full((128, C)),
            full((1, C)), full((128, C)), full((256, C)), full((1, C)), full((1, C)),
        ],
        out_specs=[out_spec] * 7,
        compiler_params=_cparams(("parallel", "arbitrary")),
        name="rwkv_prep",
    )(proj, proj, proj, proj, proj, proj, proj, proj,
      row(mu_main), row(mu_lora), row(w0), w2p, row(a0), a2p, g2p, row(k_k), row(k_a))


def _bd_mask():
    r = lax.broadcasted_iota(jnp.int32, (HG_LANES, HG_LANES), 0) // RWKV_CHUNK
    c = lax.broadcasted_iota(jnp.int32, (HG_LANES, HG_LANES), 1) // RWKV_HEAD
    return r == c


def _rwkv_scan_kernel(r_ref, lw_ref, k_ref, v_ref, kk_ref, kka_ref, g_ref,
                      lnw_ref, lnb_ref, rk_ref, o_ref, state_ref, *, n_chunks):
    L = RWKV_CHUNK
    W = HG_LANES

    @pl.when(pl.program_id(2) == 0)
    def _():
        state_ref[...] = jnp.zeros_like(state_ref)

    bd_mask = _bd_mask()
    ones_bd = _head_ones()

    def bd(x):
        tiled = jnp.concatenate([x] * HEADS_PER_STEP, axis=0)
        return jnp.where(bd_mask, tiled, 0.0).astype(bf16)

    def mm(a, b_bf16):
        return jnp.dot(a.astype(bf16), b_bf16, preferred_element_type=f32)

    def mm_nt(a, b_bf16):
        return lax.dot_general(a.astype(bf16), b_bf16, (((1,), (1,)), ((), ())),
                               preferred_element_type=f32)

    def mm_tn(a_bf16, b_bf16):
        return lax.dot_general(a_bf16, b_bf16, (((0,), (0,)), ((), ())),
                               preferred_element_type=f32)

    def split(x):
        hi = x.astype(bf16)
        return hi, (x - hi.astype(f32)).astype(bf16)

    t_idx = lax.broadcasted_iota(jnp.int32, (L, W), 0)
    s_idx = lax.broadcasted_iota(jnp.int32, (L, W), 1) % L
    strict = t_idx > s_idx
    incl = t_idx >= s_idx
    eye_all = jnp.where(t_idx == s_idx, 1.0, 0.0)
    tri = jnp.where(lax.broadcasted_iota(jnp.int32, (L, L), 0)
                    >= lax.broadcasted_iota(jnp.int32, (L, L), 1), 1.0, 0.0).astype(bf16)
    ones_lw = jnp.ones((L, W), bf16)
    blockdiag = (lax.broadcasted_iota(jnp.int32, (W, W), 0) // RWKV_HEAD
                 == lax.broadcasted_iota(jnp.int32, (W, W), 1) // RWKV_HEAD)

    state = state_ref[...]
    C = range(n_chunks)
    rows = [pl.ds(c * L, L) for c in C]
    r = [r_ref[rw, :].astype(f32) for rw in rows]
    k = [k_ref[rw, :].astype(f32) for rw in rows]
    v = [v_ref[rw, :].astype(f32) for rw in rows]
    lw = [lw_ref[rw, :] for rw in rows]
    lw_s = [split(x) for x in lw]
    cs = [jnp.dot(tri, hi, preferred_element_type=f32) + jnp.dot(tri, lo, preferred_element_type=f32)
          for hi, lo in lw_s]
    total_rows = [mm_tn(hi, ones_lw) + mm_tn(lo, ones_lw) for hi, lo in lw_s]
    g_inv = [jnp.exp(-x) for x in cs]
    to_end = [jnp.exp(x[L - 1:L, :] - x) for x in cs]
    a_t = [-kk_ref[rows[c], :].astype(f32) * jnp.exp(cs[c] - lw[c]) for c in C]
    r_t = [r[c] * jnp.exp(cs[c]) for c in C]
    kka = [kka_ref[rw, :].astype(f32) for rw in rows]
    b_t = [kka[c] * g_inv[c] for c in C]
    k_t = [k[c] * g_inv[c] for c in C]
    b_e = [kka[c] * to_end[c] for c in C]
    k_e = [k[c] * to_end[c] for c in C]
    v_bd = [bd(x) for x in v]

    ar = [jnp.concatenate([a_t[c], r_t[c]], axis=0) for c in C]
    m_b = [mm_nt(ar[c], bd(b_t[c])) for c in C]
    m_k = [mm_nt(ar[c], bd(k_t[c])) for c in C]
    m_ab = [jnp.where(strict, x[:L], 0.0) for x in m_b]
    m_rb = [jnp.where(incl, x[L:], 0.0) for x in m_b]
    m_ak = [jnp.where(strict, x[:L], 0.0) for x in m_k]
    m_rk = [jnp.where(incl, x[L:], 0.0) for x in m_k]

    p = m_ab
    tinv = [eye_all + x for x in p]
    n = 2
    while n < L:
        p = [mm(x, bd(x)) for x in p]
        tinv = [tinv[c] + mm(tinv[c], bd(p[c])) for c in C]
        n *= 2

    x1 = [mm(m_ak[c], v_bd[c]) for c in C]
    u_v = [mm(tinv[c], bd(x1[c])) for c in C]
    a_h = [mm(tinv[c], bd(a_t[c])) for c in C]
    r_h = [r_t[c] + mm(m_rb[c], bd(a_h[c])) for c in C]
    y_v = [mm(m_rb[c], bd(u_v[c])) + mm(m_rk[c], v_bd[c]) for c in C]
    p_st = [jnp.where(blockdiag, mm_tn(b_e[c].astype(bf16), a_h[c].astype(bf16)), 0.0) for c in C]
    g_st = [jnp.where(blockdiag,
                      mm_tn(jnp.concatenate([b_e[c], k_e[c]], axis=0).astype(bf16),
                            jnp.concatenate([u_v[c], v[c]], axis=0).astype(bf16)), 0.0) for c in C]
    decay_rows = [jnp.exp(x) for x in total_rows]

    y = []
    for c in C:
        state_b = state.astype(bf16)
        y.append(mm(r_h[c], state_b) + y_v[c])
        state = decay_rows[c] * state + mm(p_st[c], state_b) + g_st[c]
    state_ref[...] = state

    for c in C:
        mean = _split_dot(y[c], ones_bd) * (1.0 / RWKV_HEAD)
        yc = y[c] - mean
        var = _split_dot(yc * yc, ones_bd) * (1.0 / RWKV_HEAD)
        yn = yc * lax.rsqrt(var + GN_EPS) * lnw_ref[...] + lnb_ref[...]
        bonus = _split_dot(r[c] * k[c] * rk_ref[...], ones_bd) * v[c]
        o_ref[rows[c], :] = ((yn + bonus) * g_ref[rows[c], :].astype(f32)).astype(o_ref.dtype)


def rwkv_scan(r, lw, k, v, kk, kka, g, ln_w, ln_b, r_k, *, batch, seq, rows):
    nb = seq // rows
    n_hg = RWKV_HEADS // HEADS_PER_STEP
    T = batch * seq
    blk = pl.BlockSpec((rows, HG_LANES), lambda b, h, c: (b * nb + c, h))
    par = pl.BlockSpec((1, HG_LANES), lambda b, h, c: (0, h))
    return pl.pallas_call(
        functools.partial(_rwkv_scan_kernel, n_chunks=rows // RWKV_CHUNK),
        out_shape=jax.ShapeDtypeStruct((T, RWKV_WIDTH), bf16),
        grid=(batch, n_hg, nb),
        in_specs=[blk] * 7 + [par] * 3,
        out_specs=blk,
        scratch_shapes=[pltpu.VMEM((HG_LANES, HG_LANES), f32)],
        compiler_params=_cparams(("parallel", "parallel", "arbitrary")),
        name="rwkv_scan",
    )(r, lw, k, v, kk, kka, g, ln_w.reshape(1, -1), ln_b.reshape(1, -1), r_k.reshape(1, -1))


def _xattn_kernel(q_ref, k_ref, v_ref, o_ref):
    s = lax.dot_general(q_ref[...], k_ref[...], (((1,), (1,)), ((), ())),
                        preferred_element_type=f32) * (XATTN_HEAD_DIM ** -0.5)
    m = jnp.max(s, axis=-1, keepdims=True)
    e = jnp.exp(s - m)
    l = jnp.sum(e, axis=-1, keepdims=True)
    o = jnp.dot(e.astype(bf16), v_ref[...], preferred_element_type=f32)
    o_ref[...] = (o / l).astype(o_ref.dtype)


def xattn_core(q, kv, *, batch, seq, ts):
    ns = seq // ts
    hd = XATTN_HEAD_DIM
    return pl.pallas_call(
        _xattn_kernel,
        out_shape=jax.ShapeDtypeStruct(q.shape, bf16),
        grid=(batch, ns, XATTN_HEADS),
        in_specs=[
            pl.BlockSpec((ts, hd), lambda b, s, h: (b * ns + s, h)),
            pl.BlockSpec((MEM_LEN, hd), lambda b, s, h: (b, h)),
            pl.BlockSpec((MEM_LEN, hd), lambda b, s, h: (b, XATTN_HEADS + h)),
        ],
        out_specs=pl.BlockSpec((ts, hd), lambda b, s, h: (b * ns + s, h)),
        compiler_params=_cparams(("parallel", "parallel", "arbitrary")),
        name="xattn_core",
    )(q, kv, kv)


def _router_kernel(h_ref, g_ref, wr_ref, br_ref, hn_ref, sel_ref):
    x = h_ref[...]
    ms = jnp.mean(x * x, axis=-1, keepdims=True)
    hn = x * lax.rsqrt(ms + NORM_EPS) * g_ref[...]
    hn_ref[...] = hn.astype(hn_ref.dtype)

    w = wr_ref[...]
    w_hi = w.astype(bf16)
    w_lo = (w - w_hi.astype(f32)).astype(bf16)
    h_hi = hn.astype(bf16)
    logits = (_split_dot(hn, w_hi) + jnp.dot(h_hi, w_lo, preferred_element_type=f32)) + br_ref[...]

    lane_i = lax.broadcasted_iota(jnp.int32, logits.shape, 1)
    lane = lane_i.astype(f32)
    neg = -jnp.inf
    big = float(ROUTER_LANES)
    is_g = lane_i < MOE_GROUPS
    gl = jnp.where(is_g, logits, neg)
    gmax = jnp.max(gl, axis=-1, keepdims=True)
    g_idx = jnp.min(jnp.where(gl == gmax, lane, big), axis=-1, keepdims=True)
    g_w = 1.0 / jnp.sum(jnp.exp(gl - gmax), axis=-1, keepdims=True)

    lo_lane = MOE_GROUPS + g_idx * MOE_EXPERTS_PER_GROUP
    sel = (lane >= lo_lane) & (lane < lo_lane + MOE_EXPERTS_PER_GROUP)
    el = jnp.where(sel, logits, neg)
    v1 = jnp.max(el, axis=-1, keepdims=True)
    i1 = jnp.min(jnp.where(el == v1, lane, big), axis=-1, keepdims=True)
    el2 = jnp.where(lane == i1, neg, el)
    v2 = jnp.max(el2, axis=-1, keepdims=True)
    i2 = jnp.min(jnp.where(el2 == v2, lane, big), axis=-1, keepdims=True)
    e2 = jnp.exp(v2 - v1)
    p1 = 1.0 / (1.0 + e2)
    p2 = e2 / (1.0 + e2)
    sel_ref[...] = (jnp.where(lane_i == 0, i1 - MOE_GROUPS, 0.0)
                    + jnp.where(lane_i == 1, i2 - MOE_GROUPS, 0.0)
                    + jnp.where(lane_i == 2, p1 * g_w, 0.0)
                    + jnp.where(lane_i == 3, p2 * g_w, 0.0))


def moe_router(h, g, w_router, b_router, *, tm):
    T, D = h.shape
    return pl.pallas_call(
        _router_kernel,
        out_shape=[jax.ShapeDtypeStruct((T, D), f32),
                   jax.ShapeDtypeStruct((T, ROUTER_LANES), f32)],
        grid=(T // tm,),
        in_specs=[
            pl.BlockSpec((tm, D), lambda i: (i, 0)),
            pl.BlockSpec((1, D), lambda i: (0, 0)),
            pl.BlockSpec((D, ROUTER_LANES), lambda i: (0, 0)),
            pl.BlockSpec((1, ROUTER_LANES), lambda i: (0, 0)),
        ],
        out_specs=[pl.BlockSpec((tm, D), lambda i: (i, 0)),
                   pl.BlockSpec((tm, ROUTER_LANES), lambda i: (i, 0))],
        compiler_params=_cparams(("parallel",)),
        name="moe_router",
    )(h, g.reshape(1, D), w_router, b_router)


def _route_metadata(sel, *, tm, n_tiles):
    T = sel.shape[0]
    ids = sel[:, 0:2].astype(jnp.int32).reshape(-1)
    wts = sel[:, 2:4].reshape(-1)
    onehot = (ids[:, None] == jnp.arange(MOE_EXPERTS, dtype=jnp.int32)[None, :]).astype(jnp.int32)
    csum = jnp.cumsum(onehot, axis=0)
    counts = csum[-1]
    rank = jnp.sum(csum * onehot, axis=1) - 1
    padded = ((counts + tm - 1) // tm) * tm
    ends = jnp.cumsum(padded)
    pos = (ends - padded)[ids] + rank
    n_slots = n_tiles * tm
    src = jnp.zeros((n_slots,), jnp.int32).at[pos].set(jnp.arange(2 * T, dtype=jnp.int32) // 2)
    w_sorted = jnp.zeros((n_slots,), f32).at[pos].set(wts)
    tile_start = jnp.arange(n_tiles, dtype=jnp.int32) * tm
    n_valid = ends[-1] // tm
    tile_expert = jnp.sum((tile_start[:, None] >= ends[None, :]).astype(jnp.int32), axis=1)
    last_expert = jnp.sum((ends[-1] - 1 >= ends).astype(jnp.int32))
    tile_expert = jnp.where(tile_start < ends[-1], tile_expert, last_expert)
    return pos, src, w_sorted.reshape(n_slots, 1), tile_expert, n_valid.reshape(1)


def _moe_sparse_kernel(te_ref, src_ref, nv_ref, hn_hbm, w_ref, wg_ref, wu_ref, wd_ref, o_ref,
                       xbuf, sem, *, tm):
    i = pl.program_id(0)
    n_valid = nv_ref[0]

    def gather(tile, slot):
        def body(r, carry):
            tok = src_ref[tile * tm + r]
            pltpu.make_async_copy(hn_hbm.at[pl.ds(tok, 1), :],
                                  xbuf.at[slot, pl.ds(r, 1), :], sem.at[slot]).start()
            return carry
        lax.fori_loop(0, tm, body, 0, unroll=8)

    @pl.when(i == 0)
    def _():
        gather(0, 0)

    @pl.when(i + 1 < n_valid)
    def _():
        gather(i + 1, (i + 1) % 2)

    @pl.when(i < n_valid)
    def _():
        slot = i % 2
        pltpu.make_async_copy(hn_hbm.at[pl.ds(0, tm), :], xbuf.at[slot], sem.at[slot]).wait()
        x = xbuf[slot].astype(bf16)
        gate = jnp.dot(x, wg_ref[0], preferred_element_type=f32)
        up = jnp.dot(x, wu_ref[0], preferred_element_type=f32)
        hid = gate * _sigmoid(gate) * up
        y = jnp.dot(hid.astype(bf16), wd_ref[0], preferred_element_type=f32)
        o_ref[...] = y * w_ref[...]

    @pl.when(i >= n_valid)
    def _():
        o_ref[...] = jnp.zeros_like(o_ref)


def moe_sparse(hn, w_sorted, tile_expert, src, n_valid, wg, wu, wd, *, tm, n_tiles):
    T, D = hn.shape
    grid_spec = pltpu.PrefetchScalarGridSpec(
        num_scalar_prefetch=3,
        grid=(n_tiles,),
        in_specs=[
            pl.BlockSpec(memory_space=pl.ANY),
            pl.BlockSpec((tm, 1), lambda i, te, src, nv: (i, 0)),
            pl.BlockSpec((1, D, MOE_FF), lambda i, te, src, nv: (te[i], 0, 0)),
            pl.BlockSpec((1, D, MOE_FF), lambda i, te, src, nv: (te[i], 0, 0)),
            pl.BlockSpec((1, MOE_FF, D), lambda i, te, src, nv: (te[i], 0, 0)),
        ],
        out_specs=pl.BlockSpec((tm, D), lambda i, te, src, nv: (i, 0)),
        scratch_shapes=[pltpu.VMEM((2, tm, D), f32), pltpu.SemaphoreType.DMA((2,))],
    )
    return pl.pallas_call(
        functools.partial(_moe_sparse_kernel, tm=tm),
        out_shape=jax.ShapeDtypeStruct((n_tiles * tm, D), f32),
        grid_spec=grid_spec,
        compiler_params=_cparams(("arbitrary",)),
        name="moe_sparse",
    )(tile_expert, src, n_valid, hn, w_sorted, wg, wu, wd)


def _moe_combine_kernel(pos_ref, h_ref, y_hbm, gf_ref, o_ref, ybuf, sem, *, tm):
    i = pl.program_id(0)
    n = pl.num_programs(0)

    def gather(tile, slot):
        def body(r, carry):
            for c in range(2):
                p = pos_ref[(tile * tm + r) * 2 + c]
                pltpu.make_async_copy(y_hbm.at[pl.ds(p, 1), :],
                                      ybuf.at[slot, c, pl.ds(r, 1), :], sem.at[slot]).start()
            return carry
        lax.fori_loop(0, tm, body, 0, unroll=4)

    @pl.when(i == 0)
    def _():
        gather(0, 0)

    @pl.when(i + 1 < n)
    def _():
        gather(i + 1, (i + 1) % 2)

    slot = i % 2
    for c in range(2):
        pltpu.make_async_copy(y_hbm.at[pl.ds(0, tm), :], ybuf.at[slot, c], sem.at[slot]).wait()
    y = h_ref[...] + ybuf[slot, 0] + ybuf[slot, 1]
    ms = jnp.mean(y * y, axis=-1, keepdims=True)
    o_ref[...] = y * lax.rsqrt(ms + NORM_EPS) * gf_ref[...]


def moe_combine(h, y_sorted, pos, g_final, *, tm):
    T, D = h.shape
    grid_spec = pltpu.PrefetchScalarGridSpec(
        num_scalar_prefetch=1,
        grid=(T // tm,),
        in_specs=[
            pl.BlockSpec((tm, D), lambda i, pos: (i, 0)),
            pl.BlockSpec(memory_space=pl.ANY),
            pl.BlockSpec((1, D), lambda i, pos: (0, 0)),
        ],
        out_specs=pl.BlockSpec((tm, D), lambda i, pos: (i, 0)),
        scratch_shapes=[pltpu.VMEM((2, 2, tm, D), f32), pltpu.SemaphoreType.DMA((2,))],
    )
    return pl.pallas_call(
        functools.partial(_moe_combine_kernel, tm=tm),
        out_shape=jax.ShapeDtypeStruct((T, D), f32),
        grid_spec=grid_spec,
        compiler_params=_cparams(("arbitrary",)),
        name="moe_combine",
    )(pos, h, y_sorted, g_final.reshape(1, D))


def _pad_rows(w, rows):
    return jnp.pad(w, ((0, rows - w.shape[0]), (0, 0)))


def _pack_lora_cols(t):
    o1, o2 = DECAY_LORA, DECAY_LORA + AAA_LORA
    pad = lambda a, n: jnp.pad(a, [(0, 0)] * (a.ndim - 1) + [(0, n - a.shape[-1])])
    return jnp.concatenate([pad(t[..., :o1], 128), pad(t[..., o1:o2], 128),
                            pad(t[..., o2:], 256)], axis=-1)


def _layer(x2, mem2, p, *, batch, seq, cfg):
    D = D_MODEL
    w_in = p['w_in']
    w_in_p = jnp.concatenate([w_in[:, :MAIN_COLS], _pack_lora_cols(w_in[:, MAIN_COLS:])],
                             axis=1).astype(bf16)
    mu = p['rwkv_mu']
    mu_main = mu[:3 * RWKV_WIDTH]
    mu_lora = _pack_lora_cols(mu[3 * RWKV_WIDTH:])

    proj = norm_matmul(x2, p['norm_mix_g'], w_in_p, tm=cfg['tm'], tn=cfg['tn_in'], name="in_proj")
    pool_out = pool_mixer(proj, p['pool_w'].astype(bf16), p['pool_scale'],
                          batch=batch, seq=seq, ts=cfg['ts_pool'])
    r, lw, k, v, kk, kka, g = rwkv_prep(
        proj, mu_main, mu_lora, p['rwkv_w0'], _pad_rows(p['rwkv_w2'], 128).astype(bf16),
        p['rwkv_a0'], _pad_rows(p['rwkv_a2'], 128).astype(bf16),
        _pad_rows(p['rwkv_g2'], 256).astype(bf16), p['rwkv_k_k'], p['rwkv_k_a'],
        batch=batch, seq=seq, ts=cfg['ts_prep'])
    rwkv_out = rwkv_scan(r, lw, k, v, kk, kka, g, p['rwkv_ln_w'], p['rwkv_ln_b'], p['rwkv_r_k'],
                         batch=batch, seq=seq, rows=cfg['scan_rows'])
    w_out = p['w_out'].astype(bf16)
    h1 = matmul_residual([pool_out, rwkv_out], [w_out[:POOL_WIDTH], w_out[POOL_WIDTH:]], x2,
                         tm=cfg['tm'], tn=cfg['tn'], name="out_proj")

    q = norm_matmul(h1, p['norm_xattn_g'], p['xattn_w_q'].astype(bf16),
                    tm=cfg['tm'], tn=cfg['tn'], name="q_proj")
    kv = norm_matmul(mem2, p['norm_mem_g'], p['xattn_w_kv'].astype(bf16),
                     tm=min(cfg['tm'], mem2.shape[0]), tn=cfg['tn'], name="kv_proj")
    o = xattn_core(q, kv, batch=batch, seq=seq, ts=cfg['ts_attn'])
    h2 = matmul_residual([o], [p['xattn_w_o'].astype(bf16)], h1,
                         tm=cfg['tm'], tn=cfg['tn'], name="o_proj")

    w_router = jnp.pad(jnp.concatenate([p['moe_w_group'], p['moe_w_expert']], axis=1),
                       ((0, 0), (0, ROUTER_LANES - MOE_GROUPS - MOE_EXPERTS)))
    b_router = jnp.pad(jnp.concatenate([p['moe_b_group'], p['moe_b_expert']]),
                       (0, ROUTER_LANES - MOE_GROUPS - MOE_EXPERTS)).reshape(1, ROUTER_LANES)
    hn3, sel = moe_router(h2, p['norm_ffn_g'], w_router, b_router, tm=cfg['tm'])
    tm_e = cfg['tm_moe']
    n_tiles = (2 * hn3.shape[0]) // tm_e + MOE_EXPERTS
    pos, src, w_sorted, tile_expert, n_valid = _route_metadata(sel, tm=tm_e, n_tiles=n_tiles)
    y_sorted = moe_sparse(hn3, w_sorted, tile_expert, src, n_valid,
                          p['moe_w_gate'].astype(bf16), p['moe_w_up'].astype(bf16),
                          p['moe_w_down'].astype(bf16), tm=tm_e, n_tiles=n_tiles)
    return moe_combine(h2, y_sorted, pos, p['norm_final_g'], tm=cfg['tm_comb'])


_CFG = dict(tm=512, tn=1024, tn_in=1536, ts_pool=512, ts_prep=256, scan_rows=512,
            ts_attn=1024, tm_moe=512, tm_comb=256)


def kernel(x, mem, norm_mix_g, w_in, pool_w, pool_scale, rwkv_mu, rwkv_w0, rwkv_w2, rwkv_a0,
           rwkv_a2, rwkv_g2, rwkv_k_k, rwkv_k_a, rwkv_r_k, rwkv_ln_w, rwkv_ln_b, w_out,
           norm_xattn_g, norm_mem_g, xattn_w_q, xattn_w_kv, xattn_w_o, norm_ffn_g,
           moe_w_group, moe_b_group, moe_w_expert, moe_b_expert, moe_w_gate, moe_w_up,
           moe_w_down, norm_final_g):
    batch, seq, D = x.shape
    p = dict(norm_mix_g=norm_mix_g[0], w_in=w_in[0], pool_w=pool_w[0], pool_scale=pool_scale[0],
             rwkv_mu=rwkv_mu[0], rwkv_w0=rwkv_w0[0], rwkv_w2=rwkv_w2[0], rwkv_a0=rwkv_a0[0],
             rwkv_a2=rwkv_a2[0], rwkv_g2=rwkv_g2[0], rwkv_k_k=rwkv_k_k[0], rwkv_k_a=rwkv_k_a[0],
             rwkv_r_k=rwkv_r_k[0], rwkv_ln_w=rwkv_ln_w[0], rwkv_ln_b=rwkv_ln_b[0], w_out=w_out[0],
             norm_xattn_g=norm_xattn_g[0], norm_mem_g=norm_mem_g[0], xattn_w_q=xattn_w_q[0],
             xattn_w_kv=xattn_w_kv[0], xattn_w_o=xattn_w_o[0], norm_ffn_g=norm_ffn_g[0],
             moe_w_group=moe_w_group[0], moe_b_group=moe_b_group[0], moe_w_expert=moe_w_expert[0],
             moe_b_expert=moe_b_expert[0], moe_w_gate=moe_w_gate[0], moe_w_up=moe_w_up[0],
             moe_w_down=moe_w_down[0], norm_final_g=norm_final_g)
    out = _layer(x.reshape(batch * seq, D), mem.reshape(batch * MEM_LEN, D), p,
                 batch=batch, seq=seq, cfg=_CFG)
    return out.reshape(batch, seq, D)
```

```python
import functools

import jax
import jax.numpy as jnp
from jax import lax
from jax.experimental import pallas as pl
from jax.experimental.pallas import tpu as pltpu

f32 = jnp.float32
bf16 = jnp.bfloat16

D_MODEL = 2048
MEM_LEN = 256
NORM_EPS = 1e-6

POOL_WIDTH = 1024
POOL_WINDOWS = (2, 4, 8, 16)
POOL_GROUP = 256
POOL_HALO = 16
RWKV_WIDTH = 1024
RWKV_HEAD = 64
RWKV_HEADS = 16
GN_EPS = 64e-5
DECAY_LORA = 64
AAA_LORA = 64
GATE_LORA = 160
LORA_PAD = 512
LORA_W_OFF, LORA_A_OFF, LORA_G_OFF = 0, 128, 256
MAIN_COLS = POOL_WIDTH + 3 * RWKV_WIDTH
PROJ_COLS = MAIN_COLS + LORA_PAD

RWKV_CHUNK = 64
HEADS_PER_STEP = 4
HG_LANES = HEADS_PER_STEP * RWKV_HEAD

XATTN_HEADS = 4
XATTN_HEAD_DIM = 512

MOE_GROUPS = 4
MOE_EXPERTS_PER_GROUP = 4
MOE_EXPERTS = 16
MOE_FF = 512
ROUTER_LANES = 128

VMEM_LIMIT = 56 * 1024 * 1024


def _cparams(sem):
    return pltpu.CompilerParams(dimension_semantics=sem, vmem_limit_bytes=VMEM_LIMIT)


def _split_dot(x, w_bf16):
    hi = x.astype(bf16)
    lo = (x - hi.astype(f32)).astype(bf16)
    return (jnp.dot(hi, w_bf16, preferred_element_type=f32)
            + jnp.dot(lo, w_bf16, preferred_element_type=f32))


def _norm_mm_kernel(x_ref, g_ref, w_ref, o_ref, xn_ref):
    @pl.when(pl.program_id(1) == 0)
    def _():
        x = x_ref[...]
        ms = jnp.mean(x * x, axis=-1, keepdims=True)
        xn_ref[...] = (x * lax.rsqrt(ms + NORM_EPS) * g_ref[...]).astype(bf16)

    o_ref[...] = jnp.dot(xn_ref[...], w_ref[...], preferred_element_type=f32).astype(o_ref.dtype)


def norm_matmul(x, g, w, *, tm, tn, out_dtype=bf16, name="norm_matmul"):
    M, K = x.shape
    N = w.shape[1]
    return pl.pallas_call(
        _norm_mm_kernel,
        out_shape=jax.ShapeDtypeStruct((M, N), out_dtype),
        grid=(M // tm, N // tn),
        in_specs=[
            pl.BlockSpec((tm, K), lambda i, j: (i, 0)),
            pl.BlockSpec((1, K), lambda i, j: (0, 0)),
            pl.BlockSpec((K, tn), lambda i, j: (0, j)),
        ],
        out_specs=pl.BlockSpec((tm, tn), lambda i, j: (i, j)),
        scratch_shapes=[pltpu.VMEM((tm, K), bf16)],
        compiler_params=_cparams(("parallel", "arbitrary")),
        name=name,
    )(x, g.reshape(1, K), w)


def _resident(shape):
    return pl.BlockSpec(shape, lambda i: (0,) * len(shape), pipeline_mode=pl.Buffered(1))


def _rmsnorm_rows(x, g):
    ms = jnp.mean(x * x, axis=-1, keepdims=True)
    return x * lax.rsqrt(ms + NORM_EPS) * g


def _norm_mm_res_kernel(x_ref, g_ref, w_ref, o_ref, *, tn):
    xn = _rmsnorm_rows(x_ref[...], g_ref[...]).astype(bf16)
    for n0 in range(0, o_ref.shape[1], tn):
        o_ref[:, n0:n0 + tn] = jnp.dot(xn, w_ref[:, n0:n0 + tn],
                                       preferred_element_type=f32).astype(o_ref.dtype)


def norm_matmul_resident(x, g, w, *, tm, tn, name):
    M, K = x.shape
    N = w.shape[1]
    return pl.pallas_call(
        functools.partial(_norm_mm_res_kernel, tn=tn),
        out_shape=jax.ShapeDtypeStruct((M, N), bf16),
        grid=(M // tm,),
        in_specs=[pl.BlockSpec((tm, K), lambda i: (i, 0)), _resident((1, K)), _resident((K, N))],
        out_specs=pl.BlockSpec((tm, N), lambda i: (i, 0)),
        compiler_params=_cparams(("parallel",)),
        name=name,
    )(x, g.reshape(1, K), w)


def _mm_res_resident_kernel(*refs, n, tn):
    a_refs, w_refs, res_ref, o_ref = refs[:n], refs[n:2 * n], refs[2 * n], refs[2 * n + 1]
    for n0 in range(0, o_ref.shape[1], tn):
        acc = res_ref[:, n0:n0 + tn]
        for a_ref, w_ref in zip(a_refs, w_refs):
            acc = acc + jnp.dot(a_ref[...], w_ref[:, n0:n0 + tn], preferred_element_type=f32)
        o_ref[:, n0:n0 + tn] = acc


def matmul_residual_resident(a_list, w_list, res, *, tm, tn, name):
    M, N = res.shape
    n = len(a_list)
    return pl.pallas_call(
        functools.partial(_mm_res_resident_kernel, n=n, tn=tn),
        out_shape=jax.ShapeDtypeStruct((M, N), f32),
        grid=(M // tm,),
        in_specs=([pl.BlockSpec((tm, a.shape[1]), lambda i: (i, 0)) for a in a_list]
                  + [_resident(w.shape) for w in w_list]
                  + [pl.BlockSpec((tm, N), lambda i: (i, 0))]),
        out_specs=pl.BlockSpec((tm, N), lambda i: (i, 0)),
        compiler_params=_cparams(("parallel",)),
        name=name,
    )(*a_list, *w_list, res)


def _pool_kernel(u_ref, halo_ref, pw_ref, ps_ref, o_ref, *, ts):
    s = pl.program_id(1)
    u = u_ref[...].astype(f32)
    halo = jnp.where(s > 0, halo_ref[...].astype(f32), 0.0)
    ext = jnp.concatenate([halo, u], axis=0)
    pos = s * ts + lax.broadcasted_iota(jnp.int32, (ts, 1), 0)
    acc = ext
    shift = 1
    for gi, w in enumerate(POOL_WINDOWS):
        while shift < w:
            acc = acc + pltpu.roll(acc, shift, 0)
            shift *= 2
        lo, hi = gi * POOL_GROUP, (gi + 1) * POOL_GROUP
        cnt = jnp.minimum(pos + 1, w).astype(f32)
        pooled = acc[POOL_HALO:, lo:hi] / cnt - u[:, lo:hi]
        mixed = jnp.dot(pooled.astype(bf16), pw_ref[gi], preferred_element_type=f32)
        o_ref[:, lo:hi] = (mixed * ps_ref[:, lo:hi]).astype(o_ref.dtype)


def pool_mixer(proj, pool_w, pool_scale, *, batch, seq, ts):
    ns = seq // ts
    hb = ts // POOL_HALO
    return pl.pallas_call(
        functools.partial(_pool_kernel, ts=ts),
        out_shape=jax.ShapeDtypeStruct((batch * seq, POOL_WIDTH), bf16),
        grid=(batch, ns),
        in_specs=[
            pl.BlockSpec((ts, POOL_WIDTH), lambda b, s: (b * ns + s, 0)),
            pl.BlockSpec((POOL_HALO, POOL_WIDTH),
                         lambda b, s: (jnp.maximum((b * ns + s) * hb - 1, 0), 0)),
            pl.BlockSpec((len(POOL_WINDOWS), POOL_GROUP, POOL_GROUP), lambda b, s: (0, 0, 0)),
            pl.BlockSpec((1, POOL_WIDTH), lambda b, s: (0, 0)),
        ],
        out_specs=pl.BlockSpec((ts, POOL_WIDTH), lambda b, s: (b * ns + s, 0)),
        compiler_params=_cparams(("parallel", "arbitrary")),
        name="pool_mixer",
    )(proj, proj, pool_w, pool_scale.reshape(1, POOL_WIDTH))


def _head_ones():
    r = lax.broadcasted_iota(jnp.int32, (HG_LANES, HG_LANES), 0) // RWKV_HEAD
    c = lax.broadcasted_iota(jnp.int32, (HG_LANES, HG_LANES), 1) // RWKV_HEAD
    return jnp.where(r == c, 1.0, 0.0).astype(bf16)


def _head_sum(x, ones_bd):
    parts = [_split_dot(x[:, c:c + HG_LANES], ones_bd) for c in range(0, x.shape[1], HG_LANES)]
    return parts[0] if len(parts) == 1 else jnp.concatenate(parts, axis=1)


def _sigmoid(x):
    return 1.0 / (1.0 + jnp.exp(-x))


def _rwkv_prep_kernel(pr_ref, pk_ref, pv_ref, pl_ref, hr_ref, hk_ref, hv_ref, hl_ref,
                      mu_ref, mul_ref, w0_ref, w2_ref, a0_ref, a2_ref, g2_ref, kk_ref, ka_ref,
                      r_out, lw_out, k_out, v_out, kk_out, kka_out, g_out, *, ts):
    s = pl.program_id(1)
    row0 = lax.broadcasted_iota(jnp.int32, (ts, 1), 0) == 0
    C = RWKV_WIDTH

    def shifted(p_ref, h_ref, mu):
        p = p_ref[...].astype(f32)
        last = h_ref[...].astype(f32)[POOL_HALO - 1:POOL_HALO, :]
        last = jnp.where(s > 0, last, 0.0)
        prev = jnp.where(row0, last, pltpu.roll(p, 1, 0))
        return p + (prev - p) * mu

    r = shifted(pr_ref, hr_ref, mu_ref[:, 0:C])
    k = shifted(pk_ref, hk_ref, mu_ref[:, C:2 * C])
    v = shifted(pv_ref, hv_ref, mu_ref[:, 2 * C:3 * C])
    lo = shifted(pl_ref, hl_ref, mul_ref[...])
    w_lo = lo[:, LORA_W_OFF:LORA_W_OFF + 128]
    a_lo = lo[:, LORA_A_OFF:LORA_A_OFF + 128]
    g_lo = lo[:, LORA_G_OFF:LORA_G_OFF + 256]

    wx = w0_ref[...] + jnp.dot(jnp.tanh(w_lo).astype(bf16), w2_ref[...], preferred_element_type=f32)
    nx = -wx
    softplus = jnp.maximum(nx, 0.0) + jnp.log(1.0 + jnp.exp(-jnp.abs(nx)))
    w_log = -softplus - 0.5
    lw_out[...] = -jnp.exp(w_log)
    a = _sigmoid(a0_ref[...] + jnp.dot(a_lo.astype(bf16), a2_ref[...], preferred_element_type=f32))
    g = jnp.dot(_sigmoid(g_lo).astype(bf16), g2_ref[...], preferred_element_type=f32)

    ones_bd = _head_ones()
    kk = k * kk_ref[...]
    kk = kk * lax.rsqrt(jnp.maximum(_head_sum(kk * kk, ones_bd), 1e-24))
    k = k * (1.0 + (a - 1.0) * ka_ref[...])

    r_out[...] = r.astype(r_out.dtype)
    k_out[...] = k.astype(k_out.dtype)
    v_out[...] = v.astype(v_out.dtype)
    kk_out[...] = kk.astype(kk_out.dtype)
    kka_out[...] = (kk * a).astype(kka_out.dtype)
    g_out[...] = g.astype(g_out.dtype)


def rwkv_prep(proj, mu_main, mu_lora, w0, w2p, a0, a2p, g2p, k_k, k_a, *, batch, seq, ts):
    ns = seq // ts
    hb = ts // POOL_HALO
    C = RWKV_WIDTH
    T = batch * seq
    lora_blk = MAIN_COLS // LORA_PAD

    def main_spec(col):
        return pl.BlockSpec((ts, C), lambda b, s: (b * ns + s, col))

    def halo_spec(width, col):
        return pl.BlockSpec((POOL_HALO, width),
                            lambda b, s: (jnp.maximum((b * ns + s) * hb - 1, 0), col))

    def full(shape):
        return pl.BlockSpec(shape, lambda b, s: (0,) * len(shape))

    row = lambda t: t.reshape(1, -1)
    out_sd = lambda dt: jax.ShapeDtypeStruct((T, C), dt)
    out_spec = pl.BlockSpec((ts, C), lambda b, s: (b * ns + s, 0))
    return pl.pallas_call(
        functools.partial(_rwkv_prep_kernel, ts=ts),
        out_shape=[out_sd(bf16), out_sd(f32), out_sd(bf16), out_sd(bf16),
                   out_sd(bf16), out_sd(bf16), out_sd(bf16)],
        grid=(batch, ns),
        in_specs=[
            main_spec(1), main_spec(2), main_spec(3),
            pl.BlockSpec((ts, LORA_PAD), lambda b, s: (b * ns + s, lora_blk)),
            halo_spec(C, 1), halo_spec(C, 2), halo_spec(C, 3), halo_spec(LORA_PAD, lora_blk),
            full((1, 3 * C)), full((1, LORA_PAD)), full((1, C)), full((128, C)),
            full((1, C)), full((128, C)), full((256, C)), full((1, C)), full((1, C)),
        ],
        out_specs=[out_spec] * 7,
        compiler_params=_cparams(("parallel", "arbitrary")),
        name="rwkv_prep",
    )(proj, proj, proj, proj, proj, proj, proj, proj,
      row(mu_main), row(mu_lora), row(w0), w2p, row(a0), a2p, g2p, row(k_k), row(k_a))


def _bd_mask():
    r = lax.broadcasted_iota(jnp.int32, (HG_LANES, HG_LANES), 0) // RWKV_CHUNK
    c = lax.broadcasted_iota(jnp.int32, (HG_LANES, HG_LANES), 1) // RWKV_HEAD
    return r == c


def _rwkv_scan_kernel(r_ref, lw_ref, k_ref, v_ref, kk_ref, kka_ref, g_ref,
                      lnw_ref, lnb_ref, rk_ref, o_ref, state_ref, *, n_chunks, n_hg):
    L = RWKV_CHUNK
    W = HG_LANES

    @pl.when(pl.program_id(2) == 0)
    def _():
        state_ref[...] = jnp.zeros_like(state_ref)

    bd_mask = _bd_mask()
    ones_bd = _head_ones()

    def bd(x):
        tiled = jnp.concatenate([x] * HEADS_PER_STEP, axis=0)
        return jnp.where(bd_mask, tiled, 0.0).astype(bf16)

    def mm(a, b_bf16):
        return jnp.dot(a.astype(bf16), b_bf16, preferred_element_type=f32)

    def mm_nt(a, b_bf16):
        return lax.dot_general(a.astype(bf16), b_bf16, (((1,), (1,)), ((), ())),
                               preferred_element_type=f32)

    def mm_tn(a_bf16, b_bf16):
        return lax.dot_general(a_bf16, b_bf16, (((0,), (0,)), ((), ())),
                               preferred_element_type=f32)

    def split(x):
        hi = x.astype(bf16)
        return hi, (x - hi.astype(f32)).astype(bf16)

    t_idx = lax.broadcasted_iota(jnp.int32, (L, W), 0)
    s_idx = lax.broadcasted_iota(jnp.int32, (L, W), 1) % L
    strict = t_idx > s_idx
    incl = t_idx >= s_idx
    eye_all = jnp.where(t_idx == s_idx, 1.0, 0.0)
    tri = jnp.where(lax.broadcasted_iota(jnp.int32, (L, L), 0)
                    >= lax.broadcasted_iota(jnp.int32, (L, L), 1), 1.0, 0.0).astype(bf16)
    blockdiag = (lax.broadcasted_iota(jnp.int32, (W, W), 0) // RWKV_HEAD
                 == lax.broadcasted_iota(jnp.int32, (W, W), 1) // RWKV_HEAD)

    def phase1(items):
        I = range(len(items))
        rows = [pl.ds(c * L, L) for _, c in items]
        cols = [pl.ds(h * W, W) for h, _ in items]
        ld = lambda ref, i: ref[rows[i], cols[i]]
        r = [ld(r_ref, i).astype(f32) for i in I]
        k = [ld(k_ref, i).astype(f32) for i in I]
        v = [ld(v_ref, i).astype(f32) for i in I]
        lw = [ld(lw_ref, i) for i in I]
        lw_s = [split(x) for x in lw]
        cs = [jnp.dot(tri, hi, preferred_element_type=f32) + jnp.dot(tri, lo, preferred_element_type=f32)
              for hi, lo in lw_s]
        yield
        g_inv = [jnp.exp(-x) for x in cs]
        to_end = [jnp.exp(x[L - 1:L, :] - x) for x in cs]
        a_t = [-ld(kk_ref, i).astype(f32) * jnp.exp(cs[i] - lw[i]) for i in I]
        r_t = [r[i] * jnp.exp(cs[i]) for i in I]
        kka = [ld(kka_ref, i).astype(f32) for i in I]
        b_t = [kka[i] * g_inv[i] for i in I]
        k_t = [k[i] * g_inv[i] for i in I]
        b_e = [kka[i] * to_end[i] for i in I]
        k_e = [k[i] * to_end[i] for i in I]
        v_bd = [bd(x) for x in v]
        ar = [jnp.concatenate([a_t[i], r_t[i]], axis=0) for i in I]
        m_b = [mm_nt(ar[i], bd(b_t[i])) for i in I]
        m_k = [mm_nt(ar[i], bd(k_t[i])) for i in I]
        m_ab = [jnp.where(strict, x[:L], 0.0) for x in m_b]
        m_rb = [jnp.where(incl, x[L:], 0.0) for x in m_b]
        m_ak = [jnp.where(strict, x[:L], 0.0) for x in m_k]
        m_rk = [jnp.where(incl, x[L:], 0.0) for x in m_k]
        bonus = [mm(r[i] * k[i] * rk_ref[:, cols[i]], ones_bd) * v[i] for i in I]
        yield
        p = m_ab
        tinv = [eye_all + x for x in p]
        m = 2
        while m < L:
            p = [mm(x, bd(x)) for x in p]
            tinv = [tinv[i] + mm(tinv[i], bd(p[i])) for i in I]
            m *= 2
            yield
        x1 = [mm(m_ak[i], v_bd[i]) for i in I]
        u_v = [mm(tinv[i], bd(x1[i])) for i in I]
        a_h = [mm(tinv[i], bd(a_t[i])) for i in I]
        yield
        r_h = [r_t[i] + mm(m_rb[i], bd(a_h[i])) for i in I]
        y_v = [mm(m_rb[i], bd(u_v[i])) + mm(m_rk[i], v_bd[i]) for i in I]
        yield
        p_t = [jnp.where(blockdiag, mm_tn(a_h[i].astype(bf16), b_e[i].astype(bf16)), 0.0).astype(bf16)
               for i in I]
        g_t = [jnp.where(blockdiag,
                         mm_tn(jnp.concatenate([u_v[i], v[i]], axis=0).astype(bf16),
                               jnp.concatenate([b_e[i], k_e[i]], axis=0).astype(bf16)), 0.0) for i in I]
        decay = [jnp.exp(x[L - 1:L, :]) for x in cs]
        for i in I:
            done.append(dict(hg=items[i][0], rows=rows[i], cols=cols[i], r_h=r_h[i], y_v=y_v[i],
                             p_t=p_t[i], g_t=g_t[i], decay=decay[i], bonus=bonus[i]))
        yield

    def chain_step(ch):
        state = states[ch['hg']]
        state_b = state.astype(bf16)
        y = mm_nt(ch['r_h'], state_b) + ch['y_v']
        states[ch['hg']] = (state * ch['decay']
                            + jnp.dot(state_b, ch['p_t'], preferred_element_type=f32) + ch['g_t'])
        mean = mm(y, ones_bd) * (1.0 / RWKV_HEAD)
        yc = y - mean
        var = mm(yc * yc, ones_bd) * (1.0 / RWKV_HEAD)
        rw, cl = ch['rows'], ch['cols']
        yn = yc * lax.rsqrt(var + GN_EPS) * lnw_ref[:, cl] + lnb_ref[:, cl]
        o_ref[rw, cl] = ((yn + ch['bonus']) * g_ref[rw, cl].astype(f32)).astype(o_ref.dtype)

    states = [state_ref[h] for h in range(n_hg)]
    done = []
    half = max(n_chunks // 2, 1)
    groups = [[(h, c) for c in range(0, half) for h in range(n_hg)],
              [(h, c) for c in range(half, n_chunks) for h in range(n_hg)]]
    for _ in phase1(groups[0]):
        pass
    n_chained = 0
    for _ in phase1(groups[1]):
        if n_chained < len(groups[0]):
            chain_step(done[n_chained])
            n_chained += 1
    while n_chained < len(done):
        chain_step(done[n_chained])
        n_chained += 1
    for h in range(n_hg):
        state_ref[h] = states[h]


def rwkv_scan(r, lw, k, v, kk, kka, g, ln_w, ln_b, r_k, *, batch, seq, rows, n_hg):
    nb = seq // rows
    T = batch * seq
    width = n_hg * HG_LANES
    n_col = RWKV_WIDTH // width
    blk = pl.BlockSpec((rows, width), lambda b, h, c: (b * nb + c, h))
    par = pl.BlockSpec((1, width), lambda b, h, c: (0, h))
    return pl.pallas_call(
        functools.partial(_rwkv_scan_kernel, n_chunks=rows // RWKV_CHUNK, n_hg=n_hg),
        out_shape=jax.ShapeDtypeStruct((T, RWKV_WIDTH), bf16),
        grid=(batch, n_col, nb),
        in_specs=[blk] * 7 + [par] * 3,
        out_specs=blk,
        scratch_shapes=[pltpu.VMEM((n_hg, HG_LANES, HG_LANES), f32)],
        compiler_params=_cparams(("parallel", "parallel", "arbitrary")),
        name="rwkv_scan",
    )(r, lw, k, v, kk, kka, g, ln_w.reshape(1, -1), ln_b.reshape(1, -1), r_k.reshape(1, -1))


def _xattn_kernel(q_ref, k_ref, v_ref, o_ref):
    s = lax.dot_general(q_ref[...], k_ref[...], (((1,), (1,)), ((), ())),
                        preferred_element_type=f32) * (XATTN_HEAD_DIM ** -0.5)
    m = jnp.max(s, axis=-1, keepdims=True)
    e = jnp.exp(s - m)
    l = jnp.sum(e, axis=-1, keepdims=True)
    o = jnp.dot(e.astype(bf16), v_ref[...], preferred_element_type=f32)
    o_ref[...] = (o / l).astype(o_ref.dtype)


def xattn_core(q, kv, *, batch, seq, ts):
    ns = seq // ts
    hd = XATTN_HEAD_DIM
    return pl.pallas_call(
        _xattn_kernel,
        out_shape=jax.ShapeDtypeStruct(q.shape, bf16),
        grid=(batch, ns, XATTN_HEADS),
        in_specs=[
            pl.BlockSpec((ts, hd), lambda b, s, h: (b * ns + s, h)),
            pl.BlockSpec((MEM_LEN, hd), lambda b, s, h: (b, h)),
            pl.BlockSpec((MEM_LEN, hd), lambda b, s, h: (b, XATTN_HEADS + h)),
        ],
        out_specs=pl.BlockSpec((ts, hd), lambda b, s, h: (b * ns + s, h)),
        compiler_params=_cparams(("parallel", "parallel", "arbitrary")),
        name="xattn_core",
    )(q, kv, kv)


def _route(hn, w, b):
    w_hi = w.astype(bf16)
    w_lo = (w - w_hi.astype(f32)).astype(bf16)
    h_hi = hn.astype(bf16)
    logits = (_split_dot(hn, w_hi) + jnp.dot(h_hi, w_lo, preferred_element_type=f32)) + b

    lane_i = lax.broadcasted_iota(jnp.int32, logits.shape, 1)
    lane = lane_i.astype(f32)
    neg = -jnp.inf
    big = float(ROUTER_LANES)
    is_g = lane_i < MOE_GROUPS
    gl = jnp.where(is_g, logits, neg)
    gmax = jnp.max(gl, axis=-1, keepdims=True)
    g_idx = jnp.min(jnp.where(gl == gmax, lane, big), axis=-1, keepdims=True)
    g_w = 1.0 / jnp.sum(jnp.exp(gl - gmax), axis=-1, keepdims=True)

    lo_lane = MOE_GROUPS + g_idx * MOE_EXPERTS_PER_GROUP
    sel = (lane >= lo_lane) & (lane < lo_lane + MOE_EXPERTS_PER_GROUP)
    el = jnp.where(sel, logits, neg)
    v1 = jnp.max(el, axis=-1, keepdims=True)
    i1 = jnp.min(jnp.where(el == v1, lane, big), axis=-1, keepdims=True)
    el2 = jnp.where(lane == i1, neg, el)
    v2 = jnp.max(el2, axis=-1, keepdims=True)
    i2 = jnp.min(jnp.where(el2 == v2, lane, big), axis=-1, keepdims=True)
    e2 = jnp.exp(v2 - v1)
    p1 = 1.0 / (1.0 + e2)
    p2 = e2 / (1.0 + e2)
    return (jnp.where(lane_i == 0, i1 - MOE_GROUPS, 0.0)
            + jnp.where(lane_i == 1, i2 - MOE_GROUPS, 0.0)
            + jnp.where(lane_i == 2, p1 * g_w, 0.0)
            + jnp.where(lane_i == 3, p2 * g_w, 0.0))


def _oproj_router_kernel(o_ref, wo_ref, res_ref, g_ref, wr_ref, br_ref, h_ref, hn_ref, sel_ref, *, tn):
    for n0 in range(0, h_ref.shape[1], tn):
        h_ref[:, n0:n0 + tn] = res_ref[:, n0:n0 + tn] + jnp.dot(
            o_ref[...], wo_ref[:, n0:n0 + tn], preferred_element_type=f32)
    hn = _rmsnorm_rows(h_ref[...], g_ref[...])
    hn_ref[...] = hn
    sel_ref[...] = _route(hn, wr_ref[...], br_ref[...])


def oproj_router(o, w_o, res, g, w_router, b_router, *, tm, tn):
    T, D = res.shape
    row = lambda width: pl.BlockSpec((tm, width), lambda i: (i, 0))
    return pl.pallas_call(
        functools.partial(_oproj_router_kernel, tn=tn),
        out_shape=[jax.ShapeDtypeStruct((T, D), f32), jax.ShapeDtypeStruct((T, D), f32),
                   jax.ShapeDtypeStruct((T, ROUTER_LANES), f32)],
        grid=(T // tm,),
        in_specs=[row(D), _resident((D, D)), row(D), _resident((1, D)),
                  _resident((D, ROUTER_LANES)), _resident((1, ROUTER_LANES))],
        out_specs=[row(D), row(D), row(ROUTER_LANES)],
        compiler_params=_cparams(("parallel",)),
        name="oproj_router",
    )(o, w_o, res, g.reshape(1, D), w_router, b_router)


def _route_metadata(sel, *, tm, n_tiles):
    T = sel.shape[0]
    ids = sel[:, 0:2].astype(jnp.int32).reshape(-1)
    onehot = (ids[:, None] == jnp.arange(MOE_EXPERTS, dtype=jnp.int32)[None, :]).astype(jnp.int32)
    csum = jnp.cumsum(onehot, axis=0)
    counts = csum[-1]
    rank = jnp.sum(csum * onehot, axis=1) - 1
    padded = ((counts + tm - 1) // tm) * tm
    ends = jnp.cumsum(padded)
    pos = (ends - padded)[ids] + rank
    n_slots = n_tiles * tm
    src = jnp.zeros((n_slots,), jnp.int32).at[pos].set(jnp.arange(2 * T, dtype=jnp.int32) // 2)
    tile_start = jnp.arange(n_tiles, dtype=jnp.int32) * tm
    n_valid = ends[-1] // tm
    tile_expert = jnp.sum((tile_start[:, None] >= ends[None, :]).astype(jnp.int32), axis=1)
    last_expert = jnp.sum((ends[-1] - 1 >= ends).astype(jnp.int32))
    tile_expert = jnp.where(tile_start < ends[-1], tile_expert, last_expert)
    return pos, src, tile_expert, n_valid.reshape(1)


def _moe_sparse_kernel(te_ref, src_ref, nv_ref, hn_hbm, wg_ref, wu_ref, wd_ref, o_ref,
                       xbuf, sem, *, tm):
    i = pl.program_id(0)
    n_valid = nv_ref[0]

    def gather(tile, slot):
        def body(r, carry):
            tok = src_ref[tile * tm + r]
            pltpu.make_async_copy(hn_hbm.at[pl.ds(tok, 1), :],
                                  xbuf.at[slot, pl.ds(r, 1), :], sem.at[slot]).start()
            return carry
        lax.fori_loop(0, tm, body, 0, unroll=8)

    @pl.when(i == 0)
    def _():
        gather(0, 0)

    @pl.when(i + 1 < n_valid)
    def _():
        gather(i + 1, (i + 1) % 2)

    @pl.when(i < n_valid)
    def _():
        slot = i % 2
        pltpu.make_async_copy(hn_hbm.at[pl.ds(0, tm), :], xbuf.at[slot], sem.at[slot]).wait()
        x = xbuf[slot].astype(bf16)
        gate = jnp.dot(x, wg_ref[0], preferred_element_type=f32)
        up = jnp.dot(x, wu_ref[0], preferred_element_type=f32)
        hid = gate * _sigmoid(gate) * up
        o_ref[...] = jnp.dot(hid.astype(bf16), wd_ref[0], preferred_element_type=f32)

    @pl.when(i >= n_valid)
    def _():
        o_ref[...] = jnp.zeros_like(o_ref)


def moe_sparse(hn, tile_expert, src, n_valid, wg, wu, wd, *, tm, n_tiles):
    T, D = hn.shape
    grid_spec = pltpu.PrefetchScalarGridSpec(
        num_scalar_prefetch=3,
        grid=(n_tiles,),
        in_specs=[
            pl.BlockSpec(memory_space=pl.ANY),
            pl.BlockSpec((1, D, MOE_FF), lambda i, te, src, nv: (te[i], 0, 0)),
            pl.BlockSpec((1, D, MOE_FF), lambda i, te, src, nv: (te[i], 0, 0)),
            pl.BlockSpec((1, MOE_FF, D), lambda i, te, src, nv: (te[i], 0, 0)),
        ],
        out_specs=pl.BlockSpec((tm, D), lambda i, te, src, nv: (i, 0)),
        scratch_shapes=[pltpu.VMEM((2, tm, D), f32), pltpu.SemaphoreType.DMA((2,))],
    )
    return pl.pallas_call(
        functools.partial(_moe_sparse_kernel, tm=tm),
        out_shape=jax.ShapeDtypeStruct((n_tiles * tm, D), f32),
        grid_spec=grid_spec,
        compiler_params=_cparams(("arbitrary",)),
        name="moe_sparse",
    )(tile_expert, src, n_valid, hn, wg, wu, wd)


def _moe_combine_kernel(pos_ref, h_ref, sel_ref, y_hbm, gf_ref, o_ref, ybuf, sem, *, tm):
    i = pl.program_id(0)
    n = pl.num_programs(0)

    def gather(tile, slot):
        def body(r, carry):
            for c in range(2):
                p = pos_ref[(tile * tm + r) * 2 + c]
                pltpu.make_async_copy(y_hbm.at[pl.ds(p, 1), :],
                                      ybuf.at[slot, c, pl.ds(r, 1), :], sem.at[slot]).start()
            return carry
        lax.fori_loop(0, tm, body, 0, unroll=4)

    @pl.when(i == 0)
    def _():
        gather(0, 0)

    @pl.when(i + 1 < n)
    def _():
        gather(i + 1, (i + 1) % 2)

    slot = i % 2
    for c in range(2):
        pltpu.make_async_copy(y_hbm.at[pl.ds(0, tm), :], ybuf.at[slot, c], sem.at[slot]).wait()
    w = sel_ref[...]
    y = h_ref[...] + w[:, 2:3] * ybuf[slot, 0] + w[:, 3:4] * ybuf[slot, 1]
    o_ref[...] = _rmsnorm_rows(y, gf_ref[...])


def moe_combine(h, sel, y_sorted, pos, g_final, *, tm):
    T, D = h.shape
    grid_spec = pltpu.PrefetchScalarGridSpec(
        num_scalar_prefetch=1,
        grid=(T // tm,),
        in_specs=[
            pl.BlockSpec((tm, D), lambda i, pos: (i, 0)),
            pl.BlockSpec((tm, ROUTER_LANES), lambda i, pos: (i, 0)),
            pl.BlockSpec(memory_space=pl.ANY),
            pl.BlockSpec((1, D), lambda i, pos: (0, 0)),
        ],
        out_specs=pl.BlockSpec((tm, D), lambda i, pos: (i, 0)),
        scratch_shapes=[pltpu.VMEM((2, 2, tm, D), f32), pltpu.SemaphoreType.DMA((2,))],
    )
    return pl.pallas_call(
        functools.partial(_moe_combine_kernel, tm=tm),
        out_shape=jax.ShapeDtypeStruct((T, D), f32),
        grid_spec=grid_spec,
        compiler_params=_cparams(("arbitrary",)),
        name="moe_combine",
    )(pos, h, sel, y_sorted, g_final.reshape(1, D))


def _pad_rows(w, rows):
    return jnp.pad(w, ((0, rows - w.shape[0]), (0, 0)))


def _pack_lora_cols(t):
    o1, o2 = DECAY_LORA, DECAY_LORA + AAA_LORA
    pad = lambda a, n: jnp.pad(a, [(0, 0)] * (a.ndim - 1) + [(0, n - a.shape[-1])])
    return jnp.concatenate([pad(t[..., :o1], 128), pad(t[..., o1:o2], 128),
                            pad(t[..., o2:], 256)], axis=-1)


def _layer(x2, mem2, p, *, batch, seq, cfg):
    D = D_MODEL
    w_in = p['w_in']
    w_in_p = jnp.concatenate([w_in[:, :MAIN_COLS], _pack_lora_cols(w_in[:, MAIN_COLS:])],
                             axis=1).astype(bf16)
    mu = p['rwkv_mu']
    mu_main = mu[:3 * RWKV_WIDTH]
    mu_lora = _pack_lora_cols(mu[3 * RWKV_WIDTH:])

    proj = norm_matmul_resident(x2, p['norm_mix_g'], w_in_p, tm=cfg['tm'], tn=cfg['tn_in'],
                                name="in_proj")
    pool_out = pool_mixer(proj, p['pool_w'].astype(bf16), p['pool_scale'],
                          batch=batch, seq=seq, ts=cfg['ts_pool'])
    r, lw, k, v, kk, kka, g = rwkv_prep(
        proj, mu_main, mu_lora, p['rwkv_w0'], _pad_rows(p['rwkv_w2'], 128).astype(bf16),
        p['rwkv_a0'], _pad_rows(p['rwkv_a2'], 128).astype(bf16),
        _pad_rows(p['rwkv_g2'], 256).astype(bf16), p['rwkv_k_k'], p['rwkv_k_a'],
        batch=batch, seq=seq, ts=cfg['ts_prep'])
    rwkv_out = rwkv_scan(r, lw, k, v, kk, kka, g, p['rwkv_ln_w'], p['rwkv_ln_b'], p['rwkv_r_k'],
                         batch=batch, seq=seq, rows=cfg['scan_rows'], n_hg=cfg['scan_hg'])
    w_out = p['w_out'].astype(bf16)
    h1 = matmul_residual_resident([pool_out, rwkv_out], [w_out[:POOL_WIDTH], w_out[POOL_WIDTH:]],
                                  x2, tm=cfg['tm'], tn=cfg['tn'], name="out_proj")

    q = norm_matmul_resident(h1, p['norm_xattn_g'], p['xattn_w_q'].astype(bf16),
                             tm=cfg['tm'], tn=cfg['tn'], name="q_proj")
    kv = norm_matmul(mem2, p['norm_mem_g'], p['xattn_w_kv'].astype(bf16),
                     tm=min(cfg['tm'], mem2.shape[0]), tn=cfg['tn'], name="kv_proj")
    o = xattn_core(q, kv, batch=batch, seq=seq, ts=cfg['ts_attn'])

    w_router = jnp.pad(jnp.concatenate([p['moe_w_group'], p['moe_w_expert']], axis=1),
                       ((0, 0), (0, ROUTER_LANES - MOE_GROUPS - MOE_EXPERTS)))
    b_router = jnp.pad(jnp.concatenate([p['moe_b_group'], p['moe_b_expert']]),
                       (0, ROUTER_LANES - MOE_GROUPS - MOE_EXPERTS)).reshape(1, ROUTER_LANES)
    h2, hn3, sel = oproj_router(o, p['xattn_w_o'].astype(bf16), h1, p['norm_ffn_g'],
                                w_router, b_router, tm=cfg['tm'], tn=cfg['tn'])
    tm_e = cfg['tm_moe']
    n_tiles = (2 * hn3.shape[0]) // tm_e + MOE_EXPERTS
    pos, src, tile_expert, n_valid = _route_metadata(sel, tm=tm_e, n_tiles=n_tiles)
    y_sorted = moe_sparse(hn3, tile_expert, src, n_valid,
                          p['moe_w_gate'].astype(bf16), p['moe_w_up'].astype(bf16),
                          p['moe_w_down'].astype(bf16), tm=tm_e, n_tiles=n_tiles)
    return moe_combine(h2, sel, y_sorted, pos, p['norm_final_g'], tm=cfg['tm_comb'])


_CFG = dict(tm=512, tn=1024, tn_in=1536, ts_pool=512, ts_prep=256, scan_rows=256, scan_hg=4,
            ts_attn=1024, tm_moe=512, tm_comb=256)


def kernel(x, mem, norm_mix_g, w_in, pool_w, pool_scale, rwkv_mu, rwkv_w0, rwkv_w2, rwkv_a0,
           rwkv_a2, rwkv_g2, rwkv_k_k, rwkv_k_a, rwkv_r_k, rwkv_ln_w, rwkv_ln_b, w_out,
           norm_xattn_g, norm_mem_g, xattn_w_q, xattn_w_kv, xattn_w_o, norm_ffn_g,
           moe_w_group, moe_b_group, moe_w_expert, moe_b_expert, moe_w_gate, moe_w_up,
           moe_w_down, norm_final_g):
    batch, seq, D = x.shape
    p = dict(norm_mix_g=norm_mix_g[0], w_in=w_in[0], pool_w=pool_w[0], pool_scale=pool_scale[0],
             rwkv_mu=rwkv_mu[0], rwkv_w0=rwkv_w0[0], rwkv_w2=rwkv_w2[0], rwkv_a0=rwkv_a0[0],
             rwkv_a2=rwkv_a2[0], rwkv_g2=rwkv_g2[0], rwkv_k_k=rwkv_k_k[0], rwkv_k_a=rwkv_k_a[0],
             rwkv_r_k=rwkv_r_k[0], rwkv_ln_w=rwkv_ln_w[0], rwkv_ln_b=rwkv_ln_b[0], w_out=w_out[0],
             norm_xattn_g=norm_xattn_g[0], norm_mem_g=norm_mem_g[0], xattn_w_q=xattn_w_q[0],
             xattn_w_kv=xattn_w_kv[0], xattn_w_o=xattn_w_o[0], norm_ffn_g=norm_ffn_g[0],
             moe_w_group=moe_w_group[0], moe_b_group=moe_b_group[0], moe_w_expert=moe_w_expert[0],
             moe_b_expert=moe_b_expert[0], moe_w_gate=moe_w_gate[0], moe_w_up=moe_w_up[0],
             moe_w_down=moe_w_down[0], norm_final_g=norm_final_g)
    out = _layer(x.reshape(batch * seq, D), mem.reshape(batch * MEM_LEN, D), p,
                 batch=batch, seq=seq, cfg=_CFG)
    return out.reshape(batch, seq, D)
```

```python
import functools

import jax
import jax.numpy as jnp
from jax import lax
from jax.experimental import pallas as pl
from jax.experimental.pallas import tpu as pltpu

f32 = jnp.float32
bf16 = jnp.bfloat16

D_MODEL = 2048
MEM_LEN = 256
NORM_EPS = 1e-6

POOL_WIDTH = 1024
POOL_WINDOWS = (2, 4, 8, 16)
POOL_GROUP = 256
POOL_HALO = 16
RWKV_WIDTH = 1024
RWKV_HEAD = 64
RWKV_HEADS = 16
GN_EPS = 64e-5
DECAY_LORA = 64
AAA_LORA = 64
GATE_LORA = 160
LORA_PAD = 512
LORA_W_OFF, LORA_A_OFF, LORA_G_OFF = 0, 128, 256
MAIN_COLS = POOL_WIDTH + 3 * RWKV_WIDTH
PROJ_COLS = MAIN_COLS + LORA_PAD

RWKV_CHUNK = 64
HEADS_PER_STEP = 4
HG_LANES = HEADS_PER_STEP * RWKV_HEAD

XATTN_HEADS = 4
XATTN_HEAD_DIM = 512

MOE_GROUPS = 4
MOE_EXPERTS_PER_GROUP = 4
MOE_EXPERTS = 16
MOE_FF = 512
ROUTER_LANES = 128

VMEM_LIMIT = 56 * 1024 * 1024


def _cparams(sem):
    return pltpu.CompilerParams(dimension_semantics=sem, vmem_limit_bytes=VMEM_LIMIT)


def _split_dot(x, w_bf16):
    hi = x.astype(bf16)
    lo = (x - hi.astype(f32)).astype(bf16)
    return (jnp.dot(hi, w_bf16, preferred_element_type=f32)
            + jnp.dot(lo, w_bf16, preferred_element_type=f32))


def _norm_mm_kernel(x_ref, g_ref, w_ref, o_ref, xn_ref):
    @pl.when(pl.program_id(1) == 0)
    def _():
        x = x_ref[...]
        ms = jnp.mean(x * x, axis=-1, keepdims=True)
        xn_ref[...] = (x * lax.rsqrt(ms + NORM_EPS) * g_ref[...]).astype(bf16)

    o_ref[...] = jnp.dot(xn_ref[...], w_ref[...], preferred_element_type=f32).astype(o_ref.dtype)


def norm_matmul(x, g, w, *, tm, tn, out_dtype=bf16, name="norm_matmul"):
    M, K = x.shape
    N = w.shape[1]
    return pl.pallas_call(
        _norm_mm_kernel,
        out_shape=jax.ShapeDtypeStruct((M, N), out_dtype),
        grid=(M // tm, N // tn),
        in_specs=[
            pl.BlockSpec((tm, K), lambda i, j: (i, 0)),
            pl.BlockSpec((1, K), lambda i, j: (0, 0)),
            pl.BlockSpec((K, tn), lambda i, j: (0, j)),
        ],
        out_specs=pl.BlockSpec((tm, tn), lambda i, j: (i, j)),
        scratch_shapes=[pltpu.VMEM((tm, K), bf16)],
        compiler_params=_cparams(("parallel", "arbitrary")),
        name=name,
    )(x, g.reshape(1, K), w)


def _resident(shape):
    return pl.BlockSpec(shape, lambda i: (0,) * len(shape), pipeline_mode=pl.Buffered(1))


def _rmsnorm_rows(x, g):
    ms = jnp.mean(x * x, axis=-1, keepdims=True)
    return x * lax.rsqrt(ms + NORM_EPS) * g


def _norm_mm_res_kernel(x_ref, g_ref, w_ref, o_ref, *, tn):
    xn = _rmsnorm_rows(x_ref[...], g_ref[...]).astype(bf16)
    for n0 in range(0, o_ref.shape[1], tn):
        o_ref[:, n0:n0 + tn] = jnp.dot(xn, w_ref[:, n0:n0 + tn],
                                       preferred_element_type=f32).astype(o_ref.dtype)


def norm_matmul_resident(x, g, w, *, tm, tn, name):
    M, K = x.shape
    N = w.shape[1]
    return pl.pallas_call(
        functools.partial(_norm_mm_res_kernel, tn=tn),
        out_shape=jax.ShapeDtypeStruct((M, N), bf16),
        grid=(M // tm,),
        in_specs=[pl.BlockSpec((tm, K), lambda i: (i, 0)), _resident((1, K)), _resident((K, N))],
        out_specs=pl.BlockSpec((tm, N), lambda i: (i, 0)),
        compiler_params=_cparams(("parallel",)),
        name=name,
    )(x, g.reshape(1, K), w)


def _mm_res_resident_kernel(*refs, n, tn):
    a_refs, w_refs, res_ref, o_ref = refs[:n], refs[n:2 * n], refs[2 * n], refs[2 * n + 1]
    for n0 in range(0, o_ref.shape[1], tn):
        acc = res_ref[:, n0:n0 + tn]
        for a_ref, w_ref in zip(a_refs, w_refs):
            acc = acc + jnp.dot(a_ref[...], w_ref[:, n0:n0 + tn], preferred_element_type=f32)
        o_ref[:, n0:n0 + tn] = acc


def matmul_residual_resident(a_list, w_list, res, *, tm, tn, name):
    M, N = res.shape
    n = len(a_list)
    return pl.pallas_call(
        functools.partial(_mm_res_resident_kernel, n=n, tn=tn),
        out_shape=jax.ShapeDtypeStruct((M, N), f32),
        grid=(M // tm,),
        in_specs=([pl.BlockSpec((tm, a.shape[1]), lambda i: (i, 0)) for a in a_list]
                  + [_resident(w.shape) for w in w_list]
                  + [pl.BlockSpec((tm, N), lambda i: (i, 0))]),
        out_specs=pl.BlockSpec((tm, N), lambda i: (i, 0)),
        compiler_params=_cparams(("parallel",)),
        name=name,
    )(*a_list, *w_list, res)


def _pool_kernel(u_ref, halo_ref, pw_ref, ps_ref, o_ref, *, ts):
    s = pl.program_id(1)
    u = u_ref[...].astype(f32)
    halo = jnp.where(s > 0, halo_ref[...].astype(f32), 0.0)
    ext = jnp.concatenate([halo, u], axis=0)
    pos = s * ts + lax.broadcasted_iota(jnp.int32, (ts, 1), 0)
    acc = ext
    shift = 1
    for gi, w in enumerate(POOL_WINDOWS):
        while shift < w:
            acc = acc + pltpu.roll(acc, shift, 0)
            shift *= 2
        lo, hi = gi * POOL_GROUP, (gi + 1) * POOL_GROUP
        cnt = jnp.minimum(pos + 1, w).astype(f32)
        pooled = acc[POOL_HALO:, lo:hi] / cnt - u[:, lo:hi]
        mixed = jnp.dot(pooled.astype(bf16), pw_ref[gi], preferred_element_type=f32)
        o_ref[:, lo:hi] = (mixed * ps_ref[:, lo:hi]).astype(o_ref.dtype)


def pool_mixer(proj, pool_w, pool_scale, *, batch, seq, ts):
    ns = seq // ts
    hb = ts // POOL_HALO
    return pl.pallas_call(
        functools.partial(_pool_kernel, ts=ts),
        out_shape=jax.ShapeDtypeStruct((batch * seq, POOL_WIDTH), bf16),
        grid=(batch, ns),
        in_specs=[
            pl.BlockSpec((ts, POOL_WIDTH), lambda b, s: (b * ns + s, 0)),
            pl.BlockSpec((POOL_HALO, POOL_WIDTH),
                         lambda b, s: (jnp.maximum((b * ns + s) * hb - 1, 0), 0)),
            pl.BlockSpec((len(POOL_WINDOWS), POOL_GROUP, POOL_GROUP), lambda b, s: (0, 0, 0)),
            pl.BlockSpec((1, POOL_WIDTH), lambda b, s: (0, 0)),
        ],
        out_specs=pl.BlockSpec((ts, POOL_WIDTH), lambda b, s: (b * ns + s, 0)),
        compiler_params=_cparams(("parallel", "arbitrary")),
        name="pool_mixer",
    )(proj, proj, pool_w, pool_scale.reshape(1, POOL_WIDTH))


def _head_ones():
    r = lax.broadcasted_iota(jnp.int32, (HG_LANES, HG_LANES), 0) // RWKV_HEAD
    c = lax.broadcasted_iota(jnp.int32, (HG_LANES, HG_LANES), 1) // RWKV_HEAD
    return jnp.where(r == c, 1.0, 0.0).astype(bf16)


def _head_sum(x, ones_bd):
    parts = [_split_dot(x[:, c:c + HG_LANES], ones_bd) for c in range(0, x.shape[1], HG_LANES)]
    return parts[0] if len(parts) == 1 else jnp.concatenate(parts, axis=1)


def _sigmoid(x):
    return 1.0 / (1.0 + jnp.exp(-x))


def _rwkv_prep_kernel(pr_ref, pk_ref, pv_ref, pl_ref, hr_ref, hk_ref, hv_ref, hl_ref,
                      mu_ref, mul_ref, w0_ref, w2_ref, a0_ref, a2_ref, g2_ref, kk_ref, ka_ref,
                      r_out, lw_out, k_out, v_out, kk_out, kka_out, g_out, *, ts):
    s = pl.program_id(1)
    row0 = lax.broadcasted_iota(jnp.int32, (ts, 1), 0) == 0
    C = RWKV_WIDTH

    def shifted(p_ref, h_ref, mu):
        p = p_ref[...].astype(f32)
        last = h_ref[...].astype(f32)[POOL_HALO - 1:POOL_HALO, :]
        last = jnp.where(s > 0, last, 0.0)
        prev = jnp.where(row0, last, pltpu.roll(p, 1, 0))
        return p + (prev - p) * mu

    r = shifted(pr_ref, hr_ref, mu_ref[:, 0:C])
    k = shifted(pk_ref, hk_ref, mu_ref[:, C:2 * C])
    v = shifted(pv_ref, hv_ref, mu_ref[:, 2 * C:3 * C])
    lo = shifted(pl_ref, hl_ref, mul_ref[...])
    w_lo = lo[:, LORA_W_OFF:LORA_W_OFF + 128]
    a_lo = lo[:, LORA_A_OFF:LORA_A_OFF + 128]
    g_lo = lo[:, LORA_G_OFF:LORA_G_OFF + 256]

    wx = w0_ref[...] + jnp.dot(jnp.tanh(w_lo).astype(bf16), w2_ref[...], preferred_element_type=f32)
    nx = -wx
    softplus = jnp.maximum(nx, 0.0) + jnp.log(1.0 + jnp.exp(-jnp.abs(nx)))
    w_log = -softplus - 0.5
    lw_out[...] = -jnp.exp(w_log)
    a = _sigmoid(a0_ref[...] + jnp.dot(a_lo.astype(bf16), a2_ref[...], preferred_element_type=f32))
    g = jnp.dot(_sigmoid(g_lo).astype(bf16), g2_ref[...], preferred_element_type=f32)

    ones_bd = _head_ones()
    kk = k * kk_ref[...]
    kk = kk * lax.rsqrt(jnp.maximum(_head_sum(kk * kk, ones_bd), 1e-24))
    k = k * (1.0 + (a - 1.0) * ka_ref[...])

    r_out[...] = r.astype(r_out.dtype)
    k_out[...] = k.astype(k_out.dtype)
    v_out[...] = v.astype(v_out.dtype)
    kk_out[...] = kk.astype(kk_out.dtype)
    kka_out[...] = (kk * a).astype(kka_out.dtype)
    g_out[...] = g.astype(g_out.dtype)


def rwkv_prep(proj, mu_main, mu_lora, w0, w2p, a0, a2p, g2p, k_k, k_a, *, batch, seq, ts):
    ns = seq // ts
    hb = ts // POOL_HALO
    C = RWKV_WIDTH
    T = batch * seq
    lora_blk = MAIN_COLS // LORA_PAD

    def main_spec(col):
        return pl.BlockSpec((ts, C), lambda b, s: (b * ns + s, col))

    def halo_spec(width, col):
        return pl.BlockSpec((POOL_HALO, width),
                            lambda b, s: (jnp.maximum((b * ns + s) * hb - 1, 0), col))

    def full(shape):
        return pl.BlockSpec(shape, lambda b, s: (0,) * len(shape))

    row = lambda t: t.reshape(1, -1)
    out_sd = lambda dt: jax.ShapeDtypeStruct((T, C), dt)
    out_spec = pl.BlockSpec((ts, C), lambda b, s: (b * ns + s, 0))
    return pl.pallas_call(
        functools.partial(_rwkv_prep_kernel, ts=ts),
        out_shape=[out_sd(bf16), out_sd(f32), out_sd(bf16), out_sd(bf16),
                   out_sd(bf16), out_sd(bf16), out_sd(bf16)],
        grid=(batch, ns),
        in_specs=[
            main_spec(1), main_spec(2), main_spec(3),
            pl.BlockSpec((ts, LORA_PAD), lambda b, s: (b * ns + s, lora_blk)),
            halo_spec(C, 1), halo_spec(C, 2), halo_spec(C, 3), halo_spec(LORA_PAD, lora_blk),
            full((1, 3 * C)), full((1, LORA_PAD)), full((1, C)), full((128, C)),
            full((1, C)), full((128, C)), full((256, C)), full((1, C)), full((1, C)),
        ],
        out_specs=[out_spec] * 7,
        compiler_params=_cparams(("parallel", "arbitrary")),
        name="rwkv_prep",
    )(proj, proj, proj, proj, proj, proj, proj, proj,
      row(mu_main), row(mu_lora), row(w0), w2p, row(a0), a2p, g2p, row(k_k), row(k_a))


def _bd_mask():
    r = lax.broadcasted_iota(jnp.int32, (HG_LANES, HG_LANES), 0) // RWKV_CHUNK
    c = lax.broadcasted_iota(jnp.int32, (HG_LANES, HG_LANES), 1) // RWKV_HEAD
    return r == c


def _rwkv_scan_kernel(r_ref, lw_ref, k_ref, v_ref, kk_ref, kka_ref, g_ref,
                      lnw_ref, lnb_ref, rk_ref, o_ref, state_ref, *, n_chunks, n_hg):
    L = RWKV_CHUNK
    W = HG_LANES

    @pl.when(pl.program_id(2) == 0)
    def _():
        state_ref[...] = jnp.zeros_like(state_ref)

    bd_mask = _bd_mask()
    ones_bd = _head_ones()

    def bd(x):
        tiled = jnp.concatenate([x] * HEADS_PER_STEP, axis=0)
        return jnp.where(bd_mask, tiled, 0.0).astype(bf16)

    def mm(a, b_bf16):
        return jnp.dot(a.astype(bf16), b_bf16, preferred_element_type=f32)

    def mm_nt(a, b_bf16):
        return lax.dot_general(a.astype(bf16), b_bf16, (((1,), (1,)), ((), ())),
                               preferred_element_type=f32)

    def mm_tn(a_bf16, b_bf16):
        return lax.dot_general(a_bf16, b_bf16, (((0,), (0,)), ((), ())),
                               preferred_element_type=f32)

    def split(x):
        hi = x.astype(bf16)
        return hi, (x - hi.astype(f32)).astype(bf16)

    t_idx = lax.broadcasted_iota(jnp.int32, (L, W), 0)
    s_idx = lax.broadcasted_iota(jnp.int32, (L, W), 1) % L
    strict = t_idx > s_idx
    incl = t_idx >= s_idx
    eye_all = jnp.where(t_idx == s_idx, 1.0, 0.0)
    tri = jnp.where(lax.broadcasted_iota(jnp.int32, (L, L), 0)
                    >= lax.broadcasted_iota(jnp.int32, (L, L), 1), 1.0, 0.0).astype(bf16)
    blockdiag = (lax.broadcasted_iota(jnp.int32, (W, W), 0) // RWKV_HEAD
                 == lax.broadcasted_iota(jnp.int32, (W, W), 1) // RWKV_HEAD)

    def phase1(items):
        I = range(len(items))
        rows = [pl.ds(c * L, L) for _, c in items]
        cols = [pl.ds(h * W, W) for h, _ in items]
        ld = lambda ref, i: ref[rows[i], cols[i]]
        r = [ld(r_ref, i).astype(f32) for i in I]
        k = [ld(k_ref, i).astype(f32) for i in I]
        v = [ld(v_ref, i).astype(f32) for i in I]
        lw = [ld(lw_ref, i) for i in I]
        lw_s = [split(x) for x in lw]
        cs = [jnp.dot(tri, hi, preferred_element_type=f32) + jnp.dot(tri, lo, preferred_element_type=f32)
              for hi, lo in lw_s]
        yield
        g_inv = [jnp.exp(-x) for x in cs]
        to_end = [jnp.exp(x[L - 1:L, :] - x) for x in cs]
        a_t = [-ld(kk_ref, i).astype(f32) * jnp.exp(cs[i] - lw[i]) for i in I]
        r_t = [r[i] * jnp.exp(cs[i]) for i in I]
        kka = [ld(kka_ref, i).astype(f32) for i in I]
        b_t = [kka[i] * g_inv[i] for i in I]
        k_t = [k[i] * g_inv[i] for i in I]
        b_e = [kka[i] * to_end[i] for i in I]
        k_e = [k[i] * to_end[i] for i in I]
        v_bd = [bd(x) for x in v]
        ar = [jnp.concatenate([a_t[i], r_t[i]], axis=0) for i in I]
        m_b = [mm_nt(ar[i], bd(b_t[i])) for i in I]
        m_k = [mm_nt(ar[i], bd(k_t[i])) for i in I]
        m_ab = [jnp.where(strict, x[:L], 0.0) for x in m_b]
        m_rb = [jnp.where(incl, x[L:], 0.0) for x in m_b]
        m_ak = [jnp.where(strict, x[:L], 0.0) for x in m_k]
        m_rk = [jnp.where(incl, x[L:], 0.0) for x in m_k]
        bonus = [mm(r[i] * k[i] * rk_ref[:, cols[i]], ones_bd) * v[i] for i in I]
        yield
        p = m_ab
        tinv = [eye_all + x for x in p]
        m = 2
        while m < L:
            p = [mm(x, bd(x)) for x in p]
            tinv = [tinv[i] + mm(tinv[i], bd(p[i])) for i in I]
            m *= 2
            yield
        x1 = [mm(m_ak[i], v_bd[i]) for i in I]
        u_v = [mm(tinv[i], bd(x1[i])) for i in I]
        a_h = [mm(tinv[i], bd(a_t[i])) for i in I]
        yield
        r_h = [r_t[i] + mm(m_rb[i], bd(a_h[i])) for i in I]
        y_v = [mm(m_rb[i], bd(u_v[i])) + mm(m_rk[i], v_bd[i]) for i in I]
        yield
        p_t = [jnp.where(blockdiag, mm_tn(a_h[i].astype(bf16), b_e[i].astype(bf16)), 0.0).astype(bf16)
               for i in I]
        g_t = [jnp.where(blockdiag,
                         mm_tn(jnp.concatenate([u_v[i], v[i]], axis=0).astype(bf16),
                               jnp.concatenate([b_e[i], k_e[i]], axis=0).astype(bf16)), 0.0) for i in I]
        decay = [jnp.exp(x[L - 1:L, :]) for x in cs]
        for i in I:
            done.append(dict(hg=items[i][0], rows=rows[i], cols=cols[i], r_h=r_h[i], y_v=y_v[i],
                             p_t=p_t[i], g_t=g_t[i], decay=decay[i], bonus=bonus[i]))
        yield

    def chain_step(ch):
        state = states[ch['hg']]
        state_b = state.astype(bf16)
        y = mm_nt(ch['r_h'], state_b) + ch['y_v']
        states[ch['hg']] = (state * ch['decay']
                            + jnp.dot(state_b, ch['p_t'], preferred_element_type=f32) + ch['g_t'])
        mean = mm(y, ones_bd) * (1.0 / RWKV_HEAD)
        yc = y - mean
        var = mm(yc * yc, ones_bd) * (1.0 / RWKV_HEAD)
        rw, cl = ch['rows'], ch['cols']
        yn = yc * lax.rsqrt(var + GN_EPS) * lnw_ref[:, cl] + lnb_ref[:, cl]
        o_ref[rw, cl] = ((yn + ch['bonus']) * g_ref[rw, cl].astype(f32)).astype(o_ref.dtype)

    states = [state_ref[h] for h in range(n_hg)]
    done = []
    half = max(n_chunks // 2, 1)
    groups = [[(h, c) for c in range(0, half) for h in range(n_hg)],
              [(h, c) for c in range(half, n_chunks) for h in range(n_hg)]]
    for _ in phase1(groups[0]):
        pass
    n_chained = 0
    for _ in phase1(groups[1]):
        if n_chained < len(groups[0]):
            chain_step(done[n_chained])
            n_chained += 1
    while n_chained < len(done):
        chain_step(done[n_chained])
        n_chained += 1
    for h in range(n_hg):
        state_ref[h] = states[h]


def rwkv_scan(r, lw, k, v, kk, kka, g, ln_w, ln_b, r_k, *, batch, seq, rows, n_hg):
    nb = seq // rows
    T = batch * seq
    width = n_hg * HG_LANES
    n_col = RWKV_WIDTH // width
    blk = pl.BlockSpec((rows, width), lambda b, h, c: (b * nb + c, h))
    par = pl.BlockSpec((1, width), lambda b, h, c: (0, h))
    return pl.pallas_call(
        functools.partial(_rwkv_scan_kernel, n_chunks=rows // RWKV_CHUNK, n_hg=n_hg),
        out_shape=jax.ShapeDtypeStruct((T, RWKV_WIDTH), bf16),
        grid=(batch, n_col, nb),
        in_specs=[blk] * 7 + [par] * 3,
        out_specs=blk,
        scratch_shapes=[pltpu.VMEM((n_hg, HG_LANES, HG_LANES), f32)],
        compiler_params=_cparams(("parallel", "parallel", "arbitrary")),
        name="rwkv_scan",
    )(r, lw, k, v, kk, kka, g, ln_w.reshape(1, -1), ln_b.reshape(1, -1), r_k.reshape(1, -1))


def _xattn_kernel(q_ref, k_ref, v_ref, o_ref):
    s = lax.dot_general(q_ref[...], k_ref[...], (((1,), (1,)), ((), ())),
                        preferred_element_type=f32) * (XATTN_HEAD_DIM ** -0.5)
    m = jnp.max(s, axis=-1, keepdims=True)
    e = jnp.exp(s - m)
    l = jnp.sum(e, axis=-1, keepdims=True)
    o = jnp.dot(e.astype(bf16), v_ref[...], preferred_element_type=f32)
    o_ref[...] = (o / l).astype(o_ref.dtype)


def xattn_core(q, kv, *, batch, seq, ts):
    ns = seq // ts
    hd = XATTN_HEAD_DIM
    return pl.pallas_call(
        _xattn_kernel,
        out_shape=jax.ShapeDtypeStruct(q.shape, bf16),
        grid=(batch, ns, XATTN_HEADS),
        in_specs=[
            pl.BlockSpec((ts, hd), lambda b, s, h: (b * ns + s, h)),
            pl.BlockSpec((MEM_LEN, hd), lambda b, s, h: (b, h)),
            pl.BlockSpec((MEM_LEN, hd), lambda b, s, h: (b, XATTN_HEADS + h)),
        ],
        out_specs=pl.BlockSpec((ts, hd), lambda b, s, h: (b * ns + s, h)),
        compiler_params=_cparams(("parallel", "parallel", "arbitrary")),
        name="xattn_core",
    )(q, kv, kv)


def _route(hn, w, b):
    w_hi = w.astype(bf16)
    w_lo = (w - w_hi.astype(f32)).astype(bf16)
    h_hi = hn.astype(bf16)
    logits = (_split_dot(hn, w_hi) + jnp.dot(h_hi, w_lo, preferred_element_type=f32)) + b

    lane_i = lax.broadcasted_iota(jnp.int32, logits.shape, 1)
    lane = lane_i.astype(f32)
    neg = -jnp.inf
    big = float(ROUTER_LANES)
    is_g = lane_i < MOE_GROUPS
    gl = jnp.where(is_g, logits, neg)
    gmax = jnp.max(gl, axis=-1, keepdims=True)
    g_idx = jnp.min(jnp.where(gl == gmax, lane, big), axis=-1, keepdims=True)
    g_w = 1.0 / jnp.sum(jnp.exp(gl - gmax), axis=-1, keepdims=True)

    lo_lane = MOE_GROUPS + g_idx * MOE_EXPERTS_PER_GROUP
    sel = (lane >= lo_lane) & (lane < lo_lane + MOE_EXPERTS_PER_GROUP)
    el = jnp.where(sel, logits, neg)
    v1 = jnp.max(el, axis=-1, keepdims=True)
    i1 = jnp.min(jnp.where(el == v1, lane, big), axis=-1, keepdims=True)
    el2 = jnp.where(lane == i1, neg, el)
    v2 = jnp.max(el2, axis=-1, keepdims=True)
    i2 = jnp.min(jnp.where(el2 == v2, lane, big), axis=-1, keepdims=True)
    e2 = jnp.exp(v2 - v1)
    p1 = 1.0 / (1.0 + e2)
    p2 = e2 / (1.0 + e2)
    return (jnp.where(lane_i == 0, i1 - MOE_GROUPS, 0.0)
            + jnp.where(lane_i == 1, i2 - MOE_GROUPS, 0.0)
            + jnp.where(lane_i == 2, p1 * g_w, 0.0)
            + jnp.where(lane_i == 3, p2 * g_w, 0.0))


def _oproj_router_kernel(o_ref, wo_ref, res_ref, g_ref, wr_ref, br_ref, h_ref, xr_ref, *, tn):
    for n0 in range(0, h_ref.shape[1], tn):
        h_ref[:, n0:n0 + tn] = res_ref[:, n0:n0 + tn] + jnp.dot(
            o_ref[...], wo_ref[:, n0:n0 + tn], preferred_element_type=f32)
    hn = _rmsnorm_rows(h_ref[...], g_ref[...])
    D = hn.shape[1]
    xr_ref[:, :D] = hn
    xr_ref[:, D:] = _route(hn, wr_ref[...], br_ref[...])


def oproj_router(o, w_o, res, g, w_router, b_router, *, tm, tn):
    T, D = res.shape
    row = lambda width: pl.BlockSpec((tm, width), lambda i: (i, 0))
    return pl.pallas_call(
        functools.partial(_oproj_router_kernel, tn=tn),
        out_shape=[jax.ShapeDtypeStruct((T, D), f32),
                   jax.ShapeDtypeStruct((T, D + ROUTER_LANES), f32)],
        grid=(T // tm,),
        in_specs=[row(D), _resident((D, D)), row(D), _resident((1, D)),
                  _resident((D, ROUTER_LANES)), _resident((1, ROUTER_LANES))],
        out_specs=[row(D), row(D + ROUTER_LANES)],
        compiler_params=_cparams(("parallel",)),
        name="oproj_router",
    )(o, w_o, res, g.reshape(1, D), w_router, b_router)


MOE_PAIRS = ((0, 1), (0, 2), (0, 3), (1, 2), (1, 3), (2, 3))
MOE_CLASSES = MOE_GROUPS * len(MOE_PAIRS)


def _route_metadata(rec, *, tm, n_tiles):
    ids = rec[:, 0:2].astype(jnp.int32)
    lo = jnp.minimum(ids[:, 0], ids[:, 1])
    hi = jnp.maximum(ids[:, 0], ids[:, 1])
    a, b = lo % MOE_EXPERTS_PER_GROUP, hi % MOE_EXPERTS_PER_GROUP
    cls = (lo // MOE_EXPERTS_PER_GROUP) * len(MOE_PAIRS) + (a * (7 - a)) // 2 + b - a - 1
    onehot = (cls[:, None] == jnp.arange(MOE_CLASSES, dtype=jnp.int32)[None, :]).astype(jnp.int32)
    csum = jnp.cumsum(onehot, axis=0)
    counts = csum[-1]
    rank = jnp.sum(csum * onehot, axis=1) - 1
    padded = ((counts + tm - 1) // tm) * tm
    ends = jnp.cumsum(padded)
    pos = jnp.sum(onehot * (ends - padded)[None, :], axis=1) + rank
    tile_start = jnp.arange(n_tiles, dtype=jnp.int32) * tm
    n_valid = ends[-1] // tm
    tile_cls = jnp.sum((tile_start[:, None] >= ends[None, :]).astype(jnp.int32), axis=1)
    last_cls = jnp.sum((ends[-1] - 1 >= ends).astype(jnp.int32))
    tile_cls = jnp.where(tile_start < ends[-1], tile_cls, last_cls)
    pair = jnp.asarray(MOE_PAIRS, jnp.int32)[tile_cls % len(MOE_PAIRS)]
    base = (tile_cls // len(MOE_PAIRS)) * MOE_EXPERTS_PER_GROUP
    idle = n_valid + jnp.arange(MOE_CLASSES, dtype=jnp.int32)
    fill_start = jnp.concatenate([jnp.maximum(ends - tm, 0), jnp.minimum(idle, n_tiles - 1) * tm])
    fill_on = jnp.concatenate([counts > 0, idle < n_tiles]).astype(jnp.int32)
    return pos, base + pair[:, 0], base + pair[:, 1], n_valid.reshape(1), fill_start, fill_on


def _moe_dispatch_kernel(pos_ref, fs_ref, fo_ref, x_ref, xs_hbm, stage, zbuf, sem, zsem, *, tm, tz):
    i = pl.program_id(0)
    n = pl.num_programs(0)

    def fill_copy(c):
        return pltpu.make_async_copy(zbuf, xs_hbm.at[pl.ds(pl.multiple_of(fs_ref[c], tz), tz), :], zsem)

    @pl.when(i == 0)
    def _():
        zbuf[...] = jnp.zeros_like(zbuf)
        for c in range(2 * MOE_CLASSES):
            @pl.when(fo_ref[c] > 0)
            def _():
                fill_copy(c).start()
        for c in range(2 * MOE_CLASSES):
            @pl.when(fo_ref[c] > 0)
            def _():
                fill_copy(c).wait()

    def tile_wait(slot):
        pltpu.make_async_copy(stage.at[slot], xs_hbm.at[pl.ds(0, tm), :], sem.at[slot]).wait()

    slot = i % 2

    @pl.when(i >= 2)
    def _():
        tile_wait(slot)

    stage[slot] = x_ref[...]

    def body(r, carry):
        p = pos_ref[i * tm + r]
        pltpu.make_async_copy(stage.at[slot, pl.ds(r, 1), :], xs_hbm.at[pl.ds(p, 1), :],
                              sem.at[slot]).start()
        return carry
    lax.fori_loop(0, tm, body, 0, unroll=8)

    @pl.when(i == n - 1)
    def _():
        tile_wait(slot)

        @pl.when(n >= 2)
        def _():
            tile_wait(1 - slot)


def moe_dispatch(xr, pos, fill_start, fill_on, *, tm, tz, n_slots):
    T, width = xr.shape
    grid_spec = pltpu.PrefetchScalarGridSpec(
        num_scalar_prefetch=3,
        grid=(T // tm,),
        in_specs=[pl.BlockSpec((tm, width), lambda i, pos, fs, fo: (i, 0))],
        out_specs=pl.BlockSpec(memory_space=pl.ANY),
        scratch_shapes=[pltpu.VMEM((2, tm, width), f32), pltpu.VMEM((tz, width), f32),
                        pltpu.SemaphoreType.DMA((2,)), pltpu.SemaphoreType.DMA(())],
    )
    return pl.pallas_call(
        functools.partial(_moe_dispatch_kernel, tm=tm, tz=tz),
        out_shape=jax.ShapeDtypeStruct((n_slots, width), f32),
        grid_spec=grid_spec,
        compiler_params=_cparams(("arbitrary",)),
        name="moe_dispatch",
    )(pos, fill_start, fill_on, xr)


def _moe_pair_kernel(ea_ref, eb_ref, nv_ref, xs_ref, wga, wua, wda, wgb, wub, wdb, o_ref):
    @pl.when(pl.program_id(0) >= nv_ref[0])
    def _():
        o_ref[...] = jnp.zeros_like(o_ref)

    @pl.when(pl.program_id(0) < nv_ref[0])
    def _():
        D = o_ref.shape[1]
        x = xs_ref[:, :D].astype(bf16)
        rec = xs_ref[:, D:]
        first = rec[:, 0:1] < rec[:, 1:2]
        w_a = jnp.where(first, rec[:, 2:3], rec[:, 3:4])
        w_b = jnp.where(first, rec[:, 3:4], rec[:, 2:3])

        def expert(wg, wu, wd):
            gate = jnp.dot(x, wg[0], preferred_element_type=f32)
            up = jnp.dot(x, wu[0], preferred_element_type=f32)
            hid = gate * _sigmoid(gate) * up
            return jnp.dot(hid.astype(bf16), wd[0], preferred_element_type=f32)

        o_ref[...] = w_a * expert(wga, wua, wda) + w_b * expert(wgb, wub, wdb)


def moe_pair_experts(xs, e_a, e_b, n_valid, wg, wu, wd, *, tm, n_tiles):
    width = xs.shape[1]
    D = wg.shape[1]
    held = lambda i, ea, eb, nv: (jnp.minimum(i, nv[0] - 1), 0)
    tile = lambda i, ea, eb, nv: (i, 0)
    wa = lambda shape: pl.BlockSpec(shape, lambda i, ea, eb, nv: (ea[i], 0, 0))
    wb = lambda shape: pl.BlockSpec(shape, lambda i, ea, eb, nv: (eb[i], 0, 0))
    grid_spec = pltpu.PrefetchScalarGridSpec(
        num_scalar_prefetch=3,
        grid=(n_tiles,),
        in_specs=[pl.BlockSpec((tm, width), held),
                  wa((1, D, MOE_FF)), wa((1, D, MOE_FF)), wa((1, MOE_FF, D)),
                  wb((1, D, MOE_FF)), wb((1, D, MOE_FF)), wb((1, MOE_FF, D))],
        out_specs=pl.BlockSpec((tm, D), tile),
    )
    return pl.pallas_call(
        _moe_pair_kernel,
        out_shape=jax.ShapeDtypeStruct((n_tiles * tm, D), f32),
        grid_spec=grid_spec,
        compiler_params=_cparams(("arbitrary",)),
        name="moe_experts",
    )(e_a, e_b, n_valid, xs, wg, wu, wd, wg, wu, wd)


def _moe_combine_kernel(pos_ref, h_ref, y_hbm, gf_ref, o_ref, ybuf, sem, *, tm):
    i = pl.program_id(0)
    n = pl.num_programs(0)

    def gather(tile, slot):
        def body(r, carry):
            p = pos_ref[tile * tm + r]
            pltpu.make_async_copy(y_hbm.at[pl.ds(p, 1), :],
                                  ybuf.at[slot, pl.ds(r, 1), :], sem.at[slot]).start()
            return carry
        lax.fori_loop(0, tm, body, 0, unroll=8)

    @pl.when(i == 0)
    def _():
        gather(0, 0)

    @pl.when(i + 1 < n)
    def _():
        gather(i + 1, (i + 1) % 2)

    slot = i % 2
    pltpu.make_async_copy(y_hbm.at[pl.ds(0, tm), :], ybuf.at[slot], sem.at[slot]).wait()
    o_ref[...] = _rmsnorm_rows(h_ref[...] + ybuf[slot], gf_ref[...])


def moe_combine(h, y_sorted, pos, g_final, *, tm):
    T, D = h.shape
    grid_spec = pltpu.PrefetchScalarGridSpec(
        num_scalar_prefetch=1,
        grid=(T // tm,),
        in_specs=[
            pl.BlockSpec((tm, D), lambda i, pos: (i, 0)),
            pl.BlockSpec(memory_space=pl.ANY),
            pl.BlockSpec((1, D), lambda i, pos: (0, 0)),
        ],
        out_specs=pl.BlockSpec((tm, D), lambda i, pos: (i, 0)),
        scratch_shapes=[pltpu.VMEM((2, tm, D), f32), pltpu.SemaphoreType.DMA((2,))],
    )
    return pl.pallas_call(
        functools.partial(_moe_combine_kernel, tm=tm),
        out_shape=jax.ShapeDtypeStruct((T, D), f32),
        grid_spec=grid_spec,
        compiler_params=_cparams(("arbitrary",)),
        name="moe_combine",
    )(pos, h, y_sorted, g_final.reshape(1, D))


def _pad_rows(w, rows):
    return jnp.pad(w, ((0, rows - w.shape[0]), (0, 0)))


def _pack_lora_cols(t):
    o1, o2 = DECAY_LORA, DECAY_LORA + AAA_LORA
    pad = lambda a, n: jnp.pad(a, [(0, 0)] * (a.ndim - 1) + [(0, n - a.shape[-1])])
    return jnp.concatenate([pad(t[..., :o1], 128), pad(t[..., o1:o2], 128),
                            pad(t[..., o2:], 256)], axis=-1)


def _layer(x2, mem2, p, *, batch, seq, cfg):
    D = D_MODEL
    w_in = p['w_in']
    w_in_p = jnp.concatenate([w_in[:, :MAIN_COLS], _pack_lora_cols(w_in[:, MAIN_COLS:])],
                             axis=1).astype(bf16)
    mu = p['rwkv_mu']
    mu_main = mu[:3 * RWKV_WIDTH]
    mu_lora = _pack_lora_cols(mu[3 * RWKV_WIDTH:])

    proj = norm_matmul_resident(x2, p['norm_mix_g'], w_in_p, tm=cfg['tm'], tn=cfg['tn_in'],
                                name="in_proj")
    pool_out = pool_mixer(proj, p['pool_w'].astype(bf16), p['pool_scale'],
                          batch=batch, seq=seq, ts=cfg['ts_pool'])
    r, lw, k, v, kk, kka, g = rwkv_prep(
        proj, mu_main, mu_lora, p['rwkv_w0'], _pad_rows(p['rwkv_w2'], 128).astype(bf16),
        p['rwkv_a0'], _pad_rows(p['rwkv_a2'], 128).astype(bf16),
        _pad_rows(p['rwkv_g2'], 256).astype(bf16), p['rwkv_k_k'], p['rwkv_k_a'],
        batch=batch, seq=seq, ts=cfg['ts_prep'])
    rwkv_out = rwkv_scan(r, lw, k, v, kk, kka, g, p['rwkv_ln_w'], p['rwkv_ln_b'], p['rwkv_r_k'],
                         batch=batch, seq=seq, rows=cfg['scan_rows'], n_hg=cfg['scan_hg'])
    w_out = p['w_out'].astype(bf16)
    h1 = matmul_residual_resident([pool_out, rwkv_out], [w_out[:POOL_WIDTH], w_out[POOL_WIDTH:]],
                                  x2, tm=cfg['tm'], tn=cfg['tn'], name="out_proj")

    q = norm_matmul_resident(h1, p['norm_xattn_g'], p['xattn_w_q'].astype(bf16),
                             tm=cfg['tm'], tn=cfg['tn'], name="q_proj")
    kv = norm_matmul(mem2, p['norm_mem_g'], p['xattn_w_kv'].astype(bf16),
                     tm=min(cfg['tm'], mem2.shape[0]), tn=cfg['tn'], name="kv_proj")
    o = xattn_core(q, kv, batch=batch, seq=seq, ts=cfg['ts_attn'])

    w_router = jnp.pad(jnp.concatenate([p['moe_w_group'], p['moe_w_expert']], axis=1),
                       ((0, 0), (0, ROUTER_LANES - MOE_GROUPS - MOE_EXPERTS)))
    b_router = jnp.pad(jnp.concatenate([p['moe_b_group'], p['moe_b_expert']]),
                       (0, ROUTER_LANES - MOE_GROUPS - MOE_EXPERTS)).reshape(1, ROUTER_LANES)
    h2, xr = oproj_router(o, p['xattn_w_o'].astype(bf16), h1, p['norm_ffn_g'],
                          w_router, b_router, tm=cfg['tm'], tn=cfg['tn'])
    tm_e = cfg['tm_moe']
    n_tiles = xr.shape[0] // tm_e + MOE_CLASSES
    pos, e_a, e_b, n_valid, fill_start, fill_on = _route_metadata(
        xr[:, D:D + 2], tm=tm_e, n_tiles=n_tiles)
    xs = moe_dispatch(xr, pos, fill_start, fill_on, tm=cfg['tm_disp'], tz=tm_e,
                      n_slots=n_tiles * tm_e)
    y_sorted = moe_pair_experts(xs, e_a, e_b, n_valid, p['moe_w_gate'].astype(bf16),
                                p['moe_w_up'].astype(bf16), p['moe_w_down'].astype(bf16),
                                tm=tm_e, n_tiles=n_tiles)
    return moe_combine(h2, y_sorted, pos, p['norm_final_g'], tm=cfg['tm_comb'])


_CFG = dict(tm=512, tn=1024, tn_in=1536, ts_pool=512, ts_prep=256, scan_rows=256, scan_hg=4,
            ts_attn=1024, tm_moe=256, tm_disp=256, tm_comb=256)


def kernel(x, mem, norm_mix_g, w_in, pool_w, pool_scale, rwkv_mu, rwkv_w0, rwkv_w2, rwkv_a0,
           rwkv_a2, rwkv_g2, rwkv_k_k, rwkv_k_a, rwkv_r_k, rwkv_ln_w, rwkv_ln_b, w_out,
           norm_xattn_g, norm_mem_g, xattn_w_q, xattn_w_kv, xattn_w_o, norm_ffn_g,
           moe_w_group, moe_b_group, moe_w_expert, moe_b_expert, moe_w_gate, moe_w_up,
           moe_w_down, norm_final_g):
    batch, seq, D = x.shape
    p = dict(norm_mix_g=norm_mix_g[0], w_in=w_in[0], pool_w=pool_w[0], pool_scale=pool_scale[0],
             rwkv_mu=rwkv_mu[0], rwkv_w0=rwkv_w0[0], rwkv_w2=rwkv_w2[0], rwkv_a0=rwkv_a0[0],
             rwkv_a2=rwkv_a2[0], rwkv_g2=rwkv_g2[0], rwkv_k_k=rwkv_k_k[0], rwkv_k_a=rwkv_k_a[0],
             rwkv_r_k=rwkv_r_k[0], rwkv_ln_w=rwkv_ln_w[0], rwkv_ln_b=rwkv_ln_b[0], w_out=w_out[0],
             norm_xattn_g=norm_xattn_g[0], norm_mem_g=norm_mem_g[0], xattn_w_q=xattn_w_q[0],
             xattn_w_kv=xattn_w_kv[0], xattn_w_o=xattn_w_o[0], norm_ffn_g=norm_ffn_g[0],
             moe_w_group=moe_w_group[0], moe_b_group=moe_b_group[0], moe_w_expert=moe_w_expert[0],
             moe_b_expert=moe_b_expert[0], moe_w_gate=moe_w_gate[0], moe_w_up=moe_w_up[0],
             moe_w_down=moe_w_down[0], norm_final_g=norm_final_g)
    out = _layer(x.reshape(batch * seq, D), mem.reshape(batch * MEM_LEN, D), p,
                 batch=batch, seq=seq, cfg=_CFG)
    return out.reshape(batch, seq, D)
```

```python
import functools

import jax
import jax.numpy as jnp
from jax import lax
from jax.experimental import pallas as pl
from jax.experimental.pallas import tpu as pltpu

f32 = jnp.float32
bf16 = jnp.bfloat16

D_MODEL = 2048
MEM_LEN = 256
NORM_EPS = 1e-6

POOL_WIDTH = 1024
POOL_WINDOWS = (2, 4, 8, 16)
POOL_GROUP = 256
POOL_HALO = 16
RWKV_WIDTH = 1024
RWKV_HEAD = 64
RWKV_HEADS = 16
GN_EPS = 64e-5
DECAY_LORA = 64
AAA_LORA = 64
GATE_LORA = 160
LORA_PAD = 512
LORA_W_OFF, LORA_A_OFF, LORA_G_OFF = 0, 128, 256
MAIN_COLS = POOL_WIDTH + 3 * RWKV_WIDTH
PROJ_COLS = MAIN_COLS + LORA_PAD

RWKV_CHUNK = 64
HEADS_PER_STEP = 4
HG_LANES = HEADS_PER_STEP * RWKV_HEAD

XATTN_HEADS = 4
XATTN_HEAD_DIM = 512

MOE_GROUPS = 4
MOE_EXPERTS_PER_GROUP = 4
MOE_EXPERTS = 16
MOE_FF = 512
ROUTER_LANES = 128

VMEM_LIMIT = 56 * 1024 * 1024


def _cparams(sem):
    return pltpu.CompilerParams(dimension_semantics=sem, vmem_limit_bytes=VMEM_LIMIT)


def _split_dot(x, w_bf16):
    hi = x.astype(bf16)
    lo = (x - hi.astype(f32)).astype(bf16)
    return (jnp.dot(hi, w_bf16, preferred_element_type=f32)
            + jnp.dot(lo, w_bf16, preferred_element_type=f32))


def _norm_mm_kernel(x_ref, g_ref, w_ref, o_ref, xn_ref):
    @pl.when(pl.program_id(1) == 0)
    def _():
        x = x_ref[...]
        ms = jnp.mean(x * x, axis=-1, keepdims=True)
        xn_ref[...] = (x * lax.rsqrt(ms + NORM_EPS) * g_ref[...]).astype(bf16)

    o_ref[...] = jnp.dot(xn_ref[...], w_ref[...], preferred_element_type=f32).astype(o_ref.dtype)


def norm_matmul(x, g, w, *, tm, tn, out_dtype=bf16, name="norm_matmul"):
    M, K = x.shape
    N = w.shape[1]
    return pl.pallas_call(
        _norm_mm_kernel,
        out_shape=jax.ShapeDtypeStruct((M, N), out_dtype),
        grid=(M // tm, N // tn),
        in_specs=[
            pl.BlockSpec((tm, K), lambda i, j: (i, 0)),
            pl.BlockSpec((1, K), lambda i, j: (0, 0)),
            pl.BlockSpec((K, tn), lambda i, j: (0, j)),
        ],
        out_specs=pl.BlockSpec((tm, tn), lambda i, j: (i, j)),
        scratch_shapes=[pltpu.VMEM((tm, K), bf16)],
        compiler_params=_cparams(("parallel", "arbitrary")),
        name=name,
    )(x, g.reshape(1, K), w)


def _resident(shape):
    return pl.BlockSpec(shape, lambda i: (0,) * len(shape), pipeline_mode=pl.Buffered(1))


def _rmsnorm_rows(x, g):
    ms = jnp.mean(x * x, axis=-1, keepdims=True)
    return x * lax.rsqrt(ms + NORM_EPS) * g


def _norm_mm_res_kernel(x_ref, g_ref, w_ref, o_ref, *, tn):
    xn = _rmsnorm_rows(x_ref[...], g_ref[...]).astype(bf16)
    for n0 in range(0, o_ref.shape[1], tn):
        o_ref[:, n0:n0 + tn] = jnp.dot(xn, w_ref[:, n0:n0 + tn],
                                       preferred_element_type=f32).astype(o_ref.dtype)


def norm_matmul_resident(x, g, w, *, tm, tn, name):
    M, K = x.shape
    N = w.shape[1]
    return pl.pallas_call(
        functools.partial(_norm_mm_res_kernel, tn=tn),
        out_shape=jax.ShapeDtypeStruct((M, N), bf16),
        grid=(M // tm,),
        in_specs=[pl.BlockSpec((tm, K), lambda i: (i, 0)), _resident((1, K)), _resident((K, N))],
        out_specs=pl.BlockSpec((tm, N), lambda i: (i, 0)),
        compiler_params=_cparams(("parallel",)),
        name=name,
    )(x, g.reshape(1, K), w)


def _in_proj_kernel(x_ref, xh_ref, g_ref, w_ref, mu_ref, o_ref, *, tn, tiles_per_seq):
    first = pl.program_id(0) % tiles_per_seq == 0
    xn = _rmsnorm_rows(x_ref[...], g_ref[...]).astype(bf16)
    xh = jnp.where(first, 0.0, _rmsnorm_rows(xh_ref[...], g_ref[...])).astype(bf16)
    lhs = jnp.concatenate([xh, xn], axis=0)
    for n0 in range(0, o_ref.shape[1], tn):
        res = jnp.dot(lhs, w_ref[:, n0:n0 + tn], preferred_element_type=f32)
        p = res[POOL_HALO:]
        if n0 >= POOL_WIDTH:
            prev = pltpu.roll(res, 1, 0)[POOL_HALO:]
            p = p + (prev - p) * mu_ref[:, n0:n0 + tn]
        o_ref[:, n0:n0 + tn] = p.astype(o_ref.dtype)


def in_proj_shift(x, g, w, mu_cols, *, tm, tn, seq):
    M, K = x.shape
    N = w.shape[1]
    hb = tm // POOL_HALO
    return pl.pallas_call(
        functools.partial(_in_proj_kernel, tn=tn, tiles_per_seq=seq // tm),
        out_shape=jax.ShapeDtypeStruct((M, N), bf16),
        grid=(M // tm,),
        in_specs=[pl.BlockSpec((tm, K), lambda i: (i, 0)),
                  pl.BlockSpec((POOL_HALO, K), lambda i: (jnp.maximum(i * hb - 1, 0), 0)),
                  _resident((1, K)), _resident((K, N)), _resident((1, N))],
        out_specs=pl.BlockSpec((tm, N), lambda i: (i, 0)),
        compiler_params=_cparams(("parallel",)),
        name="in_proj",
    )(x, x, g.reshape(1, K), w, mu_cols.reshape(1, N))


def _mm_res_resident_kernel(*refs, n, tn):
    a_refs, w_refs, res_ref, o_ref = refs[:n], refs[n:2 * n], refs[2 * n], refs[2 * n + 1]
    for n0 in range(0, o_ref.shape[1], tn):
        acc = res_ref[:, n0:n0 + tn]
        for a_ref, w_ref in zip(a_refs, w_refs):
            acc = acc + jnp.dot(a_ref[...], w_ref[:, n0:n0 + tn], preferred_element_type=f32)
        o_ref[:, n0:n0 + tn] = acc


def matmul_residual_resident(a_list, w_list, res, *, tm, tn, name):
    M, N = res.shape
    n = len(a_list)
    return pl.pallas_call(
        functools.partial(_mm_res_resident_kernel, n=n, tn=tn),
        out_shape=jax.ShapeDtypeStruct((M, N), f32),
        grid=(M // tm,),
        in_specs=([pl.BlockSpec((tm, a.shape[1]), lambda i: (i, 0)) for a in a_list]
                  + [_resident(w.shape) for w in w_list]
                  + [pl.BlockSpec((tm, N), lambda i: (i, 0))]),
        out_specs=pl.BlockSpec((tm, N), lambda i: (i, 0)),
        compiler_params=_cparams(("parallel",)),
        name=name,
    )(*a_list, *w_list, res)


def _pool_kernel(u_ref, halo_ref, pw_ref, ps_ref, o_ref, *, ts):
    s = pl.program_id(1)
    u = u_ref[...].astype(f32)
    halo = jnp.where(s > 0, halo_ref[...].astype(f32), 0.0)
    ext = jnp.concatenate([halo, u], axis=0)
    pos = s * ts + lax.broadcasted_iota(jnp.int32, (ts, 1), 0)
    acc = ext
    shift = 1
    for gi, w in enumerate(POOL_WINDOWS):
        while shift < w:
            acc = acc + pltpu.roll(acc, shift, 0)
            shift *= 2
        lo, hi = gi * POOL_GROUP, (gi + 1) * POOL_GROUP
        cnt = jnp.minimum(pos + 1, w).astype(f32)
        pooled = acc[POOL_HALO:, lo:hi] / cnt - u[:, lo:hi]
        mixed = jnp.dot(pooled.astype(bf16), pw_ref[gi], preferred_element_type=f32)
        o_ref[:, lo:hi] = (mixed * ps_ref[:, lo:hi]).astype(o_ref.dtype)


def pool_mixer(proj, pool_w, pool_scale, *, batch, seq, ts):
    ns = seq // ts
    hb = ts // POOL_HALO
    return pl.pallas_call(
        functools.partial(_pool_kernel, ts=ts),
        out_shape=jax.ShapeDtypeStruct((batch * seq, POOL_WIDTH), bf16),
        grid=(batch, ns),
        in_specs=[
            pl.BlockSpec((ts, POOL_WIDTH), lambda b, s: (b * ns + s, 0)),
            pl.BlockSpec((POOL_HALO, POOL_WIDTH),
                         lambda b, s: (jnp.maximum((b * ns + s) * hb - 1, 0), 0)),
            pl.BlockSpec((len(POOL_WINDOWS), POOL_GROUP, POOL_GROUP), lambda b, s: (0, 0, 0)),
            pl.BlockSpec((1, POOL_WIDTH), lambda b, s: (0, 0)),
        ],
        out_specs=pl.BlockSpec((ts, POOL_WIDTH), lambda b, s: (b * ns + s, 0)),
        compiler_params=_cparams(("parallel", "arbitrary")),
        name="pool_mixer",
    )(proj, proj, pool_w, pool_scale.reshape(1, POOL_WIDTH))


def _head_ones():
    r = lax.broadcasted_iota(jnp.int32, (HG_LANES, HG_LANES), 0) // RWKV_HEAD
    c = lax.broadcasted_iota(jnp.int32, (HG_LANES, HG_LANES), 1) // RWKV_HEAD
    return jnp.where(r == c, 1.0, 0.0).astype(bf16)


def _head_sum(x, ones_bd):
    parts = [_split_dot(x[:, c:c + HG_LANES], ones_bd) for c in range(0, x.shape[1], HG_LANES)]
    return parts[0] if len(parts) == 1 else jnp.concatenate(parts, axis=1)


def _sigmoid(x):
    return 1.0 / (1.0 + jnp.exp(-x))


def _rwkv_prep_kernel(k_ref, lo_ref, w0_ref, w2_ref, a0_ref, a2_ref, g2_ref, kk_ref, ka_ref,
                      lw_out, k_out, kk_out, kka_out, g_out):
    k = k_ref[...].astype(f32)
    lo = lo_ref[...].astype(f32)
    w_lo = lo[:, LORA_W_OFF:LORA_W_OFF + 128]
    a_lo = lo[:, LORA_A_OFF:LORA_A_OFF + 128]
    g_lo = lo[:, LORA_G_OFF:LORA_G_OFF + 256]

    wx = w0_ref[...] + jnp.dot(jnp.tanh(w_lo).astype(bf16), w2_ref[...], preferred_element_type=f32)
    nx = -wx
    softplus = jnp.maximum(nx, 0.0) + jnp.log(1.0 + jnp.exp(-jnp.abs(nx)))
    w_log = -softplus - 0.5
    lw_out[...] = -jnp.exp(w_log)
    a = _sigmoid(a0_ref[...] + jnp.dot(a_lo.astype(bf16), a2_ref[...], preferred_element_type=f32))
    g = jnp.dot(_sigmoid(g_lo).astype(bf16), g2_ref[...], preferred_element_type=f32)

    ones_bd = _head_ones()
    kk = k * kk_ref[...]
    kk = kk * lax.rsqrt(jnp.maximum(_head_sum(kk * kk, ones_bd), 1e-24))
    k = k * (1.0 + (a - 1.0) * ka_ref[...])

    k_out[...] = k.astype(k_out.dtype)
    kk_out[...] = kk.astype(kk_out.dtype)
    kka_out[...] = (kk * a).astype(kka_out.dtype)
    g_out[...] = g.astype(g_out.dtype)


def rwkv_prep(proj, w0, w2p, a0, a2p, g2p, k_k, k_a, *, ts):
    T = proj.shape[0]
    C = RWKV_WIDTH
    k_blk = (POOL_WIDTH + C) // C
    lora_blk = MAIN_COLS // LORA_PAD

    def full(shape):
        return pl.BlockSpec(shape, lambda i: (0,) * len(shape))

    row = lambda t: t.reshape(1, -1)
    out_sd = lambda dt: jax.ShapeDtypeStruct((T, C), dt)
    out_spec = pl.BlockSpec((ts, C), lambda i: (i, 0))
    return pl.pallas_call(
        _rwkv_prep_kernel,
        out_shape=[out_sd(f32), out_sd(bf16), out_sd(bf16), out_sd(bf16), out_sd(bf16)],
        grid=(T // ts,),
        in_specs=[
            pl.BlockSpec((ts, C), lambda i: (i, k_blk)),
            pl.BlockSpec((ts, LORA_PAD), lambda i: (i, lora_blk)),
            full((1, C)), full((128, C)), full((1, C)), full((128, C)), full((256, C)),
            full((1, C)), full((1, C)),
        ],
        out_specs=[out_spec] * 5,
        compiler_params=_cparams(("parallel",)),
        name="rwkv_prep",
    )(proj, proj, row(w0), w2p, row(a0), a2p, g2p, row(k_k), row(k_a))


def _bd_mask():
    r = lax.broadcasted_iota(jnp.int32, (HG_LANES, HG_LANES), 0) // RWKV_CHUNK
    c = lax.broadcasted_iota(jnp.int32, (HG_LANES, HG_LANES), 1) // RWKV_HEAD
    return r == c


def _rwkv_scan_kernel(r_ref, v_ref, lw_ref, k_ref, kk_ref, kka_ref, g_ref,
                      lnw_ref, lnb_ref, rk_ref, o_ref, state_ref, *, n_chunks, n_hg):
    L = RWKV_CHUNK
    W = HG_LANES

    @pl.when(pl.program_id(2) == 0)
    def _():
        state_ref[...] = jnp.zeros_like(state_ref)

    bd_mask = _bd_mask()
    ones_bd = _head_ones()

    def bd(x):
        tiled = jnp.concatenate([x] * HEADS_PER_STEP, axis=0)
        return jnp.where(bd_mask, tiled, 0.0).astype(bf16)

    def mm(a, b_bf16):
        return jnp.dot(a.astype(bf16), b_bf16, preferred_element_type=f32)

    def mm_nt(a, b_bf16):
        return lax.dot_general(a.astype(bf16), b_bf16, (((1,), (1,)), ((), ())),
                               preferred_element_type=f32)

    def mm_tn(a_bf16, b_bf16):
        return lax.dot_general(a_bf16, b_bf16, (((0,), (0,)), ((), ())),
                               preferred_element_type=f32)

    def split(x):
        hi = x.astype(bf16)
        return hi, (x - hi.astype(f32)).astype(bf16)

    def head_sums(xs):
        tot = mm(jnp.concatenate(xs, axis=0), ones_bd)
        return [tot[j * L:(j + 1) * L] for j in range(len(xs))]

    t_idx = lax.broadcasted_iota(jnp.int32, (L, W), 0)
    s_idx = lax.broadcasted_iota(jnp.int32, (L, W), 1) % L
    strict = t_idx > s_idx
    incl = t_idx >= s_idx
    eye_all = jnp.where(t_idx == s_idx, 1.0, 0.0)
    tri = jnp.where(lax.broadcasted_iota(jnp.int32, (L, L), 0)
                    >= lax.broadcasted_iota(jnp.int32, (L, L), 1), 1.0, 0.0).astype(bf16)
    blockdiag = (lax.broadcasted_iota(jnp.int32, (W, W), 0) // RWKV_HEAD
                 == lax.broadcasted_iota(jnp.int32, (W, W), 1) // RWKV_HEAD)

    def phase1(items):
        I = range(len(items))
        rows = [pl.ds(c * L, L) for _, c in items]
        cols = [pl.ds(h * W, W) for h, _ in items]
        ld = lambda ref, i: ref[rows[i], cols[i]]
        r = [ld(r_ref, i).astype(f32) for i in I]
        k = [ld(k_ref, i).astype(f32) for i in I]
        v = [ld(v_ref, i).astype(f32) for i in I]
        lw = [ld(lw_ref, i) for i in I]
        lw_s = [split(x) for x in lw]
        cs = [jnp.dot(tri, hi, preferred_element_type=f32) + jnp.dot(tri, lo, preferred_element_type=f32)
              for hi, lo in lw_s]
        yield
        g_inv = [jnp.exp(-x) for x in cs]
        to_end = [jnp.exp(x[L - 1:L, :] - x) for x in cs]
        a_t = [-ld(kk_ref, i).astype(f32) * jnp.exp(cs[i] - lw[i]) for i in I]
        r_t = [r[i] * jnp.exp(cs[i]) for i in I]
        kka = [ld(kka_ref, i).astype(f32) for i in I]
        b_t = [kka[i] * g_inv[i] for i in I]
        k_t = [k[i] * g_inv[i] for i in I]
        b_e = [kka[i] * to_end[i] for i in I]
        k_e = [k[i] * to_end[i] for i in I]
        v_bd = [bd(x) for x in v]
        ar = [jnp.concatenate([a_t[i], r_t[i]], axis=0) for i in I]
        m_b = [mm_nt(ar[i], bd(b_t[i])) for i in I]
        m_k = [mm_nt(ar[i], bd(k_t[i])) for i in I]
        m_ab = [jnp.where(strict, x[:L], 0.0) for x in m_b]
        m_rb = [jnp.where(incl, x[L:], 0.0) for x in m_b]
        m_ak = [jnp.where(strict, x[:L], 0.0) for x in m_k]
        m_rk = [jnp.where(incl, x[L:], 0.0) for x in m_k]
        rk_sum = head_sums([r[i] * k[i] * rk_ref[:, cols[i]] for i in I])
        bonus = [rk_sum[i] * v[i] for i in I]
        yield
        p = m_ab
        tinv = [eye_all + x for x in p]
        m = 2
        while m < L:
            p = [mm(x, bd(x)) for x in p]
            tinv = [tinv[i] + mm(tinv[i], bd(p[i])) for i in I]
            m *= 2
            yield
        x1 = [mm(m_ak[i], v_bd[i]) for i in I]
        u_v = [mm(tinv[i], bd(x1[i])) for i in I]
        a_h = [mm(tinv[i], bd(a_t[i])) for i in I]
        yield
        r_h = [r_t[i] + mm(m_rb[i], bd(a_h[i])) for i in I]
        y_v = [mm(m_rb[i], bd(u_v[i])) + mm(m_rk[i], v_bd[i]) for i in I]
        yield
        p_t = [jnp.where(blockdiag, mm_tn(a_h[i].astype(bf16), b_e[i].astype(bf16)), 0.0).astype(bf16)
               for i in I]
        g_t = [jnp.where(blockdiag,
                         mm_tn(jnp.concatenate([u_v[i], v[i]], axis=0).astype(bf16),
                               jnp.concatenate([b_e[i], k_e[i]], axis=0).astype(bf16)), 0.0) for i in I]
        decay = [jnp.exp(x[L - 1:L, :]) for x in cs]
        for i in I:
            done.append(dict(hg=items[i][0], rows=rows[i], cols=cols[i], r_h=r_h[i], y_v=y_v[i],
                             p_t=p_t[i], g_t=g_t[i], decay=decay[i], bonus=bonus[i]))
        yield

    def chain_step(chs):
        ys = []
        for ch in chs:
            state = states[ch['hg']]
            state_b = state.astype(bf16)
            ys.append(mm_nt(ch['r_h'], state_b) + ch['y_v'])
            states[ch['hg']] = (state * ch['decay']
                                + jnp.dot(state_b, ch['p_t'], preferred_element_type=f32) + ch['g_t'])
        means = head_sums(ys)
        ycs = [y - m * (1.0 / RWKV_HEAD) for y, m in zip(ys, means)]
        vrs = head_sums([yc * yc for yc in ycs])
        for ch, yc, vr in zip(chs, ycs, vrs):
            rw, cl = ch['rows'], ch['cols']
            yn = yc * lax.rsqrt(vr * (1.0 / RWKV_HEAD) + GN_EPS) * lnw_ref[:, cl] + lnb_ref[:, cl]
            o_ref[rw, cl] = ((yn + ch['bonus']) * g_ref[rw, cl].astype(f32)).astype(o_ref.dtype)

    states = [state_ref[h] for h in range(n_hg)]
    done = []
    half = max(n_chunks // 2, 1)
    groups = [[(h, c) for c in range(0, half) for h in range(n_hg)],
              [(h, c) for c in range(half, n_chunks) for h in range(n_hg)]]
    def by_chunk(n_items):
        return [done[j:j + n_hg] for j in range(0, n_items, n_hg)]

    for _ in phase1(groups[0]):
        pass
    pending = by_chunk(len(done))
    for _ in phase1(groups[1]):
        if pending:
            chain_step(pending.pop(0))
    for chs in pending + by_chunk(len(done))[len(groups[0]) // n_hg:]:
        chain_step(chs)
    for h in range(n_hg):
        state_ref[h] = states[h]


def rwkv_scan(proj, lw, k, kk, kka, g, ln_w, ln_b, r_k, *, batch, seq, rows, n_hg):
    nb = seq // rows
    T = batch * seq
    width = n_hg * HG_LANES
    n_col = RWKV_WIDTH // width
    r_blk = POOL_WIDTH // width
    v_blk = (POOL_WIDTH + 2 * RWKV_WIDTH) // width
    blk = pl.BlockSpec((rows, width), lambda b, h, c: (b * nb + c, h))
    par = pl.BlockSpec((1, width), lambda b, h, c: (0, h))
    return pl.pallas_call(
        functools.partial(_rwkv_scan_kernel, n_chunks=rows // RWKV_CHUNK, n_hg=n_hg),
        out_shape=jax.ShapeDtypeStruct((T, RWKV_WIDTH), bf16),
        grid=(batch, n_col, nb),
        in_specs=[pl.BlockSpec((rows, width), lambda b, h, c: (b * nb + c, r_blk + h)),
                  pl.BlockSpec((rows, width), lambda b, h, c: (b * nb + c, v_blk + h))]
                 + [blk] * 5 + [par] * 3,
        out_specs=blk,
        scratch_shapes=[pltpu.VMEM((n_hg, HG_LANES, HG_LANES), f32)],
        compiler_params=_cparams(("parallel", "parallel", "arbitrary")),
        name="rwkv_scan",
    )(proj, proj, lw, k, kk, kka, g, ln_w.reshape(1, -1), ln_b.reshape(1, -1), r_k.reshape(1, -1))


def _xattn_kernel(q_ref, k_ref, v_ref, o_ref):
    s = lax.dot_general(q_ref[...], k_ref[...], (((1,), (1,)), ((), ())),
                        preferred_element_type=f32) * (XATTN_HEAD_DIM ** -0.5)
    m = jnp.max(s, axis=-1, keepdims=True)
    e = jnp.exp(s - m)
    l = jnp.sum(e, axis=-1, keepdims=True)
    o = jnp.dot(e.astype(bf16), v_ref[...], preferred_element_type=f32)
    o_ref[...] = (o / l).astype(o_ref.dtype)


def xattn_core(q, kv, *, batch, seq, ts):
    ns = seq // ts
    hd = XATTN_HEAD_DIM
    return pl.pallas_call(
        _xattn_kernel,
        out_shape=jax.ShapeDtypeStruct(q.shape, bf16),
        grid=(batch, ns, XATTN_HEADS),
        in_specs=[
            pl.BlockSpec((ts, hd), lambda b, s, h: (b * ns + s, h)),
            pl.BlockSpec((MEM_LEN, hd), lambda b, s, h: (b, h)),
            pl.BlockSpec((MEM_LEN, hd), lambda b, s, h: (b, XATTN_HEADS + h)),
        ],
        out_specs=pl.BlockSpec((ts, hd), lambda b, s, h: (b * ns + s, h)),
        compiler_params=_cparams(("parallel", "parallel", "arbitrary")),
        name="xattn_core",
    )(q, kv, kv)


def _route(hn, w, b):
    w_hi = w.astype(bf16)
    w_lo = (w - w_hi.astype(f32)).astype(bf16)
    h_hi = hn.astype(bf16)
    h_lo = (hn - h_hi.astype(f32)).astype(bf16)
    hh = jnp.dot(h_hi, jnp.concatenate([w_hi, w_lo], axis=1), preferred_element_type=f32)
    logits = (hh[:, :ROUTER_LANES] + hh[:, ROUTER_LANES:]
              + jnp.dot(h_lo, w_hi, preferred_element_type=f32)) + b

    lane_i = lax.broadcasted_iota(jnp.int32, logits.shape, 1)
    lane = lane_i.astype(f32)
    neg = -jnp.inf
    big = float(ROUTER_LANES)
    is_g = lane_i < MOE_GROUPS
    gl = jnp.where(is_g, logits, neg)
    gmax = jnp.max(gl, axis=-1, keepdims=True)
    g_idx = jnp.min(jnp.where(gl == gmax, lane, big), axis=-1, keepdims=True)
    g_w = 1.0 / jnp.sum(jnp.exp(gl - gmax), axis=-1, keepdims=True)

    lo_lane = MOE_GROUPS + g_idx * MOE_EXPERTS_PER_GROUP
    sel = (lane >= lo_lane) & (lane < lo_lane + MOE_EXPERTS_PER_GROUP)
    el = jnp.where(sel, logits, neg)
    v1 = jnp.max(el, axis=-1, keepdims=True)
    i1 = jnp.min(jnp.where(el == v1, lane, big), axis=-1, keepdims=True)
    el2 = jnp.where(lane == i1, neg, el)
    v2 = jnp.max(el2, axis=-1, keepdims=True)
    i2 = jnp.min(jnp.where(el2 == v2, lane, big), axis=-1, keepdims=True)
    e2 = jnp.exp(v2 - v1)
    p1 = 1.0 / (1.0 + e2)
    p2 = e2 / (1.0 + e2)
    return (jnp.where(lane_i == 0, i1 - MOE_GROUPS, 0.0)
            + jnp.where(lane_i == 1, i2 - MOE_GROUPS, 0.0)
            + jnp.where(lane_i == 2, p1 * g_w, 0.0)
            + jnp.where(lane_i == 3, p2 * g_w, 0.0))


def _oproj_router_kernel(o_ref, wo_ref, res_ref, g_ref, wr_ref, br_ref, h_ref, xr_ref, *, tn):
    for n0 in range(0, h_ref.shape[1], tn):
        h_ref[:, n0:n0 + tn] = res_ref[:, n0:n0 + tn] + jnp.dot(
            o_ref[...], wo_ref[:, n0:n0 + tn], preferred_element_type=f32)
    hn = _rmsnorm_rows(h_ref[...], g_ref[...])
    D = hn.shape[1]
    xr_ref[:, :D] = hn
    xr_ref[:, D:] = _route(hn, wr_ref[...], br_ref[...])


def oproj_router(o, w_o, res, g, w_router, b_router, *, tm, tn):
    T, D = res.shape
    row = lambda width: pl.BlockSpec((tm, width), lambda i: (i, 0))
    return pl.pallas_call(
        functools.partial(_oproj_router_kernel, tn=tn),
        out_shape=[jax.ShapeDtypeStruct((T, D), f32),
                   jax.ShapeDtypeStruct((T, D + ROUTER_LANES), f32)],
        grid=(T // tm,),
        in_specs=[row(D), _resident((D, D)), row(D), _resident((1, D)),
                  _resident((D, ROUTER_LANES)), _resident((1, ROUTER_LANES))],
        out_specs=[row(D), row(D + ROUTER_LANES)],
        compiler_params=_cparams(("parallel",)),
        name="oproj_router",
    )(o, w_o, res, g.reshape(1, D), w_router, b_router)


MOE_PAIRS = ((0, 1), (0, 2), (0, 3), (1, 2), (1, 3), (2, 3))
MOE_CLASSES = MOE_GROUPS * len(MOE_PAIRS)


def _route_metadata(rec, *, tm, n_tiles):
    ids = rec[:, 0:2].astype(jnp.int32)
    lo = jnp.minimum(ids[:, 0], ids[:, 1])
    hi = jnp.maximum(ids[:, 0], ids[:, 1])
    a, b = lo % MOE_EXPERTS_PER_GROUP, hi % MOE_EXPERTS_PER_GROUP
    cls = (lo // MOE_EXPERTS_PER_GROUP) * len(MOE_PAIRS) + (a * (7 - a)) // 2 + b - a - 1
    onehot = (cls[:, None] == jnp.arange(MOE_CLASSES, dtype=jnp.int32)[None, :]).astype(jnp.int32)
    csum = jnp.cumsum(onehot, axis=0)
    counts = csum[-1]
    rank = jnp.sum(csum * onehot, axis=1) - 1
    padded = ((counts + tm - 1) // tm) * tm
    ends = jnp.cumsum(padded)
    pos = jnp.sum(onehot * (ends - padded)[None, :], axis=1) + rank
    tile_start = jnp.arange(n_tiles, dtype=jnp.int32) * tm
    n_valid = ends[-1] // tm
    tile_cls = jnp.sum((tile_start[:, None] >= ends[None, :]).astype(jnp.int32), axis=1)
    last_cls = jnp.sum((ends[-1] - 1 >= ends).astype(jnp.int32))
    tile_cls = jnp.where(tile_start < ends[-1], tile_cls, last_cls)
    pair = jnp.asarray(MOE_PAIRS, jnp.int32)[tile_cls % len(MOE_PAIRS)]
    base = (tile_cls // len(MOE_PAIRS)) * MOE_EXPERTS_PER_GROUP
    idle = n_valid + jnp.arange(MOE_CLASSES, dtype=jnp.int32)
    fill_start = jnp.concatenate([jnp.maximum(ends - tm, 0), jnp.minimum(idle, n_tiles - 1) * tm])
    fill_on = jnp.concatenate([counts > 0, idle < n_tiles]).astype(jnp.int32)
    return pos, base + pair[:, 0], base + pair[:, 1], n_valid.reshape(1), fill_start, fill_on


def _moe_dispatch_kernel(pos_ref, fs_ref, fo_ref, x_ref, xs_hbm, stage, zbuf, sem, zsem, *, tm, tz):
    i = pl.program_id(0)
    n = pl.num_programs(0)

    def fill_copy(c):
        return pltpu.make_async_copy(zbuf, xs_hbm.at[pl.ds(pl.multiple_of(fs_ref[c], tz), tz), :], zsem)

    @pl.when(i == 0)
    def _():
        zbuf[...] = jnp.zeros_like(zbuf)
        for c in range(2 * MOE_CLASSES):
            @pl.when(fo_ref[c] > 0)
            def _():
                fill_copy(c).start()
        for c in range(2 * MOE_CLASSES):
            @pl.when(fo_ref[c] > 0)
            def _():
                fill_copy(c).wait()

    def tile_wait(slot):
        pltpu.make_async_copy(stage.at[slot], xs_hbm.at[pl.ds(0, tm), :], sem.at[slot]).wait()

    slot = i % 2

    @pl.when(i >= 2)
    def _():
        tile_wait(slot)

    stage[slot] = x_ref[...]

    def body(r, carry):
        p = pos_ref[i * tm + r]
        pltpu.make_async_copy(stage.at[slot, pl.ds(r, 1), :], xs_hbm.at[pl.ds(p, 1), :],
                              sem.at[slot]).start()
        return carry
    lax.fori_loop(0, tm, body, 0, unroll=8)

    @pl.when(i == n - 1)
    def _():
        tile_wait(slot)

        @pl.when(n >= 2)
        def _():
            tile_wait(1 - slot)


def moe_dispatch(xr, pos, fill_start, fill_on, *, tm, tz, n_slots):
    T, width = xr.shape
    grid_spec = pltpu.PrefetchScalarGridSpec(
        num_scalar_prefetch=3,
        grid=(T // tm,),
        in_specs=[pl.BlockSpec((tm, width), lambda i, pos, fs, fo: (i, 0))],
        out_specs=pl.BlockSpec(memory_space=pl.ANY),
        scratch_shapes=[pltpu.VMEM((2, tm, width), f32), pltpu.VMEM((tz, width), f32),
                        pltpu.SemaphoreType.DMA((2,)), pltpu.SemaphoreType.DMA(())],
    )
    return pl.pallas_call(
        functools.partial(_moe_dispatch_kernel, tm=tm, tz=tz),
        out_shape=jax.ShapeDtypeStruct((n_slots, width), f32),
        grid_spec=grid_spec,
        compiler_params=_cparams(("arbitrary",)),
        name="moe_dispatch",
    )(pos, fill_start, fill_on, xr)


def _moe_pair_kernel(ea_ref, eb_ref, nv_ref, xs_ref, wga, wua, wda, wgb, wub, wdb, o_ref):
    @pl.when(pl.program_id(0) >= nv_ref[0])
    def _():
        o_ref[...] = jnp.zeros_like(o_ref)

    @pl.when(pl.program_id(0) < nv_ref[0])
    def _():
        D = o_ref.shape[1]
        x = xs_ref[:, :D].astype(bf16)
        rec = xs_ref[:, D:]
        first = rec[:, 0:1] < rec[:, 1:2]
        w_a = jnp.where(first, rec[:, 2:3], rec[:, 3:4])
        w_b = jnp.where(first, rec[:, 3:4], rec[:, 2:3])

        def expert(wg, wu, wd):
            gate = jnp.dot(x, wg[0], preferred_element_type=f32)
            up = jnp.dot(x, wu[0], preferred_element_type=f32)
            hid = gate * _sigmoid(gate) * up
            return jnp.dot(hid.astype(bf16), wd[0], preferred_element_type=f32)

        o_ref[...] = w_a * expert(wga, wua, wda) + w_b * expert(wgb, wub, wdb)


def moe_pair_experts(xs, e_a, e_b, n_valid, wg, wu, wd, *, tm, n_tiles):
    width = xs.shape[1]
    D = wg.shape[1]
    held = lambda i, ea, eb, nv: (jnp.minimum(i, nv[0] - 1), 0)
    tile = lambda i, ea, eb, nv: (i, 0)
    wa = lambda shape: pl.BlockSpec(shape, lambda i, ea, eb, nv: (ea[i], 0, 0))
    wb = lambda shape: pl.BlockSpec(shape, lambda i, ea, eb, nv: (eb[i], 0, 0))
    grid_spec = pltpu.PrefetchScalarGridSpec(
        num_scalar_prefetch=3,
        grid=(n_tiles,),
        in_specs=[pl.BlockSpec((tm, width), held),
                  wa((1, D, MOE_FF)), wa((1, D, MOE_FF)), wa((1, MOE_FF, D)),
                  wb((1, D, MOE_FF)), wb((1, D, MOE_FF)), wb((1, MOE_FF, D))],
        out_specs=pl.BlockSpec((tm, D), tile),
    )
    return pl.pallas_call(
        _moe_pair_kernel,
        out_shape=jax.ShapeDtypeStruct((n_tiles * tm, D), f32),
        grid_spec=grid_spec,
        compiler_params=_cparams(("arbitrary",)),
        name="moe_experts",
    )(e_a, e_b, n_valid, xs, wg, wu, wd, wg, wu, wd)


def _moe_combine_kernel(pos_ref, h_ref, y_hbm, gf_ref, o_ref, ybuf, sem, *, tm):
    i = pl.program_id(0)
    n = pl.num_programs(0)

    def gather(tile, slot):
        def body(r, carry):
            p = pos_ref[tile * tm + r]
            pltpu.make_async_copy(y_hbm.at[pl.ds(p, 1), :],
                                  ybuf.at[slot, pl.ds(r, 1), :], sem.at[slot]).start()
            return carry
        lax.fori_loop(0, tm, body, 0, unroll=8)

    @pl.when(i == 0)
    def _():
        gather(0, 0)

    @pl.when(i + 1 < n)
    def _():
        gather(i + 1, (i + 1) % 2)

    slot = i % 2
    pltpu.make_async_copy(y_hbm.at[pl.ds(0, tm), :], ybuf.at[slot], sem.at[slot]).wait()
    o_ref[...] = _rmsnorm_rows(h_ref[...] + ybuf[slot], gf_ref[...])


def moe_combine(h, y_sorted, pos, g_final, *, tm):
    T, D = h.shape
    grid_spec = pltpu.PrefetchScalarGridSpec(
        num_scalar_prefetch=1,
        grid=(T // tm,),
        in_specs=[
            pl.BlockSpec((tm, D), lambda i, pos: (i, 0)),
            pl.BlockSpec(memory_space=pl.ANY),
            pl.BlockSpec((1, D), lambda i, pos: (0, 0)),
        ],
        out_specs=pl.BlockSpec((tm, D), lambda i, pos: (i, 0)),
        scratch_shapes=[pltpu.VMEM((2, tm, D), f32), pltpu.SemaphoreType.DMA((2,))],
    )
    return pl.pallas_call(
        functools.partial(_moe_combine_kernel, tm=tm),
        out_shape=jax.ShapeDtypeStruct((T, D), f32),
        grid_spec=grid_spec,
        compiler_params=_cparams(("arbitrary",)),
        name="moe_combine",
    )(pos, h, y_sorted, g_final.reshape(1, D))


def _pad_rows(w, rows):
    return jnp.pad(w, ((0, rows - w.shape[0]), (0, 0)))


def _pack_lora_cols(t):
    o1, o2 = DECAY_LORA, DECAY_LORA + AAA_LORA
    pad = lambda a, n: jnp.pad(a, [(0, 0)] * (a.ndim - 1) + [(0, n - a.shape[-1])])
    return jnp.concatenate([pad(t[..., :o1], 128), pad(t[..., o1:o2], 128),
                            pad(t[..., o2:], 256)], axis=-1)


def _layer(x2, mem2, p, *, batch, seq, cfg):
    D = D_MODEL
    w_in = p['w_in']
    w_in_p = jnp.concatenate([w_in[:, :MAIN_COLS], _pack_lora_cols(w_in[:, MAIN_COLS:])],
                             axis=1).astype(bf16)
    mu_cols = jnp.concatenate([jnp.zeros((POOL_WIDTH,), f32), p['rwkv_mu'][:3 * RWKV_WIDTH],
                               _pack_lora_cols(p['rwkv_mu'][3 * RWKV_WIDTH:])])

    proj = in_proj_shift(x2, p['norm_mix_g'], w_in_p, mu_cols, tm=cfg['tm'], tn=cfg['tn_in'], seq=seq)
    pool_out = pool_mixer(proj, p['pool_w'].astype(bf16), p['pool_scale'],
                          batch=batch, seq=seq, ts=cfg['ts_pool'])
    lw, k, kk, kka, g = rwkv_prep(
        proj, p['rwkv_w0'], _pad_rows(p['rwkv_w2'], 128).astype(bf16),
        p['rwkv_a0'], _pad_rows(p['rwkv_a2'], 128).astype(bf16),
        _pad_rows(p['rwkv_g2'], 256).astype(bf16), p['rwkv_k_k'], p['rwkv_k_a'], ts=cfg['ts_prep'])
    rwkv_out = rwkv_scan(proj, lw, k, kk, kka, g, p['rwkv_ln_w'], p['rwkv_ln_b'], p['rwkv_r_k'],
                         batch=batch, seq=seq, rows=cfg['scan_rows'], n_hg=cfg['scan_hg'])
    w_out = p['w_out'].astype(bf16)
    h1 = matmul_residual_resident([pool_out, rwkv_out], [w_out[:POOL_WIDTH], w_out[POOL_WIDTH:]],
                                  x2, tm=cfg['tm'], tn=cfg['tn'], name="out_proj")

    q = norm_matmul_resident(h1, p['norm_xattn_g'], p['xattn_w_q'].astype(bf16),
                             tm=cfg['tm'], tn=cfg['tn'], name="q_proj")
    kv = norm_matmul(mem2, p['norm_mem_g'], p['xattn_w_kv'].astype(bf16),
                     tm=min(cfg['tm'], mem2.shape[0]), tn=cfg['tn'], name="kv_proj")
    o = xattn_core(q, kv, batch=batch, seq=seq, ts=cfg['ts_attn'])

    w_router = jnp.pad(jnp.concatenate([p['moe_w_group'], p['moe_w_expert']], axis=1),
                       ((0, 0), (0, ROUTER_LANES - MOE_GROUPS - MOE_EXPERTS)))
    b_router = jnp.pad(jnp.concatenate([p['moe_b_group'], p['moe_b_expert']]),
                       (0, ROUTER_LANES - MOE_GROUPS - MOE_EXPERTS)).reshape(1, ROUTER_LANES)
    h2, xr = oproj_router(o, p['xattn_w_o'].astype(bf16), h1, p['norm_ffn_g'],
                          w_router, b_router, tm=cfg['tm'], tn=cfg['tn'])
    tm_e = cfg['tm_moe']
    n_tiles = xr.shape[0] // tm_e + MOE_CLASSES
    pos, e_a, e_b, n_valid, fill_start, fill_on = _route_metadata(
        xr[:, D:D + 2], tm=tm_e, n_tiles=n_tiles)
    xs = moe_dispatch(xr, pos, fill_start, fill_on, tm=cfg['tm_disp'], tz=tm_e,
                      n_slots=n_tiles * tm_e)
    y_sorted = moe_pair_experts(xs, e_a, e_b, n_valid, p['moe_w_gate'].astype(bf16),
                                p['moe_w_up'].astype(bf16), p['moe_w_down'].astype(bf16),
                                tm=tm_e, n_tiles=n_tiles)
    return moe_combine(h2, y_sorted, pos, p['norm_final_g'], tm=cfg['tm_comb'])


_CFG = dict(tm=512, tn=1024, tn_in=512, ts_pool=512, ts_prep=256, scan_rows=256, scan_hg=4,
            ts_attn=1024, tm_moe=256, tm_disp=256, tm_comb=256)


def kernel(x, mem, norm_mix_g, w_in, pool_w, pool_scale, rwkv_mu, rwkv_w0, rwkv_w2, rwkv_a0,
           rwkv_a2, rwkv_g2, rwkv_k_k, rwkv_k_a, rwkv_r_k, rwkv_ln_w, rwkv_ln_b, w_out,
           norm_xattn_g, norm_mem_g, xattn_w_q, xattn_w_kv, xattn_w_o, norm_ffn_g,
           moe_w_group, moe_b_group, moe_w_expert, moe_b_expert, moe_w_gate, moe_w_up,
           moe_w_down, norm_final_g):
    batch, seq, D = x.shape
    p = dict(norm_mix_g=norm_mix_g[0], w_in=w_in[0], pool_w=pool_w[0], pool_scale=pool_scale[0],
             rwkv_mu=rwkv_mu[0], rwkv_w0=rwkv_w0[0], rwkv_w2=rwkv_w2[0], rwkv_a0=rwkv_a0[0],
             rwkv_a2=rwkv_a2[0], rwkv_g2=rwkv_g2[0], rwkv_k_k=rwkv_k_k[0], rwkv_k_a=rwkv_k_a[0],
             rwkv_r_k=rwkv_r_k[0], rwkv_ln_w=rwkv_ln_w[0], rwkv_ln_b=rwkv_ln_b[0], w_out=w_out[0],
             norm_xattn_g=norm_xattn_g[0], norm_mem_g=norm_mem_g[0], xattn_w_q=xattn_w_q[0],
             xattn_w_kv=xattn_w_kv[0], xattn_w_o=xattn_w_o[0], norm_ffn_g=norm_ffn_g[0],
             moe_w_group=moe_w_group[0], moe_b_group=moe_b_group[0], moe_w_expert=moe_w_expert[0],
             moe_b_expert=moe_b_expert[0], moe_w_gate=moe_w_gate[0], moe_w_up=moe_w_up[0],
             moe_w_down=moe_w_down[0], norm_final_g=norm_final_g)
    out = _layer(x.reshape(batch * seq, D), mem.reshape(batch * MEM_LEN, D), p,
                 batch=batch, seq=seq, cfg=_CFG)
    return out.reshape(batch, seq, D)
```

```python
import functools

import jax
import jax.numpy as jnp
from jax import lax
from jax.experimental import pallas as pl
from jax.experimental.pallas import tpu as pltpu

f32 = jnp.float32
bf16 = jnp.bfloat16

D_MODEL = 2048
MEM_LEN = 256
NORM_EPS = 1e-6

POOL_WIDTH = 1024
POOL_WINDOWS = (2, 4, 8, 16)
POOL_GROUP = 256
POOL_HALO = 16
RWKV_WIDTH = 1024
RWKV_HEAD = 64
RWKV_HEADS = 16
GN_EPS = 64e-5
DECAY_SCALE = 0.6065306597126334
DECAY_LORA = 64
AAA_LORA = 64
GATE_LORA = 160
LORA_PAD = 512
LORA_W_OFF, LORA_A_OFF, LORA_G_OFF = 0, 128, 256
MAIN_COLS = POOL_WIDTH + 3 * RWKV_WIDTH
PROJ_COLS = MAIN_COLS + LORA_PAD

RWKV_CHUNK = 64
HEADS_PER_STEP = 4
HG_LANES = HEADS_PER_STEP * RWKV_HEAD

XATTN_HEADS = 4
XATTN_HEAD_DIM = 512

MOE_GROUPS = 4
MOE_EXPERTS_PER_GROUP = 4
MOE_EXPERTS = 16
MOE_FF = 512
ROUTER_LANES = 128

VMEM_LIMIT = 56 * 1024 * 1024


def _cparams(sem):
    return pltpu.CompilerParams(dimension_semantics=sem, vmem_limit_bytes=VMEM_LIMIT)


def _split_dot(x, w_bf16):
    hi = x.astype(bf16)
    lo = (x - hi.astype(f32)).astype(bf16)
    return (jnp.dot(hi, w_bf16, preferred_element_type=f32)
            + jnp.dot(lo, w_bf16, preferred_element_type=f32))


def _resident(shape):
    return pl.BlockSpec(shape, lambda i: (0,) * len(shape), pipeline_mode=pl.Buffered(1))


def _rmsnorm_rows(x, g):
    ms = jnp.mean(x * x, axis=-1, keepdims=True)
    return x * lax.rsqrt(ms + NORM_EPS) * g


def _norm_mm_res_kernel(x_ref, g_ref, w_ref, o_ref, *, tn):
    xn = _rmsnorm_rows(x_ref[...], g_ref[...]).astype(bf16)
    for n0 in range(0, o_ref.shape[1], tn):
        o_ref[:, n0:n0 + tn] = jnp.dot(xn, w_ref[:, n0:n0 + tn],
                                       preferred_element_type=f32).astype(o_ref.dtype)


def norm_matmul_resident(x, g, w, *, tm, tn, name):
    M, K = x.shape
    N = w.shape[1]
    return pl.pallas_call(
        functools.partial(_norm_mm_res_kernel, tn=tn),
        out_shape=jax.ShapeDtypeStruct((M, N), bf16),
        grid=(M // tm,),
        in_specs=[pl.BlockSpec((tm, K), lambda i: (i, 0)), _resident((1, K)), _resident((K, N))],
        out_specs=pl.BlockSpec((tm, N), lambda i: (i, 0)),
        compiler_params=_cparams(("parallel",)),
        name=name,
    )(x, g.reshape(1, K), w)


def _in_proj_kernel(x_ref, xh_ref, g_ref, w_ref, mu_ref, o_ref, *, tn, tiles_per_seq):
    first = pl.program_id(0) % tiles_per_seq == 0
    xn = _rmsnorm_rows(x_ref[...], g_ref[...]).astype(bf16)
    xh = jnp.where(first, 0.0, _rmsnorm_rows(xh_ref[...], g_ref[...])).astype(bf16)
    lhs = jnp.concatenate([xh, xn], axis=0)
    for n0 in range(0, o_ref.shape[1], tn):
        res = jnp.dot(lhs, w_ref[:, n0:n0 + tn], preferred_element_type=f32)
        p = res[POOL_HALO:]
        if n0 >= POOL_WIDTH:
            prev = pltpu.roll(res, 1, 0)[POOL_HALO:]
            p = p + (prev - p) * mu_ref[:, n0:n0 + tn]
        o_ref[:, n0:n0 + tn] = p.astype(o_ref.dtype)


def in_proj_shift(x, g, w, mu_cols, *, tm, tn, seq):
    M, K = x.shape
    N = w.shape[1]
    hb = tm // POOL_HALO
    return pl.pallas_call(
        functools.partial(_in_proj_kernel, tn=tn, tiles_per_seq=seq // tm),
        out_shape=jax.ShapeDtypeStruct((M, N), bf16),
        grid=(M // tm,),
        in_specs=[pl.BlockSpec((tm, K), lambda i: (i, 0)),
                  pl.BlockSpec((POOL_HALO, K), lambda i: (jnp.maximum(i * hb - 1, 0), 0)),
                  _resident((1, K)), _resident((K, N)), _resident((1, N))],
        out_specs=pl.BlockSpec((tm, N), lambda i: (i, 0)),
        compiler_params=_cparams(("parallel",)),
        name="in_proj",
    )(x, x, g.reshape(1, K), w, mu_cols.reshape(1, N))


def _mm_res_resident_kernel(*refs, n, tn):
    a_refs, w_refs, res_ref, o_ref = refs[:n], refs[n:2 * n], refs[2 * n], refs[2 * n + 1]
    for n0 in range(0, o_ref.shape[1], tn):
        acc = res_ref[:, n0:n0 + tn]
        for a_ref, w_ref in zip(a_refs, w_refs):
            acc = acc + jnp.dot(a_ref[...], w_ref[:, n0:n0 + tn], preferred_element_type=f32)
        o_ref[:, n0:n0 + tn] = acc


def matmul_residual_resident(a_list, w_list, res, *, tm, tn, name):
    M, N = res.shape
    n = len(a_list)
    return pl.pallas_call(
        functools.partial(_mm_res_resident_kernel, n=n, tn=tn),
        out_shape=jax.ShapeDtypeStruct((M, N), f32),
        grid=(M // tm,),
        in_specs=([pl.BlockSpec((tm, a.shape[1]), lambda i: (i, 0)) for a in a_list]
                  + [_resident(w.shape) for w in w_list]
                  + [pl.BlockSpec((tm, N), lambda i: (i, 0))]),
        out_specs=pl.BlockSpec((tm, N), lambda i: (i, 0)),
        compiler_params=_cparams(("parallel",)),
        name=name,
    )(*a_list, *w_list, res)


def _pool_kernel(u_ref, halo_ref, pw_ref, ps_ref, o_ref, *, ts):
    s = pl.program_id(1)
    u = u_ref[...].astype(f32)
    halo = jnp.where(s > 0, halo_ref[...].astype(f32), 0.0)
    ext = jnp.concatenate([halo, u], axis=0)
    pos = s * ts + lax.broadcasted_iota(jnp.int32, (ts, 1), 0)
    acc = ext
    shift = 1
    for gi, w in enumerate(POOL_WINDOWS):
        while shift < w:
            acc = acc + pltpu.roll(acc, shift, 0)
            shift *= 2
        lo, hi = gi * POOL_GROUP, (gi + 1) * POOL_GROUP
        cnt = jnp.minimum(pos + 1, w).astype(f32)
        pooled = acc[POOL_HALO:, lo:hi] / cnt - u[:, lo:hi]
        mixed = jnp.dot(pooled.astype(bf16), pw_ref[gi], preferred_element_type=f32)
        o_ref[:, lo:hi] = (mixed * ps_ref[:, lo:hi]).astype(o_ref.dtype)


def pool_mixer(proj, pool_w, pool_scale, *, batch, seq, ts):
    ns = seq // ts
    hb = ts // POOL_HALO
    return pl.pallas_call(
        functools.partial(_pool_kernel, ts=ts),
        out_shape=jax.ShapeDtypeStruct((batch * seq, POOL_WIDTH), bf16),
        grid=(batch, ns),
        in_specs=[
            pl.BlockSpec((ts, POOL_WIDTH), lambda b, s: (b * ns + s, 0)),
            pl.BlockSpec((POOL_HALO, POOL_WIDTH),
                         lambda b, s: (jnp.maximum((b * ns + s) * hb - 1, 0), 0)),
            pl.BlockSpec((len(POOL_WINDOWS), POOL_GROUP, POOL_GROUP), lambda b, s: (0, 0, 0)),
            pl.BlockSpec((1, POOL_WIDTH), lambda b, s: (0, 0)),
        ],
        out_specs=pl.BlockSpec((ts, POOL_WIDTH), lambda b, s: (b * ns + s, 0)),
        compiler_params=_cparams(("parallel", "arbitrary")),
        name="pool_mixer",
    )(proj, proj, pool_w, pool_scale.reshape(1, POOL_WIDTH))


def _head_ones():
    r = lax.broadcasted_iota(jnp.int32, (HG_LANES, HG_LANES), 0) // RWKV_HEAD
    c = lax.broadcasted_iota(jnp.int32, (HG_LANES, HG_LANES), 1) // RWKV_HEAD
    return jnp.where(r == c, 1.0, 0.0).astype(bf16)


def _head_sum(x, ones_bd):
    parts = [_split_dot(x[:, c:c + HG_LANES], ones_bd) for c in range(0, x.shape[1], HG_LANES)]
    return parts[0] if len(parts) == 1 else jnp.concatenate(parts, axis=1)


def _sigmoid(x):
    return 1.0 / (1.0 + jnp.exp(-x))


def _rwkv_prep_kernel(k_ref, lo_ref, w0_ref, w2_ref, a0_ref, a2_ref, g2_ref, kk_ref, ka_ref,
                      lw_out, k_out, kk_out, kka_out, g_out):
    k = k_ref[...].astype(f32)
    lo = lo_ref[...].astype(f32)
    w_lo = lo[:, LORA_W_OFF:LORA_W_OFF + 128]
    a_lo = lo[:, LORA_A_OFF:LORA_A_OFF + 128]
    g_lo = lo[:, LORA_G_OFF:LORA_G_OFF + 256]

    wx = w0_ref[...] + jnp.dot(jnp.tanh(w_lo).astype(bf16), w2_ref[...], preferred_element_type=f32)
    lw_out[...] = -DECAY_SCALE * _sigmoid(wx)
    a = _sigmoid(a0_ref[...] + jnp.dot(a_lo.astype(bf16), a2_ref[...], preferred_element_type=f32))
    g = jnp.dot(_sigmoid(g_lo).astype(bf16), g2_ref[...], preferred_element_type=f32)

    ones_bd = _head_ones()
    kk = k * kk_ref[...]
    kk = kk * lax.rsqrt(jnp.maximum(_head_sum(kk * kk, ones_bd), 1e-24))
    k = k * (1.0 + (a - 1.0) * ka_ref[...])

    k_out[...] = k.astype(k_out.dtype)
    kk_out[...] = kk.astype(kk_out.dtype)
    kka_out[...] = (kk * a).astype(kka_out.dtype)
    g_out[...] = g.astype(g_out.dtype)


def rwkv_prep(proj, w0, w2p, a0, a2p, g2p, k_k, k_a, *, ts):
    T = proj.shape[0]
    C = RWKV_WIDTH
    k_blk = (POOL_WIDTH + C) // C
    lora_blk = MAIN_COLS // LORA_PAD

    def full(shape):
        return pl.BlockSpec(shape, lambda i: (0,) * len(shape))

    row = lambda t: t.reshape(1, -1)
    out_sd = lambda dt: jax.ShapeDtypeStruct((T, C), dt)
    out_spec = pl.BlockSpec((ts, C), lambda i: (i, 0))
    return pl.pallas_call(
        _rwkv_prep_kernel,
        out_shape=[out_sd(f32), out_sd(bf16), out_sd(bf16), out_sd(bf16), out_sd(bf16)],
        grid=(T // ts,),
        in_specs=[
            pl.BlockSpec((ts, C), lambda i: (i, k_blk)),
            pl.BlockSpec((ts, LORA_PAD), lambda i: (i, lora_blk)),
            full((1, C)), full((128, C)), full((1, C)), full((128, C)), full((256, C)),
            full((1, C)), full((1, C)),
        ],
        out_specs=[out_spec] * 5,
        compiler_params=_cparams(("parallel",)),
        name="rwkv_prep",
    )(proj, proj, row(w0), w2p, row(a0), a2p, g2p, row(k_k), row(k_a))


def _bd_mask():
    r = lax.broadcasted_iota(jnp.int32, (HG_LANES, HG_LANES), 0) // RWKV_CHUNK
    c = lax.broadcasted_iota(jnp.int32, (HG_LANES, HG_LANES), 1) // RWKV_HEAD
    return r == c


def _rwkv_scan_kernel(r_ref, v_ref, lw_ref, k_ref, kk_ref, kka_ref, g_ref,
                      lnw_ref, lnb_ref, rk_ref, o_ref, state_ref, *, n_chunks, n_hg):
    L = RWKV_CHUNK
    W = HG_LANES

    @pl.when(pl.program_id(2) == 0)
    def _():
        state_ref[...] = jnp.zeros_like(state_ref)

    bd_mask = _bd_mask()
    ones_bd = _head_ones()

    def bd(x):
        tiled = jnp.concatenate([x] * HEADS_PER_STEP, axis=0)
        return jnp.where(bd_mask, tiled, 0.0).astype(bf16)

    def mm(a, b_bf16):
        return jnp.dot(a.astype(bf16), b_bf16, preferred_element_type=f32)

    def mm_nt(a, b_bf16):
        return lax.dot_general(a.astype(bf16), b_bf16, (((1,), (1,)), ((), ())),
                               preferred_element_type=f32)

    def mm_tn(a_bf16, b_bf16):
        return lax.dot_general(a_bf16, b_bf16, (((0,), (0,)), ((), ())),
                               preferred_element_type=f32)

    def split(x):
        hi = x.astype(bf16)
        return hi, (x - hi.astype(f32)).astype(bf16)

    def head_sums(xs):
        tot = mm(jnp.concatenate(xs, axis=0), ones_bd)
        return [tot[j * L:(j + 1) * L] for j in range(len(xs))]

    t_idx = lax.broadcasted_iota(jnp.int32, (L, W), 0)
    s_idx = lax.broadcasted_iota(jnp.int32, (L, W), 1) % L
    strict = t_idx > s_idx
    incl = t_idx >= s_idx
    eye_all = jnp.where(t_idx == s_idx, 1.0, 0.0)
    tri = jnp.where(lax.broadcasted_iota(jnp.int32, (L, L), 0)
                    >= lax.broadcasted_iota(jnp.int32, (L, L), 1), 1.0, 0.0).astype(bf16)
    blockdiag = (lax.broadcasted_iota(jnp.int32, (W, W), 0) // RWKV_HEAD
                 == lax.broadcasted_iota(jnp.int32, (W, W), 1) // RWKV_HEAD)

    def phase1(items):
        I = range(len(items))
        rows = [pl.ds(c * L, L) for _, c in items]
        cols = [pl.ds(h * W, W) for h, _ in items]
        ld = lambda ref, i: ref[rows[i], cols[i]]
        r = [ld(r_ref, i).astype(f32) for i in I]
        k = [ld(k_ref, i).astype(f32) for i in I]
        v = [ld(v_ref, i).astype(f32) for i in I]
        lw = [ld(lw_ref, i) for i in I]
        lw_s = [split(x) for x in lw]
        cs = [jnp.dot(tri, hi, preferred_element_type=f32) + jnp.dot(tri, lo, preferred_element_type=f32)
              for hi, lo in lw_s]
        yield
        g_inv = [jnp.exp(-x) for x in cs]
        to_end = [jnp.exp(x[L - 1:L, :] - x) for x in cs]
        a_t = [-ld(kk_ref, i).astype(f32) * jnp.exp(cs[i] - lw[i]) for i in I]
        r_t = [r[i] * jnp.exp(cs[i]) for i in I]
        kka = [ld(kka_ref, i).astype(f32) for i in I]
        b_t = [kka[i] * g_inv[i] for i in I]
        k_t = [k[i] * g_inv[i] for i in I]
        b_e = [kka[i] * to_end[i] for i in I]
        k_e = [k[i] * to_end[i] for i in I]
        v_bd = [bd(x) for x in v]
        ar = [jnp.concatenate([a_t[i], r_t[i]], axis=0) for i in I]
        m_b = [mm_nt(ar[i], bd(b_t[i])) for i in I]
        m_k = [mm_nt(ar[i], bd(k_t[i])) for i in I]
        m_ab = [jnp.where(strict, x[:L], 0.0) for x in m_b]
        m_rb = [jnp.where(incl, x[L:], 0.0) for x in m_b]
        m_ak = [jnp.where(strict, x[:L], 0.0) for x in m_k]
        m_rk = [jnp.where(incl, x[L:], 0.0) for x in m_k]
        rk_sum = head_sums([r[i] * k[i] * rk_ref[:, cols[i]] for i in I])
        bonus = [rk_sum[i] * v[i] for i in I]
        yield
        p = m_ab
        tinv = [eye_all + x for x in p]
        m = 2
        while m < L:
            p = [mm(x, bd(x)) for x in p]
            tinv = [tinv[i] + mm(tinv[i], bd(p[i])) for i in I]
            m *= 2
            yield
        x1 = [mm(m_ak[i], v_bd[i]) for i in I]
        u_v = [mm(tinv[i], bd(x1[i])) for i in I]
        a_h = [mm(tinv[i], bd(a_t[i])) for i in I]
        yield
        r_h = [r_t[i] + mm(m_rb[i], bd(a_h[i])) for i in I]
        y_v = [mm(m_rb[i], bd(u_v[i])) + mm(m_rk[i], v_bd[i]) for i in I]
        yield
        p_t = [jnp.where(blockdiag, mm_tn(a_h[i].astype(bf16), b_e[i].astype(bf16)), 0.0).astype(bf16)
               for i in I]
        g_t = [jnp.where(blockdiag,
                         mm_tn(jnp.concatenate([u_v[i], v[i]], axis=0).astype(bf16),
                               jnp.concatenate([b_e[i], k_e[i]], axis=0).astype(bf16)), 0.0) for i in I]
        decay = [jnp.exp(x[L - 1:L, :]) for x in cs]
        for i in I:
            done.append(dict(hg=items[i][0], rows=rows[i], cols=cols[i], r_h=r_h[i], y_v=y_v[i],
                             p_t=p_t[i], g_t=g_t[i], decay=decay[i], bonus=bonus[i]))
        yield

    def chain_step(chs):
        ys = []
        for ch in chs:
            state = states[ch['hg']]
            state_b = state.astype(bf16)
            ys.append(mm_nt(ch['r_h'], state_b) + ch['y_v'])
            states[ch['hg']] = (state * ch['decay']
                                + jnp.dot(state_b, ch['p_t'], preferred_element_type=f32) + ch['g_t'])
        means = head_sums(ys)
        ycs = [y - m * (1.0 / RWKV_HEAD) for y, m in zip(ys, means)]
        vrs = head_sums([yc * yc for yc in ycs])
        for ch, yc, vr in zip(chs, ycs, vrs):
            rw, cl = ch['rows'], ch['cols']
            yn = yc * lax.rsqrt(vr * (1.0 / RWKV_HEAD) + GN_EPS) * lnw_ref[:, cl] + lnb_ref[:, cl]
            o_ref[rw, cl] = ((yn + ch['bonus']) * g_ref[rw, cl].astype(f32)).astype(o_ref.dtype)

    states = [state_ref[h] for h in range(n_hg)]
    done = []
    half = max(n_chunks // 2, 1)
    groups = [[(h, c) for c in range(0, half) for h in range(n_hg)],
              [(h, c) for c in range(half, n_chunks) for h in range(n_hg)]]
    def by_chunk(n_items):
        return [done[j:j + n_hg] for j in range(0, n_items, n_hg)]

    for _ in phase1(groups[0]):
        pass
    pending = by_chunk(len(done))
    for _ in phase1(groups[1]):
        if pending:
            chain_step(pending.pop(0))
    for chs in pending + by_chunk(len(done))[len(groups[0]) // n_hg:]:
        chain_step(chs)
    for h in range(n_hg):
        state_ref[h] = states[h]


def rwkv_scan(proj, lw, k, kk, kka, g, ln_w, ln_b, r_k, *, batch, seq, rows, n_hg):
    nb = seq // rows
    T = batch * seq
    width = n_hg * HG_LANES
    n_col = RWKV_WIDTH // width
    r_blk = POOL_WIDTH // width
    v_blk = (POOL_WIDTH + 2 * RWKV_WIDTH) // width
    blk = pl.BlockSpec((rows, width), lambda b, h, c: (b * nb + c, h))
    par = pl.BlockSpec((1, width), lambda b, h, c: (0, h))
    return pl.pallas_call(
        functools.partial(_rwkv_scan_kernel, n_chunks=rows // RWKV_CHUNK, n_hg=n_hg),
        out_shape=jax.ShapeDtypeStruct((T, RWKV_WIDTH), bf16),
        grid=(batch, n_col, nb),
        in_specs=[pl.BlockSpec((rows, width), lambda b, h, c: (b * nb + c, r_blk + h)),
                  pl.BlockSpec((rows, width), lambda b, h, c: (b * nb + c, v_blk + h))]
                 + [blk] * 5 + [par] * 3,
        out_specs=blk,
        scratch_shapes=[pltpu.VMEM((n_hg, HG_LANES, HG_LANES), f32)],
        compiler_params=_cparams(("parallel", "parallel", "arbitrary")),
        name="rwkv_scan",
    )(proj, proj, lw, k, kk, kka, g, ln_w.reshape(1, -1), ln_b.reshape(1, -1), r_k.reshape(1, -1))


def _route(hn, w, b):
    w_hi = w.astype(bf16)
    w_lo = (w - w_hi.astype(f32)).astype(bf16)
    h_hi = hn.astype(bf16)
    h_lo = (hn - h_hi.astype(f32)).astype(bf16)
    hh = jnp.dot(h_hi, jnp.concatenate([w_hi, w_lo], axis=1), preferred_element_type=f32)
    logits = (hh[:, :ROUTER_LANES] + hh[:, ROUTER_LANES:]
              + jnp.dot(h_lo, w_hi, preferred_element_type=f32)) + b

    lane_i = lax.broadcasted_iota(jnp.int32, logits.shape, 1)
    lane = lane_i.astype(f32)
    neg = -jnp.inf
    big = float(ROUTER_LANES)
    is_g = lane_i < MOE_GROUPS
    gl = jnp.where(is_g, logits, neg)
    gmax = jnp.max(gl, axis=-1, keepdims=True)
    g_idx = jnp.min(jnp.where(gl == gmax, lane, big), axis=-1, keepdims=True)
    g_w = 1.0 / jnp.sum(jnp.exp(gl - gmax), axis=-1, keepdims=True)

    lo_lane = MOE_GROUPS + g_idx * MOE_EXPERTS_PER_GROUP
    sel = (lane >= lo_lane) & (lane < lo_lane + MOE_EXPERTS_PER_GROUP)
    el = jnp.where(sel, logits, neg)
    v1 = jnp.max(el, axis=-1, keepdims=True)
    i1 = jnp.min(jnp.where(el == v1, lane, big), axis=-1, keepdims=True)
    el2 = jnp.where(lane == i1, neg, el)
    v2 = jnp.max(el2, axis=-1, keepdims=True)
    i2 = jnp.min(jnp.where(el2 == v2, lane, big), axis=-1, keepdims=True)
    e2 = jnp.exp(v2 - v1)
    p1 = 1.0 / (1.0 + e2)
    p2 = e2 / (1.0 + e2)
    return (jnp.where(lane_i == 0, i1 - MOE_GROUPS, 0.0)
            + jnp.where(lane_i == 1, i2 - MOE_GROUPS, 0.0)
            + jnp.where(lane_i == 2, p1 * g_w, 0.0)
            + jnp.where(lane_i == 3, p2 * g_w, 0.0))


def _attn_oproj_router_kernel(q_ref, kv_ref, wo_ref, res_ref, g_ref, wr_ref, br_ref, h_ref, xr_ref,
                              *, tn):
    hd = XATTN_HEAD_DIM
    heads = []
    for h in range(XATTN_HEADS):
        k = kv_ref[:, h * hd:(h + 1) * hd]
        v = kv_ref[:, (XATTN_HEADS + h) * hd:(XATTN_HEADS + h + 1) * hd]
        s = lax.dot_general(q_ref[:, h * hd:(h + 1) * hd], k, (((1,), (1,)), ((), ())),
                            preferred_element_type=f32) * (hd ** -0.5)
        e = jnp.exp(s - jnp.max(s, axis=-1, keepdims=True))
        l = jnp.sum(e, axis=-1, keepdims=True)
        heads.append((jnp.dot(e.astype(bf16), v, preferred_element_type=f32) / l).astype(bf16))
    o = jnp.concatenate(heads, axis=1)
    for n0 in range(0, h_ref.shape[1], tn):
        h_ref[:, n0:n0 + tn] = res_ref[:, n0:n0 + tn] + jnp.dot(
            o, wo_ref[:, n0:n0 + tn], preferred_element_type=f32)
    hn = _rmsnorm_rows(h_ref[...], g_ref[...])
    D = hn.shape[1]
    xr_ref[:, :D] = hn
    xr_ref[:, D:] = _route(hn, wr_ref[...], br_ref[...])


def attn_oproj_router(q, kv, w_o, res, g, w_router, b_router, *, tm, tn, seq):
    T, D = res.shape
    tiles_per_seq = seq // tm
    row = lambda width: pl.BlockSpec((tm, width), lambda i: (i, 0))
    return pl.pallas_call(
        functools.partial(_attn_oproj_router_kernel, tn=tn),
        out_shape=[jax.ShapeDtypeStruct((T, D), f32),
                   jax.ShapeDtypeStruct((T, D + ROUTER_LANES), f32)],
        grid=(T // tm,),
        in_specs=[row(D), pl.BlockSpec((MEM_LEN, 2 * D), lambda i: (i // tiles_per_seq, 0)),
                  _resident((D, D)), row(D), _resident((1, D)),
                  _resident((D, ROUTER_LANES)), _resident((1, ROUTER_LANES))],
        out_specs=[row(D), row(D + ROUTER_LANES)],
        compiler_params=_cparams(("parallel",)),
        name="attn_oproj_router",
    )(q, kv, w_o, res, g.reshape(1, D), w_router, b_router)


MOE_PAIRS = ((0, 1), (0, 2), (0, 3), (1, 2), (1, 3), (2, 3))
MOE_CLASSES = MOE_GROUPS * len(MOE_PAIRS)


def _route_metadata(rec, *, tm, n_tiles):
    ids = rec[:, 0:2].astype(jnp.int32)
    lo = jnp.minimum(ids[:, 0], ids[:, 1])
    hi = jnp.maximum(ids[:, 0], ids[:, 1])
    a, b = lo % MOE_EXPERTS_PER_GROUP, hi % MOE_EXPERTS_PER_GROUP
    cls = (lo // MOE_EXPERTS_PER_GROUP) * len(MOE_PAIRS) + (a * (7 - a)) // 2 + b - a - 1
    onehot = (cls[:, None] == jnp.arange(MOE_CLASSES, dtype=jnp.int32)[None, :]).astype(jnp.int32)
    csum = jnp.cumsum(onehot, axis=0)
    counts = csum[-1]
    rank = jnp.sum(csum * onehot, axis=1) - 1
    padded = ((counts + tm - 1) // tm) * tm
    ends = jnp.cumsum(padded)
    pos = jnp.sum(onehot * (ends - padded)[None, :], axis=1) + rank
    tile_start = jnp.arange(n_tiles, dtype=jnp.int32) * tm
    n_valid = ends[-1] // tm
    tile_cls = jnp.sum((tile_start[:, None] >= ends[None, :]).astype(jnp.int32), axis=1)
    last_cls = jnp.sum((ends[-1] - 1 >= ends).astype(jnp.int32))
    tile_cls = jnp.where(tile_start < ends[-1], tile_cls, last_cls)
    pair = jnp.asarray(MOE_PAIRS, jnp.int32)[tile_cls % len(MOE_PAIRS)]
    base = (tile_cls // len(MOE_PAIRS)) * MOE_EXPERTS_PER_GROUP
    idle = n_valid + jnp.arange(MOE_CLASSES, dtype=jnp.int32)
    fill_start = jnp.concatenate([jnp.maximum(ends - tm, 0), jnp.minimum(idle, n_tiles - 1) * tm])
    fill_on = jnp.concatenate([counts > 0, idle < n_tiles]).astype(jnp.int32)
    return pos, base + pair[:, 0], base + pair[:, 1], n_valid.reshape(1), fill_start, fill_on


def _moe_dispatch_kernel(pos_ref, fs_ref, fo_ref, x_ref, xs_hbm, stage, zbuf, sem, zsem, *, tm, tz):
    i = pl.program_id(0)
    n = pl.num_programs(0)

    def fill_copy(c):
        return pltpu.make_async_copy(zbuf, xs_hbm.at[pl.ds(pl.multiple_of(fs_ref[c], tz), tz), :], zsem)

    @pl.when(i == 0)
    def _():
        zbuf[...] = jnp.zeros_like(zbuf)
        for c in range(2 * MOE_CLASSES):
            @pl.when(fo_ref[c] > 0)
            def _():
                fill_copy(c).start()
        for c in range(2 * MOE_CLASSES):
            @pl.when(fo_ref[c] > 0)
            def _():
                fill_copy(c).wait()

    def tile_wait(slot):
        pltpu.make_async_copy(stage.at[slot], xs_hbm.at[pl.ds(0, tm), :], sem.at[slot]).wait()

    slot = i % 2

    @pl.when(i >= 2)
    def _():
        tile_wait(slot)

    stage[slot] = x_ref[...]

    def body(r, carry):
        p = pos_ref[i * tm + r]
        pltpu.make_async_copy(stage.at[slot, pl.ds(r, 1), :], xs_hbm.at[pl.ds(p, 1), :],
                              sem.at[slot]).start()
        return carry
    lax.fori_loop(0, tm, body, 0, unroll=8)

    @pl.when(i == n - 1)
    def _():
        tile_wait(slot)

        @pl.when(n >= 2)
        def _():
            tile_wait(1 - slot)


def moe_dispatch(xr, pos, fill_start, fill_on, *, tm, tz, n_slots):
    T, width = xr.shape
    grid_spec = pltpu.PrefetchScalarGridSpec(
        num_scalar_prefetch=3,
        grid=(T // tm,),
        in_specs=[pl.BlockSpec((tm, width), lambda i, pos, fs, fo: (i, 0))],
        out_specs=pl.BlockSpec(memory_space=pl.ANY),
        scratch_shapes=[pltpu.VMEM((2, tm, width), f32), pltpu.VMEM((tz, width), f32),
                        pltpu.SemaphoreType.DMA((2,)), pltpu.SemaphoreType.DMA(())],
    )
    return pl.pallas_call(
        functools.partial(_moe_dispatch_kernel, tm=tm, tz=tz),
        out_shape=jax.ShapeDtypeStruct((n_slots, width), f32),
        grid_spec=grid_spec,
        compiler_params=_cparams(("arbitrary",)),
        name="moe_dispatch",
    )(pos, fill_start, fill_on, xr)


def _moe_pair_kernel(ea_ref, eb_ref, nv_ref, xs_ref, wga, wua, wda, wgb, wub, wdb, o_ref):
    @pl.when(pl.program_id(0) >= nv_ref[0])
    def _():
        o_ref[...] = jnp.zeros_like(o_ref)

    @pl.when(pl.program_id(0) < nv_ref[0])
    def _():
        D = o_ref.shape[1]
        x = xs_ref[:, :D].astype(bf16)
        rec = xs_ref[:, D:]
        first = rec[:, 0:1] < rec[:, 1:2]
        w_a = jnp.where(first, rec[:, 2:3], rec[:, 3:4])
        w_b = jnp.where(first, rec[:, 3:4], rec[:, 2:3])

        def expert(wg, wu, wd):
            gate = jnp.dot(x, wg[0], preferred_element_type=f32)
            up = jnp.dot(x, wu[0], preferred_element_type=f32)
            hid = gate * _sigmoid(gate) * up
            return jnp.dot(hid.astype(bf16), wd[0], preferred_element_type=f32)

        o_ref[...] = w_a * expert(wga, wua, wda) + w_b * expert(wgb, wub, wdb)


def moe_pair_experts(xs, e_a, e_b, n_valid, wg, wu, wd, *, tm, n_tiles):
    width = xs.shape[1]
    D = wg.shape[1]
    held = lambda i, ea, eb, nv: (jnp.minimum(i, nv[0] - 1), 0)
    tile = lambda i, ea, eb, nv: (i, 0)
    wa = lambda shape: pl.BlockSpec(shape, lambda i, ea, eb, nv: (ea[i], 0, 0))
    wb = lambda shape: pl.BlockSpec(shape, lambda i, ea, eb, nv: (eb[i], 0, 0))
    grid_spec = pltpu.PrefetchScalarGridSpec(
        num_scalar_prefetch=3,
        grid=(n_tiles,),
        in_specs=[pl.BlockSpec((tm, width), held),
                  wa((1, D, MOE_FF)), wa((1, D, MOE_FF)), wa((1, MOE_FF, D)),
                  wb((1, D, MOE_FF)), wb((1, D, MOE_FF)), wb((1, MOE_FF, D))],
        out_specs=pl.BlockSpec((tm, D), tile),
    )
    return pl.pallas_call(
        _moe_pair_kernel,
        out_shape=jax.ShapeDtypeStruct((n_tiles * tm, D), f32),
        grid_spec=grid_spec,
        compiler_params=_cparams(("arbitrary",)),
        name="moe_experts",
    )(e_a, e_b, n_valid, xs, wg, wu, wd, wg, wu, wd)


def _moe_combine_kernel(pos_ref, h_ref, y_hbm, gf_ref, o_ref, ybuf, sem, *, tm):
    i = pl.program_id(0)
    n = pl.num_programs(0)

    def gather(tile, slot):
        def body(r, carry):
            p = pos_ref[tile * tm + r]
            pltpu.make_async_copy(y_hbm.at[pl.ds(p, 1), :],
                                  ybuf.at[slot, pl.ds(r, 1), :], sem.at[slot]).start()
            return carry
        lax.fori_loop(0, tm, body, 0, unroll=8)

    @pl.when(i == 0)
    def _():
        gather(0, 0)

    @pl.when(i + 1 < n)
    def _():
        gather(i + 1, (i + 1) % 2)

    slot = i % 2
    pltpu.make_async_copy(y_hbm.at[pl.ds(0, tm), :], ybuf.at[slot], sem.at[slot]).wait()
    o_ref[...] = _rmsnorm_rows(h_ref[...] + ybuf[slot], gf_ref[...])


def moe_combine(h, y_sorted, pos, g_final, *, tm):
    T, D = h.shape
    grid_spec = pltpu.PrefetchScalarGridSpec(
        num_scalar_prefetch=1,
        grid=(T // tm,),
        in_specs=[
            pl.BlockSpec((tm, D), lambda i, pos: (i, 0)),
            pl.BlockSpec(memory_space=pl.ANY),
            pl.BlockSpec((1, D), lambda i, pos: (0, 0)),
        ],
        out_specs=pl.BlockSpec((tm, D), lambda i, pos: (i, 0)),
        scratch_shapes=[pltpu.VMEM((2, tm, D), f32), pltpu.SemaphoreType.DMA((2,))],
    )
    return pl.pallas_call(
        functools.partial(_moe_combine_kernel, tm=tm),
        out_shape=jax.ShapeDtypeStruct((T, D), f32),
        grid_spec=grid_spec,
        compiler_params=_cparams(("arbitrary",)),
        name="moe_combine",
    )(pos, h, y_sorted, g_final.reshape(1, D))


def _pad_rows(w, rows):
    return jnp.pad(w, ((0, rows - w.shape[0]), (0, 0)))


def _pack_lora_cols(t):
    o1, o2 = DECAY_LORA, DECAY_LORA + AAA_LORA
    pad = lambda a, n: jnp.pad(a, [(0, 0)] * (a.ndim - 1) + [(0, n - a.shape[-1])])
    return jnp.concatenate([pad(t[..., :o1], 128), pad(t[..., o1:o2], 128),
                            pad(t[..., o2:], 256)], axis=-1)


def _layer(x2, mem2, p, *, batch, seq, cfg):
    D = D_MODEL
    w_in = p['w_in']
    w_in_p = jnp.concatenate([w_in[:, :MAIN_COLS], _pack_lora_cols(w_in[:, MAIN_COLS:])],
                             axis=1).astype(bf16)
    mu_cols = jnp.concatenate([jnp.zeros((POOL_WIDTH,), f32), p['rwkv_mu'][:3 * RWKV_WIDTH],
                               _pack_lora_cols(p['rwkv_mu'][3 * RWKV_WIDTH:])])

    proj = in_proj_shift(x2, p['norm_mix_g'], w_in_p, mu_cols, tm=cfg['tm'], tn=cfg['tn_in'], seq=seq)
    pool_out = pool_mixer(proj, p['pool_w'].astype(bf16), p['pool_scale'],
                          batch=batch, seq=seq, ts=cfg['ts_pool'])
    lw, k, kk, kka, g = rwkv_prep(
        proj, p['rwkv_w0'], _pad_rows(p['rwkv_w2'], 128).astype(bf16),
        p['rwkv_a0'], _pad_rows(p['rwkv_a2'], 128).astype(bf16),
        _pad_rows(p['rwkv_g2'], 256).astype(bf16), p['rwkv_k_k'], p['rwkv_k_a'], ts=cfg['ts_prep'])
    rwkv_out = rwkv_scan(proj, lw, k, kk, kka, g, p['rwkv_ln_w'], p['rwkv_ln_b'], p['rwkv_r_k'],
                         batch=batch, seq=seq, rows=cfg['scan_rows'], n_hg=cfg['scan_hg'])
    w_out = p['w_out'].astype(bf16)
    h1 = matmul_residual_resident([pool_out, rwkv_out], [w_out[:POOL_WIDTH], w_out[POOL_WIDTH:]],
                                  x2, tm=cfg['tm'], tn=cfg['tn'], name="out_proj")

    q = norm_matmul_resident(h1, p['norm_xattn_g'], p['xattn_w_q'].astype(bf16),
                             tm=cfg['tm'], tn=cfg['tn'], name="q_proj")
    kv = norm_matmul_resident(mem2, p['norm_mem_g'], p['xattn_w_kv'].astype(bf16),
                              tm=min(cfg['tm'], mem2.shape[0]), tn=cfg['tn'], name="kv_proj")

    w_router = jnp.pad(jnp.concatenate([p['moe_w_group'], p['moe_w_expert']], axis=1),
                       ((0, 0), (0, ROUTER_LANES - MOE_GROUPS - MOE_EXPERTS)))
    b_router = jnp.pad(jnp.concatenate([p['moe_b_group'], p['moe_b_expert']]),
                       (0, ROUTER_LANES - MOE_GROUPS - MOE_EXPERTS)).reshape(1, ROUTER_LANES)
    h2, xr = attn_oproj_router(q, kv, p['xattn_w_o'].astype(bf16), h1, p['norm_ffn_g'],
                               w_router, b_router, tm=cfg['tm'], tn=cfg['tn'], seq=seq)
    tm_e = cfg['tm_moe']
    n_tiles = xr.shape[0] // tm_e + MOE_CLASSES
    pos, e_a, e_b, n_valid, fill_start, fill_on = _route_metadata(
        xr[:, D:D + 2], tm=tm_e, n_tiles=n_tiles)
    xs = moe_dispatch(xr, pos, fill_start, fill_on, tm=cfg['tm_disp'], tz=tm_e,
                      n_slots=n_tiles * tm_e)
    y_sorted = moe_pair_experts(xs, e_a, e_b, n_valid, p['moe_w_gate'].astype(bf16),
                                p['moe_w_up'].astype(bf16), p['moe_w_down'].astype(bf16),
                                tm=tm_e, n_tiles=n_tiles)
    return moe_combine(h2, y_sorted, pos, p['norm_final_g'], tm=cfg['tm_comb'])


_CFG = dict(tm=512, tn=1024, tn_in=512, ts_pool=512, ts_prep=256, scan_rows=256, scan_hg=4,
            tm_moe=256, tm_disp=256, tm_comb=256)


def kernel(x, mem, norm_mix_g, w_in, pool_w, pool_scale, rwkv_mu, rwkv_w0, rwkv_w2, rwkv_a0,
           rwkv_a2, rwkv_g2, rwkv_k_k, rwkv_k_a, rwkv_r_k, rwkv_ln_w, rwkv_ln_b, w_out,
           norm_xattn_g, norm_mem_g, xattn_w_q, xattn_w_kv, xattn_w_o, norm_ffn_g,
           moe_w_group, moe_b_group, moe_w_expert, moe_b_expert, moe_w_gate, moe_w_up,
           moe_w_down, norm_final_g):
    batch, seq, D = x.shape
    p = dict(norm_mix_g=norm_mix_g[0], w_in=w_in[0], pool_w=pool_w[0], pool_scale=pool_scale[0],
             rwkv_mu=rwkv_mu[0], rwkv_w0=rwkv_w0[0], rwkv_w2=rwkv_w2[0], rwkv_a0=rwkv_a0[0],
             rwkv_a2=rwkv_a2[0], rwkv_g2=rwkv_g2[0], rwkv_k_k=rwkv_k_k[0], rwkv_k_a=rwkv_k_a[0],
             rwkv_r_k=rwkv_r_k[0], rwkv_ln_w=rwkv_ln_w[0], rwkv_ln_b=rwkv_ln_b[0], w_out=w_out[0],
             norm_xattn_g=norm_xattn_g[0], norm_mem_g=norm_mem_g[0], xattn_w_q=xattn_w_q[0],
             xattn_w_kv=xattn_w_kv[0], xattn_w_o=xattn_w_o[0], norm_ffn_g=norm_ffn_g[0],
             moe_w_group=moe_w_group[0], moe_b_group=moe_b_group[0], moe_w_expert=moe_w_expert[0],
             moe_b_expert=moe_b_expert[0], moe_w_gate=moe_w_gate[0], moe_w_up=moe_w_up[0],
             moe_w_down=moe_w_down[0], norm_final_g=norm_final_g)
    out = _layer(x.reshape(batch * seq, D), mem.reshape(batch * MEM_LEN, D), p,
                 batch=batch, seq=seq, cfg=_CFG)
    return out.reshape(batch, seq, D)
```

```python
import functools

import jax
import jax.numpy as jnp
from jax import lax
from jax.experimental import pallas as pl
from jax.experimental.pallas import tpu as pltpu

f32 = jnp.float32
bf16 = jnp.bfloat16

D_MODEL = 2048
MEM_LEN = 256
NORM_EPS = 1e-6

POOL_WIDTH = 1024
POOL_WINDOWS = (2, 4, 8, 16)
POOL_GROUP = 256
POOL_HALO = 16
RWKV_WIDTH = 1024
RWKV_HEAD = 64
RWKV_HEADS = 16
GN_EPS = 64e-5
DECAY_SCALE = 0.6065306597126334
DECAY_LORA = 64
AAA_LORA = 64
GATE_LORA = 160
LORA_PAD = 512
LORA_W_OFF, LORA_A_OFF, LORA_G_OFF = 0, 128, 256
MAIN_COLS = POOL_WIDTH + 3 * RWKV_WIDTH
PROJ_COLS = MAIN_COLS + LORA_PAD

RWKV_CHUNK = 64
HEADS_PER_STEP = 4
HG_LANES = HEADS_PER_STEP * RWKV_HEAD

XATTN_HEADS = 4
XATTN_HEAD_DIM = 512

MOE_GROUPS = 4
MOE_EXPERTS_PER_GROUP = 4
MOE_EXPERTS = 16
MOE_FF = 512
ROUTER_LANES = 128

VMEM_LIMIT = 56 * 1024 * 1024


def _cparams(sem):
    return pltpu.CompilerParams(dimension_semantics=sem, vmem_limit_bytes=VMEM_LIMIT)


def _split_dot(x, w_bf16):
    hi = x.astype(bf16)
    lo = (x - hi.astype(f32)).astype(bf16)
    return (jnp.dot(hi, w_bf16, preferred_element_type=f32)
            + jnp.dot(lo, w_bf16, preferred_element_type=f32))


def _resident(shape):
    return pl.BlockSpec(shape, lambda i: (0,) * len(shape), pipeline_mode=pl.Buffered(1))


def _rmsnorm_rows(x, g):
    ms = jnp.mean(x * x, axis=-1, keepdims=True)
    return x * lax.rsqrt(ms + NORM_EPS) * g


def _norm_mm_res_kernel(x_ref, g_ref, w_ref, o_ref, *, tn):
    xn = _rmsnorm_rows(x_ref[...], g_ref[...]).astype(bf16)
    for n0 in range(0, o_ref.shape[1], tn):
        o_ref[:, n0:n0 + tn] = jnp.dot(xn, w_ref[:, n0:n0 + tn],
                                       preferred_element_type=f32).astype(o_ref.dtype)


def norm_matmul_resident(x, g, w, *, tm, tn, name):
    M, K = x.shape
    N = w.shape[1]
    return pl.pallas_call(
        functools.partial(_norm_mm_res_kernel, tn=tn),
        out_shape=jax.ShapeDtypeStruct((M, N), bf16),
        grid=(M // tm,),
        in_specs=[pl.BlockSpec((tm, K), lambda i: (i, 0)), _resident((1, K)), _resident((K, N))],
        out_specs=pl.BlockSpec((tm, N), lambda i: (i, 0)),
        compiler_params=_cparams(("parallel",)),
        name=name,
    )(x, g.reshape(1, K), w)


def _in_proj_kernel(x_ref, xh_ref, g_ref, w_ref, mu_ref, o_ref, *, tn, tiles_per_seq):
    first = pl.program_id(0) % tiles_per_seq == 0
    xn = _rmsnorm_rows(x_ref[...], g_ref[...]).astype(bf16)
    xh = jnp.where(first, 0.0, _rmsnorm_rows(xh_ref[...], g_ref[...])).astype(bf16)
    lhs = jnp.concatenate([xh, xn], axis=0)
    for n0 in range(0, o_ref.shape[1], tn):
        res = jnp.dot(lhs, w_ref[:, n0:n0 + tn], preferred_element_type=f32)
        p = res[POOL_HALO:]
        if n0 >= POOL_WIDTH:
            prev = pltpu.roll(res, 1, 0)[POOL_HALO:]
            p = p + (prev - p) * mu_ref[:, n0:n0 + tn]
        o_ref[:, n0:n0 + tn] = p.astype(o_ref.dtype)


def in_proj_shift(x, g, w, mu_cols, *, tm, tn, seq):
    M, K = x.shape
    N = w.shape[1]
    hb = tm // POOL_HALO
    return pl.pallas_call(
        functools.partial(_in_proj_kernel, tn=tn, tiles_per_seq=seq // tm),
        out_shape=jax.ShapeDtypeStruct((M, N), bf16),
        grid=(M // tm,),
        in_specs=[pl.BlockSpec((tm, K), lambda i: (i, 0)),
                  pl.BlockSpec((POOL_HALO, K), lambda i: (jnp.maximum(i * hb - 1, 0), 0)),
                  _resident((1, K)), _resident((K, N)), _resident((1, N))],
        out_specs=pl.BlockSpec((tm, N), lambda i: (i, 0)),
        compiler_params=_cparams(("parallel",)),
        name="in_proj",
    )(x, x, g.reshape(1, K), w, mu_cols.reshape(1, N))


def _mm_res_resident_kernel(*refs, n, tn):
    a_refs, w_refs, res_ref, o_ref = refs[:n], refs[n:2 * n], refs[2 * n], refs[2 * n + 1]
    for n0 in range(0, o_ref.shape[1], tn):
        acc = res_ref[:, n0:n0 + tn]
        for a_ref, w_ref in zip(a_refs, w_refs):
            acc = acc + jnp.dot(a_ref[...], w_ref[:, n0:n0 + tn], preferred_element_type=f32)
        o_ref[:, n0:n0 + tn] = acc


def matmul_residual_resident(a_list, w_list, res, *, tm, tn, name):
    M, N = res.shape
    n = len(a_list)
    return pl.pallas_call(
        functools.partial(_mm_res_resident_kernel, n=n, tn=tn),
        out_shape=jax.ShapeDtypeStruct((M, N), f32),
        grid=(M // tm,),
        in_specs=([pl.BlockSpec((tm, a.shape[1]), lambda i: (i, 0)) for a in a_list]
                  + [_resident(w.shape) for w in w_list]
                  + [pl.BlockSpec((tm, N), lambda i: (i, 0))]),
        out_specs=pl.BlockSpec((tm, N), lambda i: (i, 0)),
        compiler_params=_cparams(("parallel",)),
        name=name,
    )(*a_list, *w_list, res)


def _pool_kernel(u_ref, halo_ref, pw_ref, ps_ref, o_ref, *, ts):
    s = pl.program_id(1)
    u = u_ref[...].astype(f32)
    halo = jnp.where(s > 0, halo_ref[...].astype(f32), 0.0)
    ext = jnp.concatenate([halo, u], axis=0)
    pos = s * ts + lax.broadcasted_iota(jnp.int32, (ts, 1), 0)
    acc = ext
    shift = 1
    for gi, w in enumerate(POOL_WINDOWS):
        while shift < w:
            acc = acc + pltpu.roll(acc, shift, 0)
            shift *= 2
        lo, hi = gi * POOL_GROUP, (gi + 1) * POOL_GROUP
        cnt = jnp.minimum(pos + 1, w).astype(f32)
        pooled = acc[POOL_HALO:, lo:hi] / cnt - u[:, lo:hi]
        mixed = jnp.dot(pooled.astype(bf16), pw_ref[gi], preferred_element_type=f32)
        o_ref[:, lo:hi] = (mixed * ps_ref[:, lo:hi]).astype(o_ref.dtype)


def pool_mixer(proj, pool_w, pool_scale, *, batch, seq, ts):
    ns = seq // ts
    hb = ts // POOL_HALO
    return pl.pallas_call(
        functools.partial(_pool_kernel, ts=ts),
        out_shape=jax.ShapeDtypeStruct((batch * seq, POOL_WIDTH), bf16),
        grid=(batch, ns),
        in_specs=[
            pl.BlockSpec((ts, POOL_WIDTH), lambda b, s: (b * ns + s, 0)),
            pl.BlockSpec((POOL_HALO, POOL_WIDTH),
                         lambda b, s: (jnp.maximum((b * ns + s) * hb - 1, 0), 0)),
            pl.BlockSpec((len(POOL_WINDOWS), POOL_GROUP, POOL_GROUP), lambda b, s: (0, 0, 0)),
            pl.BlockSpec((1, POOL_WIDTH), lambda b, s: (0, 0)),
        ],
        out_specs=pl.BlockSpec((ts, POOL_WIDTH), lambda b, s: (b * ns + s, 0)),
        compiler_params=_cparams(("parallel", "arbitrary")),
        name="pool_mixer",
    )(proj, proj, pool_w, pool_scale.reshape(1, POOL_WIDTH))


def _head_ones():
    r = lax.broadcasted_iota(jnp.int32, (HG_LANES, HG_LANES), 0) // RWKV_HEAD
    c = lax.broadcasted_iota(jnp.int32, (HG_LANES, HG_LANES), 1) // RWKV_HEAD
    return jnp.where(r == c, 1.0, 0.0).astype(bf16)


def _head_sum(x, ones_bd):
    parts = [_split_dot(x[:, c:c + HG_LANES], ones_bd) for c in range(0, x.shape[1], HG_LANES)]
    return parts[0] if len(parts) == 1 else jnp.concatenate(parts, axis=1)


def _sigmoid(x):
    return 1.0 / (1.0 + jnp.exp(-x))


def _rwkv_prep_kernel(k_ref, lo_ref, w0_ref, w2_ref, a0_ref, a2_ref, g2_ref, kk_ref, ka_ref,
                      lw_out, k_out, kk_out, kka_out, g_out):
    k = k_ref[...].astype(f32)
    lo = lo_ref[...].astype(f32)
    w_lo = lo[:, LORA_W_OFF:LORA_W_OFF + 128]
    a_lo = lo[:, LORA_A_OFF:LORA_A_OFF + 128]
    g_lo = lo[:, LORA_G_OFF:LORA_G_OFF + 256]

    wx = w0_ref[...] + jnp.dot(jnp.tanh(w_lo).astype(bf16), w2_ref[...], preferred_element_type=f32)
    lw_out[...] = -DECAY_SCALE * _sigmoid(wx)
    a = _sigmoid(a0_ref[...] + jnp.dot(a_lo.astype(bf16), a2_ref[...], preferred_element_type=f32))
    g = jnp.dot(_sigmoid(g_lo).astype(bf16), g2_ref[...], preferred_element_type=f32)

    ones_bd = _head_ones()
    kk = k * kk_ref[...]
    kk = kk * lax.rsqrt(jnp.maximum(_head_sum(kk * kk, ones_bd), 1e-24))
    k = k * (1.0 + (a - 1.0) * ka_ref[...])

    k_out[...] = k.astype(k_out.dtype)
    kk_out[...] = kk.astype(kk_out.dtype)
    kka_out[...] = (kk * a).astype(kka_out.dtype)
    g_out[...] = g.astype(g_out.dtype)


def rwkv_prep(proj, w0, w2p, a0, a2p, g2p, k_k, k_a, *, ts):
    T = proj.shape[0]
    C = RWKV_WIDTH
    k_blk = (POOL_WIDTH + C) // C
    lora_blk = MAIN_COLS // LORA_PAD

    def full(shape):
        return pl.BlockSpec(shape, lambda i: (0,) * len(shape))

    row = lambda t: t.reshape(1, -1)
    out_sd = lambda dt: jax.ShapeDtypeStruct((T, C), dt)
    out_spec = pl.BlockSpec((ts, C), lambda i: (i, 0))
    return pl.pallas_call(
        _rwkv_prep_kernel,
        out_shape=[out_sd(f32), out_sd(bf16), out_sd(bf16), out_sd(bf16), out_sd(bf16)],
        grid=(T // ts,),
        in_specs=[
            pl.BlockSpec((ts, C), lambda i: (i, k_blk)),
            pl.BlockSpec((ts, LORA_PAD), lambda i: (i, lora_blk)),
            full((1, C)), full((128, C)), full((1, C)), full((128, C)), full((256, C)),
            full((1, C)), full((1, C)),
        ],
        out_specs=[out_spec] * 5,
        compiler_params=_cparams(("parallel",)),
        name="rwkv_prep",
    )(proj, proj, row(w0), w2p, row(a0), a2p, g2p, row(k_k), row(k_a))


def _bd_mask():
    r = lax.broadcasted_iota(jnp.int32, (HG_LANES, HG_LANES), 0) // RWKV_CHUNK
    c = lax.broadcasted_iota(jnp.int32, (HG_LANES, HG_LANES), 1) // RWKV_HEAD
    return r == c


def _rwkv_scan_kernel(r_ref, v_ref, lw_ref, k_ref, kk_ref, kka_ref, g_ref,
                      lnw_ref, lnb_ref, rk_ref, o_ref, state_ref, *, n_chunks, n_hg):
    L = RWKV_CHUNK
    W = HG_LANES

    @pl.when(pl.program_id(2) == 0)
    def _():
        state_ref[...] = jnp.zeros_like(state_ref)

    bd_mask = _bd_mask()
    ones_bd = _head_ones()

    def bd(x):
        tiled = jnp.concatenate([x] * HEADS_PER_STEP, axis=0)
        return jnp.where(bd_mask, tiled, 0.0).astype(bf16)

    def mm(a, b_bf16):
        return jnp.dot(a.astype(bf16), b_bf16, preferred_element_type=f32)

    def mm_nt(a, b_bf16):
        return lax.dot_general(a.astype(bf16), b_bf16, (((1,), (1,)), ((), ())),
                               preferred_element_type=f32)

    def mm_tn(a_bf16, b_bf16):
        return lax.dot_general(a_bf16, b_bf16, (((0,), (0,)), ((), ())),
                               preferred_element_type=f32)

    def split(x):
        hi = x.astype(bf16)
        return hi, (x - hi.astype(f32)).astype(bf16)

    def head_sums(xs):
        tot = mm(jnp.concatenate(xs, axis=0), ones_bd)
        return [tot[j * L:(j + 1) * L] for j in range(len(xs))]

    t_idx = lax.broadcasted_iota(jnp.int32, (L, W), 0)
    s_idx = lax.broadcasted_iota(jnp.int32, (L, W), 1) % L
    strict = t_idx > s_idx
    incl = t_idx >= s_idx
    eye_all = jnp.where(t_idx == s_idx, 1.0, 0.0)
    tri = jnp.where(lax.broadcasted_iota(jnp.int32, (L, L), 0)
                    >= lax.broadcasted_iota(jnp.int32, (L, L), 1), 1.0, 0.0).astype(bf16)
    blockdiag = (lax.broadcasted_iota(jnp.int32, (W, W), 0) // RWKV_HEAD
                 == lax.broadcasted_iota(jnp.int32, (W, W), 1) // RWKV_HEAD)

    def phase1(items):
        I = range(len(items))
        rows = [pl.ds(c * L, L) for _, c in items]
        cols = [pl.ds(h * W, W) for h, _ in items]
        ld = lambda ref, i: ref[rows[i], cols[i]]
        r = [ld(r_ref, i).astype(f32) for i in I]
        k = [ld(k_ref, i).astype(f32) for i in I]
        v = [ld(v_ref, i).astype(f32) for i in I]
        lw = [ld(lw_ref, i) for i in I]
        lw_s = [split(x) for x in lw]
        cs = [jnp.dot(tri, hi, preferred_element_type=f32) + jnp.dot(tri, lo, preferred_element_type=f32)
              for hi, lo in lw_s]
        yield
        g_inv = [jnp.exp(-x) for x in cs]
        to_end = [jnp.exp(x[L - 1:L, :] - x) for x in cs]
        a_t = [-ld(kk_ref, i).astype(f32) * jnp.exp(cs[i] - lw[i]) for i in I]
        r_t = [r[i] * jnp.exp(cs[i]) for i in I]
        kka = [ld(kka_ref, i).astype(f32) for i in I]
        b_t = [kka[i] * g_inv[i] for i in I]
        k_t = [k[i] * g_inv[i] for i in I]
        b_e = [kka[i] * to_end[i] for i in I]
        k_e = [k[i] * to_end[i] for i in I]
        v_bd = [bd(x) for x in v]
        ar = [jnp.concatenate([a_t[i], r_t[i]], axis=0) for i in I]
        m_b = [mm_nt(ar[i], bd(b_t[i])) for i in I]
        m_k = [mm_nt(ar[i], bd(k_t[i])) for i in I]
        m_ab = [jnp.where(strict, x[:L], 0.0) for x in m_b]
        m_rb = [jnp.where(incl, x[L:], 0.0) for x in m_b]
        m_ak = [jnp.where(strict, x[:L], 0.0) for x in m_k]
        m_rk = [jnp.where(incl, x[L:], 0.0) for x in m_k]
        rk_sum = head_sums([r[i] * k[i] * rk_ref[:, cols[i]] for i in I])
        bonus = [rk_sum[i] * v[i] for i in I]
        yield
        p = m_ab
        tinv = [eye_all + x for x in p]
        m = 2
        while m < L:
            p = [mm(x, bd(x)) for x in p]
            tinv = [tinv[i] + mm(tinv[i], bd(p[i])) for i in I]
            m *= 2
            yield
        x1 = [mm(m_ak[i], v_bd[i]) for i in I]
        u_v = [mm(tinv[i], bd(x1[i])) for i in I]
        a_h = [mm(tinv[i], bd(a_t[i])) for i in I]
        yield
        r_h = [r_t[i] + mm(m_rb[i], bd(a_h[i])) for i in I]
        y_v = [mm(m_rb[i], bd(u_v[i])) + mm(m_rk[i], v_bd[i]) for i in I]
        yield
        p_t = [jnp.where(blockdiag, mm_tn(a_h[i].astype(bf16), b_e[i].astype(bf16)), 0.0).astype(bf16)
               for i in I]
        g_t = [jnp.where(blockdiag,
                         mm_tn(jnp.concatenate([u_v[i], v[i]], axis=0).astype(bf16),
                               jnp.concatenate([b_e[i], k_e[i]], axis=0).astype(bf16)), 0.0) for i in I]
        decay = [jnp.exp(x[L - 1:L, :]) for x in cs]
        for i in I:
            done.append(dict(hg=items[i][0], rows=rows[i], cols=cols[i], r_h=r_h[i], y_v=y_v[i],
                             p_t=p_t[i], g_t=g_t[i], decay=decay[i], bonus=bonus[i]))
        yield

    def chain_step(chs):
        ys = []
        for ch in chs:
            state = states[ch['hg']]
            state_b = state.astype(bf16)
            ys.append(mm_nt(ch['r_h'], state_b) + ch['y_v'])
            states[ch['hg']] = (state * ch['decay']
                                + jnp.dot(state_b, ch['p_t'], preferred_element_type=f32) + ch['g_t'])
        means = head_sums(ys)
        ycs = [y - m * (1.0 / RWKV_HEAD) for y, m in zip(ys, means)]
        vrs = head_sums([yc * yc for yc in ycs])
        for ch, yc, vr in zip(chs, ycs, vrs):
            rw, cl = ch['rows'], ch['cols']
            yn = yc * lax.rsqrt(vr * (1.0 / RWKV_HEAD) + GN_EPS) * lnw_ref[:, cl] + lnb_ref[:, cl]
            o_ref[rw, cl] = ((yn + ch['bonus']) * g_ref[rw, cl].astype(f32)).astype(o_ref.dtype)

    states = [state_ref[h] for h in range(n_hg)]
    done = []
    half = max(n_chunks // 2, 1)
    groups = [[(h, c) for c in range(0, half) for h in range(n_hg)],
              [(h, c) for c in range(half, n_chunks) for h in range(n_hg)]]
    def by_chunk(n_items):
        return [done[j:j + n_hg] for j in range(0, n_items, n_hg)]

    for _ in phase1(groups[0]):
        pass
    pending = by_chunk(len(done))
    for _ in phase1(groups[1]):
        if pending:
            chain_step(pending.pop(0))
    for chs in pending + by_chunk(len(done))[len(groups[0]) // n_hg:]:
        chain_step(chs)
    for h in range(n_hg):
        state_ref[h] = states[h]


def rwkv_scan(proj, lw, k, kk, kka, g, ln_w, ln_b, r_k, *, batch, seq, rows, n_hg):
    nb = seq // rows
    T = batch * seq
    width = n_hg * HG_LANES
    n_col = RWKV_WIDTH // width
    r_blk = POOL_WIDTH // width
    v_blk = (POOL_WIDTH + 2 * RWKV_WIDTH) // width
    blk = pl.BlockSpec((rows, width), lambda b, h, c: (b * nb + c, h))
    par = pl.BlockSpec((1, width), lambda b, h, c: (0, h))
    return pl.pallas_call(
        functools.partial(_rwkv_scan_kernel, n_chunks=rows // RWKV_CHUNK, n_hg=n_hg),
        out_shape=jax.ShapeDtypeStruct((T, RWKV_WIDTH), bf16),
        grid=(batch, n_col, nb),
        in_specs=[pl.BlockSpec((rows, width), lambda b, h, c: (b * nb + c, r_blk + h)),
                  pl.BlockSpec((rows, width), lambda b, h, c: (b * nb + c, v_blk + h))]
                 + [blk] * 5 + [par] * 3,
        out_specs=blk,
        scratch_shapes=[pltpu.VMEM((n_hg, HG_LANES, HG_LANES), f32)],
        compiler_params=_cparams(("parallel", "parallel", "arbitrary")),
        name="rwkv_scan",
    )(proj, proj, lw, k, kk, kka, g, ln_w.reshape(1, -1), ln_b.reshape(1, -1), r_k.reshape(1, -1))


def _route(hn, w, b):
    w_hi = w.astype(bf16)
    w_lo = (w - w_hi.astype(f32)).astype(bf16)
    h_hi = hn.astype(bf16)
    h_lo = (hn - h_hi.astype(f32)).astype(bf16)
    hh = jnp.dot(h_hi, jnp.concatenate([w_hi, w_lo], axis=1), preferred_element_type=f32)
    logits = (hh[:, :ROUTER_LANES] + hh[:, ROUTER_LANES:]
              + jnp.dot(h_lo, w_hi, preferred_element_type=f32)) + b

    lane_i = lax.broadcasted_iota(jnp.int32, logits.shape, 1)
    lane = lane_i.astype(f32)
    neg = -jnp.inf
    big = float(ROUTER_LANES)
    is_g = lane_i < MOE_GROUPS
    gl = jnp.where(is_g, logits, neg)
    gmax = jnp.max(gl, axis=-1, keepdims=True)
    g_idx = jnp.min(jnp.where(gl == gmax, lane, big), axis=-1, keepdims=True)
    g_w = 1.0 / jnp.sum(jnp.exp(gl - gmax), axis=-1, keepdims=True)

    lo_lane = MOE_GROUPS + g_idx * MOE_EXPERTS_PER_GROUP
    sel = (lane >= lo_lane) & (lane < lo_lane + MOE_EXPERTS_PER_GROUP)
    el = jnp.where(sel, logits, neg)
    v1 = jnp.max(el, axis=-1, keepdims=True)
    i1 = jnp.min(jnp.where(el == v1, lane, big), axis=-1, keepdims=True)
    el2 = jnp.where(lane == i1, neg, el)
    v2 = jnp.max(el2, axis=-1, keepdims=True)
    i2 = jnp.min(jnp.where(el2 == v2, lane, big), axis=-1, keepdims=True)
    e2 = jnp.exp(v2 - v1)
    p1 = 1.0 / (1.0 + e2)
    p2 = e2 / (1.0 + e2)
    return (jnp.where(lane_i == 0, i1 - MOE_GROUPS, 0.0)
            + jnp.where(lane_i == 1, i2 - MOE_GROUPS, 0.0)
            + jnp.where(lane_i == 2, p1 * g_w, 0.0)
            + jnp.where(lane_i == 3, p2 * g_w, 0.0))


def _attn_oproj_router_kernel(q_ref, kv_ref, wo_ref, res_ref, g_ref, wr_ref, br_ref, h_ref, xr_ref,
                              rec_ref, *, tn):
    hd = XATTN_HEAD_DIM
    heads = []
    for h in range(XATTN_HEADS):
        k = kv_ref[:, h * hd:(h + 1) * hd]
        v = kv_ref[:, (XATTN_HEADS + h) * hd:(XATTN_HEADS + h + 1) * hd]
        s = lax.dot_general(q_ref[:, h * hd:(h + 1) * hd], k, (((1,), (1,)), ((), ())),
                            preferred_element_type=f32) * (hd ** -0.5)
        e = jnp.exp(s - jnp.max(s, axis=-1, keepdims=True))
        l = jnp.sum(e, axis=-1, keepdims=True)
        heads.append((jnp.dot(e.astype(bf16), v, preferred_element_type=f32) / l).astype(bf16))
    o = jnp.concatenate(heads, axis=1)
    for n0 in range(0, h_ref.shape[1], tn):
        h_ref[:, n0:n0 + tn] = res_ref[:, n0:n0 + tn] + jnp.dot(
            o, wo_ref[:, n0:n0 + tn], preferred_element_type=f32)
    hn = _rmsnorm_rows(h_ref[...], g_ref[...])
    D = hn.shape[1]
    rec = _route(hn, wr_ref[...], br_ref[...])
    xr_ref[:, :D] = hn
    xr_ref[:, D:] = rec
    rec_ref[...] = rec


def attn_oproj_router(q, kv, w_o, res, g, w_router, b_router, *, tm, tn, seq):
    T, D = res.shape
    tiles_per_seq = seq // tm
    row = lambda width: pl.BlockSpec((tm, width), lambda i: (i, 0))
    return pl.pallas_call(
        functools.partial(_attn_oproj_router_kernel, tn=tn),
        out_shape=[jax.ShapeDtypeStruct((T, D), f32),
                   jax.ShapeDtypeStruct((T, D + ROUTER_LANES), f32),
                   jax.ShapeDtypeStruct((T, ROUTER_LANES), f32)],
        grid=(T // tm,),
        in_specs=[row(D), pl.BlockSpec((MEM_LEN, 2 * D), lambda i: (i // tiles_per_seq, 0)),
                  _resident((D, D)), row(D), _resident((1, D)),
                  _resident((D, ROUTER_LANES)), _resident((1, ROUTER_LANES))],
        out_specs=[row(D), row(D + ROUTER_LANES), row(ROUTER_LANES)],
        compiler_params=_cparams(("parallel",)),
        name="attn_oproj_router",
    )(q, kv, w_o, res, g.reshape(1, D), w_router, b_router)


MOE_PAIRS = ((0, 1), (0, 2), (0, 3), (1, 3), (1, 2), (3, 2))
MOE_CLASSES = MOE_GROUPS * len(MOE_PAIRS)


def _route_metadata(rec, *, tm, n_tiles):
    ids = rec[:, 0:2].astype(jnp.int32)
    lo = jnp.minimum(ids[:, 0], ids[:, 1])
    hi = jnp.maximum(ids[:, 0], ids[:, 1])
    a, b = lo % MOE_EXPERTS_PER_GROUP, hi % MOE_EXPERTS_PER_GROUP
    pair_id = (a * (7 - a)) // 2 + b - a - 1
    pair_id = pair_id + (pair_id == 3).astype(jnp.int32) - (pair_id == 4).astype(jnp.int32)
    cls = (lo // MOE_EXPERTS_PER_GROUP) * len(MOE_PAIRS) + pair_id
    onehot = (cls[:, None] == jnp.arange(MOE_CLASSES, dtype=jnp.int32)[None, :]).astype(jnp.int32)
    csum = jnp.cumsum(onehot, axis=0)
    counts = csum[-1]
    rank = jnp.sum(csum * onehot, axis=1) - 1
    padded = ((counts + tm - 1) // tm) * tm
    ends = jnp.cumsum(padded)
    pos = jnp.sum(onehot * (ends - padded)[None, :], axis=1) + rank
    tile_start = jnp.arange(n_tiles, dtype=jnp.int32) * tm
    n_valid = ends[-1] // tm
    tile_cls = jnp.sum((tile_start[:, None] >= ends[None, :]).astype(jnp.int32), axis=1)
    last_cls = jnp.sum((ends[-1] - 1 >= ends).astype(jnp.int32))
    tile_cls = jnp.where(tile_start < ends[-1], tile_cls, last_cls)
    pair = jnp.asarray(MOE_PAIRS, jnp.int32)[tile_cls % len(MOE_PAIRS)]
    base = (tile_cls // len(MOE_PAIRS)) * MOE_EXPERTS_PER_GROUP
    idle = n_valid + jnp.arange(MOE_CLASSES, dtype=jnp.int32)
    fill_start = jnp.concatenate([jnp.maximum(ends - tm, 0), jnp.minimum(idle, n_tiles - 1) * tm])
    fill_on = jnp.concatenate([counts > 0, idle < n_tiles]).astype(jnp.int32)
    return pos, base + pair[:, 0], base + pair[:, 1], n_valid.reshape(1), fill_start, fill_on


def _moe_dispatch_kernel(pos_ref, fs_ref, fo_ref, x_ref, xs_hbm, stage, zbuf, sem, zsem, *, tm, tz):
    i = pl.program_id(0)
    n = pl.num_programs(0)

    def fill_copy(c):
        return pltpu.make_async_copy(zbuf, xs_hbm.at[pl.ds(pl.multiple_of(fs_ref[c], tz), tz), :], zsem)

    @pl.when(i == 0)
    def _():
        zbuf[...] = jnp.zeros_like(zbuf)
        for c in range(2 * MOE_CLASSES):
            @pl.when(fo_ref[c] > 0)
            def _():
                fill_copy(c).start()
        for c in range(2 * MOE_CLASSES):
            @pl.when(fo_ref[c] > 0)
            def _():
                fill_copy(c).wait()

    def tile_wait(slot):
        pltpu.make_async_copy(stage.at[slot], xs_hbm.at[pl.ds(0, tm), :], sem.at[slot]).wait()

    slot = i % 2

    @pl.when(i >= 2)
    def _():
        tile_wait(slot)

    stage[slot] = x_ref[...]

    def body(r, carry):
        p = pos_ref[i * tm + r]
        pltpu.make_async_copy(stage.at[slot, pl.ds(r, 1), :], xs_hbm.at[pl.ds(p, 1), :],
                              sem.at[slot]).start()
        return carry
    lax.fori_loop(0, tm, body, 0, unroll=8)

    @pl.when(i == n - 1)
    def _():
        tile_wait(slot)

        @pl.when(n >= 2)
        def _():
            tile_wait(1 - slot)


def moe_dispatch(xr, pos, fill_start, fill_on, *, tm, tz, n_slots):
    T, width = xr.shape
    grid_spec = pltpu.PrefetchScalarGridSpec(
        num_scalar_prefetch=3,
        grid=(T // tm,),
        in_specs=[pl.BlockSpec((tm, width), lambda i, pos, fs, fo: (i, 0))],
        out_specs=pl.BlockSpec(memory_space=pl.ANY),
        scratch_shapes=[pltpu.VMEM((2, tm, width), f32), pltpu.VMEM((tz, width), f32),
                        pltpu.SemaphoreType.DMA((2,)), pltpu.SemaphoreType.DMA(())],
    )
    return pl.pallas_call(
        functools.partial(_moe_dispatch_kernel, tm=tm, tz=tz),
        out_shape=jax.ShapeDtypeStruct((n_slots, width), f32),
        grid_spec=grid_spec,
        compiler_params=_cparams(("arbitrary",)),
        name="moe_dispatch",
    )(pos, fill_start, fill_on, xr)


def _moe_pair_kernel(ea_ref, eb_ref, nv_ref, xs_ref, wga, wua, wda, wgb, wub, wdb, o_ref):
    @pl.when(pl.program_id(0) >= nv_ref[0])
    def _():
        o_ref[...] = jnp.zeros_like(o_ref)

    @pl.when(pl.program_id(0) < nv_ref[0])
    def _():
        D = o_ref.shape[1]
        x = xs_ref[:, :D].astype(bf16)
        rec = xs_ref[:, D:]
        first = rec[:, 0:1] == ea_ref[pl.program_id(0)].astype(f32)
        w_a = jnp.where(first, rec[:, 2:3], rec[:, 3:4])
        w_b = jnp.where(first, rec[:, 3:4], rec[:, 2:3])

        def expert(wg, wu, wd):
            gate = jnp.dot(x, wg[0], preferred_element_type=f32)
            up = jnp.dot(x, wu[0], preferred_element_type=f32)
            hid = gate * _sigmoid(gate) * up
            return jnp.dot(hid.astype(bf16), wd[0], preferred_element_type=f32)

        o_ref[...] = w_a * expert(wga, wua, wda) + w_b * expert(wgb, wub, wdb)


def moe_pair_experts(xs, e_a, e_b, n_valid, wg, wu, wd, *, tm, n_tiles):
    width = xs.shape[1]
    D = wg.shape[1]
    held = lambda i, ea, eb, nv: (jnp.minimum(i, nv[0] - 1), 0)
    tile = lambda i, ea, eb, nv: (i, 0)
    wa = lambda shape: pl.BlockSpec(shape, lambda i, ea, eb, nv: (ea[i], 0, 0))
    wb = lambda shape: pl.BlockSpec(shape, lambda i, ea, eb, nv: (eb[i], 0, 0))
    grid_spec = pltpu.PrefetchScalarGridSpec(
        num_scalar_prefetch=3,
        grid=(n_tiles,),
        in_specs=[pl.BlockSpec((tm, width), held),
                  wa((1, D, MOE_FF)), wa((1, D, MOE_FF)), wa((1, MOE_FF, D)),
                  wb((1, D, MOE_FF)), wb((1, D, MOE_FF)), wb((1, MOE_FF, D))],
        out_specs=pl.BlockSpec((tm, D), tile),
    )
    return pl.pallas_call(
        _moe_pair_kernel,
        out_shape=jax.ShapeDtypeStruct((n_tiles * tm, D), f32),
        grid_spec=grid_spec,
        compiler_params=_cparams(("arbitrary",)),
        name="moe_experts",
    )(e_a, e_b, n_valid, xs, wg, wu, wd, wg, wu, wd)


def _moe_combine_kernel(pos_ref, h_ref, y_hbm, gf_ref, o_ref, ybuf, sem, *, tm):
    i = pl.program_id(0)
    n = pl.num_programs(0)

    def gather(tile, slot):
        def body(r, carry):
            p = pos_ref[tile * tm + r]
            pltpu.make_async_copy(y_hbm.at[pl.ds(p, 1), :],
                                  ybuf.at[slot, pl.ds(r, 1), :], sem.at[slot]).start()
            return carry
        lax.fori_loop(0, tm, body, 0, unroll=8)

    @pl.when(i == 0)
    def _():
        gather(0, 0)

    @pl.when(i + 1 < n)
    def _():
        gather(i + 1, (i + 1) % 2)

    slot = i % 2
    pltpu.make_async_copy(y_hbm.at[pl.ds(0, tm), :], ybuf.at[slot], sem.at[slot]).wait()
    o_ref[...] = _rmsnorm_rows(h_ref[...] + ybuf[slot], gf_ref[...])


def moe_combine(h, y_sorted, pos, g_final, *, tm):
    T, D = h.shape
    grid_spec = pltpu.PrefetchScalarGridSpec(
        num_scalar_prefetch=1,
        grid=(T // tm,),
        in_specs=[
            pl.BlockSpec((tm, D), lambda i, pos: (i, 0)),
            pl.BlockSpec(memory_space=pl.ANY),
            pl.BlockSpec((1, D), lambda i, pos: (0, 0)),
        ],
        out_specs=pl.BlockSpec((tm, D), lambda i, pos: (i, 0)),
        scratch_shapes=[pltpu.VMEM((2, tm, D), f32), pltpu.SemaphoreType.DMA((2,))],
    )
    return pl.pallas_call(
        functools.partial(_moe_combine_kernel, tm=tm),
        out_shape=jax.ShapeDtypeStruct((T, D), f32),
        grid_spec=grid_spec,
        compiler_params=_cparams(("arbitrary",)),
        name="moe_combine",
    )(pos, h, y_sorted, g_final.reshape(1, D))


def _pad_rows(w, rows):
    return jnp.pad(w, ((0, rows - w.shape[0]), (0, 0)))


def _pack_lora_cols(t):
    o1, o2 = DECAY_LORA, DECAY_LORA + AAA_LORA
    pad = lambda a, n: jnp.pad(a, [(0, 0)] * (a.ndim - 1) + [(0, n - a.shape[-1])])
    return jnp.concatenate([pad(t[..., :o1], 128), pad(t[..., o1:o2], 128),
                            pad(t[..., o2:], 256)], axis=-1)


def _layer(x2, mem2, p, *, batch, seq, cfg):
    D = D_MODEL
    w_in = p['w_in']
    w_in_p = jnp.concatenate([w_in[:, :MAIN_COLS], _pack_lora_cols(w_in[:, MAIN_COLS:])],
                             axis=1).astype(bf16)
    mu_cols = jnp.concatenate([jnp.zeros((POOL_WIDTH,), f32), p['rwkv_mu'][:3 * RWKV_WIDTH],
                               _pack_lora_cols(p['rwkv_mu'][3 * RWKV_WIDTH:])])

    proj = in_proj_shift(x2, p['norm_mix_g'], w_in_p, mu_cols, tm=cfg['tm'], tn=cfg['tn_in'], seq=seq)
    pool_out = pool_mixer(proj, p['pool_w'].astype(bf16), p['pool_scale'],
                          batch=batch, seq=seq, ts=cfg['ts_pool'])
    lw, k, kk, kka, g = rwkv_prep(
        proj, p['rwkv_w0'], _pad_rows(p['rwkv_w2'], 128).astype(bf16),
        p['rwkv_a0'], _pad_rows(p['rwkv_a2'], 128).astype(bf16),
        _pad_rows(p['rwkv_g2'], 256).astype(bf16), p['rwkv_k_k'], p['rwkv_k_a'], ts=cfg['ts_prep'])
    rwkv_out = rwkv_scan(proj, lw, k, kk, kka, g, p['rwkv_ln_w'], p['rwkv_ln_b'], p['rwkv_r_k'],
                         batch=batch, seq=seq, rows=cfg['scan_rows'], n_hg=cfg['scan_hg'])
    w_out = p['w_out'].astype(bf16)
    h1 = matmul_residual_resident([pool_out, rwkv_out], [w_out[:POOL_WIDTH], w_out[POOL_WIDTH:]],
                                  x2, tm=cfg['tm'], tn=cfg['tn'], name="out_proj")

    q = norm_matmul_resident(h1, p['norm_xattn_g'], p['xattn_w_q'].astype(bf16),
                             tm=cfg['tm'], tn=cfg['tn'], name="q_proj")
    kv = norm_matmul_resident(mem2, p['norm_mem_g'], p['xattn_w_kv'].astype(bf16),
                              tm=min(cfg['tm'], mem2.shape[0]), tn=cfg['tn'], name="kv_proj")

    w_router = jnp.pad(jnp.concatenate([p['moe_w_group'], p['moe_w_expert']], axis=1),
                       ((0, 0), (0, ROUTER_LANES - MOE_GROUPS - MOE_EXPERTS)))
    b_router = jnp.pad(jnp.concatenate([p['moe_b_group'], p['moe_b_expert']]),
                       (0, ROUTER_LANES - MOE_GROUPS - MOE_EXPERTS)).reshape(1, ROUTER_LANES)
    h2, xr, rec = attn_oproj_router(q, kv, p['xattn_w_o'].astype(bf16), h1, p['norm_ffn_g'],
                               w_router, b_router, tm=cfg['tm'], tn=cfg['tn'], seq=seq)
    tm_e = cfg['tm_moe']
    n_tiles = xr.shape[0] // tm_e + MOE_CLASSES
    pos, e_a, e_b, n_valid, fill_start, fill_on = _route_metadata(
        rec, tm=tm_e, n_tiles=n_tiles)
    xs = moe_dispatch(xr, pos, fill_start, fill_on, tm=cfg['tm_disp'], tz=tm_e,
                      n_slots=n_tiles * tm_e)
    y_sorted = moe_pair_experts(xs, e_a, e_b, n_valid, p['moe_w_gate'].astype(bf16),
                                p['moe_w_up'].astype(bf16), p['moe_w_down'].astype(bf16),
                                tm=tm_e, n_tiles=n_tiles)
    return moe_combine(h2, y_sorted, pos, p['norm_final_g'], tm=cfg['tm_comb'])


_CFG = dict(tm=512, tn=1024, tn_in=512, ts_pool=512, ts_prep=512, scan_rows=256, scan_hg=4,
            tm_moe=256, tm_disp=512, tm_comb=512)


def kernel(x, mem, norm_mix_g, w_in, pool_w, pool_scale, rwkv_mu, rwkv_w0, rwkv_w2, rwkv_a0,
           rwkv_a2, rwkv_g2, rwkv_k_k, rwkv_k_a, rwkv_r_k, rwkv_ln_w, rwkv_ln_b, w_out,
           norm_xattn_g, norm_mem_g, xattn_w_q, xattn_w_kv, xattn_w_o, norm_ffn_g,
           moe_w_group, moe_b_group, moe_w_expert, moe_b_expert, moe_w_gate, moe_w_up,
           moe_w_down, norm_final_g):
    batch, seq, D = x.shape
    p = dict(norm_mix_g=norm_mix_g[0], w_in=w_in[0], pool_w=pool_w[0], pool_scale=pool_scale[0],
             rwkv_mu=rwkv_mu[0], rwkv_w0=rwkv_w0[0], rwkv_w2=rwkv_w2[0], rwkv_a0=rwkv_a0[0],
             rwkv_a2=rwkv_a2[0], rwkv_g2=rwkv_g2[0], rwkv_k_k=rwkv_k_k[0], rwkv_k_a=rwkv_k_a[0],
             rwkv_r_k=rwkv_r_k[0], rwkv_ln_w=rwkv_ln_w[0], rwkv_ln_b=rwkv_ln_b[0], w_out=w_out[0],
             norm_xattn_g=norm_xattn_g[0], norm_mem_g=norm_mem_g[0], xattn_w_q=xattn_w_q[0],
             xattn_w_kv=xattn_w_kv[0], xattn_w_o=xattn_w_o[0], norm_ffn_g=norm_ffn_g[0],
             moe_w_group=moe_w_group[0], moe_b_group=moe_b_group[0], moe_w_expert=moe_w_expert[0],
             moe_b_expert=moe_b_expert[0], moe_w_gate=moe_w_gate[0], moe_w_up=moe_w_up[0],
             moe_w_down=moe_w_down[0], norm_final_g=norm_final_g)
    out = _layer(x.reshape(batch * seq, D), mem.reshape(batch * MEM_LEN, D), p,
                 batch=batch, seq=seq, cfg=_CFG)
    return out.reshape(batch, seq, D)
```

```python
import functools

import jax
import jax.numpy as jnp
from jax import lax
from jax.experimental import pallas as pl
from jax.experimental.pallas import tpu as pltpu

f32 = jnp.float32
bf16 = jnp.bfloat16

D_MODEL = 2048
MEM_LEN = 256
NORM_EPS = 1e-6

POOL_WIDTH = 1024
POOL_WINDOWS = (2, 4, 8, 16)
POOL_GROUP = 256
POOL_HALO = 16
RWKV_WIDTH = 1024
RWKV_HEAD = 64
RWKV_HEADS = 16
GN_EPS = 64e-5
DECAY_SCALE = 0.6065306597126334
DECAY_LORA = 64
AAA_LORA = 64
GATE_LORA = 160
LORA_PAD = 512
LORA_W_OFF, LORA_A_OFF, LORA_G_OFF = 0, 128, 256
MAIN_COLS = POOL_WIDTH + 3 * RWKV_WIDTH
PROJ_COLS = MAIN_COLS + LORA_PAD

RWKV_CHUNK = 64
HEADS_PER_STEP = 4
HG_LANES = HEADS_PER_STEP * RWKV_HEAD

XATTN_HEADS = 4
XATTN_HEAD_DIM = 512

MOE_GROUPS = 4
MOE_EXPERTS_PER_GROUP = 4
MOE_EXPERTS = 16
MOE_FF = 512
ROUTER_LANES = 128

VMEM_LIMIT = 56 * 1024 * 1024


def _cparams(sem):
    return pltpu.CompilerParams(dimension_semantics=sem, vmem_limit_bytes=VMEM_LIMIT)


def _split_dot(x, w_bf16):
    hi = x.astype(bf16)
    lo = (x - hi.astype(f32)).astype(bf16)
    return (jnp.dot(hi, w_bf16, preferred_element_type=f32)
            + jnp.dot(lo, w_bf16, preferred_element_type=f32))


def _resident(shape):
    return pl.BlockSpec(shape, lambda i: (0,) * len(shape), pipeline_mode=pl.Buffered(1))


def _rmsnorm_rows(x, g):
    ms = jnp.mean(x * x, axis=-1, keepdims=True)
    return x * lax.rsqrt(ms + NORM_EPS) * g


def _norm_mm_res_kernel(x_ref, g_ref, w_ref, o_ref, *, tn):
    xn = _rmsnorm_rows(x_ref[...], g_ref[...]).astype(bf16)
    for n0 in range(0, o_ref.shape[1], tn):
        o_ref[:, n0:n0 + tn] = jnp.dot(xn, w_ref[:, n0:n0 + tn],
                                       preferred_element_type=f32).astype(o_ref.dtype)


def norm_matmul_resident(x, g, w, *, tm, tn, name):
    M, K = x.shape
    N = w.shape[1]
    return pl.pallas_call(
        functools.partial(_norm_mm_res_kernel, tn=tn),
        out_shape=jax.ShapeDtypeStruct((M, N), bf16),
        grid=(M // tm,),
        in_specs=[pl.BlockSpec((tm, K), lambda i: (i, 0)), _resident((1, K)), _resident((K, N))],
        out_specs=pl.BlockSpec((tm, N), lambda i: (i, 0)),
        compiler_params=_cparams(("parallel",)),
        name=name,
    )(x, g.reshape(1, K), w)


def _in_proj_kernel(x_ref, xh_ref, g_ref, w_ref, wl_ref, mu_ref, o_ref, *, tn, tiles_per_seq):
    first = pl.program_id(0) % tiles_per_seq == 0
    xn = _rmsnorm_rows(x_ref[...], g_ref[...]).astype(bf16)
    xh = jnp.where(first, 0.0, _rmsnorm_rows(xh_ref[...], g_ref[...])).astype(bf16)
    lhs = jnp.concatenate([xh, xn], axis=0)
    for n0 in range(0, o_ref.shape[1], tn):
        w = w_ref[:, n0:n0 + tn] if n0 < MAIN_COLS else wl_ref[:, n0 - MAIN_COLS:n0 - MAIN_COLS + tn]
        res = jnp.dot(lhs, w, preferred_element_type=f32)
        p = res[POOL_HALO:]
        if n0 >= POOL_WIDTH:
            prev = pltpu.roll(res, 1, 0)[POOL_HALO:]
            p = p + (prev - p) * mu_ref[:, n0:n0 + tn]
        o_ref[:, n0:n0 + tn] = p.astype(o_ref.dtype)


def in_proj_shift(x, g, w_main, w_lora, mu_cols, *, tm, tn, seq):
    M, K = x.shape
    N = w_main.shape[1] + w_lora.shape[1]
    hb = tm // POOL_HALO
    return pl.pallas_call(
        functools.partial(_in_proj_kernel, tn=tn, tiles_per_seq=seq // tm),
        out_shape=jax.ShapeDtypeStruct((M, N), bf16),
        grid=(M // tm,),
        in_specs=[pl.BlockSpec((tm, K), lambda i: (i, 0)),
                  pl.BlockSpec((POOL_HALO, K), lambda i: (jnp.maximum(i * hb - 1, 0), 0)),
                  _resident((1, K)), _resident(w_main.shape), _resident(w_lora.shape),
                  _resident((1, N))],
        out_specs=pl.BlockSpec((tm, N), lambda i: (i, 0)),
        compiler_params=_cparams(("parallel",)),
        name="in_proj",
    )(x, x, g.reshape(1, K), w_main, w_lora, mu_cols.reshape(1, N))


def _mm_res_resident_kernel(*refs, n, tn):
    a_refs, w_refs, res_ref, o_ref = refs[:n], refs[n:2 * n], refs[2 * n], refs[2 * n + 1]
    for n0 in range(0, o_ref.shape[1], tn):
        acc = res_ref[:, n0:n0 + tn]
        for a_ref, w_ref in zip(a_refs, w_refs):
            acc = acc + jnp.dot(a_ref[...], w_ref[:, n0:n0 + tn], preferred_element_type=f32)
        o_ref[:, n0:n0 + tn] = acc


def matmul_residual_resident(a_list, w_list, res, *, tm, tn, name):
    M, N = res.shape
    n = len(a_list)
    return pl.pallas_call(
        functools.partial(_mm_res_resident_kernel, n=n, tn=tn),
        out_shape=jax.ShapeDtypeStruct((M, N), f32),
        grid=(M // tm,),
        in_specs=([pl.BlockSpec((tm, a.shape[1]), lambda i: (i, 0)) for a in a_list]
                  + [_resident(w.shape) for w in w_list]
                  + [pl.BlockSpec((tm, N), lambda i: (i, 0))]),
        out_specs=pl.BlockSpec((tm, N), lambda i: (i, 0)),
        compiler_params=_cparams(("parallel",)),
        name=name,
    )(*a_list, *w_list, res)


def _pool_kernel(u_ref, halo_ref, pw_ref, ps_ref, o_ref, *, ts):
    s = pl.program_id(1)
    u = u_ref[...].astype(f32)
    halo = jnp.where(s > 0, halo_ref[...].astype(f32), 0.0)
    ext = jnp.concatenate([halo, u], axis=0)
    pos = s * ts + lax.broadcasted_iota(jnp.int32, (ts, 1), 0)
    acc = ext
    shift = 1
    for gi, w in enumerate(POOL_WINDOWS):
        while shift < w:
            acc = acc + pltpu.roll(acc, shift, 0)
            shift *= 2
        lo, hi = gi * POOL_GROUP, (gi + 1) * POOL_GROUP
        cnt = jnp.minimum(pos + 1, w).astype(f32)
        pooled = acc[POOL_HALO:, lo:hi] / cnt - u[:, lo:hi]
        mixed = jnp.dot(pooled.astype(bf16), pw_ref[gi], preferred_element_type=f32)
        o_ref[:, lo:hi] = (mixed * ps_ref[:, lo:hi]).astype(o_ref.dtype)


def pool_mixer(proj, pool_w, pool_scale, *, batch, seq, ts):
    ns = seq // ts
    hb = ts // POOL_HALO
    return pl.pallas_call(
        functools.partial(_pool_kernel, ts=ts),
        out_shape=jax.ShapeDtypeStruct((batch * seq, POOL_WIDTH), bf16),
        grid=(batch, ns),
        in_specs=[
            pl.BlockSpec((ts, POOL_WIDTH), lambda b, s: (b * ns + s, 0)),
            pl.BlockSpec((POOL_HALO, POOL_WIDTH),
                         lambda b, s: (jnp.maximum((b * ns + s) * hb - 1, 0), 0)),
            pl.BlockSpec((len(POOL_WINDOWS), POOL_GROUP, POOL_GROUP), lambda b, s: (0, 0, 0)),
            pl.BlockSpec((1, POOL_WIDTH), lambda b, s: (0, 0)),
        ],
        out_specs=pl.BlockSpec((ts, POOL_WIDTH), lambda b, s: (b * ns + s, 0)),
        compiler_params=_cparams(("parallel", "arbitrary")),
        name="pool_mixer",
    )(proj, proj, pool_w, pool_scale.reshape(1, POOL_WIDTH))


def _head_ones():
    r = lax.broadcasted_iota(jnp.int32, (HG_LANES, HG_LANES), 0) // RWKV_HEAD
    c = lax.broadcasted_iota(jnp.int32, (HG_LANES, HG_LANES), 1) // RWKV_HEAD
    return jnp.where(r == c, 1.0, 0.0).astype(bf16)


def _head_sum(x, ones_bd):
    parts = [_split_dot(x[:, c:c + HG_LANES], ones_bd) for c in range(0, x.shape[1], HG_LANES)]
    return parts[0] if len(parts) == 1 else jnp.concatenate(parts, axis=1)


def _sigmoid(x):
    return 1.0 / (1.0 + jnp.exp(-x))


def _rwkv_prep_kernel(k_ref, lo_ref, w0_ref, w2_ref, a0_ref, a2_ref, g2_ref, kk_ref, ka_ref,
                      lw_out, k_out, kk_out, kka_out, g_out):
    k = k_ref[...].astype(f32)
    lo = lo_ref[...].astype(f32)
    w_lo = lo[:, LORA_W_OFF:LORA_W_OFF + 128]
    a_lo = lo[:, LORA_A_OFF:LORA_A_OFF + 128]
    g_lo = lo[:, LORA_G_OFF:LORA_G_OFF + 256]

    wx = w0_ref[...] + jnp.dot(jnp.tanh(w_lo).astype(bf16), w2_ref[...], preferred_element_type=f32)
    lw_out[...] = -DECAY_SCALE * _sigmoid(wx)
    a = _sigmoid(a0_ref[...] + jnp.dot(a_lo.astype(bf16), a2_ref[...], preferred_element_type=f32))
    g = jnp.dot(_sigmoid(g_lo).astype(bf16), g2_ref[...], preferred_element_type=f32)

    ones_bd = _head_ones()
    kk = k * kk_ref[...]
    kk = kk * lax.rsqrt(jnp.maximum(_head_sum(kk * kk, ones_bd), 1e-24))
    k = k * (1.0 + (a - 1.0) * ka_ref[...])

    k_out[...] = k.astype(k_out.dtype)
    kk_out[...] = kk.astype(kk_out.dtype)
    kka_out[...] = (kk * a).astype(kka_out.dtype)
    g_out[...] = g.astype(g_out.dtype)


def rwkv_prep(proj, w0, w2p, a0, a2p, g2p, k_k, k_a, *, ts):
    T = proj.shape[0]
    C = RWKV_WIDTH
    k_blk = (POOL_WIDTH + C) // C
    lora_blk = MAIN_COLS // LORA_PAD

    def full(shape):
        return pl.BlockSpec(shape, lambda i: (0,) * len(shape))

    row = lambda t: t.reshape(1, -1)
    out_sd = lambda dt: jax.ShapeDtypeStruct((T, C), dt)
    out_spec = pl.BlockSpec((ts, C), lambda i: (i, 0))
    return pl.pallas_call(
        _rwkv_prep_kernel,
        out_shape=[out_sd(f32), out_sd(bf16), out_sd(bf16), out_sd(bf16), out_sd(bf16)],
        grid=(T // ts,),
        in_specs=[
            pl.BlockSpec((ts, C), lambda i: (i, k_blk)),
            pl.BlockSpec((ts, LORA_PAD), lambda i: (i, lora_blk)),
            full((1, C)), full((128, C)), full((1, C)), full((128, C)), full((256, C)),
            full((1, C)), full((1, C)),
        ],
        out_specs=[out_spec] * 5,
        compiler_params=_cparams(("parallel",)),
        name="rwkv_prep",
    )(proj, proj, row(w0), w2p, row(a0), a2p, g2p, row(k_k), row(k_a))


def _bd_mask():
    r = lax.broadcasted_iota(jnp.int32, (HG_LANES, HG_LANES), 0) // RWKV_CHUNK
    c = lax.broadcasted_iota(jnp.int32, (HG_LANES, HG_LANES), 1) // RWKV_HEAD
    return r == c


def _rwkv_scan_kernel(r_ref, v_ref, lw_ref, k_ref, kk_ref, kka_ref, g_ref,
                      lnw_ref, lnb_ref, rk_ref, o_ref, state_ref, *, n_chunks, n_hg):
    L = RWKV_CHUNK
    W = HG_LANES

    @pl.when(pl.program_id(2) == 0)
    def _():
        state_ref[...] = jnp.zeros_like(state_ref)

    bd_mask = _bd_mask()
    ones_bd = _head_ones()

    def bd(x):
        tiled = jnp.concatenate([x] * HEADS_PER_STEP, axis=0)
        return jnp.where(bd_mask, tiled, 0.0).astype(bf16)

    def mm(a, b_bf16):
        return jnp.dot(a.astype(bf16), b_bf16, preferred_element_type=f32)

    def mm_nt(a, b_bf16):
        return lax.dot_general(a.astype(bf16), b_bf16, (((1,), (1,)), ((), ())),
                               preferred_element_type=f32)

    def mm_tn(a_bf16, b_bf16):
        return lax.dot_general(a_bf16, b_bf16, (((0,), (0,)), ((), ())),
                               preferred_element_type=f32)

    def split(x):
        hi = x.astype(bf16)
        return hi, (x - hi.astype(f32)).astype(bf16)

    def head_sums(xs):
        tot = mm(jnp.concatenate(xs, axis=0), ones_bd)
        return [tot[j * L:(j + 1) * L] for j in range(len(xs))]

    t_idx = lax.broadcasted_iota(jnp.int32, (L, W), 0)
    s_idx = lax.broadcasted_iota(jnp.int32, (L, W), 1) % L
    strict = t_idx > s_idx
    incl = t_idx >= s_idx
    eye_all = jnp.where(t_idx == s_idx, 1.0, 0.0)
    tri = jnp.where(lax.broadcasted_iota(jnp.int32, (L, L), 0)
                    >= lax.broadcasted_iota(jnp.int32, (L, L), 1), 1.0, 0.0).astype(bf16)
    blockdiag = (lax.broadcasted_iota(jnp.int32, (W, W), 0) // RWKV_HEAD
                 == lax.broadcasted_iota(jnp.int32, (W, W), 1) // RWKV_HEAD)

    def phase1(items):
        I = range(len(items))
        rows = [pl.ds(c * L, L) for _, c in items]
        cols = [pl.ds(h * W, W) for h, _ in items]
        ld = lambda ref, i: ref[rows[i], cols[i]]
        r = [ld(r_ref, i).astype(f32) for i in I]
        k = [ld(k_ref, i).astype(f32) for i in I]
        v = [ld(v_ref, i).astype(f32) for i in I]
        lw = [ld(lw_ref, i) for i in I]
        lw_s = [split(x) for x in lw]
        cs = [jnp.dot(tri, hi, preferred_element_type=f32) + jnp.dot(tri, lo, preferred_element_type=f32)
              for hi, lo in lw_s]
        yield
        g_inv = [jnp.exp(-x) for x in cs]
        to_end = [jnp.exp(x[L - 1:L, :] - x) for x in cs]
        a_t = [-ld(kk_ref, i).astype(f32) * jnp.exp(cs[i] - lw[i]) for i in I]
        r_t = [r[i] * jnp.exp(cs[i]) for i in I]
        kka = [ld(kka_ref, i).astype(f32) for i in I]
        b_t = [kka[i] * g_inv[i] for i in I]
        k_t = [k[i] * g_inv[i] for i in I]
        b_e = [kka[i] * to_end[i] for i in I]
        k_e = [k[i] * to_end[i] for i in I]
        v_bd = [bd(x) for x in v]
        ar = [jnp.concatenate([a_t[i], r_t[i]], axis=0) for i in I]
        m_b = [mm_nt(ar[i], bd(b_t[i])) for i in I]
        m_k = [mm_nt(ar[i], bd(k_t[i])) for i in I]
        m_ab = [jnp.where(strict, x[:L], 0.0) for x in m_b]
        m_rb = [jnp.where(incl, x[L:], 0.0) for x in m_b]
        m_ak = [jnp.where(strict, x[:L], 0.0) for x in m_k]
        m_rk = [jnp.where(incl, x[L:], 0.0) for x in m_k]
        rk_sum = head_sums([r[i] * k[i] * rk_ref[:, cols[i]] for i in I])
        bonus = [rk_sum[i] * v[i] for i in I]
        yield
        p = m_ab
        tinv = [eye_all + x for x in p]
        m = 2
        while m < L:
            p = [mm(x, bd(x)) for x in p]
            tinv = [tinv[i] + mm(tinv[i], bd(p[i])) for i in I]
            m *= 2
            yield
        x1 = [mm(m_ak[i], v_bd[i]) for i in I]
        u_v = [mm(tinv[i], bd(x1[i])) for i in I]
        a_h = [mm(tinv[i], bd(a_t[i])) for i in I]
        yield
        r_h = [r_t[i] + mm(m_rb[i], bd(a_h[i])) for i in I]
        y_v = [mm(m_rb[i], bd(u_v[i])) + mm(m_rk[i], v_bd[i]) for i in I]
        yield
        p_t = [jnp.where(blockdiag, mm_tn(a_h[i].astype(bf16), b_e[i].astype(bf16)), 0.0).astype(bf16)
               for i in I]
        g_t = [jnp.where(blockdiag,
                         mm_tn(jnp.concatenate([u_v[i], v[i]], axis=0).astype(bf16),
                               jnp.concatenate([b_e[i], k_e[i]], axis=0).astype(bf16)), 0.0) for i in I]
        decay = [jnp.exp(x[L - 1:L, :]) for x in cs]
        for i in I:
            done.append(dict(hg=items[i][0], rows=rows[i], cols=cols[i], r_h=r_h[i], y_v=y_v[i],
                             p_t=p_t[i], g_t=g_t[i], decay=decay[i], bonus=bonus[i]))
        yield

    def chain_step(chs):
        ys = []
        for ch in chs:
            state = states[ch['hg']]
            state_b = state.astype(bf16)
            ys.append(mm_nt(ch['r_h'], state_b) + ch['y_v'])
            states[ch['hg']] = (state * ch['decay']
                                + jnp.dot(state_b, ch['p_t'], preferred_element_type=f32) + ch['g_t'])
        means = head_sums(ys)
        ycs = [y - m * (1.0 / RWKV_HEAD) for y, m in zip(ys, means)]
        vrs = head_sums([yc * yc for yc in ycs])
        for ch, yc, vr in zip(chs, ycs, vrs):
            rw, cl = ch['rows'], ch['cols']
            yn = yc * lax.rsqrt(vr * (1.0 / RWKV_HEAD) + GN_EPS) * lnw_ref[:, cl] + lnb_ref[:, cl]
            o_ref[rw, cl] = ((yn + ch['bonus']) * g_ref[rw, cl].astype(f32)).astype(o_ref.dtype)

    states = [state_ref[h] for h in range(n_hg)]
    done = []
    half = max(n_chunks // 2, 1)
    groups = [[(h, c) for c in range(0, half) for h in range(n_hg)],
              [(h, c) for c in range(half, n_chunks) for h in range(n_hg)]]
    def by_chunk(n_items):
        return [done[j:j + n_hg] for j in range(0, n_items, n_hg)]

    for _ in phase1(groups[0]):
        pass
    pending = by_chunk(len(done))
    for _ in phase1(groups[1]):
        if pending:
            chain_step(pending.pop(0))
    for chs in pending + by_chunk(len(done))[len(groups[0]) // n_hg:]:
        chain_step(chs)
    for h in range(n_hg):
        state_ref[h] = states[h]


def rwkv_scan(proj, lw, k, kk, kka, g, ln_w, ln_b, r_k, *, batch, seq, rows, n_hg):
    nb = seq // rows
    T = batch * seq
    width = n_hg * HG_LANES
    n_col = RWKV_WIDTH // width
    r_blk = POOL_WIDTH // width
    v_blk = (POOL_WIDTH + 2 * RWKV_WIDTH) // width
    blk = pl.BlockSpec((rows, width), lambda b, h, c: (b * nb + c, h))
    par = pl.BlockSpec((1, width), lambda b, h, c: (0, h))
    return pl.pallas_call(
        functools.partial(_rwkv_scan_kernel, n_chunks=rows // RWKV_CHUNK, n_hg=n_hg),
        out_shape=jax.ShapeDtypeStruct((T, RWKV_WIDTH), bf16),
        grid=(batch, n_col, nb),
        in_specs=[pl.BlockSpec((rows, width), lambda b, h, c: (b * nb + c, r_blk + h)),
                  pl.BlockSpec((rows, width), lambda b, h, c: (b * nb + c, v_blk + h))]
                 + [blk] * 5 + [par] * 3,
        out_specs=blk,
        scratch_shapes=[pltpu.VMEM((n_hg, HG_LANES, HG_LANES), f32)],
        compiler_params=_cparams(("parallel", "parallel", "arbitrary")),
        name="rwkv_scan",
    )(proj, proj, lw, k, kk, kka, g, ln_w.reshape(1, -1), ln_b.reshape(1, -1), r_k.reshape(1, -1))


def _route(hn, w, b):
    w_hi = w.astype(bf16)
    w_lo = (w - w_hi.astype(f32)).astype(bf16)
    h_hi = hn.astype(bf16)
    h_lo = (hn - h_hi.astype(f32)).astype(bf16)
    hh = jnp.dot(h_hi, jnp.concatenate([w_hi, w_lo], axis=1), preferred_element_type=f32)
    logits = (hh[:, :ROUTER_LANES] + hh[:, ROUTER_LANES:]
              + jnp.dot(h_lo, w_hi, preferred_element_type=f32)) + b

    lane_i = lax.broadcasted_iota(jnp.int32, logits.shape, 1)
    lane = lane_i.astype(f32)
    neg = -jnp.inf
    big = float(ROUTER_LANES)
    is_g = lane_i < MOE_GROUPS
    gl = jnp.where(is_g, logits, neg)
    gmax = jnp.max(gl, axis=-1, keepdims=True)
    g_idx = jnp.min(jnp.where(gl == gmax, lane, big), axis=-1, keepdims=True)
    g_w = 1.0 / jnp.sum(jnp.exp(gl - gmax), axis=-1, keepdims=True)

    lo_lane = MOE_GROUPS + g_idx * MOE_EXPERTS_PER_GROUP
    sel = (lane >= lo_lane) & (lane < lo_lane + MOE_EXPERTS_PER_GROUP)
    el = jnp.where(sel, logits, neg)
    v1 = jnp.max(el, axis=-1, keepdims=True)
    i1 = jnp.min(jnp.where(el == v1, lane, big), axis=-1, keepdims=True)
    el2 = jnp.where(lane == i1, neg, el)
    v2 = jnp.max(el2, axis=-1, keepdims=True)
    i2 = jnp.min(jnp.where(el2 == v2, lane, big), axis=-1, keepdims=True)
    e2 = jnp.exp(v2 - v1)
    p1 = 1.0 / (1.0 + e2)
    p2 = e2 / (1.0 + e2)
    return (jnp.where(lane_i == 0, i1 - MOE_GROUPS, 0.0)
            + jnp.where(lane_i == 1, i2 - MOE_GROUPS, 0.0)
            + jnp.where(lane_i == 2, p1 * g_w, 0.0)
            + jnp.where(lane_i == 3, p2 * g_w, 0.0))


def _attn_oproj_router_kernel(q_ref, kv_ref, wo_ref, res_ref, g_ref, wr_ref, br_ref, h_ref, xr_ref,
                              rec_ref, *, tn):
    hd = XATTN_HEAD_DIM
    heads = []
    for h in range(XATTN_HEADS):
        k = kv_ref[:, h * hd:(h + 1) * hd]
        v = kv_ref[:, (XATTN_HEADS + h) * hd:(XATTN_HEADS + h + 1) * hd]
        s = lax.dot_general(q_ref[:, h * hd:(h + 1) * hd], k, (((1,), (1,)), ((), ())),
                            preferred_element_type=f32) * (hd ** -0.5)
        e = jnp.exp(s - jnp.max(s, axis=-1, keepdims=True))
        l = jnp.sum(e, axis=-1, keepdims=True)
        heads.append((jnp.dot(e.astype(bf16), v, preferred_element_type=f32) / l).astype(bf16))
    o = jnp.concatenate(heads, axis=1)
    for n0 in range(0, h_ref.shape[1], tn):
        h_ref[:, n0:n0 + tn] = res_ref[:, n0:n0 + tn] + jnp.dot(
            o, wo_ref[:, n0:n0 + tn], preferred_element_type=f32)
    hn = _rmsnorm_rows(h_ref[...], g_ref[...])
    D = hn.shape[1]
    rec = _route(hn, wr_ref[...], br_ref[...])
    xr_ref[:, :D] = hn
    xr_ref[:, D:] = rec
    rec_ref[...] = rec


def attn_oproj_router(q, kv, w_o, res, g, w_router, b_router, *, tm, tn, seq):
    T, D = res.shape
    tiles_per_seq = seq // tm
    row = lambda width: pl.BlockSpec((tm, width), lambda i: (i, 0))
    return pl.pallas_call(
        functools.partial(_attn_oproj_router_kernel, tn=tn),
        out_shape=[jax.ShapeDtypeStruct((T, D), f32),
                   jax.ShapeDtypeStruct((T, D + ROUTER_LANES), f32),
                   jax.ShapeDtypeStruct((T, ROUTER_LANES), f32)],
        grid=(T // tm,),
        in_specs=[row(D), pl.BlockSpec((MEM_LEN, 2 * D), lambda i: (i // tiles_per_seq, 0)),
                  _resident((D, D)), row(D), _resident((1, D)),
                  _resident((D, ROUTER_LANES)), _resident((1, ROUTER_LANES))],
        out_specs=[row(D), row(D + ROUTER_LANES), row(ROUTER_LANES)],
        compiler_params=_cparams(("parallel",)),
        name="attn_oproj_router",
    )(q, kv, w_o, res, g.reshape(1, D), w_router, b_router)


MOE_PAIRS = ((0, 1), (0, 2), (0, 3), (1, 3), (1, 2), (3, 2))
MOE_CLASSES = MOE_GROUPS * len(MOE_PAIRS)


def _route_metadata(rec, *, tm, n_tiles):
    ids = rec[:, 0:2].astype(jnp.int32)
    lo = jnp.minimum(ids[:, 0], ids[:, 1])
    hi = jnp.maximum(ids[:, 0], ids[:, 1])
    a, b = lo % MOE_EXPERTS_PER_GROUP, hi % MOE_EXPERTS_PER_GROUP
    pair_id = (a * (7 - a)) // 2 + b - a - 1
    pair_id = pair_id + (pair_id == 3).astype(jnp.int32) - (pair_id == 4).astype(jnp.int32)
    cls = (lo // MOE_EXPERTS_PER_GROUP) * len(MOE_PAIRS) + pair_id
    onehot = (cls[:, None] == jnp.arange(MOE_CLASSES, dtype=jnp.int32)[None, :]).astype(jnp.int32)
    csum = jnp.cumsum(onehot, axis=0)
    counts = csum[-1]
    rank = jnp.sum(csum * onehot, axis=1) - 1
    padded = ((counts + tm - 1) // tm) * tm
    ends = jnp.cumsum(padded)
    pos = jnp.sum(onehot * (ends - padded)[None, :], axis=1) + rank
    tile_start = jnp.arange(n_tiles, dtype=jnp.int32) * tm
    n_valid = ends[-1] // tm
    tile_cls = jnp.sum((tile_start[:, None] >= ends[None, :]).astype(jnp.int32), axis=1)
    last_cls = jnp.sum((ends[-1] - 1 >= ends).astype(jnp.int32))
    tile_cls = jnp.where(tile_start < ends[-1], tile_cls, last_cls)
    pair = jnp.asarray(MOE_PAIRS, jnp.int32)[tile_cls % len(MOE_PAIRS)]
    base = (tile_cls // len(MOE_PAIRS)) * MOE_EXPERTS_PER_GROUP
    idle = n_valid + jnp.arange(MOE_CLASSES, dtype=jnp.int32)
    fill_start = jnp.concatenate([jnp.maximum(ends - tm, 0), jnp.minimum(idle, n_tiles - 1) * tm])
    fill_on = jnp.concatenate([counts > 0, idle < n_tiles]).astype(jnp.int32)
    return pos, base + pair[:, 0], base + pair[:, 1], n_valid.reshape(1), fill_start, fill_on


def _moe_dispatch_kernel(pos_ref, fs_ref, fo_ref, x_ref, wg_ref, wu_ref, wd_ref,
                         xs_hbm, wg_out, wu_out, wd_out, stage, zbuf, sem, zsem, *, tm, tz):
    i = pl.program_id(0)
    n = pl.num_programs(0)

    def fill_copy(c):
        return pltpu.make_async_copy(zbuf, xs_hbm.at[pl.ds(pl.multiple_of(fs_ref[c], tz), tz), :], zsem)

    @pl.when(i == 0)
    def _():
        zbuf[...] = jnp.zeros_like(zbuf)
        for c in range(2 * MOE_CLASSES):
            @pl.when(fo_ref[c] > 0)
            def _():
                fill_copy(c).start()
        for c in range(2 * MOE_CLASSES):
            @pl.when(fo_ref[c] > 0)
            def _():
                fill_copy(c).wait()

    def tile_wait(slot):
        pltpu.make_async_copy(stage.at[slot], xs_hbm.at[pl.ds(0, tm), :], sem.at[slot]).wait()

    slot = i % 2

    @pl.when(i >= 2)
    def _():
        tile_wait(slot)

    stage[slot] = x_ref[...]
    wg_out[...] = wg_ref[...].astype(bf16)
    wu_out[...] = wu_ref[...].astype(bf16)
    wd_out[...] = wd_ref[...].astype(bf16)

    def body(r, carry):
        p = pos_ref[i * tm + r]
        pltpu.make_async_copy(stage.at[slot, pl.ds(r, 1), :], xs_hbm.at[pl.ds(p, 1), :],
                              sem.at[slot]).start()
        return carry
    lax.fori_loop(0, tm, body, 0, unroll=8)

    @pl.when(i == n - 1)
    def _():
        tile_wait(slot)

        @pl.when(n >= 2)
        def _():
            tile_wait(1 - slot)


def moe_dispatch(xr, pos, fill_start, fill_on, wg, wu, wd, *, tm, tz, n_slots):
    T, width = xr.shape
    n = T // tm

    def slabs(w):
        return w.reshape(n, w.shape[0] * w.shape[1] // n, w.shape[2])

    ws = [slabs(w) for w in (wg, wu, wd)]
    w_spec = lambda w: pl.BlockSpec((1,) + w.shape[1:], lambda i, pos, fs, fo: (i, 0, 0))
    grid_spec = pltpu.PrefetchScalarGridSpec(
        num_scalar_prefetch=3,
        grid=(n,),
        in_specs=[pl.BlockSpec((tm, width), lambda i, pos, fs, fo: (i, 0))] + [w_spec(w) for w in ws],
        out_specs=[pl.BlockSpec(memory_space=pl.ANY)] + [w_spec(w) for w in ws],
        scratch_shapes=[pltpu.VMEM((2, tm, width), f32), pltpu.VMEM((tz, width), f32),
                        pltpu.SemaphoreType.DMA((2,)), pltpu.SemaphoreType.DMA(())],
    )
    xs, wg_b, wu_b, wd_b = pl.pallas_call(
        functools.partial(_moe_dispatch_kernel, tm=tm, tz=tz),
        out_shape=[jax.ShapeDtypeStruct((n_slots, width), f32)]
                  + [jax.ShapeDtypeStruct(w.shape, bf16) for w in ws],
        grid_spec=grid_spec,
        compiler_params=_cparams(("arbitrary",)),
        name="moe_dispatch",
    )(pos, fill_start, fill_on, xr, *ws)
    return xs, wg_b.reshape(wg.shape), wu_b.reshape(wu.shape), wd_b.reshape(wd.shape)


def _moe_pair_kernel(ea_ref, eb_ref, nv_ref, xs_ref, wga, wua, wda, wgb, wub, wdb, o_ref):
    @pl.when(pl.program_id(0) >= nv_ref[0])
    def _():
        o_ref[...] = jnp.zeros_like(o_ref)

    @pl.when(pl.program_id(0) < nv_ref[0])
    def _():
        D = o_ref.shape[1]
        x = xs_ref[:, :D].astype(bf16)
        rec = xs_ref[:, D:]
        first = rec[:, 0:1] == ea_ref[pl.program_id(0)].astype(f32)
        w_a = jnp.where(first, rec[:, 2:3], rec[:, 3:4])
        w_b = jnp.where(first, rec[:, 3:4], rec[:, 2:3])

        def expert(wg, wu, wd):
            gate = jnp.dot(x, wg[0], preferred_element_type=f32)
            up = jnp.dot(x, wu[0], preferred_element_type=f32)
            hid = gate * _sigmoid(gate) * up
            return jnp.dot(hid.astype(bf16), wd[0], preferred_element_type=f32)

        o_ref[...] = w_a * expert(wga, wua, wda) + w_b * expert(wgb, wub, wdb)


def moe_pair_experts(xs, e_a, e_b, n_valid, wg, wu, wd, *, tm, n_tiles):
    width = xs.shape[1]
    D = wg.shape[1]
    held = lambda i, ea, eb, nv: (jnp.minimum(i, nv[0] - 1), 0)
    tile = lambda i, ea, eb, nv: (i, 0)
    wa = lambda shape: pl.BlockSpec(shape, lambda i, ea, eb, nv: (ea[i], 0, 0))
    wb = lambda shape: pl.BlockSpec(shape, lambda i, ea, eb, nv: (eb[i], 0, 0))
    grid_spec = pltpu.PrefetchScalarGridSpec(
        num_scalar_prefetch=3,
        grid=(n_tiles,),
        in_specs=[pl.BlockSpec((tm, width), held),
                  wa((1, D, MOE_FF)), wa((1, D, MOE_FF)), wa((1, MOE_FF, D)),
                  wb((1, D, MOE_FF)), wb((1, D, MOE_FF)), wb((1, MOE_FF, D))],
        out_specs=pl.BlockSpec((tm, D), tile),
    )
    return pl.pallas_call(
        _moe_pair_kernel,
        out_shape=jax.ShapeDtypeStruct((n_tiles * tm, D), f32),
        grid_spec=grid_spec,
        compiler_params=_cparams(("arbitrary",)),
        name="moe_experts",
    )(e_a, e_b, n_valid, xs, wg, wu, wd, wg, wu, wd)


def _moe_combine_kernel(pos_ref, h_ref, y_hbm, gf_ref, o_ref, ybuf, sem, *, tm):
    i = pl.program_id(0)
    n = pl.num_programs(0)

    def gather(tile, slot):
        def body(r, carry):
            p = pos_ref[tile * tm + r]
            pltpu.make_async_copy(y_hbm.at[pl.ds(p, 1), :],
                                  ybuf.at[slot, pl.ds(r, 1), :], sem.at[slot]).start()
            return carry
        lax.fori_loop(0, tm, body, 0, unroll=8)

    @pl.when(i == 0)
    def _():
        gather(0, 0)

    @pl.when(i + 1 < n)
    def _():
        gather(i + 1, (i + 1) % 2)

    slot = i % 2
    pltpu.make_async_copy(y_hbm.at[pl.ds(0, tm), :], ybuf.at[slot], sem.at[slot]).wait()
    o_ref[...] = _rmsnorm_rows(h_ref[...] + ybuf[slot], gf_ref[...])


def moe_combine(h, y_sorted, pos, g_final, *, tm):
    T, D = h.shape
    grid_spec = pltpu.PrefetchScalarGridSpec(
        num_scalar_prefetch=1,
        grid=(T // tm,),
        in_specs=[
            pl.BlockSpec((tm, D), lambda i, pos: (i, 0)),
            pl.BlockSpec(memory_space=pl.ANY),
            pl.BlockSpec((1, D), lambda i, pos: (0, 0)),
        ],
        out_specs=pl.BlockSpec((tm, D), lambda i, pos: (i, 0)),
        scratch_shapes=[pltpu.VMEM((2, tm, D), f32), pltpu.SemaphoreType.DMA((2,))],
    )
    return pl.pallas_call(
        functools.partial(_moe_combine_kernel, tm=tm),
        out_shape=jax.ShapeDtypeStruct((T, D), f32),
        grid_spec=grid_spec,
        compiler_params=_cparams(("arbitrary",)),
        name="moe_combine",
    )(pos, h, y_sorted, g_final.reshape(1, D))


def _pad_rows(w, rows):
    return jnp.pad(w, ((0, rows - w.shape[0]), (0, 0)))


def _pack_lora_cols(t):
    o1, o2 = DECAY_LORA, DECAY_LORA + AAA_LORA
    pad = lambda a, n: jnp.pad(a, [(0, 0)] * (a.ndim - 1) + [(0, n - a.shape[-1])])
    return jnp.concatenate([pad(t[..., :o1], 128), pad(t[..., o1:o2], 128),
                            pad(t[..., o2:], 256)], axis=-1)


def _layer(x2, mem2, p, *, batch, seq, cfg):
    D = D_MODEL
    w_in = p['w_in']
    w_main = w_in[:, :MAIN_COLS].astype(bf16)
    w_lora = _pack_lora_cols(w_in[:, MAIN_COLS:]).astype(bf16)
    mu_cols = jnp.concatenate([jnp.zeros((POOL_WIDTH,), f32), p['rwkv_mu'][:3 * RWKV_WIDTH],
                               _pack_lora_cols(p['rwkv_mu'][3 * RWKV_WIDTH:])])

    proj = in_proj_shift(x2, p['norm_mix_g'], w_main, w_lora, mu_cols,
                         tm=cfg['tm'], tn=cfg['tn_in'], seq=seq)
    pool_out = pool_mixer(proj, p['pool_w'].astype(bf16), p['pool_scale'],
                          batch=batch, seq=seq, ts=cfg['ts_pool'])
    lw, k, kk, kka, g = rwkv_prep(
        proj, p['rwkv_w0'], _pad_rows(p['rwkv_w2'], 128).astype(bf16),
        p['rwkv_a0'], _pad_rows(p['rwkv_a2'], 128).astype(bf16),
        _pad_rows(p['rwkv_g2'], 256).astype(bf16), p['rwkv_k_k'], p['rwkv_k_a'], ts=cfg['ts_prep'])
    rwkv_out = rwkv_scan(proj, lw, k, kk, kka, g, p['rwkv_ln_w'], p['rwkv_ln_b'], p['rwkv_r_k'],
                         batch=batch, seq=seq, rows=cfg['scan_rows'], n_hg=cfg['scan_hg'])
    w_out = p['w_out'].astype(bf16)
    h1 = matmul_residual_resident([pool_out, rwkv_out], [w_out[:POOL_WIDTH], w_out[POOL_WIDTH:]],
                                  x2, tm=cfg['tm'], tn=cfg['tn'], name="out_proj")

    q = norm_matmul_resident(h1, p['norm_xattn_g'], p['xattn_w_q'].astype(bf16),
                             tm=cfg['tm'], tn=cfg['tn'], name="q_proj")
    kv = norm_matmul_resident(mem2, p['norm_mem_g'], p['xattn_w_kv'].astype(bf16),
                              tm=min(cfg['tm'], mem2.shape[0]), tn=cfg['tn'], name="kv_proj")

    w_router = jnp.pad(jnp.concatenate([p['moe_w_group'], p['moe_w_expert']], axis=1),
                       ((0, 0), (0, ROUTER_LANES - MOE_GROUPS - MOE_EXPERTS)))
    b_router = jnp.pad(jnp.concatenate([p['moe_b_group'], p['moe_b_expert']]),
                       (0, ROUTER_LANES - MOE_GROUPS - MOE_EXPERTS)).reshape(1, ROUTER_LANES)
    h2, xr, rec = attn_oproj_router(q, kv, p['xattn_w_o'].astype(bf16), h1, p['norm_ffn_g'],
                               w_router, b_router, tm=cfg['tm'], tn=cfg['tn'], seq=seq)
    tm_e = cfg['tm_moe']
    n_tiles = xr.shape[0] // tm_e + MOE_CLASSES
    pos, e_a, e_b, n_valid, fill_start, fill_on = _route_metadata(
        rec, tm=tm_e, n_tiles=n_tiles)
    xs, wg, wu, wd = moe_dispatch(xr, pos, fill_start, fill_on, p['moe_w_gate'], p['moe_w_up'],
                                  p['moe_w_down'], tm=cfg['tm_disp'], tz=tm_e,
                                  n_slots=n_tiles * tm_e)
    y_sorted = moe_pair_experts(xs, e_a, e_b, n_valid, wg, wu, wd, tm=tm_e, n_tiles=n_tiles)
    return moe_combine(h2, y_sorted, pos, p['norm_final_g'], tm=cfg['tm_comb'])


_CFG = dict(tm=512, tn=1024, tn_in=512, ts_pool=512, ts_prep=512, scan_rows=256, scan_hg=4,
            tm_moe=256, tm_disp=512, tm_comb=512)


def kernel(x, mem, norm_mix_g, w_in, pool_w, pool_scale, rwkv_mu, rwkv_w0, rwkv_w2, rwkv_a0,
           rwkv_a2, rwkv_g2, rwkv_k_k, rwkv_k_a, rwkv_r_k, rwkv_ln_w, rwkv_ln_b, w_out,
           norm_xattn_g, norm_mem_g, xattn_w_q, xattn_w_kv, xattn_w_o, norm_ffn_g,
           moe_w_group, moe_b_group, moe_w_expert, moe_b_expert, moe_w_gate, moe_w_up,
           moe_w_down, norm_final_g):
    batch, seq, D = x.shape
    p = dict(norm_mix_g=norm_mix_g[0], w_in=w_in[0], pool_w=pool_w[0], pool_scale=pool_scale[0],
             rwkv_mu=rwkv_mu[0], rwkv_w0=rwkv_w0[0], rwkv_w2=rwkv_w2[0], rwkv_a0=rwkv_a0[0],
             rwkv_a2=rwkv_a2[0], rwkv_g2=rwkv_g2[0], rwkv_k_k=rwkv_k_k[0], rwkv_k_a=rwkv_k_a[0],
             rwkv_r_k=rwkv_r_k[0], rwkv_ln_w=rwkv_ln_w[0], rwkv_ln_b=rwkv_ln_b[0], w_out=w_out[0],
             norm_xattn_g=norm_xattn_g[0], norm_mem_g=norm_mem_g[0], xattn_w_q=xattn_w_q[0],
             xattn_w_kv=xattn_w_kv[0], xattn_w_o=xattn_w_o[0], norm_ffn_g=norm_ffn_g[0],
             moe_w_group=moe_w_group[0], moe_b_group=moe_b_group[0], moe_w_expert=moe_w_expert[0],
             moe_b_expert=moe_b_expert[0], moe_w_gate=moe_w_gate[0], moe_w_up=moe_w_up[0],
             moe_w_down=moe_w_down[0], norm_final_g=norm_final_g)
    out = _layer(x.reshape(batch * seq, D), mem.reshape(batch * MEM_LEN, D), p,
                 batch=batch, seq=seq, cfg=_CFG)
    return out.reshape(batch, seq, D)
```

```python
import functools

import jax
import jax.numpy as jnp
from jax import lax
from jax.experimental import pallas as pl
from jax.experimental.pallas import tpu as pltpu

f32 = jnp.float32
bf16 = jnp.bfloat16

D_MODEL = 2048
MEM_LEN = 256
NORM_EPS = 1e-6

POOL_WIDTH = 1024
POOL_WINDOWS = (2, 4, 8, 16)
POOL_GROUP = 256
POOL_HALO = 16
RWKV_WIDTH = 1024
RWKV_HEAD = 64
RWKV_HEADS = 16
GN_EPS = 64e-5
DECAY_SCALE = 0.6065306597126334
DECAY_LORA = 64
AAA_LORA = 64
GATE_LORA = 160
LORA_PAD = 512
LORA_W_OFF, LORA_A_OFF, LORA_G_OFF = 0, 128, 256
MAIN_COLS = POOL_WIDTH + 3 * RWKV_WIDTH
PROJ_COLS = MAIN_COLS + LORA_PAD

RWKV_CHUNK = 64
HEADS_PER_STEP = 4
HG_LANES = HEADS_PER_STEP * RWKV_HEAD

XATTN_HEADS = 4
XATTN_HEAD_DIM = 512

MOE_GROUPS = 4
MOE_EXPERTS_PER_GROUP = 4
MOE_EXPERTS = 16
MOE_FF = 512
ROUTER_LANES = 128

VMEM_LIMIT = 56 * 1024 * 1024


def _cparams(sem):
    return pltpu.CompilerParams(dimension_semantics=sem, vmem_limit_bytes=VMEM_LIMIT)


def _split_dot(x, w_bf16):
    hi = x.astype(bf16)
    lo = (x - hi.astype(f32)).astype(bf16)
    return (jnp.dot(hi, w_bf16, preferred_element_type=f32)
            + jnp.dot(lo, w_bf16, preferred_element_type=f32))


def _resident(shape):
    return pl.BlockSpec(shape, lambda i: (0,) * len(shape), pipeline_mode=pl.Buffered(1))


def _rmsnorm_rows(x, g):
    ms = jnp.mean(x * x, axis=-1, keepdims=True)
    return x * lax.rsqrt(ms + NORM_EPS) * g


def _norm_mm_res_kernel(x_ref, g_ref, w_ref, o_ref, *, tn):
    xn = _rmsnorm_rows(x_ref[...], g_ref[...]).astype(bf16)
    for n0 in range(0, o_ref.shape[1], tn):
        o_ref[:, n0:n0 + tn] = jnp.dot(xn, w_ref[:, n0:n0 + tn],
                                       preferred_element_type=f32).astype(o_ref.dtype)


def norm_matmul_resident(x, g, w, *, tm, tn, name):
    M, K = x.shape
    N = w.shape[1]
    return pl.pallas_call(
        functools.partial(_norm_mm_res_kernel, tn=tn),
        out_shape=jax.ShapeDtypeStruct((M, N), bf16),
        grid=(M // tm,),
        in_specs=[pl.BlockSpec((tm, K), lambda i: (i, 0)), _resident((1, K)), _resident((K, N))],
        out_specs=pl.BlockSpec((tm, N), lambda i: (i, 0)),
        compiler_params=_cparams(("parallel",)),
        name=name,
    )(x, g.reshape(1, K), w)


def _in_proj_kernel(x_ref, xh_ref, g_ref, w_ref, wl_ref, mu_ref, o_ref, *, tn, tiles_per_seq):
    first = pl.program_id(0) % tiles_per_seq == 0
    xn = _rmsnorm_rows(x_ref[...], g_ref[...]).astype(bf16)
    xh = jnp.where(first, 0.0, _rmsnorm_rows(xh_ref[...], g_ref[...])).astype(bf16)
    lhs = jnp.concatenate([xh, xn], axis=0)
    for n0 in range(0, o_ref.shape[1], tn):
        w = w_ref[:, n0:n0 + tn] if n0 < MAIN_COLS else wl_ref[:, n0 - MAIN_COLS:n0 - MAIN_COLS + tn]
        res = jnp.dot(lhs, w, preferred_element_type=f32)
        p = res[POOL_HALO:]
        if n0 >= POOL_WIDTH:
            prev = pltpu.roll(res, 1, 0)[POOL_HALO:]
            p = p + (prev - p) * mu_ref[:, n0:n0 + tn]
        o_ref[:, n0:n0 + tn] = p.astype(o_ref.dtype)


def in_proj_shift(x, g, w_main, w_lora, mu_cols, *, tm, tn, seq):
    M, K = x.shape
    N = w_main.shape[1] + w_lora.shape[1]
    hb = tm // POOL_HALO
    return pl.pallas_call(
        functools.partial(_in_proj_kernel, tn=tn, tiles_per_seq=seq // tm),
        out_shape=jax.ShapeDtypeStruct((M, N), bf16),
        grid=(M // tm,),
        in_specs=[pl.BlockSpec((tm, K), lambda i: (i, 0)),
                  pl.BlockSpec((POOL_HALO, K), lambda i: (jnp.maximum(i * hb - 1, 0), 0)),
                  _resident((1, K)), _resident(w_main.shape), _resident(w_lora.shape),
                  _resident((1, N))],
        out_specs=pl.BlockSpec((tm, N), lambda i: (i, 0)),
        compiler_params=_cparams(("parallel",)),
        name="in_proj",
    )(x, x, g.reshape(1, K), w_main, w_lora, mu_cols.reshape(1, N))


def _mm_res_resident_kernel(*refs, n, tn):
    a_refs, w_refs, res_ref, o_ref = refs[:n], refs[n:2 * n], refs[2 * n], refs[2 * n + 1]
    for n0 in range(0, o_ref.shape[1], tn):
        acc = res_ref[:, n0:n0 + tn]
        for a_ref, w_ref in zip(a_refs, w_refs):
            acc = acc + jnp.dot(a_ref[...], w_ref[:, n0:n0 + tn], preferred_element_type=f32)
        o_ref[:, n0:n0 + tn] = acc


def matmul_residual_resident(a_list, w_list, res, *, tm, tn, name):
    M, N = res.shape
    n = len(a_list)
    return pl.pallas_call(
        functools.partial(_mm_res_resident_kernel, n=n, tn=tn),
        out_shape=jax.ShapeDtypeStruct((M, N), f32),
        grid=(M // tm,),
        in_specs=([pl.BlockSpec((tm, a.shape[1]), lambda i: (i, 0)) for a in a_list]
                  + [_resident(w.shape) for w in w_list]
                  + [pl.BlockSpec((tm, N), lambda i: (i, 0))]),
        out_specs=pl.BlockSpec((tm, N), lambda i: (i, 0)),
        compiler_params=_cparams(("parallel",)),
        name=name,
    )(*a_list, *w_list, res)


def _pool_kernel(u_ref, halo_ref, pw_ref, ps_ref, o_ref, *, ts):
    s = pl.program_id(1)
    u = u_ref[...].astype(f32)
    halo = jnp.where(s > 0, halo_ref[...].astype(f32), 0.0)
    ext = jnp.concatenate([halo, u], axis=0)
    pos = s * ts + lax.broadcasted_iota(jnp.int32, (ts, 1), 0)
    acc = ext
    shift = 1
    for gi, w in enumerate(POOL_WINDOWS):
        while shift < w:
            acc = acc + pltpu.roll(acc, shift, 0)
            shift *= 2
        lo, hi = gi * POOL_GROUP, (gi + 1) * POOL_GROUP
        cnt = jnp.minimum(pos + 1, w).astype(f32)
        pooled = acc[POOL_HALO:, lo:hi] / cnt - u[:, lo:hi]
        mixed = jnp.dot(pooled.astype(bf16), pw_ref[gi], preferred_element_type=f32)
        o_ref[:, lo:hi] = (mixed * ps_ref[:, lo:hi]).astype(o_ref.dtype)


def pool_mixer(proj, pool_w, pool_scale, *, batch, seq, ts):
    ns = seq // ts
    hb = ts // POOL_HALO
    return pl.pallas_call(
        functools.partial(_pool_kernel, ts=ts),
        out_shape=jax.ShapeDtypeStruct((batch * seq, POOL_WIDTH), bf16),
        grid=(batch, ns),
        in_specs=[
            pl.BlockSpec((ts, POOL_WIDTH), lambda b, s: (b * ns + s, 0)),
            pl.BlockSpec((POOL_HALO, POOL_WIDTH),
                         lambda b, s: (jnp.maximum((b * ns + s) * hb - 1, 0), 0)),
            pl.BlockSpec((len(POOL_WINDOWS), POOL_GROUP, POOL_GROUP), lambda b, s: (0, 0, 0)),
            pl.BlockSpec((1, POOL_WIDTH), lambda b, s: (0, 0)),
        ],
        out_specs=pl.BlockSpec((ts, POOL_WIDTH), lambda b, s: (b * ns + s, 0)),
        compiler_params=_cparams(("parallel", "arbitrary")),
        name="pool_mixer",
    )(proj, proj, pool_w, pool_scale.reshape(1, POOL_WIDTH))


def _head_ones():
    r = lax.broadcasted_iota(jnp.int32, (HG_LANES, HG_LANES), 0) // RWKV_HEAD
    c = lax.broadcasted_iota(jnp.int32, (HG_LANES, HG_LANES), 1) // RWKV_HEAD
    return jnp.where(r == c, 1.0, 0.0).astype(bf16)


def _head_sum(x, ones_bd):
    parts = [_split_dot(x[:, c:c + HG_LANES], ones_bd) for c in range(0, x.shape[1], HG_LANES)]
    return parts[0] if len(parts) == 1 else jnp.concatenate(parts, axis=1)


def _sigmoid(x):
    return 1.0 / (1.0 + jnp.exp(-x))


def _rwkv_prep_kernel(k_ref, lo_ref, w0_ref, w2_ref, a0_ref, a2_ref, g2_ref, kk_ref, ka_ref,
                      lw_out, k_out, kk_out, kka_out, g_out):
    k = k_ref[...].astype(f32)
    lo = lo_ref[...].astype(f32)
    w_lo = lo[:, LORA_W_OFF:LORA_W_OFF + 128]
    a_lo = lo[:, LORA_A_OFF:LORA_A_OFF + 128]
    g_lo = lo[:, LORA_G_OFF:LORA_G_OFF + 256]

    wx = w0_ref[...] + jnp.dot(jnp.tanh(w_lo).astype(bf16), w2_ref[...], preferred_element_type=f32)
    lw_out[...] = -DECAY_SCALE * _sigmoid(wx)
    a = _sigmoid(a0_ref[...] + jnp.dot(a_lo.astype(bf16), a2_ref[...], preferred_element_type=f32))
    g = jnp.dot(_sigmoid(g_lo).astype(bf16), g2_ref[...], preferred_element_type=f32)

    ones_bd = _head_ones()
    kk = k * kk_ref[...]
    kk = kk * lax.rsqrt(jnp.maximum(_head_sum(kk * kk, ones_bd), 1e-24))
    k = k * (1.0 + (a - 1.0) * ka_ref[...])

    k_out[...] = k.astype(k_out.dtype)
    kk_out[...] = kk.astype(kk_out.dtype)
    kka_out[...] = (kk * a).astype(kka_out.dtype)
    g_out[...] = g.astype(g_out.dtype)


def rwkv_prep(proj, w0, w2p, a0, a2p, g2p, k_k, k_a, *, ts):
    T = proj.shape[0]
    C = RWKV_WIDTH
    k_blk = (POOL_WIDTH + C) // C
    lora_blk = MAIN_COLS // LORA_PAD

    def full(shape):
        return pl.BlockSpec(shape, lambda i: (0,) * len(shape))

    row = lambda t: t.reshape(1, -1)
    out_sd = lambda dt: jax.ShapeDtypeStruct((T, C), dt)
    out_spec = pl.BlockSpec((ts, C), lambda i: (i, 0))
    return pl.pallas_call(
        _rwkv_prep_kernel,
        out_shape=[out_sd(f32), out_sd(bf16), out_sd(bf16), out_sd(bf16), out_sd(bf16)],
        grid=(T // ts,),
        in_specs=[
            pl.BlockSpec((ts, C), lambda i: (i, k_blk)),
            pl.BlockSpec((ts, LORA_PAD), lambda i: (i, lora_blk)),
            full((1, C)), full((128, C)), full((1, C)), full((128, C)), full((256, C)),
            full((1, C)), full((1, C)),
        ],
        out_specs=[out_spec] * 5,
        compiler_params=_cparams(("parallel",)),
        name="rwkv_prep",
    )(proj, proj, row(w0), w2p, row(a0), a2p, g2p, row(k_k), row(k_a))


def _bd_mask():
    r = lax.broadcasted_iota(jnp.int32, (HG_LANES, HG_LANES), 0) // RWKV_CHUNK
    c = lax.broadcasted_iota(jnp.int32, (HG_LANES, HG_LANES), 1) // RWKV_HEAD
    return r == c


def _rwkv_scan_kernel(r_ref, v_ref, lw_ref, k_ref, kk_ref, kka_ref, g_ref,
                      lnw_ref, lnb_ref, rk_ref, *rest, n_chunks, n_hg, n_cast):
    L = RWKV_CHUNK
    W = HG_LANES
    cast_in, o_ref, cast_out, state_ref = rest[:n_cast], rest[n_cast], rest[n_cast + 1:-1], rest[-1]
    for w_ref, wb_ref in zip(cast_in, cast_out):
        wb_ref[...] = w_ref[...].astype(bf16)

    @pl.when(pl.program_id(2) == 0)
    def _():
        state_ref[...] = jnp.zeros_like(state_ref)

    bd_mask = _bd_mask()
    ones_bd = _head_ones()

    def bd(x):
        tiled = jnp.concatenate([x] * HEADS_PER_STEP, axis=0)
        return jnp.where(bd_mask, tiled, 0.0).astype(bf16)

    def mm(a, b_bf16):
        return jnp.dot(a.astype(bf16), b_bf16, preferred_element_type=f32)

    def mm_nt(a, b_bf16):
        return lax.dot_general(a.astype(bf16), b_bf16, (((1,), (1,)), ((), ())),
                               preferred_element_type=f32)

    def mm_tn(a_bf16, b_bf16):
        return lax.dot_general(a_bf16, b_bf16, (((0,), (0,)), ((), ())),
                               preferred_element_type=f32)

    def split(x):
        hi = x.astype(bf16)
        return hi, (x - hi.astype(f32)).astype(bf16)

    def head_sums(xs):
        tot = mm(jnp.concatenate(xs, axis=0), ones_bd)
        return [tot[j * L:(j + 1) * L] for j in range(len(xs))]

    t_idx = lax.broadcasted_iota(jnp.int32, (L, W), 0)
    s_idx = lax.broadcasted_iota(jnp.int32, (L, W), 1) % L
    strict = t_idx > s_idx
    incl = t_idx >= s_idx
    eye_all = jnp.where(t_idx == s_idx, 1.0, 0.0)
    tri = jnp.where(lax.broadcasted_iota(jnp.int32, (L, L), 0)
                    >= lax.broadcasted_iota(jnp.int32, (L, L), 1), 1.0, 0.0).astype(bf16)
    blockdiag = (lax.broadcasted_iota(jnp.int32, (W, W), 0) // RWKV_HEAD
                 == lax.broadcasted_iota(jnp.int32, (W, W), 1) // RWKV_HEAD)

    def phase1(items):
        I = range(len(items))
        rows = [pl.ds(c * L, L) for _, c in items]
        cols = [pl.ds(h * W, W) for h, _ in items]
        ld = lambda ref, i: ref[rows[i], cols[i]]
        r = [ld(r_ref, i).astype(f32) for i in I]
        k = [ld(k_ref, i).astype(f32) for i in I]
        v = [ld(v_ref, i).astype(f32) for i in I]
        lw = [ld(lw_ref, i) for i in I]
        lw_s = [split(x) for x in lw]
        cs = [jnp.dot(tri, hi, preferred_element_type=f32) + jnp.dot(tri, lo, preferred_element_type=f32)
              for hi, lo in lw_s]
        yield
        g_inv = [jnp.exp(-x) for x in cs]
        to_end = [jnp.exp(x[L - 1:L, :] - x) for x in cs]
        a_t = [-ld(kk_ref, i).astype(f32) * jnp.exp(cs[i] - lw[i]) for i in I]
        r_t = [r[i] * jnp.exp(cs[i]) for i in I]
        kka = [ld(kka_ref, i).astype(f32) for i in I]
        b_t = [kka[i] * g_inv[i] for i in I]
        k_t = [k[i] * g_inv[i] for i in I]
        b_e = [kka[i] * to_end[i] for i in I]
        k_e = [k[i] * to_end[i] for i in I]
        v_bd = [bd(x) for x in v]
        ar = [jnp.concatenate([a_t[i], r_t[i]], axis=0) for i in I]
        m_b = [mm_nt(ar[i], bd(b_t[i])) for i in I]
        m_k = [mm_nt(ar[i], bd(k_t[i])) for i in I]
        m_ab = [jnp.where(strict, x[:L], 0.0) for x in m_b]
        m_rb = [jnp.where(incl, x[L:], 0.0) for x in m_b]
        m_ak = [jnp.where(strict, x[:L], 0.0) for x in m_k]
        m_rk = [jnp.where(incl, x[L:], 0.0) for x in m_k]
        rk_sum = head_sums([r[i] * k[i] * rk_ref[:, cols[i]] for i in I])
        bonus = [rk_sum[i] * v[i] for i in I]
        yield
        p = m_ab
        tinv = [eye_all + x for x in p]
        m = 2
        while m < L:
            p = [mm(x, bd(x)) for x in p]
            tinv = [tinv[i] + mm(tinv[i], bd(p[i])) for i in I]
            m *= 2
            yield
        x1 = [mm(m_ak[i], v_bd[i]) for i in I]
        u_v = [mm(tinv[i], bd(x1[i])) for i in I]
        a_h = [mm(tinv[i], bd(a_t[i])) for i in I]
        yield
        r_h = [r_t[i] + mm(m_rb[i], bd(a_h[i])) for i in I]
        y_v = [mm(m_rb[i], bd(u_v[i])) + mm(m_rk[i], v_bd[i]) for i in I]
        yield
        p_t = [jnp.where(blockdiag, mm_tn(a_h[i].astype(bf16), b_e[i].astype(bf16)), 0.0).astype(bf16)
               for i in I]
        g_t = [jnp.where(blockdiag,
                         mm_tn(jnp.concatenate([u_v[i], v[i]], axis=0).astype(bf16),
                               jnp.concatenate([b_e[i], k_e[i]], axis=0).astype(bf16)), 0.0) for i in I]
        decay = [jnp.exp(x[L - 1:L, :]) for x in cs]
        for i in I:
            done.append(dict(hg=items[i][0], rows=rows[i], cols=cols[i], r_h=r_h[i], y_v=y_v[i],
                             p_t=p_t[i], g_t=g_t[i], decay=decay[i], bonus=bonus[i]))
        yield

    def chain_step(chs):
        ys = []
        for ch in chs:
            state = states[ch['hg']]
            state_b = state.astype(bf16)
            ys.append(mm_nt(ch['r_h'], state_b) + ch['y_v'])
            states[ch['hg']] = (state * ch['decay']
                                + jnp.dot(state_b, ch['p_t'], preferred_element_type=f32) + ch['g_t'])
        means = head_sums(ys)
        ycs = [y - m * (1.0 / RWKV_HEAD) for y, m in zip(ys, means)]
        vrs = head_sums([yc * yc for yc in ycs])
        for ch, yc, vr in zip(chs, ycs, vrs):
            rw, cl = ch['rows'], ch['cols']
            yn = yc * lax.rsqrt(vr * (1.0 / RWKV_HEAD) + GN_EPS) * lnw_ref[:, cl] + lnb_ref[:, cl]
            o_ref[rw, cl] = ((yn + ch['bonus']) * g_ref[rw, cl].astype(f32)).astype(o_ref.dtype)

    states = [state_ref[h] for h in range(n_hg)]
    done = []
    half = max(n_chunks // 2, 1)
    groups = [[(h, c) for c in range(0, half) for h in range(n_hg)],
              [(h, c) for c in range(half, n_chunks) for h in range(n_hg)]]
    def by_chunk(n_items):
        return [done[j:j + n_hg] for j in range(0, n_items, n_hg)]

    for _ in phase1(groups[0]):
        pass
    pending = by_chunk(len(done))
    for _ in phase1(groups[1]):
        if pending:
            chain_step(pending.pop(0))
    for chs in pending + by_chunk(len(done))[len(groups[0]) // n_hg:]:
        chain_step(chs)
    for h in range(n_hg):
        state_ref[h] = states[h]


def rwkv_scan(proj, lw, k, kk, kka, g, ln_w, ln_b, r_k, cast_weights, *, batch, seq, rows, n_hg):
    nb = seq // rows
    T = batch * seq
    width = n_hg * HG_LANES
    n_col = RWKV_WIDTH // width
    n_steps = batch * n_col * nb
    r_blk = POOL_WIDTH // width
    v_blk = (POOL_WIDTH + 2 * RWKV_WIDTH) // width
    blk = pl.BlockSpec((rows, width), lambda b, h, c: (b * nb + c, h))
    par = pl.BlockSpec((1, width), lambda b, h, c: (0, h))
    slabs = [w.reshape(n_steps, w.size // (n_steps * w.shape[-1]), w.shape[-1]) for w in cast_weights]
    slab_spec = lambda w: pl.BlockSpec((1,) + w.shape[1:], lambda b, h, c: ((b * n_col + h) * nb + c, 0, 0))
    outs = pl.pallas_call(
        functools.partial(_rwkv_scan_kernel, n_chunks=rows // RWKV_CHUNK, n_hg=n_hg,
                          n_cast=len(slabs)),
        out_shape=[jax.ShapeDtypeStruct((T, RWKV_WIDTH), bf16)]
                  + [jax.ShapeDtypeStruct(w.shape, bf16) for w in slabs],
        grid=(batch, n_col, nb),
        in_specs=[pl.BlockSpec((rows, width), lambda b, h, c: (b * nb + c, r_blk + h)),
                  pl.BlockSpec((rows, width), lambda b, h, c: (b * nb + c, v_blk + h))]
                 + [blk] * 5 + [par] * 3 + [slab_spec(w) for w in slabs],
        out_specs=[blk] + [slab_spec(w) for w in slabs],
        scratch_shapes=[pltpu.VMEM((n_hg, HG_LANES, HG_LANES), f32)],
        compiler_params=_cparams(("parallel", "parallel", "arbitrary")),
        name="rwkv_scan",
    )(proj, proj, lw, k, kk, kka, g, ln_w.reshape(1, -1), ln_b.reshape(1, -1), r_k.reshape(1, -1),
      *slabs)
    return outs[0], [o.reshape(w.shape) for o, w in zip(outs[1:], cast_weights)]


def _route(hn, w, b):
    w_hi = w.astype(bf16)
    w_lo = (w - w_hi.astype(f32)).astype(bf16)
    h_hi = hn.astype(bf16)
    h_lo = (hn - h_hi.astype(f32)).astype(bf16)
    hh = jnp.dot(h_hi, jnp.concatenate([w_hi, w_lo], axis=1), preferred_element_type=f32)
    logits = (hh[:, :ROUTER_LANES] + hh[:, ROUTER_LANES:]
              + jnp.dot(h_lo, w_hi, preferred_element_type=f32)) + b

    lane_i = lax.broadcasted_iota(jnp.int32, logits.shape, 1)
    lane = lane_i.astype(f32)
    neg = -jnp.inf
    big = float(ROUTER_LANES)
    is_g = lane_i < MOE_GROUPS
    gl = jnp.where(is_g, logits, neg)
    gmax = jnp.max(gl, axis=-1, keepdims=True)
    g_idx = jnp.min(jnp.where(gl == gmax, lane, big), axis=-1, keepdims=True)
    g_w = 1.0 / jnp.sum(jnp.exp(gl - gmax), axis=-1, keepdims=True)

    lo_lane = MOE_GROUPS + g_idx * MOE_EXPERTS_PER_GROUP
    sel = (lane >= lo_lane) & (lane < lo_lane + MOE_EXPERTS_PER_GROUP)
    el = jnp.where(sel, logits, neg)
    v1 = jnp.max(el, axis=-1, keepdims=True)
    i1 = jnp.min(jnp.where(el == v1, lane, big), axis=-1, keepdims=True)
    el2 = jnp.where(lane == i1, neg, el)
    v2 = jnp.max(el2, axis=-1, keepdims=True)
    i2 = jnp.min(jnp.where(el2 == v2, lane, big), axis=-1, keepdims=True)
    e2 = jnp.exp(v2 - v1)
    p1 = 1.0 / (1.0 + e2)
    p2 = e2 / (1.0 + e2)
    return (jnp.where(lane_i == 0, i1 - MOE_GROUPS, 0.0)
            + jnp.where(lane_i == 1, i2 - MOE_GROUPS, 0.0)
            + jnp.where(lane_i == 2, p1 * g_w, 0.0)
            + jnp.where(lane_i == 3, p2 * g_w, 0.0))


def _attn_oproj_router_kernel(q_ref, kv_ref, wo_ref, res_ref, g_ref, wr_ref, br_ref, h_ref, xr_ref,
                              rec_ref, *, tn):
    hd = XATTN_HEAD_DIM
    heads = []
    for h in range(XATTN_HEADS):
        k = kv_ref[:, h * hd:(h + 1) * hd]
        v = kv_ref[:, (XATTN_HEADS + h) * hd:(XATTN_HEADS + h + 1) * hd]
        s = lax.dot_general(q_ref[:, h * hd:(h + 1) * hd], k, (((1,), (1,)), ((), ())),
                            preferred_element_type=f32) * (hd ** -0.5)
        e = jnp.exp(s - jnp.max(s, axis=-1, keepdims=True))
        l = jnp.sum(e, axis=-1, keepdims=True)
        heads.append((jnp.dot(e.astype(bf16), v, preferred_element_type=f32) / l).astype(bf16))
    o = jnp.concatenate(heads, axis=1)
    for n0 in range(0, h_ref.shape[1], tn):
        h_ref[:, n0:n0 + tn] = res_ref[:, n0:n0 + tn] + jnp.dot(
            o, wo_ref[:, n0:n0 + tn], preferred_element_type=f32)
    hn = _rmsnorm_rows(h_ref[...], g_ref[...])
    D = hn.shape[1]
    rec = _route(hn, wr_ref[...], br_ref[...])
    xr_ref[:, :D] = hn
    xr_ref[:, D:] = rec
    rec_ref[...] = rec


def attn_oproj_router(q, kv, w_o, res, g, w_router, b_router, *, tm, tn, seq):
    T, D = res.shape
    tiles_per_seq = seq // tm
    row = lambda width: pl.BlockSpec((tm, width), lambda i: (i, 0))
    return pl.pallas_call(
        functools.partial(_attn_oproj_router_kernel, tn=tn),
        out_shape=[jax.ShapeDtypeStruct((T, D), f32),
                   jax.ShapeDtypeStruct((T, D + ROUTER_LANES), f32),
                   jax.ShapeDtypeStruct((T, ROUTER_LANES), f32)],
        grid=(T // tm,),
        in_specs=[row(D), pl.BlockSpec((MEM_LEN, 2 * D), lambda i: (i // tiles_per_seq, 0)),
                  _resident((D, D)), row(D), _resident((1, D)),
                  _resident((D, ROUTER_LANES)), _resident((1, ROUTER_LANES))],
        out_specs=[row(D), row(D + ROUTER_LANES), row(ROUTER_LANES)],
        compiler_params=_cparams(("parallel",)),
        name="attn_oproj_router",
    )(q, kv, w_o, res, g.reshape(1, D), w_router, b_router)


MOE_PAIRS = ((0, 1), (0, 2), (0, 3), (1, 3), (1, 2), (3, 2))
MOE_CLASSES = MOE_GROUPS * len(MOE_PAIRS)


def _route_metadata(rec, *, tm, n_tiles):
    ids = rec[:, 0:2].astype(jnp.int32)
    lo = jnp.minimum(ids[:, 0], ids[:, 1])
    hi = jnp.maximum(ids[:, 0], ids[:, 1])
    a, b = lo % MOE_EXPERTS_PER_GROUP, hi % MOE_EXPERTS_PER_GROUP
    pair_id = (a * (7 - a)) // 2 + b - a - 1
    pair_id = pair_id + (pair_id == 3).astype(jnp.int32) - (pair_id == 4).astype(jnp.int32)
    cls = (lo // MOE_EXPERTS_PER_GROUP) * len(MOE_PAIRS) + pair_id
    onehot = (cls[:, None] == jnp.arange(MOE_CLASSES, dtype=jnp.int32)[None, :]).astype(jnp.int32)
    csum = jnp.cumsum(onehot, axis=0)
    counts = csum[-1]
    rank = jnp.sum(csum * onehot, axis=1) - 1
    padded = ((counts + tm - 1) // tm) * tm
    ends = jnp.cumsum(padded)
    pos = jnp.sum(onehot * (ends - padded)[None, :], axis=1) + rank
    tile_start = jnp.arange(n_tiles, dtype=jnp.int32) * tm
    n_valid = ends[-1] // tm
    tile_cls = jnp.sum((tile_start[:, None] >= ends[None, :]).astype(jnp.int32), axis=1)
    last_cls = jnp.sum((ends[-1] - 1 >= ends).astype(jnp.int32))
    tile_cls = jnp.where(tile_start < ends[-1], tile_cls, last_cls)
    pair = jnp.asarray(MOE_PAIRS, jnp.int32)[tile_cls % len(MOE_PAIRS)]
    base = (tile_cls // len(MOE_PAIRS)) * MOE_EXPERTS_PER_GROUP
    idle = n_valid + jnp.arange(MOE_CLASSES, dtype=jnp.int32)
    fill_start = jnp.concatenate([jnp.maximum(ends - tm, 0), jnp.minimum(idle, n_tiles - 1) * tm])
    fill_on = jnp.concatenate([counts > 0, idle < n_tiles]).astype(jnp.int32)
    return pos, base + pair[:, 0], base + pair[:, 1], n_valid.reshape(1), fill_start, fill_on


def _moe_dispatch_kernel(pos_ref, fs_ref, fo_ref, x_ref, xs_hbm, stage, zbuf, sem, zsem, *, tm, tz):
    i = pl.program_id(0)
    n = pl.num_programs(0)

    def fill_copy(c):
        return pltpu.make_async_copy(zbuf, xs_hbm.at[pl.ds(pl.multiple_of(fs_ref[c], tz), tz), :], zsem)

    @pl.when(i == 0)
    def _():
        zbuf[...] = jnp.zeros_like(zbuf)
        for c in range(2 * MOE_CLASSES):
            @pl.when(fo_ref[c] > 0)
            def _():
                fill_copy(c).start()
        for c in range(2 * MOE_CLASSES):
            @pl.when(fo_ref[c] > 0)
            def _():
                fill_copy(c).wait()

    def tile_wait(slot):
        pltpu.make_async_copy(stage.at[slot], xs_hbm.at[pl.ds(0, tm), :], sem.at[slot]).wait()

    slot = i % 2

    @pl.when(i >= 2)
    def _():
        tile_wait(slot)

    stage[slot] = x_ref[...]

    def body(r, carry):
        p = pos_ref[i * tm + r]
        pltpu.make_async_copy(stage.at[slot, pl.ds(r, 1), :], xs_hbm.at[pl.ds(p, 1), :],
                              sem.at[slot]).start()
        return carry
    lax.fori_loop(0, tm, body, 0, unroll=8)

    @pl.when(i == n - 1)
    def _():
        tile_wait(slot)

        @pl.when(n >= 2)
        def _():
            tile_wait(1 - slot)


def moe_dispatch(xr, pos, fill_start, fill_on, *, tm, tz, n_slots):
    T, width = xr.shape
    grid_spec = pltpu.PrefetchScalarGridSpec(
        num_scalar_prefetch=3,
        grid=(T // tm,),
        in_specs=[pl.BlockSpec((tm, width), lambda i, pos, fs, fo: (i, 0))],
        out_specs=pl.BlockSpec(memory_space=pl.ANY),
        scratch_shapes=[pltpu.VMEM((2, tm, width), f32), pltpu.VMEM((tz, width), f32),
                        pltpu.SemaphoreType.DMA((2,)), pltpu.SemaphoreType.DMA(())],
    )
    return pl.pallas_call(
        functools.partial(_moe_dispatch_kernel, tm=tm, tz=tz),
        out_shape=jax.ShapeDtypeStruct((n_slots, width), f32),
        grid_spec=grid_spec,
        compiler_params=_cparams(("arbitrary",)),
        name="moe_dispatch",
    )(pos, fill_start, fill_on, xr)


def _moe_pair_kernel(ea_ref, eb_ref, nv_ref, xs_ref, wga, wua, wda, wgb, wub, wdb, o_ref):
    @pl.when(pl.program_id(0) >= nv_ref[0])
    def _():
        o_ref[...] = jnp.zeros_like(o_ref)

    @pl.when(pl.program_id(0) < nv_ref[0])
    def _():
        D = o_ref.shape[1]
        x = xs_ref[:, :D].astype(bf16)
        rec = xs_ref[:, D:]
        first = rec[:, 0:1] == ea_ref[pl.program_id(0)].astype(f32)
        w_a = jnp.where(first, rec[:, 2:3], rec[:, 3:4])
        w_b = jnp.where(first, rec[:, 3:4], rec[:, 2:3])

        def expert(wg, wu, wd):
            gate = jnp.dot(x, wg[0], preferred_element_type=f32)
            up = jnp.dot(x, wu[0], preferred_element_type=f32)
            hid = gate * _sigmoid(gate) * up
            return jnp.dot(hid.astype(bf16), wd[0], preferred_element_type=f32)

        o_ref[...] = w_a * expert(wga, wua, wda) + w_b * expert(wgb, wub, wdb)


def moe_pair_experts(xs, e_a, e_b, n_valid, wg, wu, wd, *, tm, n_tiles):
    width = xs.shape[1]
    D = wg.shape[1]
    held = lambda i, ea, eb, nv: (jnp.minimum(i, nv[0] - 1), 0)
    tile = lambda i, ea, eb, nv: (i, 0)
    wa = lambda shape: pl.BlockSpec(shape, lambda i, ea, eb, nv: (ea[i], 0, 0))
    wb = lambda shape: pl.BlockSpec(shape, lambda i, ea, eb, nv: (eb[i], 0, 0))
    grid_spec = pltpu.PrefetchScalarGridSpec(
        num_scalar_prefetch=3,
        grid=(n_tiles,),
        in_specs=[pl.BlockSpec((tm, width), held),
                  wa((1, D, MOE_FF)), wa((1, D, MOE_FF)), wa((1, MOE_FF, D)),
                  wb((1, D, MOE_FF)), wb((1, D, MOE_FF)), wb((1, MOE_FF, D))],
        out_specs=pl.BlockSpec((tm, D), tile),
    )
    return pl.pallas_call(
        _moe_pair_kernel,
        out_shape=jax.ShapeDtypeStruct((n_tiles * tm, D), f32),
        grid_spec=grid_spec,
        compiler_params=_cparams(("arbitrary",)),
        name="moe_experts",
    )(e_a, e_b, n_valid, xs, wg, wu, wd, wg, wu, wd)


def _moe_combine_kernel(pos_ref, h_ref, y_hbm, gf_ref, o_ref, ybuf, sem, *, tm):
    i = pl.program_id(0)
    n = pl.num_programs(0)

    def gather(tile, slot):
        def body(r, carry):
            p = pos_ref[tile * tm + r]
            pltpu.make_async_copy(y_hbm.at[pl.ds(p, 1), :],
                                  ybuf.at[slot, pl.ds(r, 1), :], sem.at[slot]).start()
            return carry
        lax.fori_loop(0, tm, body, 0, unroll=8)

    @pl.when(i == 0)
    def _():
        gather(0, 0)

    @pl.when(i + 1 < n)
    def _():
        gather(i + 1, (i + 1) % 2)

    slot = i % 2
    pltpu.make_async_copy(y_hbm.at[pl.ds(0, tm), :], ybuf.at[slot], sem.at[slot]).wait()
    o_ref[...] = _rmsnorm_rows(h_ref[...] + ybuf[slot], gf_ref[...])


def moe_combine(h, y_sorted, pos, g_final, *, tm):
    T, D = h.shape
    grid_spec = pltpu.PrefetchScalarGridSpec(
        num_scalar_prefetch=1,
        grid=(T // tm,),
        in_specs=[
            pl.BlockSpec((tm, D), lambda i, pos: (i, 0)),
            pl.BlockSpec(memory_space=pl.ANY),
            pl.BlockSpec((1, D), lambda i, pos: (0, 0)),
        ],
        out_specs=pl.BlockSpec((tm, D), lambda i, pos: (i, 0)),
        scratch_shapes=[pltpu.VMEM((2, tm, D), f32), pltpu.SemaphoreType.DMA((2,))],
    )
    return pl.pallas_call(
        functools.partial(_moe_combine_kernel, tm=tm),
        out_shape=jax.ShapeDtypeStruct((T, D), f32),
        grid_spec=grid_spec,
        compiler_params=_cparams(("arbitrary",)),
        name="moe_combine",
    )(pos, h, y_sorted, g_final.reshape(1, D))


def _pad_rows(w, rows):
    return jnp.pad(w, ((0, rows - w.shape[0]), (0, 0)))


def _pack_lora_cols(t):
    o1, o2 = DECAY_LORA, DECAY_LORA + AAA_LORA
    pad = lambda a, n: jnp.pad(a, [(0, 0)] * (a.ndim - 1) + [(0, n - a.shape[-1])])
    return jnp.concatenate([pad(t[..., :o1], 128), pad(t[..., o1:o2], 128),
                            pad(t[..., o2:], 256)], axis=-1)


def _layer(x2, mem2, p, *, batch, seq, cfg):
    D = D_MODEL
    w_in = p['w_in']
    w_main = w_in[:, :MAIN_COLS].astype(bf16)
    w_lora = _pack_lora_cols(w_in[:, MAIN_COLS:]).astype(bf16)
    mu_cols = jnp.concatenate([jnp.zeros((POOL_WIDTH,), f32), p['rwkv_mu'][:3 * RWKV_WIDTH],
                               _pack_lora_cols(p['rwkv_mu'][3 * RWKV_WIDTH:])])

    proj = in_proj_shift(x2, p['norm_mix_g'], w_main, w_lora, mu_cols,
                         tm=cfg['tm'], tn=cfg['tn_in'], seq=seq)
    pool_out = pool_mixer(proj, p['pool_w'].astype(bf16), p['pool_scale'],
                          batch=batch, seq=seq, ts=cfg['ts_pool'])
    lw, k, kk, kka, g = rwkv_prep(
        proj, p['rwkv_w0'], _pad_rows(p['rwkv_w2'], 128).astype(bf16),
        p['rwkv_a0'], _pad_rows(p['rwkv_a2'], 128).astype(bf16),
        _pad_rows(p['rwkv_g2'], 256).astype(bf16), p['rwkv_k_k'], p['rwkv_k_a'], ts=cfg['ts_prep'])
    rwkv_out, (w_out, w_q, w_kv, w_o, wg, wu, wd) = rwkv_scan(
        proj, lw, k, kk, kka, g, p['rwkv_ln_w'], p['rwkv_ln_b'], p['rwkv_r_k'],
        [p['w_out'], p['xattn_w_q'], p['xattn_w_kv'], p['xattn_w_o'],
         p['moe_w_gate'], p['moe_w_up'], p['moe_w_down']],
        batch=batch, seq=seq, rows=cfg['scan_rows'], n_hg=cfg['scan_hg'])
    h1 = matmul_residual_resident([pool_out, rwkv_out], [w_out[:POOL_WIDTH], w_out[POOL_WIDTH:]],
                                  x2, tm=cfg['tm'], tn=cfg['tn'], name="out_proj")

    q = norm_matmul_resident(h1, p['norm_xattn_g'], w_q,
                             tm=cfg['tm'], tn=cfg['tn'], name="q_proj")
    kv = norm_matmul_resident(mem2, p['norm_mem_g'], w_kv,
                              tm=min(cfg['tm'], mem2.shape[0]), tn=cfg['tn'], name="kv_proj")

    w_router = jnp.pad(jnp.concatenate([p['moe_w_group'], p['moe_w_expert']], axis=1),
                       ((0, 0), (0, ROUTER_LANES - MOE_GROUPS - MOE_EXPERTS)))
    b_router = jnp.pad(jnp.concatenate([p['moe_b_group'], p['moe_b_expert']]),
                       (0, ROUTER_LANES - MOE_GROUPS - MOE_EXPERTS)).reshape(1, ROUTER_LANES)
    h2, xr, rec = attn_oproj_router(q, kv, w_o, h1, p['norm_ffn_g'],
                               w_router, b_router, tm=cfg['tm'], tn=cfg['tn'], seq=seq)
    tm_e = cfg['tm_moe']
    n_tiles = xr.shape[0] // tm_e + MOE_CLASSES
    pos, e_a, e_b, n_valid, fill_start, fill_on = _route_metadata(
        rec, tm=tm_e, n_tiles=n_tiles)
    xs = moe_dispatch(xr, pos, fill_start, fill_on, tm=cfg['tm_disp'], tz=tm_e,
                      n_slots=n_tiles * tm_e)
    y_sorted = moe_pair_experts(xs, e_a, e_b, n_valid, wg, wu, wd, tm=tm_e, n_tiles=n_tiles)
    return moe_combine(h2, y_sorted, pos, p['norm_final_g'], tm=cfg['tm_comb'])


_CFG = dict(tm=512, tn=1024, tn_in=512, ts_pool=512, ts_prep=512, scan_rows=256, scan_hg=4,
            tm_moe=256, tm_disp=512, tm_comb=512)


def kernel(x, mem, norm_mix_g, w_in, pool_w, pool_scale, rwkv_mu, rwkv_w0, rwkv_w2, rwkv_a0,
           rwkv_a2, rwkv_g2, rwkv_k_k, rwkv_k_a, rwkv_r_k, rwkv_ln_w, rwkv_ln_b, w_out,
           norm_xattn_g, norm_mem_g, xattn_w_q, xattn_w_kv, xattn_w_o, norm_ffn_g,
           moe_w_group, moe_b_group, moe_w_expert, moe_b_expert, moe_w_gate, moe_w_up,
           moe_w_down, norm_final_g):
    batch, seq, D = x.shape
    p = dict(norm_mix_g=norm_mix_g[0], w_in=w_in[0], pool_w=pool_w[0], pool_scale=pool_scale[0],
             rwkv_mu=rwkv_mu[0], rwkv_w0=rwkv_w0[0], rwkv_w2=rwkv_w2[0], rwkv_a0=rwkv_a0[0],
             rwkv_a2=rwkv_a2[0], rwkv_g2=rwkv_g2[0], rwkv_k_k=rwkv_k_k[0], rwkv_k_a=rwkv_k_a[0],
             rwkv_r_k=rwkv_r_k[0], rwkv_ln_w=rwkv_ln_w[0], rwkv_ln_b=rwkv_ln_b[0], w_out=w_out[0],
             norm_xattn_g=norm_xattn_g[0], norm_mem_g=norm_mem_g[0], xattn_w_q=xattn_w_q[0],
             xattn_w_kv=xattn_w_kv[0], xattn_w_o=xattn_w_o[0], norm_ffn_g=norm_ffn_g[0],
             moe_w_group=moe_w_group[0], moe_b_group=moe_b_group[0], moe_w_expert=moe_w_expert[0],
             moe_b_expert=moe_b_expert[0], moe_w_gate=moe_w_gate[0], moe_w_up=moe_w_up[0],
             moe_w_down=moe_w_down[0], norm_final_g=norm_final_g)
    out = _layer(x.reshape(batch * seq, D), mem.reshape(batch * MEM_LEN, D), p,
                 batch=batch, seq=seq, cfg=_CFG)
    return out.reshape(batch, seq, D)
```

```python
import functools

import jax
import jax.numpy as jnp
from jax import lax
from jax.experimental import pallas as pl
from jax.experimental.pallas import tpu as pltpu

f32 = jnp.float32
bf16 = jnp.bfloat16

D_MODEL = 2048
MEM_LEN = 256
NORM_EPS = 1e-6

POOL_WIDTH = 1024
POOL_WINDOWS = (2, 4, 8, 16)
POOL_GROUP = 256
POOL_HALO = 16
RWKV_WIDTH = 1024
RWKV_HEAD = 64
RWKV_HEADS = 16
GN_EPS = 64e-5
DECAY_SCALE = 0.6065306597126334
DECAY_LORA = 64
AAA_LORA = 64
GATE_LORA = 160
LORA_PAD = 512
LORA_W_OFF, LORA_A_OFF, LORA_G_OFF = 0, 128, 256
MAIN_COLS = POOL_WIDTH + 3 * RWKV_WIDTH
PROJ_COLS = MAIN_COLS + LORA_PAD

RWKV_CHUNK = 64
HEADS_PER_STEP = 4
HG_LANES = HEADS_PER_STEP * RWKV_HEAD

XATTN_HEADS = 4
XATTN_HEAD_DIM = 512

MOE_GROUPS = 4
MOE_EXPERTS_PER_GROUP = 4
MOE_EXPERTS = 16
MOE_FF = 512
ROUTER_LANES = 128

VMEM_LIMIT = 56 * 1024 * 1024


def _cparams(sem):
    return pltpu.CompilerParams(dimension_semantics=sem, vmem_limit_bytes=VMEM_LIMIT)


def _split_dot(x, w_bf16):
    hi = x.astype(bf16)
    lo = (x - hi.astype(f32)).astype(bf16)
    return (jnp.dot(hi, w_bf16, preferred_element_type=f32)
            + jnp.dot(lo, w_bf16, preferred_element_type=f32))


def _resident(shape):
    return pl.BlockSpec(shape, lambda i: (0,) * len(shape), pipeline_mode=pl.Buffered(1))


def _rmsnorm_rows(x, g):
    ms = jnp.mean(x * x, axis=-1, keepdims=True)
    return x * lax.rsqrt(ms + NORM_EPS) * g


def _norm_mm_res_kernel(x_ref, g_ref, w_ref, o_ref, *, tn):
    xn = _rmsnorm_rows(x_ref[...], g_ref[...]).astype(bf16)
    for n0 in range(0, o_ref.shape[1], tn):
        o_ref[:, n0:n0 + tn] = jnp.dot(xn, w_ref[:, n0:n0 + tn],
                                       preferred_element_type=f32).astype(o_ref.dtype)


def norm_matmul_resident(x, g, w, *, tm, tn, name):
    M, K = x.shape
    N = w.shape[1]
    return pl.pallas_call(
        functools.partial(_norm_mm_res_kernel, tn=tn),
        out_shape=jax.ShapeDtypeStruct((M, N), bf16),
        grid=(M // tm,),
        in_specs=[pl.BlockSpec((tm, K), lambda i: (i, 0)), _resident((1, K)), _resident((K, N))],
        out_specs=pl.BlockSpec((tm, N), lambda i: (i, 0)),
        compiler_params=_cparams(("parallel",)),
        name=name,
    )(x, g.reshape(1, K), w)


def _in_proj_kernel(x_ref, xh_ref, g_ref, w_ref, wl_ref, mu_ref, o_ref, *, tn, tiles_per_seq):
    first = pl.program_id(0) % tiles_per_seq == 0
    xn = _rmsnorm_rows(x_ref[...], g_ref[...]).astype(bf16)
    xh = jnp.where(first, 0.0, _rmsnorm_rows(xh_ref[...], g_ref[...])).astype(bf16)
    lhs = jnp.concatenate([xh, xn], axis=0)
    for n0 in range(0, o_ref.shape[1], tn):
        w = w_ref[:, n0:n0 + tn] if n0 < MAIN_COLS else wl_ref[:, n0 - MAIN_COLS:n0 - MAIN_COLS + tn]
        res = jnp.dot(lhs, w, preferred_element_type=f32)
        p = res[POOL_HALO:]
        if n0 >= POOL_WIDTH:
            prev = pltpu.roll(res, 1, 0)[POOL_HALO:]
            p = p + (prev - p) * mu_ref[:, n0:n0 + tn]
        o_ref[:, n0:n0 + tn] = p.astype(o_ref.dtype)


def in_proj_shift(x, g, w_main, w_lora, mu_cols, *, tm, tn, seq):
    M, K = x.shape
    N = w_main.shape[1] + w_lora.shape[1]
    hb = tm // POOL_HALO
    return pl.pallas_call(
        functools.partial(_in_proj_kernel, tn=tn, tiles_per_seq=seq // tm),
        out_shape=jax.ShapeDtypeStruct((M, N), bf16),
        grid=(M // tm,),
        in_specs=[pl.BlockSpec((tm, K), lambda i: (i, 0)),
                  pl.BlockSpec((POOL_HALO, K), lambda i: (jnp.maximum(i * hb - 1, 0), 0)),
                  _resident((1, K)), _resident(w_main.shape), _resident(w_lora.shape),
                  _resident((1, N))],
        out_specs=pl.BlockSpec((tm, N), lambda i: (i, 0)),
        compiler_params=_cparams(("parallel",)),
        name="in_proj",
    )(x, x, g.reshape(1, K), w_main, w_lora, mu_cols.reshape(1, N))


def _mm_res_resident_kernel(*refs, n, tn):
    a_refs, w_refs, res_ref, o_ref = refs[:n], refs[n:2 * n], refs[2 * n], refs[2 * n + 1]
    for n0 in range(0, o_ref.shape[1], tn):
        acc = res_ref[:, n0:n0 + tn]
        for a_ref, w_ref in zip(a_refs, w_refs):
            acc = acc + jnp.dot(a_ref[...], w_ref[:, n0:n0 + tn], preferred_element_type=f32)
        o_ref[:, n0:n0 + tn] = acc


def matmul_residual_resident(a_list, w_list, res, *, tm, tn, name):
    M, N = res.shape
    n = len(a_list)
    return pl.pallas_call(
        functools.partial(_mm_res_resident_kernel, n=n, tn=tn),
        out_shape=jax.ShapeDtypeStruct((M, N), f32),
        grid=(M // tm,),
        in_specs=([pl.BlockSpec((tm, a.shape[1]), lambda i: (i, 0)) for a in a_list]
                  + [_resident(w.shape) for w in w_list]
                  + [pl.BlockSpec((tm, N), lambda i: (i, 0))]),
        out_specs=pl.BlockSpec((tm, N), lambda i: (i, 0)),
        compiler_params=_cparams(("parallel",)),
        name=name,
    )(*a_list, *w_list, res)


def _pool_kernel(u_ref, halo_ref, pw_ref, ps_ref, o_ref, *, ts):
    s = pl.program_id(1)
    u = u_ref[...].astype(f32)
    halo = jnp.where(s > 0, halo_ref[...].astype(f32), 0.0)
    ext = jnp.concatenate([halo, u], axis=0)
    pos = s * ts + lax.broadcasted_iota(jnp.int32, (ts, 1), 0)
    acc = ext
    shift = 1
    for gi, w in enumerate(POOL_WINDOWS):
        while shift < w:
            acc = acc + pltpu.roll(acc, shift, 0)
            shift *= 2
        lo, hi = gi * POOL_GROUP, (gi + 1) * POOL_GROUP
        cnt = jnp.minimum(pos + 1, w).astype(f32)
        pooled = acc[POOL_HALO:, lo:hi] / cnt - u[:, lo:hi]
        mixed = jnp.dot(pooled.astype(bf16), pw_ref[gi], preferred_element_type=f32)
        o_ref[:, lo:hi] = (mixed * ps_ref[:, lo:hi]).astype(o_ref.dtype)


def pool_mixer(proj, pool_w, pool_scale, *, batch, seq, ts):
    ns = seq // ts
    hb = ts // POOL_HALO
    return pl.pallas_call(
        functools.partial(_pool_kernel, ts=ts),
        out_shape=jax.ShapeDtypeStruct((batch * seq, POOL_WIDTH), bf16),
        grid=(batch, ns),
        in_specs=[
            pl.BlockSpec((ts, POOL_WIDTH), lambda b, s: (b * ns + s, 0)),
            pl.BlockSpec((POOL_HALO, POOL_WIDTH),
                         lambda b, s: (jnp.maximum((b * ns + s) * hb - 1, 0), 0)),
            pl.BlockSpec((len(POOL_WINDOWS), POOL_GROUP, POOL_GROUP), lambda b, s: (0, 0, 0)),
            pl.BlockSpec((1, POOL_WIDTH), lambda b, s: (0, 0)),
        ],
        out_specs=pl.BlockSpec((ts, POOL_WIDTH), lambda b, s: (b * ns + s, 0)),
        compiler_params=_cparams(("parallel", "arbitrary")),
        name="pool_mixer",
    )(proj, proj, pool_w, pool_scale.reshape(1, POOL_WIDTH))


def _head_ones():
    r = lax.broadcasted_iota(jnp.int32, (HG_LANES, HG_LANES), 0) // RWKV_HEAD
    c = lax.broadcasted_iota(jnp.int32, (HG_LANES, HG_LANES), 1) // RWKV_HEAD
    return jnp.where(r == c, 1.0, 0.0).astype(bf16)


def _head_sum(x, ones_bd):
    parts = [_split_dot(x[:, c:c + HG_LANES], ones_bd) for c in range(0, x.shape[1], HG_LANES)]
    return parts[0] if len(parts) == 1 else jnp.concatenate(parts, axis=1)


def _sigmoid(x):
    return 1.0 / (1.0 + jnp.exp(-x))


def _rwkv_prep_kernel(k_ref, lo_ref, w0_ref, w2_ref, a0_ref, a2_ref, g2_ref, kk_ref, ka_ref,
                      lw_out, k_out, kk_out, kka_out, g_out):
    k = k_ref[...].astype(f32)
    lo = lo_ref[...].astype(f32)
    w_lo = lo[:, LORA_W_OFF:LORA_W_OFF + 128]
    a_lo = lo[:, LORA_A_OFF:LORA_A_OFF + 128]
    g_lo = lo[:, LORA_G_OFF:LORA_G_OFF + 256]

    wx = w0_ref[...] + jnp.dot(jnp.tanh(w_lo).astype(bf16), w2_ref[...], preferred_element_type=f32)
    lw_out[...] = -DECAY_SCALE * _sigmoid(wx)
    a = _sigmoid(a0_ref[...] + jnp.dot(a_lo.astype(bf16), a2_ref[...], preferred_element_type=f32))
    g = jnp.dot(_sigmoid(g_lo).astype(bf16), g2_ref[...], preferred_element_type=f32)

    ones_bd = _head_ones()
    kk = k * kk_ref[...]
    kk = kk * lax.rsqrt(jnp.maximum(_head_sum(kk * kk, ones_bd), 1e-24))
    k = k * (1.0 + (a - 1.0) * ka_ref[...])

    k_out[...] = k.astype(k_out.dtype)
    kk_out[...] = kk.astype(kk_out.dtype)
    kka_out[...] = (kk * a).astype(kka_out.dtype)
    g_out[...] = g.astype(g_out.dtype)


def rwkv_prep(proj, w0, w2p, a0, a2p, g2p, k_k, k_a, *, ts):
    T = proj.shape[0]
    C = RWKV_WIDTH
    k_blk = (POOL_WIDTH + C) // C
    lora_blk = MAIN_COLS // LORA_PAD

    def full(shape):
        return pl.BlockSpec(shape, lambda i: (0,) * len(shape))

    row = lambda t: t.reshape(1, -1)
    out_sd = lambda dt: jax.ShapeDtypeStruct((T, C), dt)
    out_spec = pl.BlockSpec((ts, C), lambda i: (i, 0))
    return pl.pallas_call(
        _rwkv_prep_kernel,
        out_shape=[out_sd(f32), out_sd(bf16), out_sd(bf16), out_sd(bf16), out_sd(bf16)],
        grid=(T // ts,),
        in_specs=[
            pl.BlockSpec((ts, C), lambda i: (i, k_blk)),
            pl.BlockSpec((ts, LORA_PAD), lambda i: (i, lora_blk)),
            full((1, C)), full((128, C)), full((1, C)), full((128, C)), full((256, C)),
            full((1, C)), full((1, C)),
        ],
        out_specs=[out_spec] * 5,
        compiler_params=_cparams(("parallel",)),
        name="rwkv_prep",
    )(proj, proj, row(w0), w2p, row(a0), a2p, g2p, row(k_k), row(k_a))


def _bd_mask():
    r = lax.broadcasted_iota(jnp.int32, (HG_LANES, HG_LANES), 0) // RWKV_CHUNK
    c = lax.broadcasted_iota(jnp.int32, (HG_LANES, HG_LANES), 1) // RWKV_HEAD
    return r == c


def _rwkv_scan_kernel(r_ref, v_ref, lw_ref, k_ref, kk_ref, kka_ref, g_ref,
                      lnw_ref, lnb_ref, rk_ref, *rest, n_chunks, n_hg, n_cast):
    L = RWKV_CHUNK
    W = HG_LANES
    cast_in, o_ref, cast_out, state_ref = rest[:n_cast], rest[n_cast], rest[n_cast + 1:-1], rest[-1]
    for w_ref, wb_ref in zip(cast_in, cast_out):
        wb_ref[...] = w_ref[...].astype(bf16)

    @pl.when(pl.program_id(2) == 0)
    def _():
        state_ref[...] = jnp.zeros_like(state_ref)

    bd_mask = _bd_mask()
    ones_bd = _head_ones()

    def bd(x):
        tiled = jnp.concatenate([x] * HEADS_PER_STEP, axis=0)
        return jnp.where(bd_mask, tiled, 0.0).astype(bf16)

    def mm(a, b_bf16):
        return jnp.dot(a.astype(bf16), b_bf16, preferred_element_type=f32)

    def mm_nt(a, b_bf16):
        return lax.dot_general(a.astype(bf16), b_bf16, (((1,), (1,)), ((), ())),
                               preferred_element_type=f32)

    def mm_tn(a_bf16, b_bf16):
        return lax.dot_general(a_bf16, b_bf16, (((0,), (0,)), ((), ())),
                               preferred_element_type=f32)

    def split(x):
        hi = x.astype(bf16)
        return hi, (x - hi.astype(f32)).astype(bf16)

    def head_sums(xs):
        tot = mm(jnp.concatenate(xs, axis=0), ones_bd)
        return [tot[j * L:(j + 1) * L] for j in range(len(xs))]

    t_idx = lax.broadcasted_iota(jnp.int32, (L, W), 0)
    s_idx = lax.broadcasted_iota(jnp.int32, (L, W), 1) % L
    strict = t_idx > s_idx
    incl = t_idx >= s_idx
    eye_all = jnp.where(t_idx == s_idx, 1.0, 0.0)
    tri = jnp.where(lax.broadcasted_iota(jnp.int32, (L, L), 0)
                    >= lax.broadcasted_iota(jnp.int32, (L, L), 1), 1.0, 0.0).astype(bf16)
    blockdiag = (lax.broadcasted_iota(jnp.int32, (W, W), 0) // RWKV_HEAD
                 == lax.broadcasted_iota(jnp.int32, (W, W), 1) // RWKV_HEAD)

    def phase1(items):
        I = range(len(items))
        rows = [pl.ds(c * L, L) for _, c in items]
        cols = [pl.ds(h * W, W) for h, _ in items]
        ld = lambda ref, i: ref[rows[i], cols[i]]
        r = [ld(r_ref, i).astype(f32) for i in I]
        k = [ld(k_ref, i).astype(f32) for i in I]
        v = [ld(v_ref, i).astype(f32) for i in I]
        lw = [ld(lw_ref, i) for i in I]
        lw_s = [split(x) for x in lw]
        cs = [jnp.dot(tri, hi, preferred_element_type=f32) + jnp.dot(tri, lo, preferred_element_type=f32)
              for hi, lo in lw_s]
        yield
        g_inv = [jnp.exp(-x) for x in cs]
        to_end = [jnp.exp(x[L - 1:L, :] - x) for x in cs]
        a_t = [-ld(kk_ref, i).astype(f32) * jnp.exp(cs[i] - lw[i]) for i in I]
        r_t = [r[i] * jnp.exp(cs[i]) for i in I]
        kka = [ld(kka_ref, i).astype(f32) for i in I]
        b_t = [kka[i] * g_inv[i] for i in I]
        k_t = [k[i] * g_inv[i] for i in I]
        b_e = [kka[i] * to_end[i] for i in I]
        k_e = [k[i] * to_end[i] for i in I]
        v_bd = [bd(x) for x in v]
        ar = [jnp.concatenate([a_t[i], r_t[i]], axis=0) for i in I]
        m_b = [mm_nt(ar[i], bd(b_t[i])) for i in I]
        m_k = [mm_nt(ar[i], bd(k_t[i])) for i in I]
        m_ab = [jnp.where(strict, x[:L], 0.0) for x in m_b]
        m_rb = [jnp.where(incl, x[L:], 0.0) for x in m_b]
        m_ak = [jnp.where(strict, x[:L], 0.0) for x in m_k]
        m_rk = [jnp.where(incl, x[L:], 0.0) for x in m_k]
        rk_sum = head_sums([r[i] * k[i] * rk_ref[:, cols[i]] for i in I])
        bonus = [rk_sum[i] * v[i] for i in I]
        yield
        p = m_ab
        tinv = [eye_all + x for x in p]
        m = 2
        while m < L:
            p = [mm(x, bd(x)) for x in p]
            tinv = [tinv[i] + mm(tinv[i], bd(p[i])) for i in I]
            m *= 2
            yield
        x1 = [mm(m_ak[i], v_bd[i]) for i in I]
        u_v = [mm(tinv[i], bd(x1[i])) for i in I]
        a_h = [mm(tinv[i], bd(a_t[i])) for i in I]
        yield
        r_h = [r_t[i] + mm(m_rb[i], bd(a_h[i])) for i in I]
        y_v = [mm(m_rb[i], bd(u_v[i])) + mm(m_rk[i], v_bd[i]) for i in I]
        yield
        p_t = [jnp.where(blockdiag, mm_tn(a_h[i].astype(bf16), b_e[i].astype(bf16)), 0.0).astype(bf16)
               for i in I]
        g_t = [jnp.where(blockdiag,
                         mm_tn(jnp.concatenate([u_v[i], v[i]], axis=0).astype(bf16),
                               jnp.concatenate([b_e[i], k_e[i]], axis=0).astype(bf16)), 0.0) for i in I]
        decay = [jnp.exp(x[L - 1:L, :]) for x in cs]
        for i in I:
            done.append(dict(hg=items[i][0], rows=rows[i], cols=cols[i], r_h=r_h[i], y_v=y_v[i],
                             p_t=p_t[i], g_t=g_t[i], decay=decay[i], bonus=bonus[i]))
        yield

    def chain_step(chs):
        ys = []
        for ch in chs:
            state = states[ch['hg']]
            state_b = state.astype(bf16)
            ys.append(mm_nt(ch['r_h'], state_b) + ch['y_v'])
            states[ch['hg']] = (state * ch['decay']
                                + jnp.dot(state_b, ch['p_t'], preferred_element_type=f32) + ch['g_t'])
        means = head_sums(ys)
        ycs = [y - m * (1.0 / RWKV_HEAD) for y, m in zip(ys, means)]
        vrs = head_sums([yc * yc for yc in ycs])
        for ch, yc, vr in zip(chs, ycs, vrs):
            rw, cl = ch['rows'], ch['cols']
            yn = yc * lax.rsqrt(vr * (1.0 / RWKV_HEAD) + GN_EPS) * lnw_ref[:, cl] + lnb_ref[:, cl]
            o_ref[rw, cl] = ((yn + ch['bonus']) * g_ref[rw, cl].astype(f32)).astype(o_ref.dtype)

    states = [state_ref[h] for h in range(n_hg)]
    done = []
    half = max(n_chunks // 2, 1)
    groups = [[(h, c) for c in range(0, half) for h in range(n_hg)],
              [(h, c) for c in range(half, n_chunks) for h in range(n_hg)]]
    def by_chunk(n_items):
        return [done[j:j + n_hg] for j in range(0, n_items, n_hg)]

    for _ in phase1(groups[0]):
        pass
    pending = by_chunk(len(done))
    for _ in phase1(groups[1]):
        if pending:
            chain_step(pending.pop(0))
    for chs in pending + by_chunk(len(done))[len(groups[0]) // n_hg:]:
        chain_step(chs)
    for h in range(n_hg):
        state_ref[h] = states[h]


def rwkv_scan(proj, lw, k, kk, kka, g, ln_w, ln_b, r_k, cast_weights, *, batch, seq, rows, n_hg):
    nb = seq // rows
    T = batch * seq
    width = n_hg * HG_LANES
    n_col = RWKV_WIDTH // width
    n_steps = batch * n_col * nb
    r_blk = POOL_WIDTH // width
    v_blk = (POOL_WIDTH + 2 * RWKV_WIDTH) // width
    blk = pl.BlockSpec((rows, width), lambda b, h, c: (b * nb + c, h))
    par = pl.BlockSpec((1, width), lambda b, h, c: (0, h))
    slabs = [w.reshape(n_steps, w.size // (n_steps * w.shape[-1]), w.shape[-1]) for w in cast_weights]
    slab_spec = lambda w: pl.BlockSpec((1,) + w.shape[1:], lambda b, h, c: ((b * n_col + h) * nb + c, 0, 0))
    outs = pl.pallas_call(
        functools.partial(_rwkv_scan_kernel, n_chunks=rows // RWKV_CHUNK, n_hg=n_hg,
                          n_cast=len(slabs)),
        out_shape=[jax.ShapeDtypeStruct((T, RWKV_WIDTH), bf16)]
                  + [jax.ShapeDtypeStruct(w.shape, bf16) for w in slabs],
        grid=(batch, n_col, nb),
        in_specs=[pl.BlockSpec((rows, width), lambda b, h, c: (b * nb + c, r_blk + h)),
                  pl.BlockSpec((rows, width), lambda b, h, c: (b * nb + c, v_blk + h))]
                 + [blk] * 5 + [par] * 3 + [slab_spec(w) for w in slabs],
        out_specs=[blk] + [slab_spec(w) for w in slabs],
        scratch_shapes=[pltpu.VMEM((n_hg, HG_LANES, HG_LANES), f32)],
        compiler_params=_cparams(("parallel", "parallel", "arbitrary")),
        name="rwkv_scan",
    )(proj, proj, lw, k, kk, kka, g, ln_w.reshape(1, -1), ln_b.reshape(1, -1), r_k.reshape(1, -1),
      *slabs)
    return outs[0], [o.reshape(w.shape) for o, w in zip(outs[1:], cast_weights)]


def _route(hn, w, b):
    w_hi = w.astype(bf16)
    w_lo = (w - w_hi.astype(f32)).astype(bf16)
    h_hi = hn.astype(bf16)
    h_lo = (hn - h_hi.astype(f32)).astype(bf16)
    hh = jnp.dot(h_hi, jnp.concatenate([w_hi, w_lo], axis=1), preferred_element_type=f32)
    logits = (hh[:, :ROUTER_LANES] + hh[:, ROUTER_LANES:]
              + jnp.dot(h_lo, w_hi, preferred_element_type=f32)) + b

    lane_i = lax.broadcasted_iota(jnp.int32, logits.shape, 1)
    lane = lane_i.astype(f32)
    neg = -jnp.inf
    big = float(ROUTER_LANES)
    is_g = lane_i < MOE_GROUPS
    gl = jnp.where(is_g, logits, neg)
    gmax = jnp.max(gl, axis=-1, keepdims=True)
    g_idx = jnp.min(jnp.where(gl == gmax, lane, big), axis=-1, keepdims=True)
    g_w = 1.0 / jnp.sum(jnp.exp(gl - gmax), axis=-1, keepdims=True)

    lo_lane = MOE_GROUPS + g_idx * MOE_EXPERTS_PER_GROUP
    sel = (lane >= lo_lane) & (lane < lo_lane + MOE_EXPERTS_PER_GROUP)
    el = jnp.where(sel, logits, neg)
    v1 = jnp.max(el, axis=-1, keepdims=True)
    i1 = jnp.min(jnp.where(el == v1, lane, big), axis=-1, keepdims=True)
    el2 = jnp.where(lane == i1, neg, el)
    v2 = jnp.max(el2, axis=-1, keepdims=True)
    i2 = jnp.min(jnp.where(el2 == v2, lane, big), axis=-1, keepdims=True)
    e2 = jnp.exp(v2 - v1)
    p1 = 1.0 / (1.0 + e2)
    p2 = e2 / (1.0 + e2)
    return (jnp.where(lane_i == 0, i1 - MOE_GROUPS, 0.0)
            + jnp.where(lane_i == 1, i2 - MOE_GROUPS, 0.0)
            + jnp.where(lane_i == 2, p1 * g_w, 0.0)
            + jnp.where(lane_i == 3, p2 * g_w, 0.0))


def _attn_oproj_router_kernel(q_ref, kv_ref, wo_ref, res_ref, g_ref, wr_ref, br_ref, h_ref, xr_ref,
                              rec_ref, *, tn):
    hd = XATTN_HEAD_DIM
    heads = []
    for h in range(XATTN_HEADS):
        k = kv_ref[:, h * hd:(h + 1) * hd]
        v = kv_ref[:, (XATTN_HEADS + h) * hd:(XATTN_HEADS + h + 1) * hd]
        s = lax.dot_general(q_ref[:, h * hd:(h + 1) * hd], k, (((1,), (1,)), ((), ())),
                            preferred_element_type=f32) * (hd ** -0.5)
        e = jnp.exp(s - jnp.max(s, axis=-1, keepdims=True))
        l = jnp.sum(e, axis=-1, keepdims=True)
        heads.append((jnp.dot(e.astype(bf16), v, preferred_element_type=f32) / l).astype(bf16))
    o = jnp.concatenate(heads, axis=1)
    for n0 in range(0, h_ref.shape[1], tn):
        h_ref[:, n0:n0 + tn] = res_ref[:, n0:n0 + tn] + jnp.dot(
            o, wo_ref[:, n0:n0 + tn], preferred_element_type=f32)
    hn = _rmsnorm_rows(h_ref[...], g_ref[...])
    D = hn.shape[1]
    rec = _route(hn, wr_ref[...], br_ref[...])
    xr_ref[:, :D] = hn
    xr_ref[:, D:] = rec
    rec_ref[...] = rec


def attn_oproj_router(q, kv, w_o, res, g, w_router, b_router, *, tm, tn, seq):
    T, D = res.shape
    tiles_per_seq = seq // tm
    row = lambda width: pl.BlockSpec((tm, width), lambda i: (i, 0))
    return pl.pallas_call(
        functools.partial(_attn_oproj_router_kernel, tn=tn),
        out_shape=[jax.ShapeDtypeStruct((T, D), f32),
                   jax.ShapeDtypeStruct((T, D + ROUTER_LANES), f32),
                   jax.ShapeDtypeStruct((T, ROUTER_LANES), f32)],
        grid=(T // tm,),
        in_specs=[row(D), pl.BlockSpec((MEM_LEN, 2 * D), lambda i: (i // tiles_per_seq, 0)),
                  _resident((D, D)), row(D), _resident((1, D)),
                  _resident((D, ROUTER_LANES)), _resident((1, ROUTER_LANES))],
        out_specs=[row(D), row(D + ROUTER_LANES), row(ROUTER_LANES)],
        compiler_params=_cparams(("parallel",)),
        name="attn_oproj_router",
    )(q, kv, w_o, res, g.reshape(1, D), w_router, b_router)


MOE_PAIRS = ((0, 1), (0, 2), (0, 3), (1, 3), (1, 2), (3, 2))
MOE_CLASSES = MOE_GROUPS * len(MOE_PAIRS)


def _route_metadata(rec, *, tm, n_tiles):
    ids = rec[:, 0:2].astype(jnp.int32)
    lo = jnp.minimum(ids[:, 0], ids[:, 1])
    hi = jnp.maximum(ids[:, 0], ids[:, 1])
    a, b = lo % MOE_EXPERTS_PER_GROUP, hi % MOE_EXPERTS_PER_GROUP
    pair_id = (a * (7 - a)) // 2 + b - a - 1
    pair_id = pair_id + (pair_id == 3).astype(jnp.int32) - (pair_id == 4).astype(jnp.int32)
    cls = (lo // MOE_EXPERTS_PER_GROUP) * len(MOE_PAIRS) + pair_id
    onehot = (cls[:, None] == jnp.arange(MOE_CLASSES, dtype=jnp.int32)[None, :]).astype(jnp.int32)
    csum = jnp.cumsum(onehot, axis=0)
    counts = csum[-1]
    rank = jnp.sum(csum * onehot, axis=1) - 1
    padded = ((counts + tm - 1) // tm) * tm
    ends = jnp.cumsum(padded)
    pos = jnp.sum(onehot * (ends - padded)[None, :], axis=1) + rank
    tile_start = jnp.arange(n_tiles, dtype=jnp.int32) * tm
    n_valid = ends[-1] // tm
    tile_cls = jnp.sum((tile_start[:, None] >= ends[None, :]).astype(jnp.int32), axis=1)
    last_cls = jnp.sum((ends[-1] - 1 >= ends).astype(jnp.int32))
    tile_cls = jnp.where(tile_start < ends[-1], tile_cls, last_cls)
    pair = jnp.asarray(MOE_PAIRS, jnp.int32)[tile_cls % len(MOE_PAIRS)]
    base = (tile_cls // len(MOE_PAIRS)) * MOE_EXPERTS_PER_GROUP
    first_tile = (ends - padded) // tm
    starts = jnp.cumsum(counts) - counts
    j = jnp.arange(n_tiles, dtype=jnp.int32) - first_tile[tile_cls]
    t_base = starts[tile_cls] + j * tm
    t_rows = jnp.clip(counts[tile_cls] - j * tm, 1, tm)
    order = jnp.argsort(cls, stable=True).astype(jnp.int32)
    return (pos, base + pair[:, 0], base + pair[:, 1], n_valid.reshape(1),
            t_base.astype(jnp.int32), t_rows.astype(jnp.int32), order)


def _moe_pair_kernel(ea_ref, eb_ref, nv_ref, base_ref, nrows_ref, order_ref,
                     xr_hbm, wga, wua, wda, wgb, wub, wdb, o_ref, xbuf0, xbuf1, sem, *, tm):
    i = pl.program_id(0)
    nv = nv_ref[0]
    bufs = (xbuf0, xbuf1)

    def gather(tile, slot):
        base = base_ref[tile]
        last = nrows_ref[tile] - 1
        for r in range(tm):
            tok = order_ref[base + jnp.minimum(r, last)]
            pltpu.make_async_copy(xr_hbm.at[pl.ds(tok, 1), :], bufs[slot].at[pl.ds(r, 1), :],
                                  sem.at[slot]).start()

    def tile_wait(slot):
        pltpu.make_async_copy(xr_hbm.at[pl.ds(0, tm), :], bufs[slot], sem.at[slot]).wait()

    @pl.when(i == 0)
    def _():
        gather(0, 0)

    @pl.when(i >= nv)
    def _():
        o_ref[...] = jnp.zeros_like(o_ref)

    def run(slot):
        tile_wait(slot)
        D = o_ref.shape[1]
        x = bufs[slot][:, :D].astype(bf16)
        rec = bufs[slot][:, D:]
        first = rec[:, 0:1] == ea_ref[i].astype(f32)
        w_a = jnp.where(first, rec[:, 2:3], rec[:, 3:4])
        w_b = jnp.where(first, rec[:, 3:4], rec[:, 2:3])

        def expert(wg, wu, wd):
            gate = jnp.dot(x, wg[0], preferred_element_type=f32)
            up = jnp.dot(x, wu[0], preferred_element_type=f32)
            hid = gate * _sigmoid(gate) * up
            return jnp.dot(hid.astype(bf16), wd[0], preferred_element_type=f32)

        o_ref[...] = w_a * expert(wga, wua, wda) + w_b * expert(wgb, wub, wdb)
        gather(jnp.minimum(i + 1, nv - 1), 1 - slot)

        @pl.when(i == nv - 1)
        def _():
            tile_wait(1 - slot)

    for slot in range(2):
        @pl.when((i < nv) & (i % 2 == slot))
        def _():
            run(slot)


def moe_pair_experts(xr, e_a, e_b, n_valid, base, nrows, order, wg, wu, wd, *, tm, n_tiles):
    width = xr.shape[1]
    D = wg.shape[1]
    tile = lambda i, *_: (i, 0)
    wa = lambda shape: pl.BlockSpec(shape, lambda i, ea, *_: (ea[i], 0, 0))
    wb = lambda shape: pl.BlockSpec(shape, lambda i, ea, eb, *_: (eb[i], 0, 0))
    grid_spec = pltpu.PrefetchScalarGridSpec(
        num_scalar_prefetch=6,
        grid=(n_tiles,),
        in_specs=[pl.BlockSpec(memory_space=pl.ANY),
                  wa((1, D, MOE_FF)), wa((1, D, MOE_FF)), wa((1, MOE_FF, D)),
                  wb((1, D, MOE_FF)), wb((1, D, MOE_FF)), wb((1, MOE_FF, D))],
        out_specs=pl.BlockSpec((tm, D), tile),
        scratch_shapes=[pltpu.VMEM((tm, width), f32), pltpu.VMEM((tm, width), f32),
                        pltpu.SemaphoreType.DMA((2,))],
    )
    return pl.pallas_call(
        functools.partial(_moe_pair_kernel, tm=tm),
        out_shape=jax.ShapeDtypeStruct((n_tiles * tm, D), f32),
        grid_spec=grid_spec,
        compiler_params=_cparams(("arbitrary",)),
        name="moe_experts",
    )(e_a, e_b, n_valid, base, nrows, order, xr, wg, wu, wd, wg, wu, wd)


def _moe_combine_kernel(pos_ref, h_ref, y_hbm, gf_ref, o_ref, ybuf, sem, *, tm):
    i = pl.program_id(0)
    n = pl.num_programs(0)

    def gather(tile, slot):
        def body(r, carry):
            p = pos_ref[tile * tm + r]
            pltpu.make_async_copy(y_hbm.at[pl.ds(p, 1), :],
                                  ybuf.at[slot, pl.ds(r, 1), :], sem.at[slot]).start()
            return carry
        lax.fori_loop(0, tm, body, 0, unroll=8)

    @pl.when(i == 0)
    def _():
        gather(0, 0)

    @pl.when(i + 1 < n)
    def _():
        gather(i + 1, (i + 1) % 2)

    slot = i % 2
    pltpu.make_async_copy(y_hbm.at[pl.ds(0, tm), :], ybuf.at[slot], sem.at[slot]).wait()
    o_ref[...] = _rmsnorm_rows(h_ref[...] + ybuf[slot], gf_ref[...])


def moe_combine(h, y_sorted, pos, g_final, *, tm):
    T, D = h.shape
    grid_spec = pltpu.PrefetchScalarGridSpec(
        num_scalar_prefetch=1,
        grid=(T // tm,),
        in_specs=[
            pl.BlockSpec((tm, D), lambda i, pos: (i, 0)),
            pl.BlockSpec(memory_space=pl.ANY),
            pl.BlockSpec((1, D), lambda i, pos: (0, 0)),
        ],
        out_specs=pl.BlockSpec((tm, D), lambda i, pos: (i, 0)),
        scratch_shapes=[pltpu.VMEM((2, tm, D), f32), pltpu.SemaphoreType.DMA((2,))],
    )
    return pl.pallas_call(
        functools.partial(_moe_combine_kernel, tm=tm),
        out_shape=jax.ShapeDtypeStruct((T, D), f32),
        grid_spec=grid_spec,
        compiler_params=_cparams(("arbitrary",)),
        name="moe_combine",
    )(pos, h, y_sorted, g_final.reshape(1, D))


def _pad_rows(w, rows):
    return jnp.pad(w, ((0, rows - w.shape[0]), (0, 0)))


def _pack_lora_cols(t):
    o1, o2 = DECAY_LORA, DECAY_LORA + AAA_LORA
    pad = lambda a, n: jnp.pad(a, [(0, 0)] * (a.ndim - 1) + [(0, n - a.shape[-1])])
    return jnp.concatenate([pad(t[..., :o1], 128), pad(t[..., o1:o2], 128),
                            pad(t[..., o2:], 256)], axis=-1)


def _layer(x2, mem2, p, *, batch, seq, cfg):
    D = D_MODEL
    w_in = p['w_in']
    w_main = w_in[:, :MAIN_COLS].astype(bf16)
    w_lora = _pack_lora_cols(w_in[:, MAIN_COLS:]).astype(bf16)
    mu_cols = jnp.concatenate([jnp.zeros((POOL_WIDTH,), f32), p['rwkv_mu'][:3 * RWKV_WIDTH],
                               _pack_lora_cols(p['rwkv_mu'][3 * RWKV_WIDTH:])])

    proj = in_proj_shift(x2, p['norm_mix_g'], w_main, w_lora, mu_cols,
                         tm=cfg['tm'], tn=cfg['tn_in'], seq=seq)
    pool_out = pool_mixer(proj, p['pool_w'].astype(bf16), p['pool_scale'],
                          batch=batch, seq=seq, ts=cfg['ts_pool'])
    lw, k, kk, kka, g = rwkv_prep(
        proj, p['rwkv_w0'], _pad_rows(p['rwkv_w2'], 128).astype(bf16),
        p['rwkv_a0'], _pad_rows(p['rwkv_a2'], 128).astype(bf16),
        _pad_rows(p['rwkv_g2'], 256).astype(bf16), p['rwkv_k_k'], p['rwkv_k_a'], ts=cfg['ts_prep'])
    rwkv_out, (w_out, w_q, w_kv, w_o, wg, wu, wd) = rwkv_scan(
        proj, lw, k, kk, kka, g, p['rwkv_ln_w'], p['rwkv_ln_b'], p['rwkv_r_k'],
        [p['w_out'], p['xattn_w_q'], p['xattn_w_kv'], p['xattn_w_o'],
         p['moe_w_gate'], p['moe_w_up'], p['moe_w_down']],
        batch=batch, seq=seq, rows=cfg['scan_rows'], n_hg=cfg['scan_hg'])
    h1 = matmul_residual_resident([pool_out, rwkv_out], [w_out[:POOL_WIDTH], w_out[POOL_WIDTH:]],
                                  x2, tm=cfg['tm'], tn=cfg['tn'], name="out_proj")

    q = norm_matmul_resident(h1, p['norm_xattn_g'], w_q,
                             tm=cfg['tm'], tn=cfg['tn'], name="q_proj")
    kv = norm_matmul_resident(mem2, p['norm_mem_g'], w_kv,
                              tm=min(cfg['tm'], mem2.shape[0]), tn=cfg['tn'], name="kv_proj")

    w_router = jnp.pad(jnp.concatenate([p['moe_w_group'], p['moe_w_expert']], axis=1),
                       ((0, 0), (0, ROUTER_LANES - MOE_GROUPS - MOE_EXPERTS)))
    b_router = jnp.pad(jnp.concatenate([p['moe_b_group'], p['moe_b_expert']]),
                       (0, ROUTER_LANES - MOE_GROUPS - MOE_EXPERTS)).reshape(1, ROUTER_LANES)
    h2, xr, rec = attn_oproj_router(q, kv, w_o, h1, p['norm_ffn_g'],
                               w_router, b_router, tm=cfg['tm'], tn=cfg['tn'], seq=seq)
    tm_e = cfg['tm_moe']
    n_tiles = xr.shape[0] // tm_e + MOE_CLASSES
    pos, e_a, e_b, n_valid, t_base, t_rows, order = _route_metadata(
        rec, tm=tm_e, n_tiles=n_tiles)
    y_sorted = moe_pair_experts(xr, e_a, e_b, n_valid, t_base, t_rows, order, wg, wu, wd,
                                tm=tm_e, n_tiles=n_tiles)
    return moe_combine(h2, y_sorted, pos, p['norm_final_g'], tm=cfg['tm_comb'])


_CFG = dict(tm=512, tn=1024, tn_in=512, ts_pool=512, ts_prep=512, scan_rows=256, scan_hg=4,
            tm_moe=256, tm_comb=512)


def kernel(x, mem, norm_mix_g, w_in, pool_w, pool_scale, rwkv_mu, rwkv_w0, rwkv_w2, rwkv_a0,
           rwkv_a2, rwkv_g2, rwkv_k_k, rwkv_k_a, rwkv_r_k, rwkv_ln_w, rwkv_ln_b, w_out,
           norm_xattn_g, norm_mem_g, xattn_w_q, xattn_w_kv, xattn_w_o, norm_ffn_g,
           moe_w_group, moe_b_group, moe_w_expert, moe_b_expert, moe_w_gate, moe_w_up,
           moe_w_down, norm_final_g):
    batch, seq, D = x.shape
    p = dict(norm_mix_g=norm_mix_g[0], w_in=w_in[0], pool_w=pool_w[0], pool_scale=pool_scale[0],
             rwkv_mu=rwkv_mu[0], rwkv_w0=rwkv_w0[0], rwkv_w2=rwkv_w2[0], rwkv_a0=rwkv_a0[0],
             rwkv_a2=rwkv_a2[0], rwkv_g2=rwkv_g2[0], rwkv_k_k=rwkv_k_k[0], rwkv_k_a=rwkv_k_a[0],
             rwkv_r_k=rwkv_r_k[0], rwkv_ln_w=rwkv_ln_w[0], rwkv_ln_b=rwkv_ln_b[0], w_out=w_out[0],
             norm_xattn_g=norm_xattn_g[0], norm_mem_g=norm_mem_g[0], xattn_w_q=xattn_w_q[0],
             xattn_w_kv=xattn_w_kv[0], xattn_w_o=xattn_w_o[0], norm_ffn_g=norm_ffn_g[0],
             moe_w_group=moe_w_group[0], moe_b_group=moe_b_group[0], moe_w_expert=moe_w_expert[0],
             moe_b_expert=moe_b_expert[0], moe_w_gate=moe_w_gate[0], moe_w_up=moe_w_up[0],
             moe_w_down=moe_w_down[0], norm_final_g=norm_final_g)
    out = _layer(x.reshape(batch * seq, D), mem.reshape(batch * MEM_LEN, D), p,
                 batch=batch, seq=seq, cfg=_CFG)
    return out.reshape(batch, seq, D)
```

```python
import functools

import jax
import jax.numpy as jnp
from jax import lax
from jax.experimental import pallas as pl
from jax.experimental.pallas import tpu as pltpu

f32 = jnp.float32
bf16 = jnp.bfloat16

D_MODEL = 2048
MEM_LEN = 256
NORM_EPS = 1e-6

POOL_WIDTH = 1024
POOL_WINDOWS = (2, 4, 8, 16)
POOL_GROUP = 256
POOL_HALO = 16
RWKV_WIDTH = 1024
RWKV_HEAD = 64
RWKV_HEADS = 16
GN_EPS = 64e-5
DECAY_SCALE = 0.6065306597126334
DECAY_LORA = 64
AAA_LORA = 64
GATE_LORA = 160
LORA_PAD = 512
LORA_W_OFF, LORA_A_OFF, LORA_G_OFF = 0, 128, 256
MAIN_COLS = POOL_WIDTH + 3 * RWKV_WIDTH
PROJ_COLS = MAIN_COLS + LORA_PAD

RWKV_CHUNK = 64
HEADS_PER_STEP = 4
HG_LANES = HEADS_PER_STEP * RWKV_HEAD

XATTN_HEADS = 4
XATTN_HEAD_DIM = 512

MOE_GROUPS = 4
MOE_EXPERTS_PER_GROUP = 4
MOE_EXPERTS = 16
MOE_FF = 512
ROUTER_LANES = 128

VMEM_LIMIT = 56 * 1024 * 1024


def _cparams(sem):
    return pltpu.CompilerParams(dimension_semantics=sem, vmem_limit_bytes=VMEM_LIMIT)


def _split_dot(x, w_bf16):
    hi = x.astype(bf16)
    lo = (x - hi.astype(f32)).astype(bf16)
    return (jnp.dot(hi, w_bf16, preferred_element_type=f32)
            + jnp.dot(lo, w_bf16, preferred_element_type=f32))


def _resident(shape):
    return pl.BlockSpec(shape, lambda i: (0,) * len(shape), pipeline_mode=pl.Buffered(1))


def _rmsnorm_rows(x, g):
    ms = jnp.mean(x * x, axis=-1, keepdims=True)
    return x * lax.rsqrt(ms + NORM_EPS) * g


def _norm_mm_res_kernel(x_ref, g_ref, w_ref, o_ref, *, tn):
    xn = _rmsnorm_rows(x_ref[...], g_ref[...]).astype(bf16)
    for n0 in range(0, o_ref.shape[1], tn):
        o_ref[:, n0:n0 + tn] = jnp.dot(xn, w_ref[:, n0:n0 + tn],
                                       preferred_element_type=f32).astype(o_ref.dtype)


def norm_matmul_resident(x, g, w, *, tm, tn, name):
    M, K = x.shape
    N = w.shape[1]
    return pl.pallas_call(
        functools.partial(_norm_mm_res_kernel, tn=tn),
        out_shape=jax.ShapeDtypeStruct((M, N), bf16),
        grid=(M // tm,),
        in_specs=[pl.BlockSpec((tm, K), lambda i: (i, 0)), _resident((1, K)), _resident((K, N))],
        out_specs=pl.BlockSpec((tm, N), lambda i: (i, 0)),
        compiler_params=_cparams(("parallel",)),
        name=name,
    )(x, g.reshape(1, K), w)


def _in_proj_kernel(x_ref, xh_ref, g_ref, w_ref, wl_ref, mu_ref, o_ref, *, tn, tiles_per_seq):
    first = pl.program_id(0) % tiles_per_seq == 0
    xn = _rmsnorm_rows(x_ref[...], g_ref[...]).astype(bf16)
    xh = jnp.where(first, 0.0, _rmsnorm_rows(xh_ref[...], g_ref[...])).astype(bf16)
    lhs = jnp.concatenate([xh, xn], axis=0)
    for n0 in range(0, o_ref.shape[1], tn):
        w = w_ref[:, n0:n0 + tn] if n0 < MAIN_COLS else wl_ref[:, n0 - MAIN_COLS:n0 - MAIN_COLS + tn]
        res = jnp.dot(lhs, w, preferred_element_type=f32)
        p = res[POOL_HALO:]
        if n0 >= POOL_WIDTH:
            prev = pltpu.roll(res, 1, 0)[POOL_HALO:]
            p = p + (prev - p) * mu_ref[:, n0:n0 + tn]
        o_ref[:, n0:n0 + tn] = p.astype(o_ref.dtype)


def in_proj_shift(x, g, w_main, w_lora, mu_cols, *, tm, tn, seq):
    M, K = x.shape
    N = MAIN_COLS + w_lora.shape[1]
    hb = tm // POOL_HALO
    return pl.pallas_call(
        functools.partial(_in_proj_kernel, tn=tn, tiles_per_seq=seq // tm),
        out_shape=jax.ShapeDtypeStruct((M, N), bf16),
        grid=(M // tm,),
        in_specs=[pl.BlockSpec((tm, K), lambda i: (i, 0)),
                  pl.BlockSpec((POOL_HALO, K), lambda i: (jnp.maximum(i * hb - 1, 0), 0)),
                  _resident((1, K)), _resident(w_main.shape), _resident(w_lora.shape),
                  _resident((1, N))],
        out_specs=pl.BlockSpec((tm, N), lambda i: (i, 0)),
        compiler_params=_cparams(("parallel",)),
        name="in_proj",
    )(x, x, g.reshape(1, K), w_main, w_lora, mu_cols.reshape(1, N))


def _mm_res_resident_kernel(*refs, n, tn):
    a_refs, w_refs, res_ref, o_ref = refs[:n], refs[n:2 * n], refs[2 * n], refs[2 * n + 1]
    for n0 in range(0, o_ref.shape[1], tn):
        acc = res_ref[:, n0:n0 + tn]
        for a_ref, w_ref in zip(a_refs, w_refs):
            acc = acc + jnp.dot(a_ref[...], w_ref[:, n0:n0 + tn], preferred_element_type=f32)
        o_ref[:, n0:n0 + tn] = acc


def matmul_residual_resident(a_list, w_list, res, *, tm, tn, name):
    M, N = res.shape
    n = len(a_list)
    return pl.pallas_call(
        functools.partial(_mm_res_resident_kernel, n=n, tn=tn),
        out_shape=jax.ShapeDtypeStruct((M, N), f32),
        grid=(M // tm,),
        in_specs=([pl.BlockSpec((tm, a.shape[1]), lambda i: (i, 0)) for a in a_list]
                  + [_resident(w.shape) for w in w_list]
                  + [pl.BlockSpec((tm, N), lambda i: (i, 0))]),
        out_specs=pl.BlockSpec((tm, N), lambda i: (i, 0)),
        compiler_params=_cparams(("parallel",)),
        name=name,
    )(*a_list, *w_list, res)


def _pool_kernel(u_ref, halo_ref, pw_ref, ps_ref, o_ref, *, ts):
    s = pl.program_id(1)
    u = u_ref[...].astype(f32)
    halo = jnp.where(s > 0, halo_ref[...].astype(f32), 0.0)
    ext = jnp.concatenate([halo, u], axis=0)
    pos = s * ts + lax.broadcasted_iota(jnp.int32, (ts, 1), 0)
    acc = ext
    shift = 1
    for gi, w in enumerate(POOL_WINDOWS):
        while shift < w:
            acc = acc + pltpu.roll(acc, shift, 0)
            shift *= 2
        lo, hi = gi * POOL_GROUP, (gi + 1) * POOL_GROUP
        cnt = jnp.minimum(pos + 1, w).astype(f32)
        pooled = acc[POOL_HALO:, lo:hi] / cnt - u[:, lo:hi]
        mixed = jnp.dot(pooled.astype(bf16), pw_ref[gi], preferred_element_type=f32)
        o_ref[:, lo:hi] = (mixed * ps_ref[:, lo:hi]).astype(o_ref.dtype)


def pool_mixer(proj, pool_w, pool_scale, *, batch, seq, ts):
    ns = seq // ts
    hb = ts // POOL_HALO
    return pl.pallas_call(
        functools.partial(_pool_kernel, ts=ts),
        out_shape=jax.ShapeDtypeStruct((batch * seq, POOL_WIDTH), bf16),
        grid=(batch, ns),
        in_specs=[
            pl.BlockSpec((ts, POOL_WIDTH), lambda b, s: (b * ns + s, 0)),
            pl.BlockSpec((POOL_HALO, POOL_WIDTH),
                         lambda b, s: (jnp.maximum((b * ns + s) * hb - 1, 0), 0)),
            pl.BlockSpec((len(POOL_WINDOWS), POOL_GROUP, POOL_GROUP), lambda b, s: (0, 0, 0)),
            pl.BlockSpec((1, POOL_WIDTH), lambda b, s: (0, 0)),
        ],
        out_specs=pl.BlockSpec((ts, POOL_WIDTH), lambda b, s: (b * ns + s, 0)),
        compiler_params=_cparams(("parallel", "arbitrary")),
        name="pool_mixer",
    )(proj, proj, pool_w, pool_scale.reshape(1, POOL_WIDTH))


def _head_ones():
    r = lax.broadcasted_iota(jnp.int32, (HG_LANES, HG_LANES), 0) // RWKV_HEAD
    c = lax.broadcasted_iota(jnp.int32, (HG_LANES, HG_LANES), 1) // RWKV_HEAD
    return jnp.where(r == c, 1.0, 0.0).astype(bf16)


def _head_sum(x, ones_bd):
    parts = [_split_dot(x[:, c:c + HG_LANES], ones_bd) for c in range(0, x.shape[1], HG_LANES)]
    return parts[0] if len(parts) == 1 else jnp.concatenate(parts, axis=1)


def _sigmoid(x):
    return 1.0 / (1.0 + jnp.exp(-x))


def _rwkv_prep_kernel(k_ref, lo_ref, w0_ref, w2_ref, a0_ref, a2_ref, g2_ref, kk_ref, ka_ref,
                      lw_out, k_out, kk_out, kka_out, g_out):
    k = k_ref[...].astype(f32)
    lo = lo_ref[...].astype(f32)
    w_lo = lo[:, LORA_W_OFF:LORA_W_OFF + 128]
    a_lo = lo[:, LORA_A_OFF:LORA_A_OFF + 128]
    g_lo = lo[:, LORA_G_OFF:LORA_G_OFF + 256]

    wx = w0_ref[...] + jnp.dot(jnp.tanh(w_lo).astype(bf16), w2_ref[...], preferred_element_type=f32)
    lw_out[...] = -DECAY_SCALE * _sigmoid(wx)
    a = _sigmoid(a0_ref[...] + jnp.dot(a_lo.astype(bf16), a2_ref[...], preferred_element_type=f32))
    g = jnp.dot(_sigmoid(g_lo).astype(bf16), g2_ref[...], preferred_element_type=f32)

    ones_bd = _head_ones()
    kk = k * kk_ref[...]
    kk = kk * lax.rsqrt(jnp.maximum(_head_sum(kk * kk, ones_bd), 1e-24))
    k = k * (1.0 + (a - 1.0) * ka_ref[...])

    k_out[...] = k.astype(k_out.dtype)
    kk_out[...] = kk.astype(kk_out.dtype)
    kka_out[...] = (kk * a).astype(kka_out.dtype)
    g_out[...] = g.astype(g_out.dtype)


def rwkv_prep(proj, w0, w2p, a0, a2p, g2p, k_k, k_a, *, ts):
    T = proj.shape[0]
    C = RWKV_WIDTH
    k_blk = (POOL_WIDTH + C) // C
    lora_blk = MAIN_COLS // LORA_PAD

    def full(shape):
        return pl.BlockSpec(shape, lambda i: (0,) * len(shape))

    row = lambda t: t.reshape(1, -1)
    out_sd = lambda dt: jax.ShapeDtypeStruct((T, C), dt)
    out_spec = pl.BlockSpec((ts, C), lambda i: (i, 0))
    return pl.pallas_call(
        _rwkv_prep_kernel,
        out_shape=[out_sd(f32), out_sd(bf16), out_sd(bf16), out_sd(bf16), out_sd(bf16)],
        grid=(T // ts,),
        in_specs=[
            pl.BlockSpec((ts, C), lambda i: (i, k_blk)),
            pl.BlockSpec((ts, LORA_PAD), lambda i: (i, lora_blk)),
            full((1, C)), full((128, C)), full((1, C)), full((128, C)), full((256, C)),
            full((1, C)), full((1, C)),
        ],
        out_specs=[out_spec] * 5,
        compiler_params=_cparams(("parallel",)),
        name="rwkv_prep",
    )(proj, proj, row(w0), w2p, row(a0), a2p, g2p, row(k_k), row(k_a))


def _bd_mask():
    r = lax.broadcasted_iota(jnp.int32, (HG_LANES, HG_LANES), 0) // RWKV_CHUNK
    c = lax.broadcasted_iota(jnp.int32, (HG_LANES, HG_LANES), 1) // RWKV_HEAD
    return r == c


def _rwkv_scan_kernel(r_ref, v_ref, lw_ref, k_ref, kk_ref, kka_ref, g_ref,
                      lnw_ref, lnb_ref, rk_ref, *rest, n_chunks, n_hg, n_cast):
    L = RWKV_CHUNK
    W = HG_LANES
    cast_in, o_ref, cast_out, state_ref = rest[:n_cast], rest[n_cast], rest[n_cast + 1:-1], rest[-1]
    for w_ref, wb_ref in zip(cast_in, cast_out):
        wb_ref[...] = w_ref[...].astype(bf16)

    @pl.when(pl.program_id(2) == 0)
    def _():
        state_ref[...] = jnp.zeros_like(state_ref)

    bd_mask = _bd_mask()
    ones_bd = _head_ones()

    def bd(x):
        tiled = jnp.concatenate([x] * HEADS_PER_STEP, axis=0)
        return jnp.where(bd_mask, tiled, 0.0).astype(bf16)

    def bd_t(x):
        tiled = jnp.concatenate([x] * HEADS_PER_STEP, axis=0)
        return jnp.where(bd_mask, tiled, 0.0).T.astype(bf16)

    def mm(a, b_bf16):
        return jnp.dot(a.astype(bf16), b_bf16, preferred_element_type=f32)

    def mm_nt(a, b_bf16):
        return lax.dot_general(a.astype(bf16), b_bf16, (((1,), (1,)), ((), ())),
                               preferred_element_type=f32)

    def mm_tn(a_bf16, b_bf16):
        return lax.dot_general(a_bf16, b_bf16, (((0,), (0,)), ((), ())),
                               preferred_element_type=f32)

    def split(x):
        hi = x.astype(bf16)
        return hi, (x - hi.astype(f32)).astype(bf16)

    def head_sums(xs):
        tot = mm(jnp.concatenate(xs, axis=0), ones_bd)
        return [tot[j * L:(j + 1) * L] for j in range(len(xs))]

    t_idx = lax.broadcasted_iota(jnp.int32, (L, W), 0)
    s_idx = lax.broadcasted_iota(jnp.int32, (L, W), 1) % L
    strict = t_idx > s_idx
    incl = t_idx >= s_idx
    eye_all = jnp.where(t_idx == s_idx, 1.0, 0.0)
    tri = jnp.where(lax.broadcasted_iota(jnp.int32, (L, L), 0)
                    >= lax.broadcasted_iota(jnp.int32, (L, L), 1), 1.0, 0.0).astype(bf16)
    blockdiag = (lax.broadcasted_iota(jnp.int32, (W, W), 0) // RWKV_HEAD
                 == lax.broadcasted_iota(jnp.int32, (W, W), 1) // RWKV_HEAD)

    def phase1(items):
        I = range(len(items))
        rows = [pl.ds(c * L, L) for _, c in items]
        cols = [pl.ds(h * W, W) for h, _ in items]
        ld = lambda ref, i: ref[rows[i], cols[i]]
        r = [ld(r_ref, i).astype(f32) for i in I]
        k = [ld(k_ref, i).astype(f32) for i in I]
        v = [ld(v_ref, i).astype(f32) for i in I]
        lw = [ld(lw_ref, i) for i in I]
        lw_s = [split(x) for x in lw]
        cs = [jnp.dot(tri, hi, preferred_element_type=f32) + jnp.dot(tri, lo, preferred_element_type=f32)
              for hi, lo in lw_s]
        yield
        g_inv = [jnp.exp(-x) for x in cs]
        to_end = [jnp.exp(x[L - 1:L, :] - x) for x in cs]
        a_t = [-ld(kk_ref, i).astype(f32) * jnp.exp(cs[i] - lw[i]) for i in I]
        r_t = [r[i] * jnp.exp(cs[i]) for i in I]
        kka = [ld(kka_ref, i).astype(f32) for i in I]
        b_t = [kka[i] * g_inv[i] for i in I]
        k_t = [k[i] * g_inv[i] for i in I]
        b_e = [kka[i] * to_end[i] for i in I]
        k_e = [k[i] * to_end[i] for i in I]
        v_bd = [bd(x) for x in v]
        ar = [jnp.concatenate([a_t[i], r_t[i]], axis=0) for i in I]
        m_b = [mm(ar[i], bd_t(b_t[i])) for i in I]
        m_k = [mm(ar[i], bd_t(k_t[i])) for i in I]
        m_ab = [jnp.where(strict, x[:L], 0.0) for x in m_b]
        m_rb = [jnp.where(incl, x[L:], 0.0) for x in m_b]
        m_ak = [jnp.where(strict, x[:L], 0.0) for x in m_k]
        m_rk = [jnp.where(incl, x[L:], 0.0) for x in m_k]
        rk_sum = head_sums([r[i] * k[i] * rk_ref[:, cols[i]] for i in I])
        bonus = [rk_sum[i] * v[i] for i in I]
        yield
        p = m_ab
        tinv = [eye_all + x for x in p]
        m = 2
        while m < L:
            p = [mm(x, bd(x)) for x in p]
            tinv = [tinv[i] + mm(tinv[i], bd(p[i])) for i in I]
            m *= 2
            yield
        x1 = [mm(m_ak[i], v_bd[i]) for i in I]
        u_v = [mm(tinv[i], bd(x1[i])) for i in I]
        a_h = [mm(tinv[i], bd(a_t[i])) for i in I]
        yield
        r_h = [r_t[i] + mm(m_rb[i], bd(a_h[i])) for i in I]
        y_v = [mm(m_rb[i], bd(u_v[i])) + mm(m_rk[i], v_bd[i]) for i in I]
        yield
        p_t = [jnp.where(blockdiag, mm_tn(a_h[i].astype(bf16), b_e[i].astype(bf16)), 0.0).astype(bf16)
               for i in I]
        g_t = [jnp.where(blockdiag,
                         mm_tn(jnp.concatenate([u_v[i], v[i]], axis=0).astype(bf16),
                               jnp.concatenate([b_e[i], k_e[i]], axis=0).astype(bf16)), 0.0) for i in I]
        decay = [jnp.exp(x[L - 1:L, :]) for x in cs]
        for i in I:
            done.append(dict(hg=items[i][0], rows=rows[i], cols=cols[i], r_h=r_h[i], y_v=y_v[i],
                             p_t=p_t[i], g_t=g_t[i], decay=decay[i], bonus=bonus[i]))
        yield

    def chain_step(chs):
        ys = []
        for ch in chs:
            state = states[ch['hg']]
            state_b = state.astype(bf16)
            ys.append(mm_nt(ch['r_h'], state_b) + ch['y_v'])
            states[ch['hg']] = (state * ch['decay']
                                + jnp.dot(state_b, ch['p_t'], preferred_element_type=f32) + ch['g_t'])
        means = head_sums(ys)
        ycs = [y - m * (1.0 / RWKV_HEAD) for y, m in zip(ys, means)]
        vrs = head_sums([yc * yc for yc in ycs])
        for ch, yc, vr in zip(chs, ycs, vrs):
            rw, cl = ch['rows'], ch['cols']
            yn = yc * lax.rsqrt(vr * (1.0 / RWKV_HEAD) + GN_EPS) * lnw_ref[:, cl] + lnb_ref[:, cl]
            o_ref[rw, cl] = ((yn + ch['bonus']) * g_ref[rw, cl].astype(f32)).astype(o_ref.dtype)

    states = [state_ref[h] for h in range(n_hg)]
    done = []
    half = max(n_chunks // 2, 1)
    groups = [[(h, c) for c in range(0, half) for h in range(n_hg)],
              [(h, c) for c in range(half, n_chunks) for h in range(n_hg)]]
    def by_chunk(n_items):
        return [done[j:j + n_hg] for j in range(0, n_items, n_hg)]

    for _ in phase1(groups[0]):
        pass
    pending = by_chunk(len(done))
    for _ in phase1(groups[1]):
        if pending:
            chain_step(pending.pop(0))
    for chs in pending + by_chunk(len(done))[len(groups[0]) // n_hg:]:
        chain_step(chs)
    for h in range(n_hg):
        state_ref[h] = states[h]


def rwkv_scan(proj, lw, k, kk, kka, g, ln_w, ln_b, r_k, cast_weights, *, batch, seq, rows, n_hg):
    nb = seq // rows
    T = batch * seq
    width = n_hg * HG_LANES
    n_col = RWKV_WIDTH // width
    n_steps = batch * n_col * nb
    r_blk = POOL_WIDTH // width
    v_blk = (POOL_WIDTH + 2 * RWKV_WIDTH) // width
    blk = pl.BlockSpec((rows, width), lambda b, h, c: (b * nb + c, h))
    par = pl.BlockSpec((1, width), lambda b, h, c: (0, h))
    slabs = [w.reshape(n_steps, w.size // (n_steps * w.shape[-1]), w.shape[-1]) for w in cast_weights]
    slab_spec = lambda w: pl.BlockSpec((1,) + w.shape[1:], lambda b, h, c: ((b * n_col + h) * nb + c, 0, 0))
    outs = pl.pallas_call(
        functools.partial(_rwkv_scan_kernel, n_chunks=rows // RWKV_CHUNK, n_hg=n_hg,
                          n_cast=len(slabs)),
        out_shape=[jax.ShapeDtypeStruct((T, RWKV_WIDTH), bf16)]
                  + [jax.ShapeDtypeStruct(w.shape, bf16) for w in slabs],
        grid=(batch, n_col, nb),
        in_specs=[pl.BlockSpec((rows, width), lambda b, h, c: (b * nb + c, r_blk + h)),
                  pl.BlockSpec((rows, width), lambda b, h, c: (b * nb + c, v_blk + h))]
                 + [blk] * 5 + [par] * 3 + [slab_spec(w) for w in slabs],
        out_specs=[blk] + [slab_spec(w) for w in slabs],
        scratch_shapes=[pltpu.VMEM((n_hg, HG_LANES, HG_LANES), f32)],
        compiler_params=_cparams(("parallel", "parallel", "arbitrary")),
        name="rwkv_scan",
    )(proj, proj, lw, k, kk, kka, g, ln_w.reshape(1, -1), ln_b.reshape(1, -1), r_k.reshape(1, -1),
      *slabs)
    return outs[0], [o.reshape(w.shape) for o, w in zip(outs[1:], cast_weights)]


def _route(hn, w, b):
    w_hi = w.astype(bf16)
    w_lo = (w - w_hi.astype(f32)).astype(bf16)
    h_hi = hn.astype(bf16)
    h_lo = (hn - h_hi.astype(f32)).astype(bf16)
    hh = jnp.dot(h_hi, jnp.concatenate([w_hi, w_lo], axis=1), preferred_element_type=f32)
    logits = (hh[:, :ROUTER_LANES] + hh[:, ROUTER_LANES:]
              + jnp.dot(h_lo, w_hi, preferred_element_type=f32)) + b

    lane_i = lax.broadcasted_iota(jnp.int32, logits.shape, 1)
    lane = lane_i.astype(f32)
    neg = -jnp.inf
    big = float(ROUTER_LANES)
    is_g = lane_i < MOE_GROUPS
    gl = jnp.where(is_g, logits, neg)
    gmax = jnp.max(gl, axis=-1, keepdims=True)
    g_idx = jnp.min(jnp.where(gl == gmax, lane, big), axis=-1, keepdims=True)
    g_w = 1.0 / jnp.sum(jnp.exp(gl - gmax), axis=-1, keepdims=True)

    lo_lane = MOE_GROUPS + g_idx * MOE_EXPERTS_PER_GROUP
    sel = (lane >= lo_lane) & (lane < lo_lane + MOE_EXPERTS_PER_GROUP)
    el = jnp.where(sel, logits, neg)
    v1 = jnp.max(el, axis=-1, keepdims=True)
    i1 = jnp.min(jnp.where(el == v1, lane, big), axis=-1, keepdims=True)
    el2 = jnp.where(lane == i1, neg, el)
    v2 = jnp.max(el2, axis=-1, keepdims=True)
    i2 = jnp.min(jnp.where(el2 == v2, lane, big), axis=-1, keepdims=True)
    e2 = jnp.exp(v2 - v1)
    p1 = 1.0 / (1.0 + e2)
    p2 = e2 / (1.0 + e2)
    return (jnp.where(lane_i == 0, i1 - MOE_GROUPS, 0.0)
            + jnp.where(lane_i == 1, i2 - MOE_GROUPS, 0.0)
            + jnp.where(lane_i == 2, p1 * g_w, 0.0)
            + jnp.where(lane_i == 3, p2 * g_w, 0.0))


def _attn_oproj_router_kernel(q_ref, kv_ref, wo_ref, res_ref, g_ref, wr_ref, br_ref, h_ref, xr_ref,
                              rec_ref, *, tn):
    hd = XATTN_HEAD_DIM
    heads = []
    for h in range(XATTN_HEADS):
        k = kv_ref[:, h * hd:(h + 1) * hd]
        v = kv_ref[:, (XATTN_HEADS + h) * hd:(XATTN_HEADS + h + 1) * hd]
        s = lax.dot_general(q_ref[:, h * hd:(h + 1) * hd], k, (((1,), (1,)), ((), ())),
                            preferred_element_type=f32) * (hd ** -0.5)
        e = jnp.exp(s - jnp.max(s, axis=-1, keepdims=True))
        l = jnp.sum(e, axis=-1, keepdims=True)
        heads.append((jnp.dot(e.astype(bf16), v, preferred_element_type=f32) / l).astype(bf16))
    o = jnp.concatenate(heads, axis=1)
    for n0 in range(0, h_ref.shape[1], tn):
        h_ref[:, n0:n0 + tn] = res_ref[:, n0:n0 + tn] + jnp.dot(
            o, wo_ref[:, n0:n0 + tn], preferred_element_type=f32)
    hn = _rmsnorm_rows(h_ref[...], g_ref[...])
    D = hn.shape[1]
    rec = _route(hn, wr_ref[...], br_ref[...])
    xr_ref[:, :D] = hn
    xr_ref[:, D:] = rec
    rec_ref[...] = rec


def attn_oproj_router(q, kv, w_o, res, g, w_router, b_router, *, tm, tn, seq):
    T, D = res.shape
    tiles_per_seq = seq // tm
    row = lambda width: pl.BlockSpec((tm, width), lambda i: (i, 0))
    return pl.pallas_call(
        functools.partial(_attn_oproj_router_kernel, tn=tn),
        out_shape=[jax.ShapeDtypeStruct((T, D), f32),
                   jax.ShapeDtypeStruct((T, D + ROUTER_LANES), f32),
                   jax.ShapeDtypeStruct((T, ROUTER_LANES), f32)],
        grid=(T // tm,),
        in_specs=[row(D), pl.BlockSpec((MEM_LEN, 2 * D), lambda i: (i // tiles_per_seq, 0)),
                  _resident((D, D)), row(D), _resident((1, D)),
                  _resident((D, ROUTER_LANES)), _resident((1, ROUTER_LANES))],
        out_specs=[row(D), row(D + ROUTER_LANES), row(ROUTER_LANES)],
        compiler_params=_cparams(("parallel",)),
        name="attn_oproj_router",
    )(q, kv, w_o, res, g.reshape(1, D), w_router, b_router)


MOE_PAIRS = ((0, 1), (0, 2), (0, 3), (1, 3), (1, 2), (3, 2))
MOE_CLASSES = MOE_GROUPS * len(MOE_PAIRS)


def _route_metadata(rec, *, tm, n_tiles):
    ids = rec[:, 0:2].astype(jnp.int32)
    lo = jnp.minimum(ids[:, 0], ids[:, 1])
    hi = jnp.maximum(ids[:, 0], ids[:, 1])
    a, b = lo % MOE_EXPERTS_PER_GROUP, hi % MOE_EXPERTS_PER_GROUP
    pair_id = (a * (7 - a)) // 2 + b - a - 1
    pair_id = pair_id + (pair_id == 3).astype(jnp.int32) - (pair_id == 4).astype(jnp.int32)
    cls = (lo // MOE_EXPERTS_PER_GROUP) * len(MOE_PAIRS) + pair_id
    onehot = (cls[:, None] == jnp.arange(MOE_CLASSES, dtype=jnp.int32)[None, :]).astype(jnp.int32)
    csum = jnp.cumsum(onehot, axis=0)
    counts = csum[-1]
    rank = jnp.sum(csum * onehot, axis=1) - 1
    padded = ((counts + tm - 1) // tm) * tm
    ends = jnp.cumsum(padded)
    pos = jnp.sum(onehot * (ends - padded)[None, :], axis=1) + rank
    tile_start = jnp.arange(n_tiles, dtype=jnp.int32) * tm
    n_valid = ends[-1] // tm
    tile_cls = jnp.sum((tile_start[:, None] >= ends[None, :]).astype(jnp.int32), axis=1)
    last_cls = jnp.sum((ends[-1] - 1 >= ends).astype(jnp.int32))
    tile_cls = jnp.where(tile_start < ends[-1], tile_cls, last_cls)
    pair = jnp.asarray(MOE_PAIRS, jnp.int32)[tile_cls % len(MOE_PAIRS)]
    base = (tile_cls // len(MOE_PAIRS)) * MOE_EXPERTS_PER_GROUP
    idle = n_valid + jnp.arange(MOE_CLASSES, dtype=jnp.int32)
    fill_start = jnp.concatenate([jnp.maximum(ends - tm, 0), jnp.minimum(idle, n_tiles - 1) * tm])
    fill_on = jnp.concatenate([counts > 0, idle < n_tiles]).astype(jnp.int32)
    return pos, base + pair[:, 0], base + pair[:, 1], n_valid.reshape(1), fill_start, fill_on


def _moe_dispatch_kernel(pos_ref, fs_ref, fo_ref, x_ref, xs_hbm, stage, zbuf, sem, zsem, *, tm, tz):
    i = pl.program_id(0)
    n = pl.num_programs(0)

    def fill_copy(c):
        return pltpu.make_async_copy(zbuf, xs_hbm.at[pl.ds(pl.multiple_of(fs_ref[c], tz), tz), :], zsem)

    @pl.when(i == 0)
    def _():
        zbuf[...] = jnp.zeros_like(zbuf)
        for c in range(2 * MOE_CLASSES):
            @pl.when(fo_ref[c] > 0)
            def _():
                fill_copy(c).start()
        for c in range(2 * MOE_CLASSES):
            @pl.when(fo_ref[c] > 0)
            def _():
                fill_copy(c).wait()

    def tile_wait(slot):
        pltpu.make_async_copy(stage.at[slot], xs_hbm.at[pl.ds(0, tm), :], sem.at[slot]).wait()

    slot = i % 2

    @pl.when(i >= 2)
    def _():
        tile_wait(slot)

    stage[slot] = x_ref[...]

    def body(r, carry):
        p = pos_ref[i * tm + r]
        pltpu.make_async_copy(stage.at[slot, pl.ds(r, 1), :], xs_hbm.at[pl.ds(p, 1), :],
                              sem.at[slot]).start()
        return carry
    lax.fori_loop(0, tm, body, 0, unroll=8)

    @pl.when(i == n - 1)
    def _():
        tile_wait(slot)

        @pl.when(n >= 2)
        def _():
            tile_wait(1 - slot)


def moe_dispatch(xr, pos, fill_start, fill_on, *, tm, tz, n_slots):
    T, width = xr.shape
    grid_spec = pltpu.PrefetchScalarGridSpec(
        num_scalar_prefetch=3,
        grid=(T // tm,),
        in_specs=[pl.BlockSpec((tm, width), lambda i, pos, fs, fo: (i, 0))],
        out_specs=pl.BlockSpec(memory_space=pl.ANY),
        scratch_shapes=[pltpu.VMEM((2, tm, width), f32), pltpu.VMEM((tz, width), f32),
                        pltpu.SemaphoreType.DMA((2,)), pltpu.SemaphoreType.DMA(())],
    )
    return pl.pallas_call(
        functools.partial(_moe_dispatch_kernel, tm=tm, tz=tz),
        out_shape=jax.ShapeDtypeStruct((n_slots, width), f32),
        grid_spec=grid_spec,
        compiler_params=_cparams(("arbitrary",)),
        name="moe_dispatch",
    )(pos, fill_start, fill_on, xr)


def _moe_pair_kernel(ea_ref, eb_ref, nv_ref, xs_ref, wga, wua, wda, wgb, wub, wdb, o_ref):
    @pl.when(pl.program_id(0) >= nv_ref[0])
    def _():
        o_ref[...] = jnp.zeros_like(o_ref)

    @pl.when(pl.program_id(0) < nv_ref[0])
    def _():
        D = o_ref.shape[1]
        x = xs_ref[:, :D].astype(bf16)
        rec = xs_ref[:, D:]
        first = rec[:, 0:1] == ea_ref[pl.program_id(0)].astype(f32)
        w_a = jnp.where(first, rec[:, 2:3], rec[:, 3:4])
        w_b = jnp.where(first, rec[:, 3:4], rec[:, 2:3])

        def expert(wg, wu, wd):
            gate = jnp.dot(x, wg[0], preferred_element_type=f32)
            up = jnp.dot(x, wu[0], preferred_element_type=f32)
            hid = gate * _sigmoid(gate) * up
            return jnp.dot(hid.astype(bf16), wd[0], preferred_element_type=f32)

        o_ref[...] = w_a * expert(wga, wua, wda) + w_b * expert(wgb, wub, wdb)


def moe_pair_experts(xs, e_a, e_b, n_valid, wg, wu, wd, *, tm, n_tiles):
    width = xs.shape[1]
    D = wg.shape[1]
    held = lambda i, ea, eb, nv: (jnp.minimum(i, nv[0] - 1), 0)
    tile = lambda i, ea, eb, nv: (i, 0)
    wa = lambda shape: pl.BlockSpec(shape, lambda i, ea, eb, nv: (ea[i], 0, 0))
    wb = lambda shape: pl.BlockSpec(shape, lambda i, ea, eb, nv: (eb[i], 0, 0))
    grid_spec = pltpu.PrefetchScalarGridSpec(
        num_scalar_prefetch=3,
        grid=(n_tiles,),
        in_specs=[pl.BlockSpec((tm, width), held),
                  wa((1, D, MOE_FF)), wa((1, D, MOE_FF)), wa((1, MOE_FF, D)),
                  wb((1, D, MOE_FF)), wb((1, D, MOE_FF)), wb((1, MOE_FF, D))],
        out_specs=pl.BlockSpec((tm, D), tile),
    )
    return pl.pallas_call(
        _moe_pair_kernel,
        out_shape=jax.ShapeDtypeStruct((n_tiles * tm, D), f32),
        grid_spec=grid_spec,
        compiler_params=_cparams(("arbitrary",)),
        name="moe_experts",
    )(e_a, e_b, n_valid, xs, wg, wu, wd, wg, wu, wd)


def _moe_combine_kernel(pos_ref, h_ref, y_hbm, gf_ref, o_ref, ybuf, sem, *, tm):
    i = pl.program_id(0)
    n = pl.num_programs(0)

    def gather(tile, slot):
        def body(r, carry):
            p = pos_ref[tile * tm + r]
            pltpu.make_async_copy(y_hbm.at[pl.ds(p, 1), :],
                                  ybuf.at[slot, pl.ds(r, 1), :], sem.at[slot]).start()
            return carry
        lax.fori_loop(0, tm, body, 0, unroll=8)

    @pl.when(i == 0)
    def _():
        gather(0, 0)

    @pl.when(i + 1 < n)
    def _():
        gather(i + 1, (i + 1) % 2)

    slot = i % 2
    pltpu.make_async_copy(y_hbm.at[pl.ds(0, tm), :], ybuf.at[slot], sem.at[slot]).wait()
    o_ref[...] = _rmsnorm_rows(h_ref[...] + ybuf[slot], gf_ref[...])


def moe_combine(h, y_sorted, pos, g_final, *, tm):
    T, D = h.shape
    grid_spec = pltpu.PrefetchScalarGridSpec(
        num_scalar_prefetch=1,
        grid=(T // tm,),
        in_specs=[
            pl.BlockSpec((tm, D), lambda i, pos: (i, 0)),
            pl.BlockSpec(memory_space=pl.ANY),
            pl.BlockSpec((1, D), lambda i, pos: (0, 0)),
        ],
        out_specs=pl.BlockSpec((tm, D), lambda i, pos: (i, 0)),
        scratch_shapes=[pltpu.VMEM((2, tm, D), f32), pltpu.SemaphoreType.DMA((2,))],
    )
    return pl.pallas_call(
        functools.partial(_moe_combine_kernel, tm=tm),
        out_shape=jax.ShapeDtypeStruct((T, D), f32),
        grid_spec=grid_spec,
        compiler_params=_cparams(("arbitrary",)),
        name="moe_combine",
    )(pos, h, y_sorted, g_final.reshape(1, D))


def _pad_rows(w, rows):
    return jnp.pad(w, ((0, rows - w.shape[0]), (0, 0)))


def _pack_lora_cols(t):
    o1, o2 = DECAY_LORA, DECAY_LORA + AAA_LORA
    pad = lambda a, n: jnp.pad(a, [(0, 0)] * (a.ndim - 1) + [(0, n - a.shape[-1])])
    return jnp.concatenate([pad(t[..., :o1], 128), pad(t[..., o1:o2], 128),
                            pad(t[..., o2:], 256)], axis=-1)


def _layer(x2, mem2, p, *, batch, seq, cfg):
    D = D_MODEL
    w_in = p['w_in']
    w_main = w_in.astype(bf16)
    w_lora = _pack_lora_cols(w_in[:, MAIN_COLS:]).astype(bf16)
    mu_cols = jnp.concatenate([jnp.zeros((POOL_WIDTH,), f32), p['rwkv_mu'][:3 * RWKV_WIDTH],
                               _pack_lora_cols(p['rwkv_mu'][3 * RWKV_WIDTH:])])

    proj = in_proj_shift(x2, p['norm_mix_g'], w_main, w_lora, mu_cols,
                         tm=cfg['tm'], tn=cfg['tn_in'], seq=seq)
    pool_out = pool_mixer(proj, p['pool_w'].astype(bf16), p['pool_scale'],
                          batch=batch, seq=seq, ts=cfg['ts_pool'])
    lw, k, kk, kka, g = rwkv_prep(
        proj, p['rwkv_w0'], _pad_rows(p['rwkv_w2'], 128).astype(bf16),
        p['rwkv_a0'], _pad_rows(p['rwkv_a2'], 128).astype(bf16),
        _pad_rows(p['rwkv_g2'], 256).astype(bf16), p['rwkv_k_k'], p['rwkv_k_a'], ts=cfg['ts_prep'])
    rwkv_out, (w_out, w_q, w_kv, w_o, wg, wu, wd) = rwkv_scan(
        proj, lw, k, kk, kka, g, p['rwkv_ln_w'], p['rwkv_ln_b'], p['rwkv_r_k'],
        [p['w_out'], p['xattn_w_q'], p['xattn_w_kv'], p['xattn_w_o'],
         p['moe_w_gate'], p['moe_w_up'], p['moe_w_down']],
        batch=batch, seq=seq, rows=cfg['scan_rows'], n_hg=cfg['scan_hg'])
    h1 = matmul_residual_resident([pool_out, rwkv_out], [w_out[:POOL_WIDTH], w_out[POOL_WIDTH:]],
                                  x2, tm=cfg['tm'], tn=cfg['tn'], name="out_proj")

    q = norm_matmul_resident(h1, p['norm_xattn_g'], w_q,
                             tm=cfg['tm'], tn=cfg['tn'], name="q_proj")
    kv = norm_matmul_resident(mem2, p['norm_mem_g'], w_kv,
                              tm=min(cfg['tm'], mem2.shape[0]), tn=cfg['tn'], name="kv_proj")

    w_router = jnp.pad(jnp.concatenate([p['moe_w_group'], p['moe_w_expert']], axis=1),
                       ((0, 0), (0, ROUTER_LANES - MOE_GROUPS - MOE_EXPERTS)))
    b_router = jnp.pad(jnp.concatenate([p['moe_b_group'], p['moe_b_expert']]),
                       (0, ROUTER_LANES - MOE_GROUPS - MOE_EXPERTS)).reshape(1, ROUTER_LANES)
    h2, xr, rec = attn_oproj_router(q, kv, w_o, h1, p['norm_ffn_g'],
                               w_router, b_router, tm=cfg['tm'], tn=cfg['tn'], seq=seq)
    tm_e = cfg['tm_moe']
    n_tiles = xr.shape[0] // tm_e + MOE_CLASSES
    pos, e_a, e_b, n_valid, fill_start, fill_on = _route_metadata(
        rec, tm=tm_e, n_tiles=n_tiles)
    xs = moe_dispatch(xr, pos, fill_start, fill_on, tm=cfg['tm_disp'], tz=tm_e,
                      n_slots=n_tiles * tm_e)
    y_sorted = moe_pair_experts(xs, e_a, e_b, n_valid, wg, wu, wd, tm=tm_e, n_tiles=n_tiles)
    return moe_combine(h2, y_sorted, pos, p['norm_final_g'], tm=cfg['tm_comb'])


_CFG = dict(tm=512, tn=1024, tn_in=512, ts_pool=512, ts_prep=512, scan_rows=256, scan_hg=4,
            tm_moe=256, tm_disp=512, tm_comb=512)


def kernel(x, mem, norm_mix_g, w_in, pool_w, pool_scale, rwkv_mu, rwkv_w0, rwkv_w2, rwkv_a0,
           rwkv_a2, rwkv_g2, rwkv_k_k, rwkv_k_a, rwkv_r_k, rwkv_ln_w, rwkv_ln_b, w_out,
           norm_xattn_g, norm_mem_g, xattn_w_q, xattn_w_kv, xattn_w_o, norm_ffn_g,
           moe_w_group, moe_b_group, moe_w_expert, moe_b_expert, moe_w_gate, moe_w_up,
           moe_w_down, norm_final_g):
    batch, seq, D = x.shape
    p = dict(norm_mix_g=norm_mix_g[0], w_in=w_in[0], pool_w=pool_w[0], pool_scale=pool_scale[0],
             rwkv_mu=rwkv_mu[0], rwkv_w0=rwkv_w0[0], rwkv_w2=rwkv_w2[0], rwkv_a0=rwkv_a0[0],
             rwkv_a2=rwkv_a2[0], rwkv_g2=rwkv_g2[0], rwkv_k_k=rwkv_k_k[0], rwkv_k_a=rwkv_k_a[0],
             rwkv_r_k=rwkv_r_k[0], rwkv_ln_w=rwkv_ln_w[0], rwkv_ln_b=rwkv_ln_b[0], w_out=w_out[0],
             norm_xattn_g=norm_xattn_g[0], norm_mem_g=norm_mem_g[0], xattn_w_q=xattn_w_q[0],
             xattn_w_kv=xattn_w_kv[0], xattn_w_o=xattn_w_o[0], norm_ffn_g=norm_ffn_g[0],
             moe_w_group=moe_w_group[0], moe_b_group=moe_b_group[0], moe_w_expert=moe_w_expert[0],
             moe_b_expert=moe_b_expert[0], moe_w_gate=moe_w_gate[0], moe_w_up=moe_w_up[0],
             moe_w_down=moe_w_down[0], norm_final_g=norm_final_g)
    out = _layer(x.reshape(batch * seq, D), mem.reshape(batch * MEM_LEN, D), p,
                 batch=batch, seq=seq, cfg=_CFG)
    return out.reshape(batch, seq, D)
```

```python
import functools

import jax
import jax.numpy as jnp
from jax import lax
from jax.experimental import pallas as pl
from jax.experimental.pallas import tpu as pltpu

f32 = jnp.float32
bf16 = jnp.bfloat16

D_MODEL = 2048
MEM_LEN = 256
NORM_EPS = 1e-6

POOL_WIDTH = 1024
POOL_WINDOWS = (2, 4, 8, 16)
POOL_GROUP = 256
POOL_HALO = 16
RWKV_WIDTH = 1024
RWKV_HEAD = 64
RWKV_HEADS = 16
GN_EPS = 64e-5
DECAY_SCALE = 0.6065306597126334
DECAY_LORA = 64
AAA_LORA = 64
GATE_LORA = 160
LORA_PAD = 512
LORA_W_OFF, LORA_A_OFF, LORA_G_OFF = 0, 128, 256
MAIN_COLS = POOL_WIDTH + 3 * RWKV_WIDTH
PROJ_COLS = MAIN_COLS + LORA_PAD

RWKV_CHUNK = 64
HEADS_PER_STEP = 4
HG_LANES = HEADS_PER_STEP * RWKV_HEAD

XATTN_HEADS = 4
XATTN_HEAD_DIM = 512

MOE_GROUPS = 4
MOE_EXPERTS_PER_GROUP = 4
MOE_EXPERTS = 16
MOE_FF = 512
ROUTER_LANES = 128

VMEM_LIMIT = 56 * 1024 * 1024


def _cparams(sem):
    return pltpu.CompilerParams(dimension_semantics=sem, vmem_limit_bytes=VMEM_LIMIT)


def _split_dot(x, w_bf16):
    hi = x.astype(bf16)
    lo = (x - hi.astype(f32)).astype(bf16)
    return (jnp.dot(hi, w_bf16, preferred_element_type=f32)
            + jnp.dot(lo, w_bf16, preferred_element_type=f32))


def _resident(shape):
    return pl.BlockSpec(shape, lambda i: (0,) * len(shape), pipeline_mode=pl.Buffered(1))


def _rmsnorm_rows(x, g):
    ms = jnp.mean(x * x, axis=-1, keepdims=True)
    return x * lax.rsqrt(ms + NORM_EPS) * g


def _norm_mm_res_kernel(x_ref, g_ref, w_ref, o_ref, *, tn):
    xn = _rmsnorm_rows(x_ref[...], g_ref[...]).astype(bf16)
    for n0 in range(0, o_ref.shape[1], tn):
        o_ref[:, n0:n0 + tn] = jnp.dot(xn, w_ref[:, n0:n0 + tn],
                                       preferred_element_type=f32).astype(o_ref.dtype)


def norm_matmul_resident(x, g, w, *, tm, tn, name):
    M, K = x.shape
    N = w.shape[1]
    return pl.pallas_call(
        functools.partial(_norm_mm_res_kernel, tn=tn),
        out_shape=jax.ShapeDtypeStruct((M, N), bf16),
        grid=(M // tm,),
        in_specs=[pl.BlockSpec((tm, K), lambda i: (i, 0)), _resident((1, K)), _resident((K, N))],
        out_specs=pl.BlockSpec((tm, N), lambda i: (i, 0)),
        compiler_params=_cparams(("parallel",)),
        name=name,
    )(x, g.reshape(1, K), w)


def _in_proj_kernel(x_ref, xh_ref, g_ref, w_ref, wl_ref, mu_ref, o_ref, *, tn, tiles_per_seq):
    first = pl.program_id(0) % tiles_per_seq == 0
    xn = _rmsnorm_rows(x_ref[...], g_ref[...]).astype(bf16)
    xh = jnp.where(first, 0.0, _rmsnorm_rows(xh_ref[...], g_ref[...])).astype(bf16)
    lhs = jnp.concatenate([xh, xn], axis=0)
    for n0 in range(0, o_ref.shape[1], tn):
        w = w_ref[:, n0:n0 + tn] if n0 < MAIN_COLS else wl_ref[:, n0 - MAIN_COLS:n0 - MAIN_COLS + tn]
        res = jnp.dot(lhs, w, preferred_element_type=f32)
        p = res[POOL_HALO:]
        if n0 >= POOL_WIDTH:
            prev = pltpu.roll(res, 1, 0)[POOL_HALO:]
            p = p + (prev - p) * mu_ref[:, n0:n0 + tn]
        o_ref[:, n0:n0 + tn] = p.astype(o_ref.dtype)


def in_proj_shift(x, g, w_main, w_lora, mu_cols, *, tm, tn, seq):
    M, K = x.shape
    N = MAIN_COLS + w_lora.shape[1]
    hb = tm // POOL_HALO
    return pl.pallas_call(
        functools.partial(_in_proj_kernel, tn=tn, tiles_per_seq=seq // tm),
        out_shape=jax.ShapeDtypeStruct((M, N), bf16),
        grid=(M // tm,),
        in_specs=[pl.BlockSpec((tm, K), lambda i: (i, 0)),
                  pl.BlockSpec((POOL_HALO, K), lambda i: (jnp.maximum(i * hb - 1, 0), 0)),
                  _resident((1, K)), _resident(w_main.shape), _resident(w_lora.shape),
                  _resident((1, N))],
        out_specs=pl.BlockSpec((tm, N), lambda i: (i, 0)),
        compiler_params=_cparams(("parallel",)),
        name="in_proj",
    )(x, x, g.reshape(1, K), w_main, w_lora, mu_cols.reshape(1, N))


def _mm_res_resident_kernel(*refs, n, tn):
    a_refs, w_refs, res_ref, o_ref = refs[:n], refs[n:2 * n], refs[2 * n], refs[2 * n + 1]
    for n0 in range(0, o_ref.shape[1], tn):
        acc = res_ref[:, n0:n0 + tn]
        for a_ref, w_ref in zip(a_refs, w_refs):
            acc = acc + jnp.dot(a_ref[...], w_ref[:, n0:n0 + tn], preferred_element_type=f32)
        o_ref[:, n0:n0 + tn] = acc


def matmul_residual_resident(a_list, w_list, res, *, tm, tn, name):
    M, N = res.shape
    n = len(a_list)
    return pl.pallas_call(
        functools.partial(_mm_res_resident_kernel, n=n, tn=tn),
        out_shape=jax.ShapeDtypeStruct((M, N), f32),
        grid=(M // tm,),
        in_specs=([pl.BlockSpec((tm, a.shape[1]), lambda i: (i, 0)) for a in a_list]
                  + [_resident(w.shape) for w in w_list]
                  + [pl.BlockSpec((tm, N), lambda i: (i, 0))]),
        out_specs=pl.BlockSpec((tm, N), lambda i: (i, 0)),
        compiler_params=_cparams(("parallel",)),
        name=name,
    )(*a_list, *w_list, res)


def _pool_kernel(u_ref, halo_ref, pw_ref, ps_ref, o_ref, *, ts):
    s = pl.program_id(1)
    u = u_ref[...].astype(f32)
    halo = jnp.where(s > 0, halo_ref[...].astype(f32), 0.0)
    ext = jnp.concatenate([halo, u], axis=0)
    pos = s * ts + lax.broadcasted_iota(jnp.int32, (ts, 1), 0)
    acc = ext
    shift = 1
    for gi, w in enumerate(POOL_WINDOWS):
        while shift < w:
            acc = acc + pltpu.roll(acc, shift, 0)
            shift *= 2
        lo, hi = gi * POOL_GROUP, (gi + 1) * POOL_GROUP
        cnt = jnp.minimum(pos + 1, w).astype(f32)
        pooled = acc[POOL_HALO:, lo:hi] / cnt - u[:, lo:hi]
        mixed = jnp.dot(pooled.astype(bf16), pw_ref[gi], preferred_element_type=f32)
        o_ref[:, lo:hi] = (mixed * ps_ref[:, lo:hi]).astype(o_ref.dtype)


def pool_mixer(proj, pool_w, pool_scale, *, batch, seq, ts):
    ns = seq // ts
    hb = ts // POOL_HALO
    return pl.pallas_call(
        functools.partial(_pool_kernel, ts=ts),
        out_shape=jax.ShapeDtypeStruct((batch * seq, POOL_WIDTH), bf16),
        grid=(batch, ns),
        in_specs=[
            pl.BlockSpec((ts, POOL_WIDTH), lambda b, s: (b * ns + s, 0)),
            pl.BlockSpec((POOL_HALO, POOL_WIDTH),
                         lambda b, s: (jnp.maximum((b * ns + s) * hb - 1, 0), 0)),
            pl.BlockSpec((len(POOL_WINDOWS), POOL_GROUP, POOL_GROUP), lambda b, s: (0, 0, 0)),
            pl.BlockSpec((1, POOL_WIDTH), lambda b, s: (0, 0)),
        ],
        out_specs=pl.BlockSpec((ts, POOL_WIDTH), lambda b, s: (b * ns + s, 0)),
        compiler_params=_cparams(("parallel", "arbitrary")),
        name="pool_mixer",
    )(proj, proj, pool_w, pool_scale.reshape(1, POOL_WIDTH))


def _head_ones():
    r = lax.broadcasted_iota(jnp.int32, (HG_LANES, HG_LANES), 0) // RWKV_HEAD
    c = lax.broadcasted_iota(jnp.int32, (HG_LANES, HG_LANES), 1) // RWKV_HEAD
    return jnp.where(r == c, 1.0, 0.0).astype(bf16)


def _head_sum(x, ones_bd):
    parts = [_split_dot(x[:, c:c + HG_LANES], ones_bd) for c in range(0, x.shape[1], HG_LANES)]
    return parts[0] if len(parts) == 1 else jnp.concatenate(parts, axis=1)


def _sigmoid(x):
    return 1.0 / (1.0 + jnp.exp(-x))


def _rwkv_prep_kernel(k_ref, lo_ref, w0_ref, w2_ref, a0_ref, a2_ref, g2_ref, kk_ref, ka_ref,
                      lw_out, k_out, kk_out, kka_out, g_out):
    k = k_ref[...].astype(f32)
    lo = lo_ref[...].astype(f32)
    w_lo = lo[:, LORA_W_OFF:LORA_W_OFF + 128]
    a_lo = lo[:, LORA_A_OFF:LORA_A_OFF + 128]
    g_lo = lo[:, LORA_G_OFF:LORA_G_OFF + 256]

    wx = w0_ref[...] + jnp.dot(jnp.tanh(w_lo).astype(bf16), w2_ref[...], preferred_element_type=f32)
    lw_out[...] = -DECAY_SCALE * _sigmoid(wx)
    a = _sigmoid(a0_ref[...] + jnp.dot(a_lo.astype(bf16), a2_ref[...], preferred_element_type=f32))
    g = jnp.dot(_sigmoid(g_lo).astype(bf16), g2_ref[...], preferred_element_type=f32)

    ones_bd = _head_ones()
    kk = k * kk_ref[...]
    kk = kk * lax.rsqrt(jnp.maximum(_head_sum(kk * kk, ones_bd), 1e-24))
    k = k * (1.0 + (a - 1.0) * ka_ref[...])

    k_out[...] = k.astype(k_out.dtype)
    kk_out[...] = kk.astype(kk_out.dtype)
    kka_out[...] = (kk * a).astype(kka_out.dtype)
    g_out[...] = g.astype(g_out.dtype)


def rwkv_prep(proj, w0, w2p, a0, a2p, g2p, k_k, k_a, *, ts):
    T = proj.shape[0]
    C = RWKV_WIDTH
    k_blk = (POOL_WIDTH + C) // C
    lora_blk = MAIN_COLS // LORA_PAD

    def full(shape):
        return pl.BlockSpec(shape, lambda i: (0,) * len(shape))

    row = lambda t: t.reshape(1, -1)
    out_sd = lambda dt: jax.ShapeDtypeStruct((T, C), dt)
    out_spec = pl.BlockSpec((ts, C), lambda i: (i, 0))
    return pl.pallas_call(
        _rwkv_prep_kernel,
        out_shape=[out_sd(f32), out_sd(bf16), out_sd(bf16), out_sd(bf16), out_sd(bf16)],
        grid=(T // ts,),
        in_specs=[
            pl.BlockSpec((ts, C), lambda i: (i, k_blk)),
            pl.BlockSpec((ts, LORA_PAD), lambda i: (i, lora_blk)),
            full((1, C)), full((128, C)), full((1, C)), full((128, C)), full((256, C)),
            full((1, C)), full((1, C)),
        ],
        out_specs=[out_spec] * 5,
        compiler_params=_cparams(("parallel",)),
        name="rwkv_prep",
    )(proj, proj, row(w0), w2p, row(a0), a2p, g2p, row(k_k), row(k_a))


def _bd_mask():
    r = lax.broadcasted_iota(jnp.int32, (HG_LANES, HG_LANES), 0) // RWKV_CHUNK
    c = lax.broadcasted_iota(jnp.int32, (HG_LANES, HG_LANES), 1) // RWKV_HEAD
    return r == c


def _rwkv_scan_kernel(r_ref, v_ref, lw_ref, k_ref, kk_ref, kka_ref, g_ref,
                      lnw_ref, lnb_ref, rk_ref, *rest, n_chunks, n_hg, n_cast):
    L = RWKV_CHUNK
    W = HG_LANES
    cast_in, o_ref, cast_out, state_ref = rest[:n_cast], rest[n_cast], rest[n_cast + 1:-1], rest[-1]
    for w_ref, wb_ref in zip(cast_in, cast_out):
        wb_ref[...] = w_ref[...].astype(bf16)

    @pl.when(pl.program_id(2) == 0)
    def _():
        state_ref[...] = jnp.zeros_like(state_ref)

    bd_mask = _bd_mask()
    ones_bd = _head_ones()

    def bd(x):
        tiled = jnp.concatenate([x] * HEADS_PER_STEP, axis=0)
        return jnp.where(bd_mask, tiled, 0.0).astype(bf16)

    def bd_t(x):
        tiled = jnp.concatenate([x] * HEADS_PER_STEP, axis=0)
        return jnp.where(bd_mask, tiled, 0.0).T.astype(bf16)

    def mm(a, b_bf16):
        return jnp.dot(a.astype(bf16), b_bf16, preferred_element_type=f32)

    def mm_tn(a_bf16, b_bf16):
        return lax.dot_general(a_bf16, b_bf16, (((0,), (0,)), ((), ())),
                               preferred_element_type=f32)

    def split(x):
        hi = x.astype(bf16)
        return hi, (x - hi.astype(f32)).astype(bf16)

    def head_sums(xs):
        tot = mm(jnp.concatenate(xs, axis=0), ones_bd)
        return [tot[j * L:(j + 1) * L] for j in range(len(xs))]

    t_idx = lax.broadcasted_iota(jnp.int32, (L, W), 0)
    s_idx = lax.broadcasted_iota(jnp.int32, (L, W), 1) % L
    strict = t_idx > s_idx
    incl = t_idx >= s_idx
    eye_all = jnp.where(t_idx == s_idx, 1.0, 0.0)
    tri = jnp.where(lax.broadcasted_iota(jnp.int32, (L, L), 0)
                    >= lax.broadcasted_iota(jnp.int32, (L, L), 1), 1.0, 0.0).astype(bf16)
    blockdiag = (lax.broadcasted_iota(jnp.int32, (W, W), 0) // RWKV_HEAD
                 == lax.broadcasted_iota(jnp.int32, (W, W), 1) // RWKV_HEAD)

    def phase1(items):
        I = range(len(items))
        rows = [pl.ds(c * L, L) for _, c in items]
        cols = [pl.ds(h * W, W) for h, _ in items]
        ld = lambda ref, i: ref[rows[i], cols[i]]
        r = [ld(r_ref, i).astype(f32) for i in I]
        k = [ld(k_ref, i).astype(f32) for i in I]
        v = [ld(v_ref, i).astype(f32) for i in I]
        lw = [ld(lw_ref, i) for i in I]
        lw_s = [split(x) for x in lw]
        cs = [jnp.dot(tri, hi, preferred_element_type=f32) + jnp.dot(tri, lo, preferred_element_type=f32)
              for hi, lo in lw_s]
        yield
        g_inv = [jnp.exp(-x) for x in cs]
        to_end = [jnp.exp(x[L - 1:L, :] - x) for x in cs]
        a_t = [-ld(kk_ref, i).astype(f32) * jnp.exp(cs[i] - lw[i]) for i in I]
        r_t = [r[i] * jnp.exp(cs[i]) for i in I]
        kka = [ld(kka_ref, i).astype(f32) for i in I]
        b_t = [kka[i] * g_inv[i] for i in I]
        k_t = [k[i] * g_inv[i] for i in I]
        b_e = [kka[i] * to_end[i] for i in I]
        k_e = [k[i] * to_end[i] for i in I]
        v_bd = [bd(x) for x in v]
        ar = [jnp.concatenate([a_t[i], r_t[i]], axis=0) for i in I]
        m_b = [mm(ar[i], bd_t(b_t[i])) for i in I]
        m_k = [mm(ar[i], bd_t(k_t[i])) for i in I]
        m_ab = [jnp.where(strict, x[:L], 0.0) for x in m_b]
        m_rb = [jnp.where(incl, x[L:], 0.0) for x in m_b]
        m_ak = [jnp.where(strict, x[:L], 0.0) for x in m_k]
        m_rk = [jnp.where(incl, x[L:], 0.0) for x in m_k]
        rk_sum = head_sums([r[i] * k[i] * rk_ref[:, cols[i]] for i in I])
        bonus = [rk_sum[i] * v[i] for i in I]
        yield
        p = m_ab
        tinv = [eye_all + x for x in p]
        m = 2
        while m < L:
            p = [mm(x, bd(x)) for x in p]
            tinv = [tinv[i] + mm(tinv[i], bd(p[i])) for i in I]
            m *= 2
            yield
        x1 = [mm(m_ak[i], v_bd[i]) for i in I]
        u_v = [mm(tinv[i], bd(x1[i])) for i in I]
        a_h = [mm(tinv[i], bd(a_t[i])) for i in I]
        yield
        r_h = [r_t[i] + mm(m_rb[i], bd(a_h[i])) for i in I]
        y_v = [mm(m_rb[i], bd(u_v[i])) + mm(m_rk[i], v_bd[i]) for i in I]
        yield
        p_t = [jnp.where(blockdiag, mm_tn(a_h[i].astype(bf16), b_e[i].astype(bf16)), 0.0).astype(bf16)
               for i in I]
        g_t = [jnp.where(blockdiag,
                         mm_tn(jnp.concatenate([u_v[i], v[i]], axis=0).astype(bf16),
                               jnp.concatenate([b_e[i], k_e[i]], axis=0).astype(bf16)), 0.0) for i in I]
        decay = [jnp.exp(x[L - 1:L, :]) for x in cs]
        for i in I:
            done.append(dict(hg=items[i][0], rows=rows[i], cols=cols[i], r_h=r_h[i], y_v=y_v[i],
                             p_t=p_t[i], g_t=g_t[i], decay=decay[i], bonus=bonus[i]))
        yield

    def chain_step(chs):
        ys = []
        for ch in chs:
            state = states[ch['hg']]
            state_b = state.astype(bf16)
            ys.append(mm(ch['r_h'], state.T.astype(bf16)) + ch['y_v'])
            states[ch['hg']] = (state * ch['decay']
                                + jnp.dot(state_b, ch['p_t'], preferred_element_type=f32) + ch['g_t'])
        means = head_sums(ys)
        ycs = [y - m * (1.0 / RWKV_HEAD) for y, m in zip(ys, means)]
        vrs = head_sums([yc * yc for yc in ycs])
        for ch, yc, vr in zip(chs, ycs, vrs):
            rw, cl = ch['rows'], ch['cols']
            yn = yc * lax.rsqrt(vr * (1.0 / RWKV_HEAD) + GN_EPS) * lnw_ref[:, cl] + lnb_ref[:, cl]
            o_ref[rw, cl] = ((yn + ch['bonus']) * g_ref[rw, cl].astype(f32)).astype(o_ref.dtype)

    states = [state_ref[h] for h in range(n_hg)]
    done = []
    half = max(n_chunks // 2, 1)
    groups = [[(h, c) for c in range(0, half) for h in range(n_hg)],
              [(h, c) for c in range(half, n_chunks) for h in range(n_hg)]]
    def by_chunk(n_items):
        return [done[j:j + n_hg] for j in range(0, n_items, n_hg)]

    for _ in phase1(groups[0]):
        pass
    pending = by_chunk(len(done))
    for _ in phase1(groups[1]):
        if pending:
            chain_step(pending.pop(0))
    for chs in pending + by_chunk(len(done))[len(groups[0]) // n_hg:]:
        chain_step(chs)
    for h in range(n_hg):
        state_ref[h] = states[h]


def rwkv_scan(proj, lw, k, kk, kka, g, ln_w, ln_b, r_k, cast_weights, *, batch, seq, rows, n_hg):
    nb = seq // rows
    T = batch * seq
    width = n_hg * HG_LANES
    n_col = RWKV_WIDTH // width
    n_steps = batch * n_col * nb
    r_blk = POOL_WIDTH // width
    v_blk = (POOL_WIDTH + 2 * RWKV_WIDTH) // width
    blk = pl.BlockSpec((rows, width), lambda b, h, c: (b * nb + c, h))
    par = pl.BlockSpec((1, width), lambda b, h, c: (0, h))
    slabs = [w.reshape(n_steps, w.size // (n_steps * w.shape[-1]), w.shape[-1]) for w in cast_weights]
    slab_spec = lambda w: pl.BlockSpec((1,) + w.shape[1:], lambda b, h, c: ((b * n_col + h) * nb + c, 0, 0))
    outs = pl.pallas_call(
        functools.partial(_rwkv_scan_kernel, n_chunks=rows // RWKV_CHUNK, n_hg=n_hg,
                          n_cast=len(slabs)),
        out_shape=[jax.ShapeDtypeStruct((T, RWKV_WIDTH), bf16)]
                  + [jax.ShapeDtypeStruct(w.shape, bf16) for w in slabs],
        grid=(batch, n_col, nb),
        in_specs=[pl.BlockSpec((rows, width), lambda b, h, c: (b * nb + c, r_blk + h)),
                  pl.BlockSpec((rows, width), lambda b, h, c: (b * nb + c, v_blk + h))]
                 + [blk] * 5 + [par] * 3 + [slab_spec(w) for w in slabs],
        out_specs=[blk] + [slab_spec(w) for w in slabs],
        scratch_shapes=[pltpu.VMEM((n_hg, HG_LANES, HG_LANES), f32)],
        compiler_params=_cparams(("parallel", "parallel", "arbitrary")),
        name="rwkv_scan",
    )(proj, proj, lw, k, kk, kka, g, ln_w.reshape(1, -1), ln_b.reshape(1, -1), r_k.reshape(1, -1),
      *slabs)
    return outs[0], [o.reshape(w.shape) for o, w in zip(outs[1:], cast_weights)]


def _route(hn, w, b):
    w_hi = w.astype(bf16)
    w_lo = (w - w_hi.astype(f32)).astype(bf16)
    h_hi = hn.astype(bf16)
    h_lo = (hn - h_hi.astype(f32)).astype(bf16)
    hh = jnp.dot(h_hi, jnp.concatenate([w_hi, w_lo], axis=1), preferred_element_type=f32)
    logits = (hh[:, :ROUTER_LANES] + hh[:, ROUTER_LANES:]
              + jnp.dot(h_lo, w_hi, preferred_element_type=f32)) + b

    lane_i = lax.broadcasted_iota(jnp.int32, logits.shape, 1)
    lane = lane_i.astype(f32)
    neg = -jnp.inf
    big = float(ROUTER_LANES)
    is_g = lane_i < MOE_GROUPS
    gl = jnp.where(is_g, logits, neg)
    gmax = jnp.max(gl, axis=-1, keepdims=True)
    g_idx = jnp.min(jnp.where(gl == gmax, lane, big), axis=-1, keepdims=True)
    g_w = 1.0 / jnp.sum(jnp.exp(gl - gmax), axis=-1, keepdims=True)

    lo_lane = MOE_GROUPS + g_idx * MOE_EXPERTS_PER_GROUP
    sel = (lane >= lo_lane) & (lane < lo_lane + MOE_EXPERTS_PER_GROUP)
    el = jnp.where(sel, logits, neg)
    v1 = jnp.max(el, axis=-1, keepdims=True)
    i1 = jnp.min(jnp.where(el == v1, lane, big), axis=-1, keepdims=True)
    el2 = jnp.where(lane == i1, neg, el)
    v2 = jnp.max(el2, axis=-1, keepdims=True)
    i2 = jnp.min(jnp.where(el2 == v2, lane, big), axis=-1, keepdims=True)
    e2 = jnp.exp(v2 - v1)
    p1 = 1.0 / (1.0 + e2)
    p2 = e2 / (1.0 + e2)
    return (jnp.where(lane_i == 0, i1 - MOE_GROUPS, 0.0)
            + jnp.where(lane_i == 1, i2 - MOE_GROUPS, 0.0)
            + jnp.where(lane_i == 2, p1 * g_w, 0.0)
            + jnp.where(lane_i == 3, p2 * g_w, 0.0))


def _attn_oproj_router_kernel(q_ref, kv_ref, wo_ref, res_ref, g_ref, wr_ref, br_ref, h_ref, xr_ref,
                              rec_ref, *, tn):
    hd = XATTN_HEAD_DIM
    heads = []
    for h in range(XATTN_HEADS):
        k = kv_ref[:, h * hd:(h + 1) * hd]
        v = kv_ref[:, (XATTN_HEADS + h) * hd:(XATTN_HEADS + h + 1) * hd]
        s = lax.dot_general(q_ref[:, h * hd:(h + 1) * hd], k, (((1,), (1,)), ((), ())),
                            preferred_element_type=f32) * (hd ** -0.5)
        e = jnp.exp(s - jnp.max(s, axis=-1, keepdims=True))
        l = jnp.sum(e, axis=-1, keepdims=True)
        heads.append((jnp.dot(e.astype(bf16), v, preferred_element_type=f32) / l).astype(bf16))
    o = jnp.concatenate(heads, axis=1)
    for n0 in range(0, h_ref.shape[1], tn):
        h_ref[:, n0:n0 + tn] = res_ref[:, n0:n0 + tn] + jnp.dot(
            o, wo_ref[:, n0:n0 + tn], preferred_element_type=f32)
    hn = _rmsnorm_rows(h_ref[...], g_ref[...])
    D = hn.shape[1]
    rec = _route(hn, wr_ref[...], br_ref[...])
    xr_ref[:, :D] = hn
    xr_ref[:, D:] = rec
    rec_ref[...] = rec


def attn_oproj_router(q, kv, w_o, res, g, w_router, b_router, *, tm, tn, seq):
    T, D = res.shape
    tiles_per_seq = seq // tm
    row = lambda width: pl.BlockSpec((tm, width), lambda i: (i, 0))
    return pl.pallas_call(
        functools.partial(_attn_oproj_router_kernel, tn=tn),
        out_shape=[jax.ShapeDtypeStruct((T, D), f32),
                   jax.ShapeDtypeStruct((T, D + ROUTER_LANES), f32),
                   jax.ShapeDtypeStruct((T, ROUTER_LANES), f32)],
        grid=(T // tm,),
        in_specs=[row(D), pl.BlockSpec((MEM_LEN, 2 * D), lambda i: (i // tiles_per_seq, 0)),
                  _resident((D, D)), row(D), _resident((1, D)),
                  _resident((D, ROUTER_LANES)), _resident((1, ROUTER_LANES))],
        out_specs=[row(D), row(D + ROUTER_LANES), row(ROUTER_LANES)],
        compiler_params=_cparams(("parallel",)),
        name="attn_oproj_router",
    )(q, kv, w_o, res, g.reshape(1, D), w_router, b_router)


MOE_PAIRS = ((0, 1), (0, 2), (0, 3), (1, 3), (1, 2), (3, 2))
MOE_CLASSES = MOE_GROUPS * len(MOE_PAIRS)


def _route_metadata(rec, *, tm, n_tiles):
    ids = rec[:, 0:2].astype(jnp.int32)
    lo = jnp.minimum(ids[:, 0], ids[:, 1])
    hi = jnp.maximum(ids[:, 0], ids[:, 1])
    a, b = lo % MOE_EXPERTS_PER_GROUP, hi % MOE_EXPERTS_PER_GROUP
    pair_id = (a * (7 - a)) // 2 + b - a - 1
    pair_id = pair_id + (pair_id == 3).astype(jnp.int32) - (pair_id == 4).astype(jnp.int32)
    cls = (lo // MOE_EXPERTS_PER_GROUP) * len(MOE_PAIRS) + pair_id
    onehot = (cls[:, None] == jnp.arange(MOE_CLASSES, dtype=jnp.int32)[None, :]).astype(jnp.int32)
    csum = jnp.cumsum(onehot, axis=0)
    counts = csum[-1]
    rank = jnp.sum(csum * onehot, axis=1) - 1
    padded = ((counts + tm - 1) // tm) * tm
    ends = jnp.cumsum(padded)
    pos = jnp.sum(onehot * (ends - padded)[None, :], axis=1) + rank
    tile_start = jnp.arange(n_tiles, dtype=jnp.int32) * tm
    n_valid = ends[-1] // tm
    tile_cls = jnp.sum((tile_start[:, None] >= ends[None, :]).astype(jnp.int32), axis=1)
    last_cls = jnp.sum((ends[-1] - 1 >= ends).astype(jnp.int32))
    tile_cls = jnp.where(tile_start < ends[-1], tile_cls, last_cls)
    pair = jnp.asarray(MOE_PAIRS, jnp.int32)[tile_cls % len(MOE_PAIRS)]
    base = (tile_cls // len(MOE_PAIRS)) * MOE_EXPERTS_PER_GROUP
    idle = n_valid + jnp.arange(MOE_CLASSES, dtype=jnp.int32)
    fill_start = jnp.concatenate([jnp.maximum(ends - tm, 0), jnp.minimum(idle, n_tiles - 1) * tm])
    fill_on = jnp.concatenate([counts > 0, idle < n_tiles]).astype(jnp.int32)
    first_tile = (ends - padded) // tm
    j = jnp.arange(n_tiles, dtype=jnp.int32) - first_tile[tile_cls]
    tile_rows = jnp.clip(counts[tile_cls] - j * tm, 0, tm)
    return (pos, base + pair[:, 0], base + pair[:, 1], n_valid.reshape(1), tile_rows,
            fill_start, fill_on)


def _moe_dispatch_kernel(pos_ref, fs_ref, fo_ref, x_ref, xs_hbm, stage, zbuf, sem, zsem, *, tm, tz):
    i = pl.program_id(0)
    n = pl.num_programs(0)

    def fill_copy(c):
        return pltpu.make_async_copy(zbuf, xs_hbm.at[pl.ds(pl.multiple_of(fs_ref[c], tz), tz), :], zsem)

    @pl.when(i == 0)
    def _():
        zbuf[...] = jnp.zeros_like(zbuf)
        for c in range(2 * MOE_CLASSES):
            @pl.when(fo_ref[c] > 0)
            def _():
                fill_copy(c).start()
        for c in range(2 * MOE_CLASSES):
            @pl.when(fo_ref[c] > 0)
            def _():
                fill_copy(c).wait()

    def tile_wait(slot):
        pltpu.make_async_copy(stage.at[slot], xs_hbm.at[pl.ds(0, tm), :], sem.at[slot]).wait()

    slot = i % 2

    @pl.when(i >= 2)
    def _():
        tile_wait(slot)

    stage[slot] = x_ref[...]

    def body(r, carry):
        p = pos_ref[i * tm + r]
        pltpu.make_async_copy(stage.at[slot, pl.ds(r, 1), :], xs_hbm.at[pl.ds(p, 1), :],
                              sem.at[slot]).start()
        return carry
    lax.fori_loop(0, tm, body, 0, unroll=8)

    @pl.when(i == n - 1)
    def _():
        tile_wait(slot)

        @pl.when(n >= 2)
        def _():
            tile_wait(1 - slot)


def moe_dispatch(xr, pos, fill_start, fill_on, *, tm, tz, n_slots):
    T, width = xr.shape
    grid_spec = pltpu.PrefetchScalarGridSpec(
        num_scalar_prefetch=3,
        grid=(T // tm,),
        in_specs=[pl.BlockSpec((tm, width), lambda i, pos, fs, fo: (i, 0))],
        out_specs=pl.BlockSpec(memory_space=pl.ANY),
        scratch_shapes=[pltpu.VMEM((2, tm, width), f32), pltpu.VMEM((tz, width), f32),
                        pltpu.SemaphoreType.DMA((2,)), pltpu.SemaphoreType.DMA(())],
    )
    return pl.pallas_call(
        functools.partial(_moe_dispatch_kernel, tm=tm, tz=tz),
        out_shape=jax.ShapeDtypeStruct((n_slots, width), f32),
        grid_spec=grid_spec,
        compiler_params=_cparams(("arbitrary",)),
        name="moe_dispatch",
    )(pos, fill_start, fill_on, xr)


def _moe_pair_kernel(ea_ref, eb_ref, nv_ref, nr_ref, xs_ref, wga, wua, wda, wgb, wub, wdb, o_ref):
    i = pl.program_id(0)
    tm, D = o_ref.shape
    live = i < nv_ref[0]
    short = nr_ref[i] <= tm // 2

    def run(n):
        x = xs_ref[:n, :D].astype(bf16)
        rec = xs_ref[:n, D:]
        first = rec[:, 0:1] == ea_ref[i].astype(f32)
        w_a = jnp.where(first, rec[:, 2:3], rec[:, 3:4])
        w_b = jnp.where(first, rec[:, 3:4], rec[:, 2:3])

        def expert(wg, wu, wd):
            gate = jnp.dot(x, wg[0], preferred_element_type=f32)
            up = jnp.dot(x, wu[0], preferred_element_type=f32)
            hid = gate * _sigmoid(gate) * up
            return jnp.dot(hid.astype(bf16), wd[0], preferred_element_type=f32)

        o_ref[:n, :] = w_a * expert(wga, wua, wda) + w_b * expert(wgb, wub, wdb)
        if n < tm:
            o_ref[n:, :] = jnp.zeros((tm - n, D), f32)

    @pl.when(jnp.logical_not(live))
    def _():
        o_ref[...] = jnp.zeros_like(o_ref)

    @pl.when(live & jnp.logical_not(short))
    def _():
        run(tm)

    @pl.when(live & short)
    def _():
        run(tm // 2)


def moe_pair_experts(xs, e_a, e_b, n_valid, tile_rows, wg, wu, wd, *, tm, n_tiles):
    width = xs.shape[1]
    D = wg.shape[1]
    held = lambda i, ea, eb, nv, nr: (jnp.minimum(i, nv[0] - 1), 0)
    tile = lambda i, ea, eb, nv, nr: (i, 0)
    wa = lambda shape: pl.BlockSpec(shape, lambda i, ea, eb, nv, nr: (ea[i], 0, 0))
    wb = lambda shape: pl.BlockSpec(shape, lambda i, ea, eb, nv, nr: (eb[i], 0, 0))
    grid_spec = pltpu.PrefetchScalarGridSpec(
        num_scalar_prefetch=4,
        grid=(n_tiles,),
        in_specs=[pl.BlockSpec((tm, width), held),
                  wa((1, D, MOE_FF)), wa((1, D, MOE_FF)), wa((1, MOE_FF, D)),
                  wb((1, D, MOE_FF)), wb((1, D, MOE_FF)), wb((1, MOE_FF, D))],
        out_specs=pl.BlockSpec((tm, D), tile),
    )
    return pl.pallas_call(
        _moe_pair_kernel,
        out_shape=jax.ShapeDtypeStruct((n_tiles * tm, D), f32),
        grid_spec=grid_spec,
        compiler_params=_cparams(("arbitrary",)),
        name="moe_experts",
    )(e_a, e_b, n_valid, tile_rows, xs, wg, wu, wd, wg, wu, wd)


def _moe_combine_kernel(pos_ref, h_ref, y_hbm, gf_ref, o_ref, ybuf, sem, *, tm):
    i = pl.program_id(0)
    n = pl.num_programs(0)

    def gather(tile, slot):
        def body(r, carry):
            p = pos_ref[tile * tm + r]
            pltpu.make_async_copy(y_hbm.at[pl.ds(p, 1), :],
                                  ybuf.at[slot, pl.ds(r, 1), :], sem.at[slot]).start()
            return carry
        lax.fori_loop(0, tm, body, 0, unroll=8)

    @pl.when(i == 0)
    def _():
        gather(0, 0)

    @pl.when(i + 1 < n)
    def _():
        gather(i + 1, (i + 1) % 2)

    slot = i % 2
    pltpu.make_async_copy(y_hbm.at[pl.ds(0, tm), :], ybuf.at[slot], sem.at[slot]).wait()
    o_ref[...] = _rmsnorm_rows(h_ref[...] + ybuf[slot], gf_ref[...])


def moe_combine(h, y_sorted, pos, g_final, *, tm):
    T, D = h.shape
    grid_spec = pltpu.PrefetchScalarGridSpec(
        num_scalar_prefetch=1,
        grid=(T // tm,),
        in_specs=[
            pl.BlockSpec((tm, D), lambda i, pos: (i, 0)),
            pl.BlockSpec(memory_space=pl.ANY),
            pl.BlockSpec((1, D), lambda i, pos: (0, 0)),
        ],
        out_specs=pl.BlockSpec((tm, D), lambda i, pos: (i, 0)),
        scratch_shapes=[pltpu.VMEM((2, tm, D), f32), pltpu.SemaphoreType.DMA((2,))],
    )
    return pl.pallas_call(
        functools.partial(_moe_combine_kernel, tm=tm),
        out_shape=jax.ShapeDtypeStruct((T, D), f32),
        grid_spec=grid_spec,
        compiler_params=_cparams(("arbitrary",)),
        name="moe_combine",
    )(pos, h, y_sorted, g_final.reshape(1, D))


def _pad_rows(w, rows):
    return jnp.pad(w, ((0, rows - w.shape[0]), (0, 0)))


def _pack_lora_cols(t):
    o1, o2 = DECAY_LORA, DECAY_LORA + AAA_LORA
    pad = lambda a, n: jnp.pad(a, [(0, 0)] * (a.ndim - 1) + [(0, n - a.shape[-1])])
    return jnp.concatenate([pad(t[..., :o1], 128), pad(t[..., o1:o2], 128),
                            pad(t[..., o2:], 256)], axis=-1)


def _layer(x2, mem2, p, *, batch, seq, cfg):
    D = D_MODEL
    w_in = p['w_in']
    w_main = w_in.astype(bf16)
    w_lora = _pack_lora_cols(w_in[:, MAIN_COLS:]).astype(bf16)
    mu_cols = jnp.concatenate([jnp.zeros((POOL_WIDTH,), f32), p['rwkv_mu'][:3 * RWKV_WIDTH],
                               _pack_lora_cols(p['rwkv_mu'][3 * RWKV_WIDTH:])])

    proj = in_proj_shift(x2, p['norm_mix_g'], w_main, w_lora, mu_cols,
                         tm=cfg['tm'], tn=cfg['tn_in'], seq=seq)
    pool_out = pool_mixer(proj, p['pool_w'].astype(bf16), p['pool_scale'],
                          batch=batch, seq=seq, ts=cfg['ts_pool'])
    lw, k, kk, kka, g = rwkv_prep(
        proj, p['rwkv_w0'], _pad_rows(p['rwkv_w2'], 128).astype(bf16),
        p['rwkv_a0'], _pad_rows(p['rwkv_a2'], 128).astype(bf16),
        _pad_rows(p['rwkv_g2'], 256).astype(bf16), p['rwkv_k_k'], p['rwkv_k_a'], ts=cfg['ts_prep'])
    rwkv_out, (w_out, w_q, w_kv, w_o, wg, wu, wd) = rwkv_scan(
        proj, lw, k, kk, kka, g, p['rwkv_ln_w'], p['rwkv_ln_b'], p['rwkv_r_k'],
        [p['w_out'], p['xattn_w_q'], p['xattn_w_kv'], p['xattn_w_o'],
         p['moe_w_gate'], p['moe_w_up'], p['moe_w_down']],
        batch=batch, seq=seq, rows=cfg['scan_rows'], n_hg=cfg['scan_hg'])
    h1 = matmul_residual_resident([pool_out, rwkv_out], [w_out[:POOL_WIDTH], w_out[POOL_WIDTH:]],
                                  x2, tm=cfg['tm'], tn=cfg['tn'], name="out_proj")

    q = norm_matmul_resident(h1, p['norm_xattn_g'], w_q,
                             tm=cfg['tm'], tn=cfg['tn'], name="q_proj")
    kv = norm_matmul_resident(mem2, p['norm_mem_g'], w_kv,
                              tm=min(cfg['tm'], mem2.shape[0]), tn=cfg['tn'], name="kv_proj")

    w_router = jnp.pad(jnp.concatenate([p['moe_w_group'], p['moe_w_expert']], axis=1),
                       ((0, 0), (0, ROUTER_LANES - MOE_GROUPS - MOE_EXPERTS)))
    b_router = jnp.pad(jnp.concatenate([p['moe_b_group'], p['moe_b_expert']]),
                       (0, ROUTER_LANES - MOE_GROUPS - MOE_EXPERTS)).reshape(1, ROUTER_LANES)
    h2, xr, rec = attn_oproj_router(q, kv, w_o, h1, p['norm_ffn_g'],
                               w_router, b_router, tm=cfg['tm'], tn=cfg['tn'], seq=seq)
    tm_e = cfg['tm_moe']
    n_tiles = xr.shape[0] // tm_e + MOE_CLASSES
    pos, e_a, e_b, n_valid, tile_rows, fill_start, fill_on = _route_metadata(
        rec, tm=tm_e, n_tiles=n_tiles)
    xs = moe_dispatch(xr, pos, fill_start, fill_on, tm=cfg['tm_disp'], tz=tm_e,
                      n_slots=n_tiles * tm_e)
    y_sorted = moe_pair_experts(xs, e_a, e_b, n_valid, tile_rows, wg, wu, wd, tm=tm_e, n_tiles=n_tiles)
    return moe_combine(h2, y_sorted, pos, p['norm_final_g'], tm=cfg['tm_comb'])


_CFG = dict(tm=512, tn=1024, tn_in=512, ts_pool=512, ts_prep=512, scan_rows=256, scan_hg=4,
            tm_moe=256, tm_disp=512, tm_comb=512)


def kernel(x, mem, norm_mix_g, w_in, pool_w, pool_scale, rwkv_mu, rwkv_w0, rwkv_w2, rwkv_a0,
           rwkv_a2, rwkv_g2, rwkv_k_k, rwkv_k_a, rwkv_r_k, rwkv_ln_w, rwkv_ln_b, w_out,
           norm_xattn_g, norm_mem_g, xattn_w_q, xattn_w_kv, xattn_w_o, norm_ffn_g,
           moe_w_group, moe_b_group, moe_w_expert, moe_b_expert, moe_w_gate, moe_w_up,
           moe_w_down, norm_final_g):
    batch, seq, D = x.shape
    p = dict(norm_mix_g=norm_mix_g[0], w_in=w_in[0], pool_w=pool_w[0], pool_scale=pool_scale[0],
             rwkv_mu=rwkv_mu[0], rwkv_w0=rwkv_w0[0], rwkv_w2=rwkv_w2[0], rwkv_a0=rwkv_a0[0],
             rwkv_a2=rwkv_a2[0], rwkv_g2=rwkv_g2[0], rwkv_k_k=rwkv_k_k[0], rwkv_k_a=rwkv_k_a[0],
             rwkv_r_k=rwkv_r_k[0], rwkv_ln_w=rwkv_ln_w[0], rwkv_ln_b=rwkv_ln_b[0], w_out=w_out[0],
             norm_xattn_g=norm_xattn_g[0], norm_mem_g=norm_mem_g[0], xattn_w_q=xattn_w_q[0],
             xattn_w_kv=xattn_w_kv[0], xattn_w_o=xattn_w_o[0], norm_ffn_g=norm_ffn_g[0],
             moe_w_group=moe_w_group[0], moe_b_group=moe_b_group[0], moe_w_expert=moe_w_expert[0],
             moe_b_expert=moe_b_expert[0], moe_w_gate=moe_w_gate[0], moe_w_up=moe_w_up[0],
             moe_w_down=moe_w_down[0], norm_final_g=norm_final_g)
    out = _layer(x.reshape(batch * seq, D), mem.reshape(batch * MEM_LEN, D), p,
                 batch=batch, seq=seq, cfg=_CFG)
    return out.reshape(batch, seq, D)
```

```python
import functools

import jax
import jax.numpy as jnp
from jax import lax
from jax.experimental import pallas as pl
from jax.experimental.pallas import tpu as pltpu

f32 = jnp.float32
bf16 = jnp.bfloat16

D_MODEL = 2048
MEM_LEN = 256
NORM_EPS = 1e-6

POOL_WIDTH = 1024
POOL_WINDOWS = (2, 4, 8, 16)
POOL_GROUP = 256
POOL_HALO = 16
RWKV_WIDTH = 1024
RWKV_HEAD = 64
RWKV_HEADS = 16
GN_EPS = 64e-5
DECAY_SCALE = 0.6065306597126334
DECAY_LORA = 64
AAA_LORA = 64
GATE_LORA = 160
LORA_PAD = 512
LORA_W_OFF, LORA_A_OFF, LORA_G_OFF = 0, 128, 256
MAIN_COLS = POOL_WIDTH + 3 * RWKV_WIDTH
PROJ_COLS = MAIN_COLS + LORA_PAD

RWKV_CHUNK = 64
HEADS_PER_STEP = 4
HG_LANES = HEADS_PER_STEP * RWKV_HEAD

XATTN_HEADS = 4
XATTN_HEAD_DIM = 512

MOE_GROUPS = 4
MOE_EXPERTS_PER_GROUP = 4
MOE_EXPERTS = 16
MOE_FF = 512
ROUTER_LANES = 128

VMEM_LIMIT = 56 * 1024 * 1024


def _cparams(sem):
    return pltpu.CompilerParams(dimension_semantics=sem, vmem_limit_bytes=VMEM_LIMIT)


def _split_dot(x, w_bf16):
    hi = x.astype(bf16)
    lo = (x - hi.astype(f32)).astype(bf16)
    return (jnp.dot(hi, w_bf16, preferred_element_type=f32)
            + jnp.dot(lo, w_bf16, preferred_element_type=f32))


def _resident(shape):
    return pl.BlockSpec(shape, lambda i: (0,) * len(shape), pipeline_mode=pl.Buffered(1))


def _rmsnorm_rows(x, g):
    ms = jnp.mean(x * x, axis=-1, keepdims=True)
    return x * lax.rsqrt(ms + NORM_EPS) * g


def _norm_mm_res_kernel(x_ref, g_ref, w_ref, o_ref, *, tn):
    xn = _rmsnorm_rows(x_ref[...], g_ref[...]).astype(bf16)
    for n0 in range(0, o_ref.shape[1], tn):
        o_ref[:, n0:n0 + tn] = jnp.dot(xn, w_ref[:, n0:n0 + tn],
                                       preferred_element_type=f32).astype(o_ref.dtype)


def norm_matmul_resident(x, g, w, *, tm, tn, name):
    M, K = x.shape
    N = w.shape[1]
    return pl.pallas_call(
        functools.partial(_norm_mm_res_kernel, tn=tn),
        out_shape=jax.ShapeDtypeStruct((M, N), bf16),
        grid=(M // tm,),
        in_specs=[pl.BlockSpec((tm, K), lambda i: (i, 0)), _resident((1, K)), _resident((K, N))],
        out_specs=pl.BlockSpec((tm, N), lambda i: (i, 0)),
        compiler_params=_cparams(("parallel",)),
        name=name,
    )(x, g.reshape(1, K), w)


def _in_proj_kernel(x_ref, xh_ref, g_ref, w_ref, wl_ref, mu_ref, pw_ref, ps_ref, o_ref, pool_ref,
                    *, tn, tiles_per_seq):
    tm = x_ref.shape[0]
    tile_in_seq = pl.program_id(0) % tiles_per_seq
    first = tile_in_seq == 0
    xn = _rmsnorm_rows(x_ref[...], g_ref[...]).astype(bf16)
    xh = jnp.where(first, 0.0, _rmsnorm_rows(xh_ref[...], g_ref[...])).astype(bf16)
    lhs = jnp.concatenate([xh, xn], axis=0)
    pos = tile_in_seq * tm + lax.broadcasted_iota(jnp.int32, (tm, 1), 0)
    n_cols = POOL_WIDTH + o_ref.shape[1]
    for n0 in range(0, n_cols, tn):
        w = w_ref[:, n0:n0 + tn] if n0 < MAIN_COLS else wl_ref[:, n0 - MAIN_COLS:n0 - MAIN_COLS + tn]
        res = jnp.dot(lhs, w, preferred_element_type=f32)
        p = res[POOL_HALO:]
        if n0 >= POOL_WIDTH:
            prev = pltpu.roll(res, 1, 0)[POOL_HALO:]
            z = p + (prev - p) * mu_ref[:, n0:n0 + tn]
            o_ref[:, n0 - POOL_WIDTH:n0 - POOL_WIDTH + tn] = z.astype(o_ref.dtype)
            continue
        acc = res
        shift = 1
        for gi in range(n0 // POOL_GROUP, (n0 + tn) // POOL_GROUP):
            win = POOL_WINDOWS[gi]
            while shift < win:
                acc = acc + pltpu.roll(acc, shift, 0)
                shift *= 2
            c0 = gi * POOL_GROUP - n0
            cnt = jnp.minimum(pos + 1, win).astype(f32)
            pooled = acc[POOL_HALO:, c0:c0 + POOL_GROUP] / cnt - p[:, c0:c0 + POOL_GROUP]
            mixed = jnp.dot(pooled.astype(bf16), pw_ref[gi], preferred_element_type=f32)
            lo, hi = gi * POOL_GROUP, (gi + 1) * POOL_GROUP
            pool_ref[:, lo:hi] = (mixed * ps_ref[:, lo:hi]).astype(pool_ref.dtype)


def in_proj_mixers(x, g, w_main, w_lora, mu_cols, pool_w, pool_scale, *, tm, tn, seq):
    M, K = x.shape
    n_rwkv = MAIN_COLS - POOL_WIDTH + w_lora.shape[1]
    hb = tm // POOL_HALO
    return pl.pallas_call(
        functools.partial(_in_proj_kernel, tn=tn, tiles_per_seq=seq // tm),
        out_shape=[jax.ShapeDtypeStruct((M, n_rwkv), bf16),
                   jax.ShapeDtypeStruct((M, POOL_WIDTH), bf16)],
        grid=(M // tm,),
        in_specs=[pl.BlockSpec((tm, K), lambda i: (i, 0)),
                  pl.BlockSpec((POOL_HALO, K), lambda i: (jnp.maximum(i * hb - 1, 0), 0)),
                  _resident((1, K)), _resident(w_main.shape), _resident(w_lora.shape),
                  _resident((1, POOL_WIDTH + n_rwkv)), _resident(pool_w.shape),
                  _resident((1, POOL_WIDTH))],
        out_specs=[pl.BlockSpec((tm, n_rwkv), lambda i: (i, 0)),
                   pl.BlockSpec((tm, POOL_WIDTH), lambda i: (i, 0))],
        compiler_params=_cparams(("parallel",)),
        name="in_proj",
    )(x, x, g.reshape(1, K), w_main, w_lora, mu_cols.reshape(1, -1), pool_w,
      pool_scale.reshape(1, POOL_WIDTH))


def _mm_res_resident_kernel(*refs, n, tn):
    a_refs, w_refs, res_ref, o_ref = refs[:n], refs[n:2 * n], refs[2 * n], refs[2 * n + 1]
    for n0 in range(0, o_ref.shape[1], tn):
        acc = res_ref[:, n0:n0 + tn]
        for a_ref, w_ref in zip(a_refs, w_refs):
            acc = acc + jnp.dot(a_ref[...], w_ref[:, n0:n0 + tn], preferred_element_type=f32)
        o_ref[:, n0:n0 + tn] = acc


def matmul_residual_resident(a_list, w_list, res, *, tm, tn, name):
    M, N = res.shape
    n = len(a_list)
    return pl.pallas_call(
        functools.partial(_mm_res_resident_kernel, n=n, tn=tn),
        out_shape=jax.ShapeDtypeStruct((M, N), f32),
        grid=(M // tm,),
        in_specs=([pl.BlockSpec((tm, a.shape[1]), lambda i: (i, 0)) for a in a_list]
                  + [_resident(w.shape) for w in w_list]
                  + [pl.BlockSpec((tm, N), lambda i: (i, 0))]),
        out_specs=pl.BlockSpec((tm, N), lambda i: (i, 0)),
        compiler_params=_cparams(("parallel",)),
        name=name,
    )(*a_list, *w_list, res)


def _head_ones():
    r = lax.broadcasted_iota(jnp.int32, (HG_LANES, HG_LANES), 0) // RWKV_HEAD
    c = lax.broadcasted_iota(jnp.int32, (HG_LANES, HG_LANES), 1) // RWKV_HEAD
    return jnp.where(r == c, 1.0, 0.0).astype(bf16)


def _head_sum(x, ones_bd):
    parts = [_split_dot(x[:, c:c + HG_LANES], ones_bd) for c in range(0, x.shape[1], HG_LANES)]
    return parts[0] if len(parts) == 1 else jnp.concatenate(parts, axis=1)


def _sigmoid(x):
    return 1.0 / (1.0 + jnp.exp(-x))


def _rwkv_prep_kernel(k_ref, lo_ref, w0_ref, w2_ref, a0_ref, a2_ref, g2_ref, kk_ref, ka_ref,
                      lw_out, k_out, kk_out, kka_out, g_out):
    k = k_ref[...].astype(f32)
    lo = lo_ref[...].astype(f32)
    w_lo = lo[:, LORA_W_OFF:LORA_W_OFF + 128]
    a_lo = lo[:, LORA_A_OFF:LORA_A_OFF + 128]
    g_lo = lo[:, LORA_G_OFF:LORA_G_OFF + 256]

    wx = w0_ref[...] + jnp.dot(jnp.tanh(w_lo).astype(bf16), w2_ref[...], preferred_element_type=f32)
    lw_out[...] = -DECAY_SCALE * _sigmoid(wx)
    a = _sigmoid(a0_ref[...] + jnp.dot(a_lo.astype(bf16), a2_ref[...], preferred_element_type=f32))
    g = jnp.dot(_sigmoid(g_lo).astype(bf16), g2_ref[...], preferred_element_type=f32)

    ones_bd = _head_ones()
    kk = k * kk_ref[...]
    kk = kk * lax.rsqrt(jnp.maximum(_head_sum(kk * kk, ones_bd), 1e-24))
    k = k * (1.0 + (a - 1.0) * ka_ref[...])

    k_out[...] = k.astype(k_out.dtype)
    kk_out[...] = kk.astype(kk_out.dtype)
    kka_out[...] = (kk * a).astype(kka_out.dtype)
    g_out[...] = g.astype(g_out.dtype)


def rwkv_prep(proj, w0, w2p, a0, a2p, g2p, k_k, k_a, *, ts):
    T = proj.shape[0]
    C = RWKV_WIDTH
    k_blk = 1
    lora_blk = 3 * C // LORA_PAD

    def full(shape):
        return pl.BlockSpec(shape, lambda i: (0,) * len(shape))

    row = lambda t: t.reshape(1, -1)
    out_sd = lambda dt: jax.ShapeDtypeStruct((T, C), dt)
    out_spec = pl.BlockSpec((ts, C), lambda i: (i, 0))
    return pl.pallas_call(
        _rwkv_prep_kernel,
        out_shape=[out_sd(f32), out_sd(bf16), out_sd(bf16), out_sd(bf16), out_sd(bf16)],
        grid=(T // ts,),
        in_specs=[
            pl.BlockSpec((ts, C), lambda i: (i, k_blk)),
            pl.BlockSpec((ts, LORA_PAD), lambda i: (i, lora_blk)),
            full((1, C)), full((128, C)), full((1, C)), full((128, C)), full((256, C)),
            full((1, C)), full((1, C)),
        ],
        out_specs=[out_spec] * 5,
        compiler_params=_cparams(("parallel",)),
        name="rwkv_prep",
    )(proj, proj, row(w0), w2p, row(a0), a2p, g2p, row(k_k), row(k_a))


def _bd_mask():
    r = lax.broadcasted_iota(jnp.int32, (HG_LANES, HG_LANES), 0) // RWKV_CHUNK
    c = lax.broadcasted_iota(jnp.int32, (HG_LANES, HG_LANES), 1) // RWKV_HEAD
    return r == c


def _rwkv_scan_kernel(r_ref, v_ref, lw_ref, k_ref, kk_ref, kka_ref, g_ref,
                      lnw_ref, lnb_ref, rk_ref, *rest, n_chunks, n_hg, n_cast):
    L = RWKV_CHUNK
    W = HG_LANES
    cast_in, o_ref, cast_out, state_ref = rest[:n_cast], rest[n_cast], rest[n_cast + 1:-1], rest[-1]
    for w_ref, wb_ref in zip(cast_in, cast_out):
        wb_ref[...] = w_ref[...].astype(bf16)

    @pl.when(pl.program_id(2) == 0)
    def _():
        state_ref[...] = jnp.zeros_like(state_ref)

    bd_mask = _bd_mask()
    ones_bd = _head_ones()

    def bd(x):
        tiled = jnp.concatenate([x] * HEADS_PER_STEP, axis=0)
        return jnp.where(bd_mask, tiled, 0.0).astype(bf16)

    def bd_t(x):
        tiled = jnp.concatenate([x] * HEADS_PER_STEP, axis=0)
        return jnp.where(bd_mask, tiled, 0.0).T.astype(bf16)

    def mm(a, b_bf16):
        return jnp.dot(a.astype(bf16), b_bf16, preferred_element_type=f32)

    def mm_nt(a, b_bf16):
        return lax.dot_general(a.astype(bf16), b_bf16, (((1,), (1,)), ((), ())),
                               preferred_element_type=f32)

    def mm_tn(a_bf16, b_bf16):
        return lax.dot_general(a_bf16, b_bf16, (((0,), (0,)), ((), ())),
                               preferred_element_type=f32)

    def split(x):
        hi = x.astype(bf16)
        return hi, (x - hi.astype(f32)).astype(bf16)

    def head_sums(xs):
        tot = mm(jnp.concatenate(xs, axis=0), ones_bd)
        return [tot[j * L:(j + 1) * L] for j in range(len(xs))]

    t_idx = lax.broadcasted_iota(jnp.int32, (L, W), 0)
    s_idx = lax.broadcasted_iota(jnp.int32, (L, W), 1) % L
    strict = t_idx > s_idx
    incl = t_idx >= s_idx
    eye_all = jnp.where(t_idx == s_idx, 1.0, 0.0)
    tri = jnp.where(lax.broadcasted_iota(jnp.int32, (L, L), 0)
                    >= lax.broadcasted_iota(jnp.int32, (L, L), 1), 1.0, 0.0).astype(bf16)
    blockdiag = (lax.broadcasted_iota(jnp.int32, (W, W), 0) // RWKV_HEAD
                 == lax.broadcasted_iota(jnp.int32, (W, W), 1) // RWKV_HEAD)

    def phase1(items):
        I = range(len(items))
        rows = [pl.ds(c * L, L) for _, c in items]
        cols = [pl.ds(h * W, W) for h, _ in items]
        ld = lambda ref, i: ref[rows[i], cols[i]]
        r = [ld(r_ref, i).astype(f32) for i in I]
        k = [ld(k_ref, i).astype(f32) for i in I]
        v = [ld(v_ref, i).astype(f32) for i in I]
        lw = [ld(lw_ref, i) for i in I]
        lw_s = [split(x) for x in lw]
        cs = [jnp.dot(tri, hi, preferred_element_type=f32) + jnp.dot(tri, lo, preferred_element_type=f32)
              for hi, lo in lw_s]
        yield
        g_inv = [jnp.exp(-x) for x in cs]
        to_end = [jnp.exp(x[L - 1:L, :] - x) for x in cs]
        a_t = [-ld(kk_ref, i).astype(f32) * jnp.exp(cs[i] - lw[i]) for i in I]
        r_t = [r[i] * jnp.exp(cs[i]) for i in I]
        kka = [ld(kka_ref, i).astype(f32) for i in I]
        b_t = [kka[i] * g_inv[i] for i in I]
        k_t = [k[i] * g_inv[i] for i in I]
        b_e = [kka[i] * to_end[i] for i in I]
        k_e = [k[i] * to_end[i] for i in I]
        v_bd = [bd(x) for x in v]
        ar = [jnp.concatenate([a_t[i], r_t[i]], axis=0) for i in I]
        m_b = [mm(ar[i], bd_t(b_t[i])) for i in I]
        m_k = [mm(ar[i], bd_t(k_t[i])) for i in I]
        m_ab = [jnp.where(strict, x[:L], 0.0) for x in m_b]
        m_rb = [jnp.where(incl, x[L:], 0.0) for x in m_b]
        m_ak = [jnp.where(strict, x[:L], 0.0) for x in m_k]
        m_rk = [jnp.where(incl, x[L:], 0.0) for x in m_k]
        rk_sum = head_sums([r[i] * k[i] * rk_ref[:, cols[i]] for i in I])
        bonus = [rk_sum[i] * v[i] for i in I]
        yield
        p = m_ab
        tinv = [eye_all + x for x in p]
        m = 2
        while m < L:
            p = [mm(x, bd(x)) for x in p]
            tinv = [tinv[i] + mm(tinv[i], bd(p[i])) for i in I]
            m *= 2
            yield
        x1 = [mm(m_ak[i], v_bd[i]) for i in I]
        u_v = [mm(tinv[i], bd(x1[i])) for i in I]
        a_h = [mm(tinv[i], bd(a_t[i])) for i in I]
        yield
        r_h = [r_t[i] + mm(m_rb[i], bd(a_h[i])) for i in I]
        y_v = [mm(m_rb[i], bd(u_v[i])) + mm(m_rk[i], v_bd[i]) for i in I]
        yield
        p_t = [jnp.where(blockdiag, mm_tn(a_h[i].astype(bf16), b_e[i].astype(bf16)), 0.0).astype(bf16)
               for i in I]
        g_t = [jnp.where(blockdiag,
                         mm_tn(jnp.concatenate([u_v[i], v[i]], axis=0).astype(bf16),
                               jnp.concatenate([b_e[i], k_e[i]], axis=0).astype(bf16)), 0.0) for i in I]
        decay = [jnp.exp(x[L - 1:L, :]) for x in cs]
        for i in I:
            done.append(dict(hg=items[i][0], rows=rows[i], cols=cols[i], r_h=r_h[i], y_v=y_v[i],
                             p_t=p_t[i], g_t=g_t[i], decay=decay[i], bonus=bonus[i]))
        yield

    def chain_step(chs):
        ys = []
        for ch in chs:
            state = states[ch['hg']]
            state_b = state.astype(bf16)
            ys.append(mm_nt(ch['r_h'], state_b) + ch['y_v'])
            states[ch['hg']] = (state * ch['decay']
                                + jnp.dot(state_b, ch['p_t'], preferred_element_type=f32) + ch['g_t'])
        means = head_sums(ys)
        ycs = [y - m * (1.0 / RWKV_HEAD) for y, m in zip(ys, means)]
        vrs = head_sums([yc * yc for yc in ycs])
        for ch, yc, vr in zip(chs, ycs, vrs):
            rw, cl = ch['rows'], ch['cols']
            yn = yc * lax.rsqrt(vr * (1.0 / RWKV_HEAD) + GN_EPS) * lnw_ref[:, cl] + lnb_ref[:, cl]
            o_ref[rw, cl] = ((yn + ch['bonus']) * g_ref[rw, cl].astype(f32)).astype(o_ref.dtype)

    states = [state_ref[h] for h in range(n_hg)]
    done = []
    half = max(n_chunks // 2, 1)
    groups = [[(h, c) for c in range(0, half) for h in range(n_hg)],
              [(h, c) for c in range(half, n_chunks) for h in range(n_hg)]]
    def by_chunk(n_items):
        return [done[j:j + n_hg] for j in range(0, n_items, n_hg)]

    for _ in phase1(groups[0]):
        pass
    pending = by_chunk(len(done))
    for _ in phase1(groups[1]):
        if pending:
            chain_step(pending.pop(0))
    for chs in pending + by_chunk(len(done))[len(groups[0]) // n_hg:]:
        chain_step(chs)
    for h in range(n_hg):
        state_ref[h] = states[h]


def rwkv_scan(proj, lw, k, kk, kka, g, ln_w, ln_b, r_k, cast_weights, *, batch, seq, rows, n_hg):
    nb = seq // rows
    T = batch * seq
    width = n_hg * HG_LANES
    n_col = RWKV_WIDTH // width
    n_steps = batch * n_col * nb
    r_blk = 0
    v_blk = 2 * RWKV_WIDTH // width
    blk = pl.BlockSpec((rows, width), lambda b, h, c: (b * nb + c, h))
    par = pl.BlockSpec((1, width), lambda b, h, c: (0, h))
    slabs = [w.reshape(n_steps, w.size // (n_steps * w.shape[-1]), w.shape[-1]) for w in cast_weights]
    slab_spec = lambda w: pl.BlockSpec((1,) + w.shape[1:], lambda b, h, c: ((b * n_col + h) * nb + c, 0, 0))
    outs = pl.pallas_call(
        functools.partial(_rwkv_scan_kernel, n_chunks=rows // RWKV_CHUNK, n_hg=n_hg,
                          n_cast=len(slabs)),
        out_shape=[jax.ShapeDtypeStruct((T, RWKV_WIDTH), bf16)]
                  + [jax.ShapeDtypeStruct(w.shape, bf16) for w in slabs],
        grid=(batch, n_col, nb),
        in_specs=[pl.BlockSpec((rows, width), lambda b, h, c: (b * nb + c, r_blk + h)),
                  pl.BlockSpec((rows, width), lambda b, h, c: (b * nb + c, v_blk + h))]
                 + [blk] * 5 + [par] * 3 + [slab_spec(w) for w in slabs],
        out_specs=[blk] + [slab_spec(w) for w in slabs],
        scratch_shapes=[pltpu.VMEM((n_hg, HG_LANES, HG_LANES), f32)],
        compiler_params=_cparams(("parallel", "parallel", "arbitrary")),
        name="rwkv_scan",
    )(proj, proj, lw, k, kk, kka, g, ln_w.reshape(1, -1), ln_b.reshape(1, -1), r_k.reshape(1, -1),
      *slabs)
    return outs[0], [o.reshape(w.shape) for o, w in zip(outs[1:], cast_weights)]


def _route(hn, w, b):
    w_hi = w.astype(bf16)
    w_lo = (w - w_hi.astype(f32)).astype(bf16)
    h_hi = hn.astype(bf16)
    h_lo = (hn - h_hi.astype(f32)).astype(bf16)
    hh = jnp.dot(h_hi, jnp.concatenate([w_hi, w_lo], axis=1), preferred_element_type=f32)
    logits = (hh[:, :ROUTER_LANES] + hh[:, ROUTER_LANES:]
              + jnp.dot(h_lo, w_hi, preferred_element_type=f32)) + b

    lane_i = lax.broadcasted_iota(jnp.int32, logits.shape, 1)
    lane = lane_i.astype(f32)
    neg = -jnp.inf
    big = float(ROUTER_LANES)
    is_g = lane_i < MOE_GROUPS
    gl = jnp.where(is_g, logits, neg)
    gmax = jnp.max(gl, axis=-1, keepdims=True)
    g_idx = jnp.min(jnp.where(gl == gmax, lane, big), axis=-1, keepdims=True)
    g_w = 1.0 / jnp.sum(jnp.exp(gl - gmax), axis=-1, keepdims=True)

    lo_lane = MOE_GROUPS + g_idx * MOE_EXPERTS_PER_GROUP
    sel = (lane >= lo_lane) & (lane < lo_lane + MOE_EXPERTS_PER_GROUP)
    el = jnp.where(sel, logits, neg)
    v1 = jnp.max(el, axis=-1, keepdims=True)
    i1 = jnp.min(jnp.where(el == v1, lane, big), axis=-1, keepdims=True)
    el2 = jnp.where(lane == i1, neg, el)
    v2 = jnp.max(el2, axis=-1, keepdims=True)
    i2 = jnp.min(jnp.where(el2 == v2, lane, big), axis=-1, keepdims=True)
    e2 = jnp.exp(v2 - v1)
    p1 = 1.0 / (1.0 + e2)
    p2 = e2 / (1.0 + e2)
    return (jnp.where(lane_i == 0, i1 - MOE_GROUPS, 0.0)
            + jnp.where(lane_i == 1, i2 - MOE_GROUPS, 0.0)
            + jnp.where(lane_i == 2, p1 * g_w, 0.0)
            + jnp.where(lane_i == 3, p2 * g_w, 0.0))


def _attn_oproj_router_kernel(q_ref, kv_ref, wo_ref, res_ref, g_ref, wr_ref, br_ref, h_ref, xr_ref,
                              rec_ref, *, tn):
    hd = XATTN_HEAD_DIM
    heads = []
    for h in range(XATTN_HEADS):
        k = kv_ref[:, h * hd:(h + 1) * hd]
        v = kv_ref[:, (XATTN_HEADS + h) * hd:(XATTN_HEADS + h + 1) * hd]
        s = lax.dot_general(q_ref[:, h * hd:(h + 1) * hd], k, (((1,), (1,)), ((), ())),
                            preferred_element_type=f32) * (hd ** -0.5)
        e = jnp.exp(s - jnp.max(s, axis=-1, keepdims=True))
        l = jnp.sum(e, axis=-1, keepdims=True)
        heads.append((jnp.dot(e.astype(bf16), v, preferred_element_type=f32) / l).astype(bf16))
    o = jnp.concatenate(heads, axis=1)
    for n0 in range(0, h_ref.shape[1], tn):
        h_ref[:, n0:n0 + tn] = res_ref[:, n0:n0 + tn] + jnp.dot(
            o, wo_ref[:, n0:n0 + tn], preferred_element_type=f32)
    hn = _rmsnorm_rows(h_ref[...], g_ref[...])
    D = hn.shape[1]
    rec = _route(hn, wr_ref[...], br_ref[...])
    xr_ref[:, :D] = hn
    xr_ref[:, D:] = rec
    rec_ref[...] = rec


def attn_oproj_router(q, kv, w_o, res, g, w_router, b_router, *, tm, tn, seq):
    T, D = res.shape
    tiles_per_seq = seq // tm
    row = lambda width: pl.BlockSpec((tm, width), lambda i: (i, 0))
    return pl.pallas_call(
        functools.partial(_attn_oproj_router_kernel, tn=tn),
        out_shape=[jax.ShapeDtypeStruct((T, D), f32),
                   jax.ShapeDtypeStruct((T, D + ROUTER_LANES), f32),
                   jax.ShapeDtypeStruct((T, ROUTER_LANES), f32)],
        grid=(T // tm,),
        in_specs=[row(D), pl.BlockSpec((MEM_LEN, 2 * D), lambda i: (i // tiles_per_seq, 0)),
                  _resident((D, D)), row(D), _resident((1, D)),
                  _resident((D, ROUTER_LANES)), _resident((1, ROUTER_LANES))],
        out_specs=[row(D), row(D + ROUTER_LANES), row(ROUTER_LANES)],
        compiler_params=_cparams(("parallel",)),
        name="attn_oproj_router",
    )(q, kv, w_o, res, g.reshape(1, D), w_router, b_router)


MOE_PAIRS = ((0, 1), (0, 2), (0, 3), (1, 3), (1, 2), (3, 2))
MOE_CLASSES = MOE_GROUPS * len(MOE_PAIRS)


def _route_metadata(rec, *, tm, n_tiles):
    ids = rec[:, 0:2].astype(jnp.int32)
    lo = jnp.minimum(ids[:, 0], ids[:, 1])
    hi = jnp.maximum(ids[:, 0], ids[:, 1])
    a, b = lo % MOE_EXPERTS_PER_GROUP, hi % MOE_EXPERTS_PER_GROUP
    pair_id = (a * (7 - a)) // 2 + b - a - 1
    pair_id = pair_id + (pair_id == 3).astype(jnp.int32) - (pair_id == 4).astype(jnp.int32)
    cls = (lo // MOE_EXPERTS_PER_GROUP) * len(MOE_PAIRS) + pair_id
    onehot = (cls[:, None] == jnp.arange(MOE_CLASSES, dtype=jnp.int32)[None, :]).astype(jnp.int32)
    csum = jnp.cumsum(onehot, axis=0)
    counts = csum[-1]
    rank = jnp.sum(csum * onehot, axis=1) - 1
    padded = ((counts + tm - 1) // tm) * tm
    ends = jnp.cumsum(padded)
    pos = jnp.sum(onehot * (ends - padded)[None, :], axis=1) + rank
    tile_start = jnp.arange(n_tiles, dtype=jnp.int32) * tm
    n_valid = ends[-1] // tm
    tile_cls = jnp.sum((tile_start[:, None] >= ends[None, :]).astype(jnp.int32), axis=1)
    last_cls = jnp.sum((ends[-1] - 1 >= ends).astype(jnp.int32))
    tile_cls = jnp.where(tile_start < ends[-1], tile_cls, last_cls)
    pair = jnp.asarray(MOE_PAIRS, jnp.int32)[tile_cls % len(MOE_PAIRS)]
    base = (tile_cls // len(MOE_PAIRS)) * MOE_EXPERTS_PER_GROUP
    idle = n_valid + jnp.arange(MOE_CLASSES, dtype=jnp.int32)
    fill_start = jnp.concatenate([jnp.maximum(ends - tm, 0), jnp.minimum(idle, n_tiles - 1) * tm])
    fill_on = jnp.concatenate([counts > 0, idle < n_tiles]).astype(jnp.int32)
    return pos, base + pair[:, 0], base + pair[:, 1], n_valid.reshape(1), fill_start, fill_on


def _moe_dispatch_kernel(pos_ref, fs_ref, fo_ref, x_ref, xs_hbm, stage, zbuf, sem, zsem, *, tm, tz):
    i = pl.program_id(0)
    n = pl.num_programs(0)

    def fill_copy(c):
        return pltpu.make_async_copy(zbuf, xs_hbm.at[pl.ds(pl.multiple_of(fs_ref[c], tz), tz), :], zsem)

    @pl.when(i == 0)
    def _():
        zbuf[...] = jnp.zeros_like(zbuf)
        for c in range(2 * MOE_CLASSES):
            @pl.when(fo_ref[c] > 0)
            def _():
                fill_copy(c).start()
        for c in range(2 * MOE_CLASSES):
            @pl.when(fo_ref[c] > 0)
            def _():
                fill_copy(c).wait()

    def tile_wait(slot):
        pltpu.make_async_copy(stage.at[slot], xs_hbm.at[pl.ds(0, tm), :], sem.at[slot]).wait()

    slot = i % 2

    @pl.when(i >= 2)
    def _():
        tile_wait(slot)

    stage[slot] = x_ref[...]

    def body(r, carry):
        p = pos_ref[i * tm + r]
        pltpu.make_async_copy(stage.at[slot, pl.ds(r, 1), :], xs_hbm.at[pl.ds(p, 1), :],
                              sem.at[slot]).start()
        return carry
    lax.fori_loop(0, tm, body, 0, unroll=8)

    @pl.when(i == n - 1)
    def _():
        tile_wait(slot)

        @pl.when(n >= 2)
        def _():
            tile_wait(1 - slot)


def moe_dispatch(xr, pos, fill_start, fill_on, *, tm, tz, n_slots):
    T, width = xr.shape
    grid_spec = pltpu.PrefetchScalarGridSpec(
        num_scalar_prefetch=3,
        grid=(T // tm,),
        in_specs=[pl.BlockSpec((tm, width), lambda i, pos, fs, fo: (i, 0))],
        out_specs=pl.BlockSpec(memory_space=pl.ANY),
        scratch_shapes=[pltpu.VMEM((2, tm, width), f32), pltpu.VMEM((tz, width), f32),
                        pltpu.SemaphoreType.DMA((2,)), pltpu.SemaphoreType.DMA(())],
    )
    return pl.pallas_call(
        functools.partial(_moe_dispatch_kernel, tm=tm, tz=tz),
        out_shape=jax.ShapeDtypeStruct((n_slots, width), f32),
        grid_spec=grid_spec,
        compiler_params=_cparams(("arbitrary",)),
        name="moe_dispatch",
    )(pos, fill_start, fill_on, xr)


def _moe_pair_kernel(ea_ref, eb_ref, nv_ref, xs_ref, wga, wua, wda, wgb, wub, wdb, o_ref):
    @pl.when(pl.program_id(0) >= nv_ref[0])
    def _():
        o_ref[...] = jnp.zeros_like(o_ref)

    @pl.when(pl.program_id(0) < nv_ref[0])
    def _():
        D = o_ref.shape[1]
        x = xs_ref[:, :D].astype(bf16)
        rec = xs_ref[:, D:]
        first = rec[:, 0:1] == ea_ref[pl.program_id(0)].astype(f32)
        w_a = jnp.where(first, rec[:, 2:3], rec[:, 3:4])
        w_b = jnp.where(first, rec[:, 3:4], rec[:, 2:3])

        def expert(wg, wu, wd):
            gate = jnp.dot(x, wg[0], preferred_element_type=f32)
            up = jnp.dot(x, wu[0], preferred_element_type=f32)
            hid = gate * _sigmoid(gate) * up
            return jnp.dot(hid.astype(bf16), wd[0], preferred_element_type=f32)

        o_ref[...] = w_a * expert(wga, wua, wda) + w_b * expert(wgb, wub, wdb)


def moe_pair_experts(xs, e_a, e_b, n_valid, wg, wu, wd, *, tm, n_tiles):
    width = xs.shape[1]
    D = wg.shape[1]
    held = lambda i, ea, eb, nv: (jnp.minimum(i, nv[0] - 1), 0)
    tile = lambda i, ea, eb, nv: (i, 0)
    wa = lambda shape: pl.BlockSpec(shape, lambda i, ea, eb, nv: (ea[i], 0, 0))
    wb = lambda shape: pl.BlockSpec(shape, lambda i, ea, eb, nv: (eb[i], 0, 0))
    grid_spec = pltpu.PrefetchScalarGridSpec(
        num_scalar_prefetch=3,
        grid=(n_tiles,),
        in_specs=[pl.BlockSpec((tm, width), held),
                  wa((1, D, MOE_FF)), wa((1, D, MOE_FF)), wa((1, MOE_FF, D)),
                  wb((1, D, MOE_FF)), wb((1, D, MOE_FF)), wb((1, MOE_FF, D))],
        out_specs=pl.BlockSpec((tm, D), tile),
    )
    return pl.pallas_call(
        _moe_pair_kernel,
        out_shape=jax.ShapeDtypeStruct((n_tiles * tm, D), f32),
        grid_spec=grid_spec,
        compiler_params=_cparams(("arbitrary",)),
        name="moe_experts",
    )(e_a, e_b, n_valid, xs, wg, wu, wd, wg, wu, wd)


def _moe_combine_kernel(pos_ref, h_ref, y_hbm, gf_ref, o_ref, ybuf, sem, *, tm):
    i = pl.program_id(0)
    n = pl.num_programs(0)

    def gather(tile, slot):
        def body(r, carry):
            p = pos_ref[tile * tm + r]
            pltpu.make_async_copy(y_hbm.at[pl.ds(p, 1), :],
                                  ybuf.at[slot, pl.ds(r, 1), :], sem.at[slot]).start()
            return carry
        lax.fori_loop(0, tm, body, 0, unroll=8)

    @pl.when(i == 0)
    def _():
        gather(0, 0)

    @pl.when(i + 1 < n)
    def _():
        gather(i + 1, (i + 1) % 2)

    slot = i % 2
    pltpu.make_async_copy(y_hbm.at[pl.ds(0, tm), :], ybuf.at[slot], sem.at[slot]).wait()
    o_ref[...] = _rmsnorm_rows(h_ref[...] + ybuf[slot], gf_ref[...])


def moe_combine(h, y_sorted, pos, g_final, *, tm):
    T, D = h.shape
    grid_spec = pltpu.PrefetchScalarGridSpec(
        num_scalar_prefetch=1,
        grid=(T // tm,),
        in_specs=[
            pl.BlockSpec((tm, D), lambda i, pos: (i, 0)),
            pl.BlockSpec(memory_space=pl.ANY),
            pl.BlockSpec((1, D), lambda i, pos: (0, 0)),
        ],
        out_specs=pl.BlockSpec((tm, D), lambda i, pos: (i, 0)),
        scratch_shapes=[pltpu.VMEM((2, tm, D), f32), pltpu.SemaphoreType.DMA((2,))],
    )
    return pl.pallas_call(
        functools.partial(_moe_combine_kernel, tm=tm),
        out_shape=jax.ShapeDtypeStruct((T, D), f32),
        grid_spec=grid_spec,
        compiler_params=_cparams(("arbitrary",)),
        name="moe_combine",
    )(pos, h, y_sorted, g_final.reshape(1, D))


def _pad_rows(w, rows):
    return jnp.pad(w, ((0, rows - w.shape[0]), (0, 0)))


def _pack_lora_cols(t):
    o1, o2 = DECAY_LORA, DECAY_LORA + AAA_LORA
    pad = lambda a, n: jnp.pad(a, [(0, 0)] * (a.ndim - 1) + [(0, n - a.shape[-1])])
    return jnp.concatenate([pad(t[..., :o1], 128), pad(t[..., o1:o2], 128),
                            pad(t[..., o2:], 256)], axis=-1)


def _layer(x2, mem2, p, *, batch, seq, cfg):
    D = D_MODEL
    w_in = p['w_in']
    w_main = w_in.astype(bf16)
    w_lora = _pack_lora_cols(w_in[:, MAIN_COLS:]).astype(bf16)
    mu_cols = jnp.concatenate([jnp.zeros((POOL_WIDTH,), f32), p['rwkv_mu'][:3 * RWKV_WIDTH],
                               _pack_lora_cols(p['rwkv_mu'][3 * RWKV_WIDTH:])])

    proj, pool_out = in_proj_mixers(x2, p['norm_mix_g'], w_main, w_lora, mu_cols,
                                    p['pool_w'].astype(bf16), p['pool_scale'],
                                    tm=cfg['tm'], tn=cfg['tn_in'], seq=seq)
    lw, k, kk, kka, g = rwkv_prep(
        proj, p['rwkv_w0'], _pad_rows(p['rwkv_w2'], 128).astype(bf16),
        p['rwkv_a0'], _pad_rows(p['rwkv_a2'], 128).astype(bf16),
        _pad_rows(p['rwkv_g2'], 256).astype(bf16), p['rwkv_k_k'], p['rwkv_k_a'], ts=cfg['ts_prep'])
    rwkv_out, (w_out, w_q, w_kv, w_o, wg, wu, wd) = rwkv_scan(
        proj, lw, k, kk, kka, g, p['rwkv_ln_w'], p['rwkv_ln_b'], p['rwkv_r_k'],
        [p['w_out'], p['xattn_w_q'], p['xattn_w_kv'], p['xattn_w_o'],
         p['moe_w_gate'], p['moe_w_up'], p['moe_w_down']],
        batch=batch, seq=seq, rows=cfg['scan_rows'], n_hg=cfg['scan_hg'])
    h1 = matmul_residual_resident([pool_out, rwkv_out], [w_out[:POOL_WIDTH], w_out[POOL_WIDTH:]],
                                  x2, tm=cfg['tm'], tn=cfg['tn'], name="out_proj")

    q = norm_matmul_resident(h1, p['norm_xattn_g'], w_q,
                             tm=cfg['tm_q'], tn=cfg['tn'], name="q_proj")
    kv = norm_matmul_resident(mem2, p['norm_mem_g'], w_kv,
                              tm=min(cfg['tm'], mem2.shape[0]), tn=cfg['tn'], name="kv_proj")

    w_router = jnp.pad(jnp.concatenate([p['moe_w_group'], p['moe_w_expert']], axis=1),
                       ((0, 0), (0, ROUTER_LANES - MOE_GROUPS - MOE_EXPERTS)))
    b_router = jnp.pad(jnp.concatenate([p['moe_b_group'], p['moe_b_expert']]),
                       (0, ROUTER_LANES - MOE_GROUPS - MOE_EXPERTS)).reshape(1, ROUTER_LANES)
    h2, xr, rec = attn_oproj_router(q, kv, w_o, h1, p['norm_ffn_g'],
                               w_router, b_router, tm=cfg['tm'], tn=cfg['tn'], seq=seq)
    tm_e = cfg['tm_moe']
    n_tiles = xr.shape[0] // tm_e + MOE_CLASSES
    pos, e_a, e_b, n_valid, fill_start, fill_on = _route_metadata(
        rec, tm=tm_e, n_tiles=n_tiles)
    xs = moe_dispatch(xr, pos, fill_start, fill_on, tm=cfg['tm_disp'], tz=tm_e,
                      n_slots=n_tiles * tm_e)
    y_sorted = moe_pair_experts(xs, e_a, e_b, n_valid, wg, wu, wd, tm=tm_e, n_tiles=n_tiles)
    return moe_combine(h2, y_sorted, pos, p['norm_final_g'], tm=cfg['tm_comb'])


_CFG = dict(tm=512, tm_q=1024, tn=1024, tn_in=512, ts_prep=512, scan_rows=256, scan_hg=4,
            tm_moe=256, tm_disp=512, tm_comb=512)


def kernel(x, mem, norm_mix_g, w_in, pool_w, pool_scale, rwkv_mu, rwkv_w0, rwkv_w2, rwkv_a0,
           rwkv_a2, rwkv_g2, rwkv_k_k, rwkv_k_a, rwkv_r_k, rwkv_ln_w, rwkv_ln_b, w_out,
           norm_xattn_g, norm_mem_g, xattn_w_q, xattn_w_kv, xattn_w_o, norm_ffn_g,
           moe_w_group, moe_b_group, moe_w_expert, moe_b_expert, moe_w_gate, moe_w_up,
           moe_w_down, norm_final_g):
    batch, seq, D = x.shape
    p = dict(norm_mix_g=norm_mix_g[0], w_in=w_in[0], pool_w=pool_w[0], pool_scale=pool_scale[0],
             rwkv_mu=rwkv_mu[0], rwkv_w0=rwkv_w0[0], rwkv_w2=rwkv_w2[0], rwkv_a0=rwkv_a0[0],
             rwkv_a2=rwkv_a2[0], rwkv_g2=rwkv_g2[0], rwkv_k_k=rwkv_k_k[0], rwkv_k_a=rwkv_k_a[0],
             rwkv_r_k=rwkv_r_k[0], rwkv_ln_w=rwkv_ln_w[0], rwkv_ln_b=rwkv_ln_b[0], w_out=w_out[0],
             norm_xattn_g=norm_xattn_g[0], norm_mem_g=norm_mem_g[0], xattn_w_q=xattn_w_q[0],
             xattn_w_kv=xattn_w_kv[0], xattn_w_o=xattn_w_o[0], norm_ffn_g=norm_ffn_g[0],
             moe_w_group=moe_w_group[0], moe_b_group=moe_b_group[0], moe_w_expert=moe_w_expert[0],
             moe_b_expert=moe_b_expert[0], moe_w_gate=moe_w_gate[0], moe_w_up=moe_w_up[0],
             moe_w_down=moe_w_down[0], norm_final_g=norm_final_g)
    out = _layer(x.reshape(batch * seq, D), mem.reshape(batch * MEM_LEN, D), p,
                 batch=batch, seq=seq, cfg=_CFG)
    return out.reshape(batch, seq, D)
```

```python
import functools

import jax
import jax.numpy as jnp
from jax import lax
from jax.experimental import pallas as pl
from jax.experimental.pallas import tpu as pltpu

f32 = jnp.float32
bf16 = jnp.bfloat16

D_MODEL = 2048
MEM_LEN = 256
NORM_EPS = 1e-6

POOL_WIDTH = 1024
POOL_WINDOWS = (2, 4, 8, 16)
POOL_GROUP = 256
POOL_HALO = 16
RWKV_WIDTH = 1024
RWKV_HEAD = 64
RWKV_HEADS = 16
GN_EPS = 64e-5
DECAY_SCALE = 0.6065306597126334
DECAY_LORA = 64
AAA_LORA = 64
GATE_LORA = 160
LORA_PAD = 512
LORA_W_OFF, LORA_A_OFF, LORA_G_OFF = 0, 128, 256
MAIN_COLS = POOL_WIDTH + 3 * RWKV_WIDTH
PROJ_COLS = MAIN_COLS + LORA_PAD

RWKV_CHUNK = 64
HEADS_PER_STEP = 4
HG_LANES = HEADS_PER_STEP * RWKV_HEAD

XATTN_HEADS = 4
XATTN_HEAD_DIM = 512

MOE_GROUPS = 4
MOE_EXPERTS_PER_GROUP = 4
MOE_EXPERTS = 16
MOE_FF = 512
ROUTER_LANES = 128

VMEM_LIMIT = 56 * 1024 * 1024


def _cparams(sem):
    return pltpu.CompilerParams(dimension_semantics=sem, vmem_limit_bytes=VMEM_LIMIT)


def _split_dot(x, w_bf16):
    hi = x.astype(bf16)
    lo = (x - hi.astype(f32)).astype(bf16)
    return (jnp.dot(hi, w_bf16, preferred_element_type=f32)
            + jnp.dot(lo, w_bf16, preferred_element_type=f32))


def _resident(shape):
    return pl.BlockSpec(shape, lambda i: (0,) * len(shape), pipeline_mode=pl.Buffered(1))


def _rmsnorm_rows(x, g):
    ms = jnp.mean(x * x, axis=-1, keepdims=True)
    return x * lax.rsqrt(ms + NORM_EPS) * g


def _norm_mm_res_kernel(x_ref, g_ref, w_ref, o_ref, *, tn):
    xn = _rmsnorm_rows(x_ref[...], g_ref[...]).astype(bf16)
    for n0 in range(0, o_ref.shape[1], tn):
        o_ref[:, n0:n0 + tn] = jnp.dot(xn, w_ref[:, n0:n0 + tn],
                                       preferred_element_type=f32).astype(o_ref.dtype)


def norm_matmul_resident(x, g, w, *, tm, tn, name):
    M, K = x.shape
    N = w.shape[1]
    return pl.pallas_call(
        functools.partial(_norm_mm_res_kernel, tn=tn),
        out_shape=jax.ShapeDtypeStruct((M, N), bf16),
        grid=(M // tm,),
        in_specs=[pl.BlockSpec((tm, K), lambda i: (i, 0)), _resident((1, K)), _resident((K, N))],
        out_specs=pl.BlockSpec((tm, N), lambda i: (i, 0)),
        compiler_params=_cparams(("parallel",)),
        name=name,
    )(x, g.reshape(1, K), w)


def _in_proj_kernel(x_ref, xh_ref, g_ref, w_ref, wl_ref, mu_ref, pw_ref, ps_ref, o_ref, pool_ref,
                    *, tn, tiles_per_seq):
    tm = x_ref.shape[0]
    tile_in_seq = pl.program_id(0) % tiles_per_seq
    first = tile_in_seq == 0
    xn = _rmsnorm_rows(x_ref[...], g_ref[...]).astype(bf16)
    xh = jnp.where(first, 0.0, _rmsnorm_rows(xh_ref[...], g_ref[...])).astype(bf16)
    lhs = jnp.concatenate([xh, xn], axis=0)
    pos = tile_in_seq * tm + lax.broadcasted_iota(jnp.int32, (tm, 1), 0)
    n_cols = POOL_WIDTH + o_ref.shape[1]
    def slab(n0):
        w = w_ref[:, n0:n0 + tn] if n0 < MAIN_COLS else wl_ref[:, n0 - MAIN_COLS:n0 - MAIN_COLS + tn]
        return jnp.dot(lhs, w, preferred_element_type=f32)

    def pool_slab(n0, res):
        p = res[POOL_HALO:]
        acc = res
        shift = 1
        for gi in range(n0 // POOL_GROUP, (n0 + tn) // POOL_GROUP):
            win = POOL_WINDOWS[gi]
            while shift < win:
                acc = acc + pltpu.roll(acc, shift, 0)
                shift *= 2
            c0 = gi * POOL_GROUP - n0
            cnt = jnp.minimum(pos + 1, win).astype(f32)
            pooled = acc[POOL_HALO:, c0:c0 + POOL_GROUP] / cnt - p[:, c0:c0 + POOL_GROUP]
            mixed = jnp.dot(pooled.astype(bf16), pw_ref[gi], preferred_element_type=f32)
            lo, hi = gi * POOL_GROUP, (gi + 1) * POOL_GROUP
            pool_ref[:, lo:hi] = (mixed * ps_ref[:, lo:hi]).astype(pool_ref.dtype)

    def shift_slab(n0, res):
        p = res[POOL_HALO:]
        prev = pltpu.roll(res, 1, 0)[POOL_HALO:]
        z = p + (prev - p) * mu_ref[:, n0:n0 + tn]
        o_ref[:, n0 - POOL_WIDTH:n0 - POOL_WIDTH + tn] = z.astype(o_ref.dtype)

    pending = None
    for n0 in range(0, n_cols, tn):
        res = slab(n0)
        if pending is not None:
            pending()
        pending = functools.partial(pool_slab if n0 < POOL_WIDTH else shift_slab, n0, res)
    pending()


def in_proj_mixers(x, g, w_main, w_lora, mu_cols, pool_w, pool_scale, *, tm, tn, seq):
    M, K = x.shape
    n_rwkv = MAIN_COLS - POOL_WIDTH + w_lora.shape[1]
    hb = tm // POOL_HALO
    return pl.pallas_call(
        functools.partial(_in_proj_kernel, tn=tn, tiles_per_seq=seq // tm),
        out_shape=[jax.ShapeDtypeStruct((M, n_rwkv), bf16),
                   jax.ShapeDtypeStruct((M, POOL_WIDTH), bf16)],
        grid=(M // tm,),
        in_specs=[pl.BlockSpec((tm, K), lambda i: (i, 0)),
                  pl.BlockSpec((POOL_HALO, K), lambda i: (jnp.maximum(i * hb - 1, 0), 0)),
                  _resident((1, K)), _resident(w_main.shape), _resident(w_lora.shape),
                  _resident((1, POOL_WIDTH + n_rwkv)), _resident(pool_w.shape),
                  _resident((1, POOL_WIDTH))],
        out_specs=[pl.BlockSpec((tm, n_rwkv), lambda i: (i, 0)),
                   pl.BlockSpec((tm, POOL_WIDTH), lambda i: (i, 0))],
        compiler_params=_cparams(("parallel",)),
        name="in_proj",
    )(x, x, g.reshape(1, K), w_main, w_lora, mu_cols.reshape(1, -1), pool_w,
      pool_scale.reshape(1, POOL_WIDTH))


def _mm_res_resident_kernel(*refs, n, tn):
    a_refs, w_refs, res_ref, o_ref = refs[:n], refs[n:2 * n], refs[2 * n], refs[2 * n + 1]
    for n0 in range(0, o_ref.shape[1], tn):
        acc = res_ref[:, n0:n0 + tn]
        for a_ref, w_ref in zip(a_refs, w_refs):
            acc = acc + jnp.dot(a_ref[...], w_ref[:, n0:n0 + tn], preferred_element_type=f32)
        o_ref[:, n0:n0 + tn] = acc


def matmul_residual_resident(a_list, w_list, res, *, tm, tn, name):
    M, N = res.shape
    n = len(a_list)
    return pl.pallas_call(
        functools.partial(_mm_res_resident_kernel, n=n, tn=tn),
        out_shape=jax.ShapeDtypeStruct((M, N), f32),
        grid=(M // tm,),
        in_specs=([pl.BlockSpec((tm, a.shape[1]), lambda i: (i, 0)) for a in a_list]
                  + [_resident(w.shape) for w in w_list]
                  + [pl.BlockSpec((tm, N), lambda i: (i, 0))]),
        out_specs=pl.BlockSpec((tm, N), lambda i: (i, 0)),
        compiler_params=_cparams(("parallel",)),
        name=name,
    )(*a_list, *w_list, res)


def _head_ones():
    r = lax.broadcasted_iota(jnp.int32, (HG_LANES, HG_LANES), 0) // RWKV_HEAD
    c = lax.broadcasted_iota(jnp.int32, (HG_LANES, HG_LANES), 1) // RWKV_HEAD
    return jnp.where(r == c, 1.0, 0.0).astype(bf16)


def _head_sum(x, ones_bd):
    parts = [_split_dot(x[:, c:c + HG_LANES], ones_bd) for c in range(0, x.shape[1], HG_LANES)]
    return parts[0] if len(parts) == 1 else jnp.concatenate(parts, axis=1)


def _sigmoid(x):
    return 1.0 / (1.0 + jnp.exp(-x))


def _rwkv_prep_kernel(k_ref, lo_ref, w0_ref, w2_ref, a0_ref, a2_ref, g2_ref, kk_ref, ka_ref,
                      lw_out, k_out, kk_out, kka_out, g_out):
    k = k_ref[...].astype(f32)
    lo = lo_ref[...].astype(f32)
    w_lo = lo[:, LORA_W_OFF:LORA_W_OFF + 128]
    a_lo = lo[:, LORA_A_OFF:LORA_A_OFF + 128]
    g_lo = lo[:, LORA_G_OFF:LORA_G_OFF + 256]

    wx = w0_ref[...] + jnp.dot(jnp.tanh(w_lo).astype(bf16), w2_ref[...], preferred_element_type=f32)
    lw_out[...] = -DECAY_SCALE * _sigmoid(wx)
    a = _sigmoid(a0_ref[...] + jnp.dot(a_lo.astype(bf16), a2_ref[...], preferred_element_type=f32))
    g = jnp.dot(_sigmoid(g_lo).astype(bf16), g2_ref[...], preferred_element_type=f32)

    ones_bd = _head_ones()
    kk = k * kk_ref[...]
    kk = kk * lax.rsqrt(jnp.maximum(_head_sum(kk * kk, ones_bd), 1e-24))
    k = k * (1.0 + (a - 1.0) * ka_ref[...])

    k_out[...] = k.astype(k_out.dtype)
    kk_out[...] = kk.astype(kk_out.dtype)
    kka_out[...] = (kk * a).astype(kka_out.dtype)
    g_out[...] = g.astype(g_out.dtype)


def rwkv_prep(proj, w0, w2p, a0, a2p, g2p, k_k, k_a, *, ts):
    T = proj.shape[0]
    C = RWKV_WIDTH
    k_blk = 1
    lora_blk = 3 * C // LORA_PAD

    def full(shape):
        return pl.BlockSpec(shape, lambda i: (0,) * len(shape))

    row = lambda t: t.reshape(1, -1)
    out_sd = lambda dt: jax.ShapeDtypeStruct((T, C), dt)
    out_spec = pl.BlockSpec((ts, C), lambda i: (i, 0))
    return pl.pallas_call(
        _rwkv_prep_kernel,
        out_shape=[out_sd(f32), out_sd(bf16), out_sd(bf16), out_sd(bf16), out_sd(bf16)],
        grid=(T // ts,),
        in_specs=[
            pl.BlockSpec((ts, C), lambda i: (i, k_blk)),
            pl.BlockSpec((ts, LORA_PAD), lambda i: (i, lora_blk)),
            full((1, C)), full((128, C)), full((1, C)), full((128, C)), full((256, C)),
            full((1, C)), full((1, C)),
        ],
        out_specs=[out_spec] * 5,
        compiler_params=_cparams(("parallel",)),
        name="rwkv_prep",
    )(proj, proj, row(w0), w2p, row(a0), a2p, g2p, row(k_k), row(k_a))


def _bd_mask():
    r = lax.broadcasted_iota(jnp.int32, (HG_LANES, HG_LANES), 0) // RWKV_CHUNK
    c = lax.broadcasted_iota(jnp.int32, (HG_LANES, HG_LANES), 1) // RWKV_HEAD
    return r == c


def _rwkv_scan_kernel(r_ref, v_ref, lw_ref, k_ref, kk_ref, kka_ref, g_ref,
                      lnw_ref, lnb_ref, rk_ref, *rest, n_chunks, n_hg, n_cast):
    L = RWKV_CHUNK
    W = HG_LANES
    cast_in, o_ref, cast_out, state_ref = rest[:n_cast], rest[n_cast], rest[n_cast + 1:-1], rest[-1]
    for w_ref, wb_ref in zip(cast_in, cast_out):
        wb_ref[...] = w_ref[...].astype(bf16)

    @pl.when(pl.program_id(2) == 0)
    def _():
        state_ref[...] = jnp.zeros_like(state_ref)

    bd_mask = _bd_mask()
    ones_bd = _head_ones()

    def bd(x):
        tiled = jnp.concatenate([x] * HEADS_PER_STEP, axis=0)
        return jnp.where(bd_mask, tiled, 0.0).astype(bf16)

    def bd_t(x):
        tiled = jnp.concatenate([x] * HEADS_PER_STEP, axis=0)
        return jnp.where(bd_mask, tiled, 0.0).T.astype(bf16)

    def mm(a, b_bf16):
        return jnp.dot(a.astype(bf16), b_bf16, preferred_element_type=f32)

    def mm_nt(a, b_bf16):
        return lax.dot_general(a.astype(bf16), b_bf16, (((1,), (1,)), ((), ())),
                               preferred_element_type=f32)

    def mm_tn(a_bf16, b_bf16):
        return lax.dot_general(a_bf16, b_bf16, (((0,), (0,)), ((), ())),
                               preferred_element_type=f32)

    def split(x):
        hi = x.astype(bf16)
        return hi, (x - hi.astype(f32)).astype(bf16)

    def head_sums(xs):
        tot = mm(jnp.concatenate(xs, axis=0), ones_bd)
        return [tot[j * L:(j + 1) * L] for j in range(len(xs))]

    t_idx = lax.broadcasted_iota(jnp.int32, (L, W), 0)
    s_idx = lax.broadcasted_iota(jnp.int32, (L, W), 1) % L
    strict = t_idx > s_idx
    incl = t_idx >= s_idx
    eye_all = jnp.where(t_idx == s_idx, 1.0, 0.0)
    tri = jnp.where(lax.broadcasted_iota(jnp.int32, (L, L), 0)
                    >= lax.broadcasted_iota(jnp.int32, (L, L), 1), 1.0, 0.0).astype(bf16)
    blockdiag = (lax.broadcasted_iota(jnp.int32, (W, W), 0) // RWKV_HEAD
                 == lax.broadcasted_iota(jnp.int32, (W, W), 1) // RWKV_HEAD)

    def phase1(items):
        I = range(len(items))
        rows = [pl.ds(c * L, L) for _, c in items]
        cols = [pl.ds(h * W, W) for h, _ in items]
        ld = lambda ref, i: ref[rows[i], cols[i]]
        r = [ld(r_ref, i).astype(f32) for i in I]
        k = [ld(k_ref, i).astype(f32) for i in I]
        v = [ld(v_ref, i).astype(f32) for i in I]
        lw = [ld(lw_ref, i) for i in I]
        lw_s = [split(x) for x in lw]
        cs = [jnp.dot(tri, hi, preferred_element_type=f32) + jnp.dot(tri, lo, preferred_element_type=f32)
              for hi, lo in lw_s]
        yield
        g_inv = [jnp.exp(-x) for x in cs]
        to_end = [jnp.exp(x[L - 1:L, :] - x) for x in cs]
        a_t = [-ld(kk_ref, i).astype(f32) * jnp.exp(cs[i] - lw[i]) for i in I]
        r_t = [r[i] * jnp.exp(cs[i]) for i in I]
        kka = [ld(kka_ref, i).astype(f32) for i in I]
        b_t = [kka[i] * g_inv[i] for i in I]
        k_t = [k[i] * g_inv[i] for i in I]
        b_e = [kka[i] * to_end[i] for i in I]
        k_e = [k[i] * to_end[i] for i in I]
        v_bd = [bd(x) for x in v]
        ar = [jnp.concatenate([a_t[i], r_t[i]], axis=0) for i in I]
        m_b = [mm(ar[i], bd_t(b_t[i])) for i in I]
        m_k = [mm(ar[i], bd_t(k_t[i])) for i in I]
        m_ab = [jnp.where(strict, x[:L], 0.0) for x in m_b]
        m_rb = [jnp.where(incl, x[L:], 0.0) for x in m_b]
        m_ak = [jnp.where(strict, x[:L], 0.0) for x in m_k]
        m_rk = [jnp.where(incl, x[L:], 0.0) for x in m_k]
        rk_sum = head_sums([r[i] * k[i] * rk_ref[:, cols[i]] for i in I])
        bonus = [rk_sum[i] * v[i] for i in I]
        yield
        p = m_ab
        tinv = [eye_all + x for x in p]
        m = 2
        while m < L:
            p = [mm(x, bd(x)) for x in p]
            tinv = [tinv[i] + mm(tinv[i], bd(p[i])) for i in I]
            m *= 2
            yield
        x1 = [mm(m_ak[i], v_bd[i]) for i in I]
        u_v = [mm(tinv[i], bd(x1[i])) for i in I]
        a_h = [mm(tinv[i], bd(a_t[i])) for i in I]
        yield
        r_h = [r_t[i] + mm(m_rb[i], bd(a_h[i])) for i in I]
        y_v = [mm(m_rb[i], bd(u_v[i])) + mm(m_rk[i], v_bd[i]) for i in I]
        yield
        p_t = [jnp.where(blockdiag, mm_tn(a_h[i].astype(bf16), b_e[i].astype(bf16)), 0.0).astype(bf16)
               for i in I]
        g_t = [jnp.where(blockdiag,
                         mm_tn(jnp.concatenate([u_v[i], v[i]], axis=0).astype(bf16),
                               jnp.concatenate([b_e[i], k_e[i]], axis=0).astype(bf16)), 0.0) for i in I]
        decay = [jnp.exp(x[L - 1:L, :]) for x in cs]
        for i in I:
            done.append(dict(hg=items[i][0], rows=rows[i], cols=cols[i], r_h=r_h[i], y_v=y_v[i],
                             p_t=p_t[i], g_t=g_t[i], decay=decay[i], bonus=bonus[i]))
        yield

    def chain_step(chs):
        ys = []
        for ch in chs:
            state = states[ch['hg']]
            state_b = state.astype(bf16)
            ys.append(mm_nt(ch['r_h'], state_b) + ch['y_v'])
            states[ch['hg']] = (state * ch['decay']
                                + jnp.dot(state_b, ch['p_t'], preferred_element_type=f32) + ch['g_t'])
        means = head_sums(ys)
        ycs = [y - m * (1.0 / RWKV_HEAD) for y, m in zip(ys, means)]
        vrs = head_sums([yc * yc for yc in ycs])
        for ch, yc, vr in zip(chs, ycs, vrs):
            rw, cl = ch['rows'], ch['cols']
            yn = yc * lax.rsqrt(vr * (1.0 / RWKV_HEAD) + GN_EPS) * lnw_ref[:, cl] + lnb_ref[:, cl]
            o_ref[rw, cl] = ((yn + ch['bonus']) * g_ref[rw, cl].astype(f32)).astype(o_ref.dtype)

    states = [state_ref[h] for h in range(n_hg)]
    done = []
    half = max(n_chunks // 2, 1)
    groups = [[(h, c) for c in range(0, half) for h in range(n_hg)],
              [(h, c) for c in range(half, n_chunks) for h in range(n_hg)]]
    def by_chunk(n_items):
        return [done[j:j + n_hg] for j in range(0, n_items, n_hg)]

    for _ in phase1(groups[0]):
        pass
    pending = by_chunk(len(done))
    for _ in phase1(groups[1]):
        if pending:
            chain_step(pending.pop(0))
    for chs in pending + by_chunk(len(done))[len(groups[0]) // n_hg:]:
        chain_step(chs)
    for h in range(n_hg):
        state_ref[h] = states[h]


def rwkv_scan(proj, lw, k, kk, kka, g, ln_w, ln_b, r_k, cast_weights, *, batch, seq, rows, n_hg):
    nb = seq // rows
    T = batch * seq
    width = n_hg * HG_LANES
    n_col = RWKV_WIDTH // width
    n_steps = batch * n_col * nb
    r_blk = 0
    v_blk = 2 * RWKV_WIDTH // width
    blk = pl.BlockSpec((rows, width), lambda b, h, c: (b * nb + c, h))
    par = pl.BlockSpec((1, width), lambda b, h, c: (0, h))
    slabs = [w.reshape(n_steps, w.size // (n_steps * w.shape[-1]), w.shape[-1]) for w in cast_weights]
    slab_spec = lambda w: pl.BlockSpec((1,) + w.shape[1:], lambda b, h, c: ((b * n_col + h) * nb + c, 0, 0))
    outs = pl.pallas_call(
        functools.partial(_rwkv_scan_kernel, n_chunks=rows // RWKV_CHUNK, n_hg=n_hg,
                          n_cast=len(slabs)),
        out_shape=[jax.ShapeDtypeStruct((T, RWKV_WIDTH), bf16)]
                  + [jax.ShapeDtypeStruct(w.shape, bf16) for w in slabs],
        grid=(batch, n_col, nb),
        in_specs=[pl.BlockSpec((rows, width), lambda b, h, c: (b * nb + c, r_blk + h)),
                  pl.BlockSpec((rows, width), lambda b, h, c: (b * nb + c, v_blk + h))]
                 + [blk] * 5 + [par] * 3 + [slab_spec(w) for w in slabs],
        out_specs=[blk] + [slab_spec(w) for w in slabs],
        scratch_shapes=[pltpu.VMEM((n_hg, HG_LANES, HG_LANES), f32)],
        compiler_params=_cparams(("parallel", "parallel", "arbitrary")),
        name="rwkv_scan",
    )(proj, proj, lw, k, kk, kka, g, ln_w.reshape(1, -1), ln_b.reshape(1, -1), r_k.reshape(1, -1),
      *slabs)
    return outs[0], [o.reshape(w.shape) for o, w in zip(outs[1:], cast_weights)]


def _route(hn, w, b):
    w_hi = w.astype(bf16)
    w_lo = (w - w_hi.astype(f32)).astype(bf16)
    h_hi = hn.astype(bf16)
    h_lo = (hn - h_hi.astype(f32)).astype(bf16)
    hh = jnp.dot(h_hi, jnp.concatenate([w_hi, w_lo], axis=1), preferred_element_type=f32)
    logits = (hh[:, :ROUTER_LANES] + hh[:, ROUTER_LANES:]
              + jnp.dot(h_lo, w_hi, preferred_element_type=f32)) + b

    lane_i = lax.broadcasted_iota(jnp.int32, logits.shape, 1)
    lane = lane_i.astype(f32)
    neg = -jnp.inf
    big = float(ROUTER_LANES)
    is_g = lane_i < MOE_GROUPS
    gl = jnp.where(is_g, logits, neg)
    gmax = jnp.max(gl, axis=-1, keepdims=True)
    g_idx = jnp.min(jnp.where(gl == gmax, lane, big), axis=-1, keepdims=True)
    g_w = 1.0 / jnp.sum(jnp.exp(gl - gmax), axis=-1, keepdims=True)

    lo_lane = MOE_GROUPS + g_idx * MOE_EXPERTS_PER_GROUP
    sel = (lane >= lo_lane) & (lane < lo_lane + MOE_EXPERTS_PER_GROUP)
    el = jnp.where(sel, logits, neg)
    v1 = jnp.max(el, axis=-1, keepdims=True)
    i1 = jnp.min(jnp.where(el == v1, lane, big), axis=-1, keepdims=True)
    el2 = jnp.where(lane == i1, neg, el)
    v2 = jnp.max(el2, axis=-1, keepdims=True)
    i2 = jnp.min(jnp.where(el2 == v2, lane, big), axis=-1, keepdims=True)
    e2 = jnp.exp(v2 - v1)
    p1 = 1.0 / (1.0 + e2)
    p2 = e2 / (1.0 + e2)
    return (jnp.where(lane_i == 0, i1 - MOE_GROUPS, 0.0)
            + jnp.where(lane_i == 1, i2 - MOE_GROUPS, 0.0)
            + jnp.where(lane_i == 2, p1 * g_w, 0.0)
            + jnp.where(lane_i == 3, p2 * g_w, 0.0))


def _attn_oproj_router_kernel(q_ref, kv_ref, wo_ref, res_ref, g_ref, wr_ref, br_ref, h_ref, xr_ref,
                              rec_ref, *, tn):
    hd = XATTN_HEAD_DIM
    H = range(XATTN_HEADS)
    s = [lax.dot_general(q_ref[:, h * hd:(h + 1) * hd], kv_ref[:, h * hd:(h + 1) * hd],
                         (((1,), (1,)), ((), ())), preferred_element_type=f32) * (hd ** -0.5) for h in H]
    e = [jnp.exp(x - jnp.max(x, axis=-1, keepdims=True)) for x in s]
    l = [jnp.sum(x, axis=-1, keepdims=True) for x in e]
    pv = [jnp.dot(e[h].astype(bf16), kv_ref[:, (XATTN_HEADS + h) * hd:(XATTN_HEADS + h + 1) * hd],
                  preferred_element_type=f32) for h in H]
    o = jnp.concatenate([(pv[h] / l[h]).astype(bf16) for h in H], axis=1)
    for n0 in range(0, h_ref.shape[1], tn):
        h_ref[:, n0:n0 + tn] = res_ref[:, n0:n0 + tn] + jnp.dot(
            o, wo_ref[:, n0:n0 + tn], preferred_element_type=f32)
    hn = _rmsnorm_rows(h_ref[...], g_ref[...])
    D = hn.shape[1]
    rec = _route(hn, wr_ref[...], br_ref[...])
    xr_ref[:, :D] = hn
    xr_ref[:, D:] = rec
    rec_ref[...] = rec


def attn_oproj_router(q, kv, w_o, res, g, w_router, b_router, *, tm, tn, seq):
    T, D = res.shape
    tiles_per_seq = seq // tm
    row = lambda width: pl.BlockSpec((tm, width), lambda i: (i, 0))
    return pl.pallas_call(
        functools.partial(_attn_oproj_router_kernel, tn=tn),
        out_shape=[jax.ShapeDtypeStruct((T, D), f32),
                   jax.ShapeDtypeStruct((T, D + ROUTER_LANES), f32),
                   jax.ShapeDtypeStruct((T, ROUTER_LANES), f32)],
        grid=(T // tm,),
        in_specs=[row(D), pl.BlockSpec((MEM_LEN, 2 * D), lambda i: (i // tiles_per_seq, 0)),
                  _resident((D, D)), row(D), _resident((1, D)),
                  _resident((D, ROUTER_LANES)), _resident((1, ROUTER_LANES))],
        out_specs=[row(D), row(D + ROUTER_LANES), row(ROUTER_LANES)],
        compiler_params=_cparams(("parallel",)),
        name="attn_oproj_router",
    )(q, kv, w_o, res, g.reshape(1, D), w_router, b_router)


MOE_PAIRS = ((0, 1), (0, 2), (0, 3), (1, 3), (1, 2), (3, 2))
MOE_CLASSES = MOE_GROUPS * len(MOE_PAIRS)


def _route_metadata(rec, *, tm, n_tiles):
    ids = rec[:, 0:2].astype(jnp.int32)
    lo = jnp.minimum(ids[:, 0], ids[:, 1])
    hi = jnp.maximum(ids[:, 0], ids[:, 1])
    a, b = lo % MOE_EXPERTS_PER_GROUP, hi % MOE_EXPERTS_PER_GROUP
    pair_id = (a * (7 - a)) // 2 + b - a - 1
    pair_id = pair_id + (pair_id == 3).astype(jnp.int32) - (pair_id == 4).astype(jnp.int32)
    cls = (lo // MOE_EXPERTS_PER_GROUP) * len(MOE_PAIRS) + pair_id
    onehot = (cls[:, None] == jnp.arange(MOE_CLASSES, dtype=jnp.int32)[None, :]).astype(jnp.int32)
    csum = jnp.cumsum(onehot, axis=0)
    counts = csum[-1]
    rank = jnp.sum(csum * onehot, axis=1) - 1
    padded = ((counts + tm - 1) // tm) * tm
    ends = jnp.cumsum(padded)
    pos = jnp.sum(onehot * (ends - padded)[None, :], axis=1) + rank
    tile_start = jnp.arange(n_tiles, dtype=jnp.int32) * tm
    n_valid = ends[-1] // tm
    tile_cls = jnp.sum((tile_start[:, None] >= ends[None, :]).astype(jnp.int32), axis=1)
    last_cls = jnp.sum((ends[-1] - 1 >= ends).astype(jnp.int32))
    tile_cls = jnp.where(tile_start < ends[-1], tile_cls, last_cls)
    pair = jnp.asarray(MOE_PAIRS, jnp.int32)[tile_cls % len(MOE_PAIRS)]
    base = (tile_cls // len(MOE_PAIRS)) * MOE_EXPERTS_PER_GROUP
    idle = n_valid + jnp.arange(MOE_CLASSES, dtype=jnp.int32)
    fill_start = jnp.concatenate([jnp.maximum(ends - tm, 0), jnp.minimum(idle, n_tiles - 1) * tm])
    fill_on = jnp.concatenate([counts > 0, idle < n_tiles]).astype(jnp.int32)
    return pos, base + pair[:, 0], base + pair[:, 1], n_valid.reshape(1), fill_start, fill_on


def _moe_dispatch_kernel(pos_ref, fs_ref, fo_ref, x_ref, xs_hbm, stage, zbuf, sem, zsem, *, tm, tz):
    i = pl.program_id(0)
    n = pl.num_programs(0)

    def fill_copy(c):
        return pltpu.make_async_copy(zbuf, xs_hbm.at[pl.ds(pl.multiple_of(fs_ref[c], tz), tz), :], zsem)

    @pl.when(i == 0)
    def _():
        zbuf[...] = jnp.zeros_like(zbuf)
        for c in range(2 * MOE_CLASSES):
            @pl.when(fo_ref[c] > 0)
            def _():
                fill_copy(c).start()
        for c in range(2 * MOE_CLASSES):
            @pl.when(fo_ref[c] > 0)
            def _():
                fill_copy(c).wait()

    def tile_wait(slot):
        pltpu.make_async_copy(stage.at[slot], xs_hbm.at[pl.ds(0, tm), :], sem.at[slot]).wait()

    slot = i % 2

    @pl.when(i >= 2)
    def _():
        tile_wait(slot)

    stage[slot] = x_ref[...]

    def body(r, carry):
        p = pos_ref[i * tm + r]
        pltpu.make_async_copy(stage.at[slot, pl.ds(r, 1), :], xs_hbm.at[pl.ds(p, 1), :],
                              sem.at[slot]).start()
        return carry
    lax.fori_loop(0, tm, body, 0, unroll=8)

    @pl.when(i == n - 1)
    def _():
        tile_wait(slot)

        @pl.when(n >= 2)
        def _():
            tile_wait(1 - slot)


def moe_dispatch(xr, pos, fill_start, fill_on, *, tm, tz, n_slots):
    T, width = xr.shape
    grid_spec = pltpu.PrefetchScalarGridSpec(
        num_scalar_prefetch=3,
        grid=(T // tm,),
        in_specs=[pl.BlockSpec((tm, width), lambda i, pos, fs, fo: (i, 0))],
        out_specs=pl.BlockSpec(memory_space=pl.ANY),
        scratch_shapes=[pltpu.VMEM((2, tm, width), f32), pltpu.VMEM((tz, width), f32),
                        pltpu.SemaphoreType.DMA((2,)), pltpu.SemaphoreType.DMA(())],
    )
    return pl.pallas_call(
        functools.partial(_moe_dispatch_kernel, tm=tm, tz=tz),
        out_shape=jax.ShapeDtypeStruct((n_slots, width), f32),
        grid_spec=grid_spec,
        compiler_params=_cparams(("arbitrary",)),
        name="moe_dispatch",
    )(pos, fill_start, fill_on, xr)


def _moe_pair_kernel(ea_ref, eb_ref, nv_ref, xs_ref, wga, wua, wda, wgb, wub, wdb, o_ref):
    @pl.when(pl.program_id(0) >= nv_ref[0])
    def _():
        o_ref[...] = jnp.zeros_like(o_ref)

    @pl.when(pl.program_id(0) < nv_ref[0])
    def _():
        D = o_ref.shape[1]
        x = xs_ref[:, :D].astype(bf16)
        rec = xs_ref[:, D:]
        first = rec[:, 0:1] == ea_ref[pl.program_id(0)].astype(f32)
        w_a = jnp.where(first, rec[:, 2:3], rec[:, 3:4])
        w_b = jnp.where(first, rec[:, 3:4], rec[:, 2:3])

        gates = [jnp.dot(x, wg[0], preferred_element_type=f32) for wg in (wga, wgb)]
        ups = [jnp.dot(x, wu[0], preferred_element_type=f32) for wu in (wua, wub)]
        hids = [(g * _sigmoid(g) * u).astype(bf16) for g, u in zip(gates, ups)]
        y_a, y_b = [jnp.dot(h, wd[0], preferred_element_type=f32) for h, wd in zip(hids, (wda, wdb))]
        o_ref[...] = w_a * y_a + w_b * y_b


def moe_pair_experts(xs, e_a, e_b, n_valid, wg, wu, wd, *, tm, n_tiles):
    width = xs.shape[1]
    D = wg.shape[1]
    held = lambda i, ea, eb, nv: (jnp.minimum(i, nv[0] - 1), 0)
    tile = lambda i, ea, eb, nv: (i, 0)
    wa = lambda shape: pl.BlockSpec(shape, lambda i, ea, eb, nv: (ea[i], 0, 0))
    wb = lambda shape: pl.BlockSpec(shape, lambda i, ea, eb, nv: (eb[i], 0, 0))
    grid_spec = pltpu.PrefetchScalarGridSpec(
        num_scalar_prefetch=3,
        grid=(n_tiles,),
        in_specs=[pl.BlockSpec((tm, width), held),
                  wa((1, D, MOE_FF)), wa((1, D, MOE_FF)), wa((1, MOE_FF, D)),
                  wb((1, D, MOE_FF)), wb((1, D, MOE_FF)), wb((1, MOE_FF, D))],
        out_specs=pl.BlockSpec((tm, D), tile),
    )
    return pl.pallas_call(
        _moe_pair_kernel,
        out_shape=jax.ShapeDtypeStruct((n_tiles * tm, D), f32),
        grid_spec=grid_spec,
        compiler_params=_cparams(("arbitrary",)),
        name="moe_experts",
    )(e_a, e_b, n_valid, xs, wg, wu, wd, wg, wu, wd)


def _moe_combine_kernel(pos_ref, h_ref, y_hbm, gf_ref, o_ref, ybuf, sem, *, tm):
    i = pl.program_id(0)
    n = pl.num_programs(0)

    def gather(tile, slot):
        def body(r, carry):
            p = pos_ref[tile * tm + r]
            pltpu.make_async_copy(y_hbm.at[pl.ds(p, 1), :],
                                  ybuf.at[slot, pl.ds(r, 1), :], sem.at[slot]).start()
            return carry
        lax.fori_loop(0, tm, body, 0, unroll=8)

    @pl.when(i == 0)
    def _():
        gather(0, 0)

    @pl.when(i + 1 < n)
    def _():
        gather(i + 1, (i + 1) % 2)

    slot = i % 2
    pltpu.make_async_copy(y_hbm.at[pl.ds(0, tm), :], ybuf.at[slot], sem.at[slot]).wait()
    o_ref[...] = _rmsnorm_rows(h_ref[...] + ybuf[slot], gf_ref[...])


def moe_combine(h, y_sorted, pos, g_final, *, tm):
    T, D = h.shape
    grid_spec = pltpu.PrefetchScalarGridSpec(
        num_scalar_prefetch=1,
        grid=(T // tm,),
        in_specs=[
            pl.BlockSpec((tm, D), lambda i, pos: (i, 0)),
            pl.BlockSpec(memory_space=pl.ANY),
            pl.BlockSpec((1, D), lambda i, pos: (0, 0)),
        ],
        out_specs=pl.BlockSpec((tm, D), lambda i, pos: (i, 0)),
        scratch_shapes=[pltpu.VMEM((2, tm, D), f32), pltpu.SemaphoreType.DMA((2,))],
    )
    return pl.pallas_call(
        functools.partial(_moe_combine_kernel, tm=tm),
        out_shape=jax.ShapeDtypeStruct((T, D), f32),
        grid_spec=grid_spec,
        compiler_params=_cparams(("arbitrary",)),
        name="moe_combine",
    )(pos, h, y_sorted, g_final.reshape(1, D))


def _pad_rows(w, rows):
    return jnp.pad(w, ((0, rows - w.shape[0]), (0, 0)))


def _pack_lora_cols(t):
    o1, o2 = DECAY_LORA, DECAY_LORA + AAA_LORA
    pad = lambda a, n: jnp.pad(a, [(0, 0)] * (a.ndim - 1) + [(0, n - a.shape[-1])])
    return jnp.concatenate([pad(t[..., :o1], 128), pad(t[..., o1:o2], 128),
                            pad(t[..., o2:], 256)], axis=-1)


def _layer(x2, mem2, p, *, batch, seq, cfg):
    D = D_MODEL
    w_in = p['w_in']
    w_main = w_in.astype(bf16)
    w_lora = _pack_lora_cols(w_in[:, MAIN_COLS:]).astype(bf16)
    mu_cols = jnp.concatenate([jnp.zeros((POOL_WIDTH,), f32), p['rwkv_mu'][:3 * RWKV_WIDTH],
                               _pack_lora_cols(p['rwkv_mu'][3 * RWKV_WIDTH:])])

    proj, pool_out = in_proj_mixers(x2, p['norm_mix_g'], w_main, w_lora, mu_cols,
                                    p['pool_w'].astype(bf16), p['pool_scale'],
                                    tm=cfg['tm'], tn=cfg['tn_in'], seq=seq)
    lw, k, kk, kka, g = rwkv_prep(
        proj, p['rwkv_w0'], _pad_rows(p['rwkv_w2'], 128).astype(bf16),
        p['rwkv_a0'], _pad_rows(p['rwkv_a2'], 128).astype(bf16),
        _pad_rows(p['rwkv_g2'], 256).astype(bf16), p['rwkv_k_k'], p['rwkv_k_a'], ts=cfg['ts_prep'])
    rwkv_out, (w_out, w_q, w_kv, w_o, wg, wu, wd) = rwkv_scan(
        proj, lw, k, kk, kka, g, p['rwkv_ln_w'], p['rwkv_ln_b'], p['rwkv_r_k'],
        [p['w_out'], p['xattn_w_q'], p['xattn_w_kv'], p['xattn_w_o'],
         p['moe_w_gate'], p['moe_w_up'], p['moe_w_down']],
        batch=batch, seq=seq, rows=cfg['scan_rows'], n_hg=cfg['scan_hg'])
    h1 = matmul_residual_resident([pool_out, rwkv_out], [w_out[:POOL_WIDTH], w_out[POOL_WIDTH:]],
                                  x2, tm=cfg['tm'], tn=cfg['tn'], name="out_proj")

    q = norm_matmul_resident(h1, p['norm_xattn_g'], w_q,
                             tm=cfg['tm_q'], tn=cfg['tn'], name="q_proj")
    kv = norm_matmul_resident(mem2, p['norm_mem_g'], w_kv,
                              tm=min(cfg['tm'], mem2.shape[0]), tn=cfg['tn'], name="kv_proj")

    w_router = jnp.pad(jnp.concatenate([p['moe_w_group'], p['moe_w_expert']], axis=1),
                       ((0, 0), (0, ROUTER_LANES - MOE_GROUPS - MOE_EXPERTS)))
    b_router = jnp.pad(jnp.concatenate([p['moe_b_group'], p['moe_b_expert']]),
                       (0, ROUTER_LANES - MOE_GROUPS - MOE_EXPERTS)).reshape(1, ROUTER_LANES)
    h2, xr, rec = attn_oproj_router(q, kv, w_o, h1, p['norm_ffn_g'],
                               w_router, b_router, tm=cfg['tm'], tn=cfg['tn'], seq=seq)
    tm_e = cfg['tm_moe']
    n_tiles = xr.shape[0] // tm_e + MOE_CLASSES
    pos, e_a, e_b, n_valid, fill_start, fill_on = _route_metadata(
        rec, tm=tm_e, n_tiles=n_tiles)
    xs = moe_dispatch(xr, pos, fill_start, fill_on, tm=cfg['tm_disp'], tz=tm_e,
                      n_slots=n_tiles * tm_e)
    y_sorted = moe_pair_experts(xs, e_a, e_b, n_valid, wg, wu, wd, tm=tm_e, n_tiles=n_tiles)
    return moe_combine(h2, y_sorted, pos, p['norm_final_g'], tm=cfg['tm_comb'])


_CFG = dict(tm=512, tm_q=1024, tn=1024, tn_in=512, ts_prep=512, scan_rows=256, scan_hg=4,
            tm_moe=256, tm_disp=512, tm_comb=512)


def kernel(x, mem, norm_mix_g, w_in, pool_w, pool_scale, rwkv_mu, rwkv_w0, rwkv_w2, rwkv_a0,
           rwkv_a2, rwkv_g2, rwkv_k_k, rwkv_k_a, rwkv_r_k, rwkv_ln_w, rwkv_ln_b, w_out,
           norm_xattn_g, norm_mem_g, xattn_w_q, xattn_w_kv, xattn_w_o, norm_ffn_g,
           moe_w_group, moe_b_group, moe_w_expert, moe_b_expert, moe_w_gate, moe_w_up,
           moe_w_down, norm_final_g):
    batch, seq, D = x.shape
    p = dict(norm_mix_g=norm_mix_g[0], w_in=w_in[0], pool_w=pool_w[0], pool_scale=pool_scale[0],
             rwkv_mu=rwkv_mu[0], rwkv_w0=rwkv_w0[0], rwkv_w2=rwkv_w2[0], rwkv_a0=rwkv_a0[0],
             rwkv_a2=rwkv_a2[0], rwkv_g2=rwkv_g2[0], rwkv_k_k=rwkv_k_k[0], rwkv_k_a=rwkv_k_a[0],
             rwkv_r_k=rwkv_r_k[0], rwkv_ln_w=rwkv_ln_w[0], rwkv_ln_b=rwkv_ln_b[0], w_out=w_out[0],
             norm_xattn_g=norm_xattn_g[0], norm_mem_g=norm_mem_g[0], xattn_w_q=xattn_w_q[0],
             xattn_w_kv=xattn_w_kv[0], xattn_w_o=xattn_w_o[0], norm_ffn_g=norm_ffn_g[0],
             moe_w_group=moe_w_group[0], moe_b_group=moe_b_group[0], moe_w_expert=moe_w_expert[0],
             moe_b_expert=moe_b_expert[0], moe_w_gate=moe_w_gate[0], moe_w_up=moe_w_up[0],
             moe_w_down=moe_w_down[0], norm_final_g=norm_final_g)
    out = _layer(x.reshape(batch * seq, D), mem.reshape(batch * MEM_LEN, D), p,
                 batch=batch, seq=seq, cfg=_CFG)
    return out.reshape(batch, seq, D)
```

```python
import functools

import jax
import jax.numpy as jnp
from jax import lax
from jax.experimental import pallas as pl
from jax.experimental.pallas import tpu as pltpu

f32 = jnp.float32
bf16 = jnp.bfloat16

D_MODEL = 2048
MEM_LEN = 256
NORM_EPS = 1e-6

POOL_WIDTH = 1024
POOL_WINDOWS = (2, 4, 8, 16)
POOL_GROUP = 256
POOL_HALO = 16
RWKV_WIDTH = 1024
RWKV_HEAD = 64
RWKV_HEADS = 16
GN_EPS = 64e-5
DECAY_SCALE = 0.6065306597126334
DECAY_LORA = 64
AAA_LORA = 64
GATE_LORA = 160
LORA_PAD = 512
LORA_W_OFF, LORA_A_OFF, LORA_G_OFF = 0, 128, 256
MAIN_COLS = POOL_WIDTH + 3 * RWKV_WIDTH
PROJ_COLS = MAIN_COLS + LORA_PAD

RWKV_CHUNK = 64
HEADS_PER_STEP = 4
HG_LANES = HEADS_PER_STEP * RWKV_HEAD

XATTN_HEADS = 4
XATTN_HEAD_DIM = 512

MOE_GROUPS = 4
MOE_EXPERTS_PER_GROUP = 4
MOE_EXPERTS = 16
MOE_FF = 512
ROUTER_LANES = 128

VMEM_LIMIT = 56 * 1024 * 1024


def _cparams(sem):
    return pltpu.CompilerParams(dimension_semantics=sem, vmem_limit_bytes=VMEM_LIMIT)


def _split_dot(x, w_bf16):
    hi = x.astype(bf16)
    lo = (x - hi.astype(f32)).astype(bf16)
    return (jnp.dot(hi, w_bf16, preferred_element_type=f32)
            + jnp.dot(lo, w_bf16, preferred_element_type=f32))


def _resident(shape):
    return pl.BlockSpec(shape, lambda i: (0,) * len(shape), pipeline_mode=pl.Buffered(1))


def _rmsnorm_rows(x, g):
    ms = jnp.mean(x * x, axis=-1, keepdims=True)
    return x * lax.rsqrt(ms + NORM_EPS) * g


def _norm_mm_res_kernel(x_ref, g_ref, w_ref, o_ref, *, tn):
    xn = _rmsnorm_rows(x_ref[...], g_ref[...]).astype(bf16)
    for n0 in range(0, o_ref.shape[1], tn):
        o_ref[:, n0:n0 + tn] = jnp.dot(xn, w_ref[:, n0:n0 + tn],
                                       preferred_element_type=f32).astype(o_ref.dtype)


def norm_matmul_resident(x, g, w, *, tm, tn, name):
    M, K = x.shape
    N = w.shape[1]
    return pl.pallas_call(
        functools.partial(_norm_mm_res_kernel, tn=tn),
        out_shape=jax.ShapeDtypeStruct((M, N), bf16),
        grid=(M // tm,),
        in_specs=[pl.BlockSpec((tm, K), lambda i: (i, 0)), _resident((1, K)), _resident((K, N))],
        out_specs=pl.BlockSpec((tm, N), lambda i: (i, 0)),
        compiler_params=_cparams(("parallel",)),
        name=name,
    )(x, g.reshape(1, K), w)


def _in_proj_kernel(x_ref, xh_ref, g_ref, w_ref, wl_ref, mu_ref, pw_ref, ps_ref, o_ref, pool_ref,
                    *, tn, tiles_per_seq):
    tm = x_ref.shape[0]
    tile_in_seq = pl.program_id(0) % tiles_per_seq
    first = tile_in_seq == 0
    xn = _rmsnorm_rows(x_ref[...], g_ref[...]).astype(bf16)
    xh = jnp.where(first, 0.0, _rmsnorm_rows(xh_ref[...], g_ref[...])).astype(bf16)
    lhs = jnp.concatenate([xh, xn], axis=0)
    pos = tile_in_seq * tm + lax.broadcasted_iota(jnp.int32, (tm, 1), 0)
    n_cols = POOL_WIDTH + o_ref.shape[1]
    def slab(n0):
        w = w_ref[:, n0:n0 + tn] if n0 < MAIN_COLS else wl_ref[:, n0 - MAIN_COLS:n0 - MAIN_COLS + tn]
        return jnp.dot(lhs, w, preferred_element_type=f32)

    def pool_slab(n0, res):
        p = res[POOL_HALO:]
        acc = res
        shift = 1
        for gi in range(n0 // POOL_GROUP, (n0 + tn) // POOL_GROUP):
            win = POOL_WINDOWS[gi]
            while shift < win:
                acc = acc + pltpu.roll(acc, shift, 0)
                shift *= 2
            c0 = gi * POOL_GROUP - n0
            cnt = jnp.minimum(pos + 1, win).astype(f32)
            pooled = acc[POOL_HALO:, c0:c0 + POOL_GROUP] / cnt - p[:, c0:c0 + POOL_GROUP]
            mixed = jnp.dot(pooled.astype(bf16), pw_ref[gi], preferred_element_type=f32)
            lo, hi = gi * POOL_GROUP, (gi + 1) * POOL_GROUP
            pool_ref[:, lo:hi] = (mixed * ps_ref[:, lo:hi]).astype(pool_ref.dtype)

    def shift_slab(n0, res):
        p = res[POOL_HALO:]
        prev = pltpu.roll(res, 1, 0)[POOL_HALO:]
        z = p + (prev - p) * mu_ref[:, n0:n0 + tn]
        o_ref[:, n0 - POOL_WIDTH:n0 - POOL_WIDTH + tn] = z.astype(o_ref.dtype)

    pending = None
    for n0 in range(0, n_cols, tn):
        res = slab(n0)
        if pending is not None:
            pending()
        pending = functools.partial(pool_slab if n0 < POOL_WIDTH else shift_slab, n0, res)
    pending()


def in_proj_mixers(x, g, w_main, w_lora, mu_cols, pool_w, pool_scale, *, tm, tn, seq):
    M, K = x.shape
    n_rwkv = MAIN_COLS - POOL_WIDTH + w_lora.shape[1]
    hb = tm // POOL_HALO
    return pl.pallas_call(
        functools.partial(_in_proj_kernel, tn=tn, tiles_per_seq=seq // tm),
        out_shape=[jax.ShapeDtypeStruct((M, n_rwkv), bf16),
                   jax.ShapeDtypeStruct((M, POOL_WIDTH), bf16)],
        grid=(M // tm,),
        in_specs=[pl.BlockSpec((tm, K), lambda i: (i, 0)),
                  pl.BlockSpec((POOL_HALO, K), lambda i: (jnp.maximum(i * hb - 1, 0), 0)),
                  _resident((1, K)), _resident(w_main.shape), _resident(w_lora.shape),
                  _resident((1, POOL_WIDTH + n_rwkv)), _resident(pool_w.shape),
                  _resident((1, POOL_WIDTH))],
        out_specs=[pl.BlockSpec((tm, n_rwkv), lambda i: (i, 0)),
                   pl.BlockSpec((tm, POOL_WIDTH), lambda i: (i, 0))],
        compiler_params=_cparams(("parallel",)),
        name="in_proj",
    )(x, x, g.reshape(1, K), w_main, w_lora, mu_cols.reshape(1, -1), pool_w,
      pool_scale.reshape(1, POOL_WIDTH))


def _mm_res_resident_kernel(*refs, n, tn):
    a_refs, w_refs, res_ref, o_ref = refs[:n], refs[n:2 * n], refs[2 * n], refs[2 * n + 1]
    for n0 in range(0, o_ref.shape[1], tn):
        acc = res_ref[:, n0:n0 + tn]
        for a_ref, w_ref in zip(a_refs, w_refs):
            acc = acc + jnp.dot(a_ref[...], w_ref[:, n0:n0 + tn], preferred_element_type=f32)
        o_ref[:, n0:n0 + tn] = acc


def matmul_residual_resident(a_list, w_list, res, *, tm, tn, name):
    M, N = res.shape
    n = len(a_list)
    return pl.pallas_call(
        functools.partial(_mm_res_resident_kernel, n=n, tn=tn),
        out_shape=jax.ShapeDtypeStruct((M, N), f32),
        grid=(M // tm,),
        in_specs=([pl.BlockSpec((tm, a.shape[1]), lambda i: (i, 0)) for a in a_list]
                  + [_resident(w.shape) for w in w_list]
                  + [pl.BlockSpec((tm, N), lambda i: (i, 0))]),
        out_specs=pl.BlockSpec((tm, N), lambda i: (i, 0)),
        compiler_params=_cparams(("parallel",)),
        name=name,
    )(*a_list, *w_list, res)


def _head_ones():
    r = lax.broadcasted_iota(jnp.int32, (HG_LANES, HG_LANES), 0) // RWKV_HEAD
    c = lax.broadcasted_iota(jnp.int32, (HG_LANES, HG_LANES), 1) // RWKV_HEAD
    return jnp.where(r == c, 1.0, 0.0).astype(bf16)


def _head_sum(x, ones_bd):
    parts = [_split_dot(x[:, c:c + HG_LANES], ones_bd) for c in range(0, x.shape[1], HG_LANES)]
    return parts[0] if len(parts) == 1 else jnp.concatenate(parts, axis=1)


def _sigmoid(x):
    return 1.0 / (1.0 + jnp.exp(-x))


def _rwkv_prep_kernel(k_ref, lo_ref, w0_ref, w2_ref, a0_ref, a2_ref, g2_ref, kk_ref, ka_ref,
                      lw_out, k_out, kk_out, kka_out, g_out):
    k = k_ref[...].astype(f32)
    lo = lo_ref[...].astype(f32)
    w_lo = lo[:, LORA_W_OFF:LORA_W_OFF + 128]
    a_lo = lo[:, LORA_A_OFF:LORA_A_OFF + 128]
    g_lo = lo[:, LORA_G_OFF:LORA_G_OFF + 256]

    wx = w0_ref[...] + jnp.dot(jnp.tanh(w_lo).astype(bf16), w2_ref[...], preferred_element_type=f32)
    lw_out[...] = -DECAY_SCALE * _sigmoid(wx)
    a = _sigmoid(a0_ref[...] + jnp.dot(a_lo.astype(bf16), a2_ref[...], preferred_element_type=f32))
    g = jnp.dot(_sigmoid(g_lo).astype(bf16), g2_ref[...], preferred_element_type=f32)

    ones_bd = _head_ones()
    kk = k * kk_ref[...]
    kk = kk * lax.rsqrt(jnp.maximum(_head_sum(kk * kk, ones_bd), 1e-24))
    k = k * (1.0 + (a - 1.0) * ka_ref[...])

    k_out[...] = k.astype(k_out.dtype)
    kk_out[...] = kk.astype(kk_out.dtype)
    kka_out[...] = (kk * a).astype(kka_out.dtype)
    g_out[...] = g.astype(g_out.dtype)


def rwkv_prep(proj, w0, w2p, a0, a2p, g2p, k_k, k_a, *, ts):
    T = proj.shape[0]
    C = RWKV_WIDTH
    k_blk = 1
    lora_blk = 3 * C // LORA_PAD

    def full(shape):
        return pl.BlockSpec(shape, lambda i: (0,) * len(shape))

    row = lambda t: t.reshape(1, -1)
    out_sd = lambda dt: jax.ShapeDtypeStruct((T, C), dt)
    out_spec = pl.BlockSpec((ts, C), lambda i: (i, 0))
    return pl.pallas_call(
        _rwkv_prep_kernel,
        out_shape=[out_sd(f32), out_sd(bf16), out_sd(bf16), out_sd(bf16), out_sd(bf16)],
        grid=(T // ts,),
        in_specs=[
            pl.BlockSpec((ts, C), lambda i: (i, k_blk)),
            pl.BlockSpec((ts, LORA_PAD), lambda i: (i, lora_blk)),
            full((1, C)), full((128, C)), full((1, C)), full((128, C)), full((256, C)),
            full((1, C)), full((1, C)),
        ],
        out_specs=[out_spec] * 5,
        compiler_params=_cparams(("parallel",)),
        name="rwkv_prep",
    )(proj, proj, row(w0), w2p, row(a0), a2p, g2p, row(k_k), row(k_a))


def _bd_mask():
    r = lax.broadcasted_iota(jnp.int32, (HG_LANES, HG_LANES), 0) // RWKV_CHUNK
    c = lax.broadcasted_iota(jnp.int32, (HG_LANES, HG_LANES), 1) // RWKV_HEAD
    return r == c


def _rwkv_scan_kernel(r_ref, v_ref, lw_ref, k_ref, kk_ref, kka_ref, g_ref,
                      lnw_ref, lnb_ref, rk_ref, *rest, n_chunks, n_hg, n_cast):
    L = RWKV_CHUNK
    W = HG_LANES
    cast_in, o_ref, cast_out, state_ref = rest[:n_cast], rest[n_cast], rest[n_cast + 1:-1], rest[-1]
    for w_ref, wb_ref in zip(cast_in, cast_out):
        wb_ref[...] = w_ref[...].astype(bf16)

    @pl.when(pl.program_id(2) == 0)
    def _():
        state_ref[...] = jnp.zeros_like(state_ref)

    bd_mask = _bd_mask()
    ones_bd = _head_ones()

    def bd(x):
        tiled = jnp.concatenate([x] * HEADS_PER_STEP, axis=0)
        return jnp.where(bd_mask, tiled, 0.0).astype(bf16)

    def bd_t(x):
        tiled = jnp.concatenate([x] * HEADS_PER_STEP, axis=0)
        return jnp.where(bd_mask, tiled, 0.0).T.astype(bf16)

    def mm(a, b_bf16):
        return jnp.dot(a.astype(bf16), b_bf16, preferred_element_type=f32)

    def mm_nt(a, b_bf16):
        return lax.dot_general(a.astype(bf16), b_bf16, (((1,), (1,)), ((), ())),
                               preferred_element_type=f32)

    def mm_tn(a_bf16, b_bf16):
        return lax.dot_general(a_bf16, b_bf16, (((0,), (0,)), ((), ())),
                               preferred_element_type=f32)

    def split(x):
        hi = x.astype(bf16)
        return hi, (x - hi.astype(f32)).astype(bf16)

    def head_sums(xs):
        tot = mm(jnp.concatenate(xs, axis=0), ones_bd)
        return [tot[j * L:(j + 1) * L] for j in range(len(xs))]

    t_idx = lax.broadcasted_iota(jnp.int32, (L, W), 0)
    s_idx = lax.broadcasted_iota(jnp.int32, (L, W), 1) % L
    strict = t_idx > s_idx
    incl = t_idx >= s_idx
    eye_all = jnp.where(t_idx == s_idx, 1.0, 0.0)
    tri = jnp.where(lax.broadcasted_iota(jnp.int32, (L, L), 0)
                    >= lax.broadcasted_iota(jnp.int32, (L, L), 1), 1.0, 0.0).astype(bf16)
    blockdiag = (lax.broadcasted_iota(jnp.int32, (W, W), 0) // RWKV_HEAD
                 == lax.broadcasted_iota(jnp.int32, (W, W), 1) // RWKV_HEAD)

    def phase1(items):
        I = range(len(items))
        rows = [pl.ds(c * L, L) for _, c in items]
        cols = [pl.ds(h * W, W) for h, _ in items]
        ld = lambda ref, i: ref[rows[i], cols[i]]
        r = [ld(r_ref, i).astype(f32) for i in I]
        k = [ld(k_ref, i).astype(f32) for i in I]
        v = [ld(v_ref, i).astype(f32) for i in I]
        lw = [ld(lw_ref, i) for i in I]
        lw_s = [split(x) for x in lw]
        cs = [jnp.dot(tri, hi, preferred_element_type=f32) + jnp.dot(tri, lo, preferred_element_type=f32)
              for hi, lo in lw_s]
        yield
        g_inv = [jnp.exp(-x) for x in cs]
        to_end = [jnp.exp(x[L - 1:L, :] - x) for x in cs]
        a_t = [-ld(kk_ref, i).astype(f32) * jnp.exp(cs[i] - lw[i]) for i in I]
        r_t = [r[i] * jnp.exp(cs[i]) for i in I]
        kka = [ld(kka_ref, i).astype(f32) for i in I]
        b_t = [kka[i] * g_inv[i] for i in I]
        k_t = [k[i] * g_inv[i] for i in I]
        b_e = [kka[i] * to_end[i] for i in I]
        k_e = [k[i] * to_end[i] for i in I]
        v_bd = [bd(x) for x in v]
        ar = [jnp.concatenate([a_t[i], r_t[i]], axis=0) for i in I]
        m_b = [mm(ar[i], bd_t(b_t[i])) for i in I]
        m_k = [mm(ar[i], bd_t(k_t[i])) for i in I]
        m_ab = [jnp.where(strict, x[:L], 0.0) for x in m_b]
        m_rb = [jnp.where(incl, x[L:], 0.0) for x in m_b]
        m_ak = [jnp.where(strict, x[:L], 0.0) for x in m_k]
        m_rk = [jnp.where(incl, x[L:], 0.0) for x in m_k]
        rk_sum = head_sums([r[i] * k[i] * rk_ref[:, cols[i]] for i in I])
        bonus = [rk_sum[i] * v[i] for i in I]
        yield
        tinv = [eye_all + x for x in m_ab]
        p = [mm(x, bd(x)) for x in m_ab]
        m = 2
        while m < L:
            w_p = [bd(x) for x in p]
            m *= 2
            if m < L:
                both = [mm(jnp.concatenate([tinv[i], p[i]], axis=0), w_p[i]) for i in I]
                tinv = [tinv[i] + both[i][:L] for i in I]
                p = [x[L:] for x in both]
            else:
                tinv = [tinv[i] + mm(tinv[i], w_p[i]) for i in I]
            yield
        xm = [mm(jnp.concatenate([m_ak[i], m_rk[i]], axis=0), v_bd[i]) for i in I]
        u_v = [mm(tinv[i], bd(xm[i][:L])) for i in I]
        a_h = [mm(tinv[i], bd(a_t[i])) for i in I]
        yield
        r_h = [r_t[i] + mm(m_rb[i], bd(a_h[i])) for i in I]
        y_v = [mm(m_rb[i], bd(u_v[i])) + xm[i][L:] for i in I]
        yield
        p_t = [jnp.where(blockdiag, mm_tn(a_h[i].astype(bf16), b_e[i].astype(bf16)), 0.0).astype(bf16)
               for i in I]
        g_t = [jnp.where(blockdiag,
                         mm_tn(jnp.concatenate([u_v[i], v[i]], axis=0).astype(bf16),
                               jnp.concatenate([b_e[i], k_e[i]], axis=0).astype(bf16)), 0.0) for i in I]
        decay = [jnp.exp(x[L - 1:L, :]) for x in cs]
        for i in I:
            done.append(dict(hg=items[i][0], rows=rows[i], cols=cols[i], r_h=r_h[i], y_v=y_v[i],
                             p_t=p_t[i], g_t=g_t[i], decay=decay[i], bonus=bonus[i]))
        yield

    def chain_step(chs):
        ys = []
        for ch in chs:
            state = states[ch['hg']]
            state_b = state.astype(bf16)
            ys.append(mm_nt(ch['r_h'], state_b) + ch['y_v'])
            states[ch['hg']] = (state * ch['decay']
                                + jnp.dot(state_b, ch['p_t'], preferred_element_type=f32) + ch['g_t'])
        means = head_sums(ys)
        ycs = [y - m * (1.0 / RWKV_HEAD) for y, m in zip(ys, means)]
        vrs = head_sums([yc * yc for yc in ycs])
        for ch, yc, vr in zip(chs, ycs, vrs):
            rw, cl = ch['rows'], ch['cols']
            yn = yc * lax.rsqrt(vr * (1.0 / RWKV_HEAD) + GN_EPS) * lnw_ref[:, cl] + lnb_ref[:, cl]
            o_ref[rw, cl] = ((yn + ch['bonus']) * g_ref[rw, cl].astype(f32)).astype(o_ref.dtype)

    states = [state_ref[h] for h in range(n_hg)]
    done = []
    half = max(n_chunks // 2, 1)
    groups = [[(h, c) for c in range(0, half) for h in range(n_hg)],
              [(h, c) for c in range(half, n_chunks) for h in range(n_hg)]]
    def by_chunk(n_items):
        return [done[j:j + n_hg] for j in range(0, n_items, n_hg)]

    for _ in phase1(groups[0]):
        pass
    pending = by_chunk(len(done))
    for _ in phase1(groups[1]):
        if pending:
            chain_step(pending.pop(0))
    for chs in pending + by_chunk(len(done))[len(groups[0]) // n_hg:]:
        chain_step(chs)
    for h in range(n_hg):
        state_ref[h] = states[h]


def rwkv_scan(proj, lw, k, kk, kka, g, ln_w, ln_b, r_k, cast_weights, *, batch, seq, rows, n_hg):
    nb = seq // rows
    T = batch * seq
    width = n_hg * HG_LANES
    n_col = RWKV_WIDTH // width
    n_steps = batch * n_col * nb
    r_blk = 0
    v_blk = 2 * RWKV_WIDTH // width
    blk = pl.BlockSpec((rows, width), lambda b, h, c: (b * nb + c, h))
    par = pl.BlockSpec((1, width), lambda b, h, c: (0, h))
    slabs = [w.reshape(n_steps, w.size // (n_steps * w.shape[-1]), w.shape[-1]) for w in cast_weights]
    slab_spec = lambda w: pl.BlockSpec((1,) + w.shape[1:], lambda b, h, c: ((b * n_col + h) * nb + c, 0, 0))
    outs = pl.pallas_call(
        functools.partial(_rwkv_scan_kernel, n_chunks=rows // RWKV_CHUNK, n_hg=n_hg,
                          n_cast=len(slabs)),
        out_shape=[jax.ShapeDtypeStruct((T, RWKV_WIDTH), bf16)]
                  + [jax.ShapeDtypeStruct(w.shape, bf16) for w in slabs],
        grid=(batch, n_col, nb),
        in_specs=[pl.BlockSpec((rows, width), lambda b, h, c: (b * nb + c, r_blk + h)),
                  pl.BlockSpec((rows, width), lambda b, h, c: (b * nb + c, v_blk + h))]
                 + [blk] * 5 + [par] * 3 + [slab_spec(w) for w in slabs],
        out_specs=[blk] + [slab_spec(w) for w in slabs],
        scratch_shapes=[pltpu.VMEM((n_hg, HG_LANES, HG_LANES), f32)],
        compiler_params=_cparams(("parallel", "parallel", "arbitrary")),
        name="rwkv_scan",
    )(proj, proj, lw, k, kk, kka, g, ln_w.reshape(1, -1), ln_b.reshape(1, -1), r_k.reshape(1, -1),
      *slabs)
    return outs[0], [o.reshape(w.shape) for o, w in zip(outs[1:], cast_weights)]


def _route(hn, w, b):
    w_hi = w.astype(bf16)
    w_lo = (w - w_hi.astype(f32)).astype(bf16)
    h_hi = hn.astype(bf16)
    h_lo = (hn - h_hi.astype(f32)).astype(bf16)
    hh = jnp.dot(h_hi, jnp.concatenate([w_hi, w_lo], axis=1), preferred_element_type=f32)
    logits = (hh[:, :ROUTER_LANES] + hh[:, ROUTER_LANES:]
              + jnp.dot(h_lo, w_hi, preferred_element_type=f32)) + b

    lane_i = lax.broadcasted_iota(jnp.int32, logits.shape, 1)
    lane = lane_i.astype(f32)
    neg = -jnp.inf
    big = float(ROUTER_LANES)
    is_g = lane_i < MOE_GROUPS
    gl = jnp.where(is_g, logits, neg)
    gmax = jnp.max(gl, axis=-1, keepdims=True)
    g_idx = jnp.min(jnp.where(gl == gmax, lane, big), axis=-1, keepdims=True)
    g_w = 1.0 / jnp.sum(jnp.exp(gl - gmax), axis=-1, keepdims=True)

    lo_lane = MOE_GROUPS + g_idx * MOE_EXPERTS_PER_GROUP
    sel = (lane >= lo_lane) & (lane < lo_lane + MOE_EXPERTS_PER_GROUP)
    el = jnp.where(sel, logits, neg)
    v1 = jnp.max(el, axis=-1, keepdims=True)
    i1 = jnp.min(jnp.where(el == v1, lane, big), axis=-1, keepdims=True)
    el2 = jnp.where(lane == i1, neg, el)
    v2 = jnp.max(el2, axis=-1, keepdims=True)
    i2 = jnp.min(jnp.where(el2 == v2, lane, big), axis=-1, keepdims=True)
    e2 = jnp.exp(v2 - v1)
    p1 = 1.0 / (1.0 + e2)
    p2 = e2 / (1.0 + e2)
    return (jnp.where(lane_i == 0, i1 - MOE_GROUPS, 0.0)
            + jnp.where(lane_i == 1, i2 - MOE_GROUPS, 0.0)
            + jnp.where(lane_i == 2, p1 * g_w, 0.0)
            + jnp.where(lane_i == 3, p2 * g_w, 0.0))


def _attn_oproj_router_kernel(q_ref, kv_ref, wo_ref, res_ref, g_ref, wr_ref, br_ref, h_ref, xr_ref,
                              rec_ref, *, tn):
    hd = XATTN_HEAD_DIM
    H = range(XATTN_HEADS)
    s = [lax.dot_general(q_ref[:, h * hd:(h + 1) * hd], kv_ref[:, h * hd:(h + 1) * hd],
                         (((1,), (1,)), ((), ())), preferred_element_type=f32) * (hd ** -0.5) for h in H]
    e = [jnp.exp(x - jnp.max(x, axis=-1, keepdims=True)) for x in s]
    l = [jnp.sum(x, axis=-1, keepdims=True) for x in e]
    pv = [jnp.dot(e[h].astype(bf16), kv_ref[:, (XATTN_HEADS + h) * hd:(XATTN_HEADS + h + 1) * hd],
                  preferred_element_type=f32) for h in H]
    o = jnp.concatenate([(pv[h] / l[h]).astype(bf16) for h in H], axis=1)
    for n0 in range(0, h_ref.shape[1], tn):
        h_ref[:, n0:n0 + tn] = res_ref[:, n0:n0 + tn] + jnp.dot(
            o, wo_ref[:, n0:n0 + tn], preferred_element_type=f32)
    hn = _rmsnorm_rows(h_ref[...], g_ref[...])
    D = hn.shape[1]
    rec = _route(hn, wr_ref[...], br_ref[...])
    xr_ref[:, :D] = hn
    xr_ref[:, D:] = rec
    rec_ref[...] = rec


def attn_oproj_router(q, kv, w_o, res, g, w_router, b_router, *, tm, tn, seq):
    T, D = res.shape
    tiles_per_seq = seq // tm
    row = lambda width: pl.BlockSpec((tm, width), lambda i: (i, 0))
    return pl.pallas_call(
        functools.partial(_attn_oproj_router_kernel, tn=tn),
        out_shape=[jax.ShapeDtypeStruct((T, D), f32),
                   jax.ShapeDtypeStruct((T, D + ROUTER_LANES), f32),
                   jax.ShapeDtypeStruct((T, ROUTER_LANES), f32)],
        grid=(T // tm,),
        in_specs=[row(D), pl.BlockSpec((MEM_LEN, 2 * D), lambda i: (i // tiles_per_seq, 0)),
                  _resident((D, D)), row(D), _resident((1, D)),
                  _resident((D, ROUTER_LANES)), _resident((1, ROUTER_LANES))],
        out_specs=[row(D), row(D + ROUTER_LANES), row(ROUTER_LANES)],
        compiler_params=_cparams(("parallel",)),
        name="attn_oproj_router",
    )(q, kv, w_o, res, g.reshape(1, D), w_router, b_router)


MOE_PAIRS = ((0, 1), (0, 2), (0, 3), (1, 3), (1, 2), (3, 2))
MOE_CLASSES = MOE_GROUPS * len(MOE_PAIRS)


def _route_metadata(rec, *, tm, n_tiles):
    ids = rec[:, 0:2].astype(jnp.int32)
    lo = jnp.minimum(ids[:, 0], ids[:, 1])
    hi = jnp.maximum(ids[:, 0], ids[:, 1])
    a, b = lo % MOE_EXPERTS_PER_GROUP, hi % MOE_EXPERTS_PER_GROUP
    pair_id = (a * (7 - a)) // 2 + b - a - 1
    pair_id = pair_id + (pair_id == 3).astype(jnp.int32) - (pair_id == 4).astype(jnp.int32)
    cls = (lo // MOE_EXPERTS_PER_GROUP) * len(MOE_PAIRS) + pair_id
    onehot = (cls[:, None] == jnp.arange(MOE_CLASSES, dtype=jnp.int32)[None, :]).astype(jnp.int32)
    csum = jnp.cumsum(onehot, axis=0)
    counts = csum[-1]
    rank = jnp.sum(csum * onehot, axis=1) - 1
    padded = ((counts + tm - 1) // tm) * tm
    ends = jnp.cumsum(padded)
    pos = jnp.sum(onehot * (ends - padded)[None, :], axis=1) + rank
    tile_start = jnp.arange(n_tiles, dtype=jnp.int32) * tm
    n_valid = ends[-1] // tm
    tile_cls = jnp.sum((tile_start[:, None] >= ends[None, :]).astype(jnp.int32), axis=1)
    last_cls = jnp.sum((ends[-1] - 1 >= ends).astype(jnp.int32))
    tile_cls = jnp.where(tile_start < ends[-1], tile_cls, last_cls)
    pair = jnp.asarray(MOE_PAIRS, jnp.int32)[tile_cls % len(MOE_PAIRS)]
    base = (tile_cls // len(MOE_PAIRS)) * MOE_EXPERTS_PER_GROUP
    idle = n_valid + jnp.arange(MOE_CLASSES, dtype=jnp.int32)
    fill_start = jnp.concatenate([jnp.maximum(ends - tm, 0), jnp.minimum(idle, n_tiles - 1) * tm])
    fill_on = jnp.concatenate([counts > 0, idle < n_tiles]).astype(jnp.int32)
    return pos, base + pair[:, 0], base + pair[:, 1], n_valid.reshape(1), fill_start, fill_on


def _moe_dispatch_kernel(pos_ref, fs_ref, fo_ref, x_ref, xs_hbm, stage, zbuf, sem, zsem, *, tm, tz):
    i = pl.program_id(0)
    n = pl.num_programs(0)

    def fill_copy(c):
        return pltpu.make_async_copy(zbuf, xs_hbm.at[pl.ds(pl.multiple_of(fs_ref[c], tz), tz), :], zsem)

    @pl.when(i == 0)
    def _():
        zbuf[...] = jnp.zeros_like(zbuf)
        for c in range(2 * MOE_CLASSES):
            @pl.when(fo_ref[c] > 0)
            def _():
                fill_copy(c).start()
        for c in range(2 * MOE_CLASSES):
            @pl.when(fo_ref[c] > 0)
            def _():
                fill_copy(c).wait()

    def tile_wait(slot):
        pltpu.make_async_copy(stage.at[slot], xs_hbm.at[pl.ds(0, tm), :], sem.at[slot]).wait()

    slot = i % 2

    @pl.when(i >= 2)
    def _():
        tile_wait(slot)

    stage[slot] = x_ref[...]

    def body(r, carry):
        p = pos_ref[i * tm + r]
        pltpu.make_async_copy(stage.at[slot, pl.ds(r, 1), :], xs_hbm.at[pl.ds(p, 1), :],
                              sem.at[slot]).start()
        return carry
    lax.fori_loop(0, tm, body, 0, unroll=8)

    @pl.when(i == n - 1)
    def _():
        tile_wait(slot)

        @pl.when(n >= 2)
        def _():
            tile_wait(1 - slot)


def moe_dispatch(xr, pos, fill_start, fill_on, *, tm, tz, n_slots):
    T, width = xr.shape
    grid_spec = pltpu.PrefetchScalarGridSpec(
        num_scalar_prefetch=3,
        grid=(T // tm,),
        in_specs=[pl.BlockSpec((tm, width), lambda i, pos, fs, fo: (i, 0))],
        out_specs=pl.BlockSpec(memory_space=pl.ANY),
        scratch_shapes=[pltpu.VMEM((2, tm, width), f32), pltpu.VMEM((tz, width), f32),
                        pltpu.SemaphoreType.DMA((2,)), pltpu.SemaphoreType.DMA(())],
    )
    return pl.pallas_call(
        functools.partial(_moe_dispatch_kernel, tm=tm, tz=tz),
        out_shape=jax.ShapeDtypeStruct((n_slots, width), f32),
        grid_spec=grid_spec,
        compiler_params=_cparams(("arbitrary",)),
        name="moe_dispatch",
    )(pos, fill_start, fill_on, xr)


def _moe_pair_kernel(ea_ref, eb_ref, nv_ref, xs_ref, wga, wua, wda, wgb, wub, wdb, o_ref):
    @pl.when(pl.program_id(0) >= nv_ref[0])
    def _():
        o_ref[...] = jnp.zeros_like(o_ref)

    @pl.when(pl.program_id(0) < nv_ref[0])
    def _():
        D = o_ref.shape[1]
        x = xs_ref[:, :D].astype(bf16)
        rec = xs_ref[:, D:]
        first = rec[:, 0:1] == ea_ref[pl.program_id(0)].astype(f32)
        w_a = jnp.where(first, rec[:, 2:3], rec[:, 3:4])
        w_b = jnp.where(first, rec[:, 3:4], rec[:, 2:3])

        gates = [jnp.dot(x, wg[0], preferred_element_type=f32) for wg in (wga, wgb)]
        ups = [jnp.dot(x, wu[0], preferred_element_type=f32) for wu in (wua, wub)]
        hids = [(g * _sigmoid(g) * u).astype(bf16) for g, u in zip(gates, ups)]
        y_a, y_b = [jnp.dot(h, wd[0], preferred_element_type=f32) for h, wd in zip(hids, (wda, wdb))]
        o_ref[...] = w_a * y_a + w_b * y_b


def moe_pair_experts(xs, e_a, e_b, n_valid, wg, wu, wd, *, tm, n_tiles):
    width = xs.shape[1]
    D = wg.shape[1]
    held = lambda i, ea, eb, nv: (jnp.minimum(i, nv[0] - 1), 0)
    tile = lambda i, ea, eb, nv: (i, 0)
    wa = lambda shape: pl.BlockSpec(shape, lambda i, ea, eb, nv: (ea[i], 0, 0))
    wb = lambda shape: pl.BlockSpec(shape, lambda i, ea, eb, nv: (eb[i], 0, 0))
    grid_spec = pltpu.PrefetchScalarGridSpec(
        num_scalar_prefetch=3,
        grid=(n_tiles,),
        in_specs=[pl.BlockSpec((tm, width), held),
                  wa((1, D, MOE_FF)), wa((1, D, MOE_FF)), wa((1, MOE_FF, D)),
                  wb((1, D, MOE_FF)), wb((1, D, MOE_FF)), wb((1, MOE_FF, D))],
        out_specs=pl.BlockSpec((tm, D), tile),
    )
    return pl.pallas_call(
        _moe_pair_kernel,
        out_shape=jax.ShapeDtypeStruct((n_tiles * tm, D), f32),
        grid_spec=grid_spec,
        compiler_params=_cparams(("arbitrary",)),
        name="moe_experts",
    )(e_a, e_b, n_valid, xs, wg, wu, wd, wg, wu, wd)


def _moe_combine_kernel(pos_ref, h_ref, y_hbm, gf_ref, o_ref, ybuf, sem, *, tm):
    i = pl.program_id(0)
    n = pl.num_programs(0)

    def gather(tile, slot):
        def body(r, carry):
            p = pos_ref[tile * tm + r]
            pltpu.make_async_copy(y_hbm.at[pl.ds(p, 1), :],
                                  ybuf.at[slot, pl.ds(r, 1), :], sem.at[slot]).start()
            return carry
        lax.fori_loop(0, tm, body, 0, unroll=8)

    @pl.when(i == 0)
    def _():
        gather(0, 0)

    @pl.when(i + 1 < n)
    def _():
        gather(i + 1, (i + 1) % 2)

    slot = i % 2
    pltpu.make_async_copy(y_hbm.at[pl.ds(0, tm), :], ybuf.at[slot], sem.at[slot]).wait()
    o_ref[...] = _rmsnorm_rows(h_ref[...] + ybuf[slot], gf_ref[...])


def moe_combine(h, y_sorted, pos, g_final, *, tm):
    T, D = h.shape
    grid_spec = pltpu.PrefetchScalarGridSpec(
        num_scalar_prefetch=1,
        grid=(T // tm,),
        in_specs=[
            pl.BlockSpec((tm, D), lambda i, pos: (i, 0)),
            pl.BlockSpec(memory_space=pl.ANY),
            pl.BlockSpec((1, D), lambda i, pos: (0, 0)),
        ],
        out_specs=pl.BlockSpec((tm, D), lambda i, pos: (i, 0)),
        scratch_shapes=[pltpu.VMEM((2, tm, D), f32), pltpu.SemaphoreType.DMA((2,))],
    )
    return pl.pallas_call(
        functools.partial(_moe_combine_kernel, tm=tm),
        out_shape=jax.ShapeDtypeStruct((T, D), f32),
        grid_spec=grid_spec,
        compiler_params=_cparams(("arbitrary",)),
        name="moe_combine",
    )(pos, h, y_sorted, g_final.reshape(1, D))


def _pad_rows(w, rows):
    return jnp.pad(w, ((0, rows - w.shape[0]), (0, 0)))


def _pack_lora_cols(t):
    o1, o2 = DECAY_LORA, DECAY_LORA + AAA_LORA
    pad = lambda a, n: jnp.pad(a, [(0, 0)] * (a.ndim - 1) + [(0, n - a.shape[-1])])
    return jnp.concatenate([pad(t[..., :o1], 128), pad(t[..., o1:o2], 128),
                            pad(t[..., o2:], 256)], axis=-1)


def _layer(x2, mem2, p, *, batch, seq, cfg):
    D = D_MODEL
    w_in = p['w_in']
    w_main = w_in.astype(bf16)
    w_lora = _pack_lora_cols(w_in[:, MAIN_COLS:]).astype(bf16)
    mu_cols = jnp.concatenate([jnp.zeros((POOL_WIDTH,), f32), p['rwkv_mu'][:3 * RWKV_WIDTH],
                               _pack_lora_cols(p['rwkv_mu'][3 * RWKV_WIDTH:])])

    proj, pool_out = in_proj_mixers(x2, p['norm_mix_g'], w_main, w_lora, mu_cols,
                                    p['pool_w'].astype(bf16), p['pool_scale'],
                                    tm=cfg['tm'], tn=cfg['tn_in'], seq=seq)
    lw, k, kk, kka, g = rwkv_prep(
        proj, p['rwkv_w0'], _pad_rows(p['rwkv_w2'], 128).astype(bf16),
        p['rwkv_a0'], _pad_rows(p['rwkv_a2'], 128).astype(bf16),
        _pad_rows(p['rwkv_g2'], 256).astype(bf16), p['rwkv_k_k'], p['rwkv_k_a'], ts=cfg['ts_prep'])
    rwkv_out, (w_out, w_q, w_kv, w_o, wg, wu, wd) = rwkv_scan(
        proj, lw, k, kk, kka, g, p['rwkv_ln_w'], p['rwkv_ln_b'], p['rwkv_r_k'],
        [p['w_out'], p['xattn_w_q'], p['xattn_w_kv'], p['xattn_w_o'],
         p['moe_w_gate'], p['moe_w_up'], p['moe_w_down']],
        batch=batch, seq=seq, rows=cfg['scan_rows'], n_hg=cfg['scan_hg'])
    h1 = matmul_residual_resident([pool_out, rwkv_out], [w_out[:POOL_WIDTH], w_out[POOL_WIDTH:]],
                                  x2, tm=cfg['tm'], tn=cfg['tn'], name="out_proj")

    q = norm_matmul_resident(h1, p['norm_xattn_g'], w_q,
                             tm=cfg['tm_q'], tn=cfg['tn'], name="q_proj")
    kv = norm_matmul_resident(mem2, p['norm_mem_g'], w_kv,
                              tm=min(cfg['tm'], mem2.shape[0]), tn=cfg['tn'], name="kv_proj")

    w_router = jnp.pad(jnp.concatenate([p['moe_w_group'], p['moe_w_expert']], axis=1),
                       ((0, 0), (0, ROUTER_LANES - MOE_GROUPS - MOE_EXPERTS)))
    b_router = jnp.pad(jnp.concatenate([p['moe_b_group'], p['moe_b_expert']]),
                       (0, ROUTER_LANES - MOE_GROUPS - MOE_EXPERTS)).reshape(1, ROUTER_LANES)
    h2, xr, rec = attn_oproj_router(q, kv, w_o, h1, p['norm_ffn_g'],
                               w_router, b_router, tm=cfg['tm'], tn=cfg['tn'], seq=seq)
    tm_e = cfg['tm_moe']
    n_tiles = xr.shape[0] // tm_e + MOE_CLASSES
    pos, e_a, e_b, n_valid, fill_start, fill_on = _route_metadata(
        rec, tm=tm_e, n_tiles=n_tiles)
    xs = moe_dispatch(xr, pos, fill_start, fill_on, tm=cfg['tm_disp'], tz=tm_e,
                      n_slots=n_tiles * tm_e)
    y_sorted = moe_pair_experts(xs, e_a, e_b, n_valid, wg, wu, wd, tm=tm_e, n_tiles=n_tiles)
    return moe_combine(h2, y_sorted, pos, p['norm_final_g'], tm=cfg['tm_comb'])


_CFG = dict(tm=512, tm_q=1024, tn=1024, tn_in=512, ts_prep=512, scan_rows=256, scan_hg=4,
            tm_moe=256, tm_disp=512, tm_comb=512)


def kernel(x, mem, norm_mix_g, w_in, pool_w, pool_scale, rwkv_mu, rwkv_w0, rwkv_w2, rwkv_a0,
           rwkv_a2, rwkv_g2, rwkv_k_k, rwkv_k_a, rwkv_r_k, rwkv_ln_w, rwkv_ln_b, w_out,
           norm_xattn_g, norm_mem_g, xattn_w_q, xattn_w_kv, xattn_w_o, norm_ffn_g,
           moe_w_group, moe_b_group, moe_w_expert, moe_b_expert, moe_w_gate, moe_w_up,
           moe_w_down, norm_final_g):
    batch, seq, D = x.shape
    p = dict(norm_mix_g=norm_mix_g[0], w_in=w_in[0], pool_w=pool_w[0], pool_scale=pool_scale[0],
             rwkv_mu=rwkv_mu[0], rwkv_w0=rwkv_w0[0], rwkv_w2=rwkv_w2[0], rwkv_a0=rwkv_a0[0],
             rwkv_a2=rwkv_a2[0], rwkv_g2=rwkv_g2[0], rwkv_k_k=rwkv_k_k[0], rwkv_k_a=rwkv_k_a[0],
             rwkv_r_k=rwkv_r_k[0], rwkv_ln_w=rwkv_ln_w[0], rwkv_ln_b=rwkv_ln_b[0], w_out=w_out[0],
             norm_xattn_g=norm_xattn_g[0], norm_mem_g=norm_mem_g[0], xattn_w_q=xattn_w_q[0],
             xattn_w_kv=xattn_w_kv[0], xattn_w_o=xattn_w_o[0], norm_ffn_g=norm_ffn_g[0],
             moe_w_group=moe_w_group[0], moe_b_group=moe_b_group[0], moe_w_expert=moe_w_expert[0],
             moe_b_expert=moe_b_expert[0], moe_w_gate=moe_w_gate[0], moe_w_up=moe_w_up[0],
             moe_w_down=moe_w_down[0], norm_final_g=norm_final_g)
    out = _layer(x.reshape(batch * seq, D), mem.reshape(batch * MEM_LEN, D), p,
                 batch=batch, seq=seq, cfg=_CFG)
    return out.reshape(batch, seq, D)
```

```python
import functools

import jax
import jax.numpy as jnp
from jax import lax
from jax.experimental import pallas as pl
from jax.experimental.pallas import tpu as pltpu

f32 = jnp.float32
bf16 = jnp.bfloat16

D_MODEL = 2048
MEM_LEN = 256
NORM_EPS = 1e-6

POOL_WIDTH = 1024
POOL_WINDOWS = (2, 4, 8, 16)
POOL_GROUP = 256
POOL_HALO = 16
RWKV_WIDTH = 1024
RWKV_HEAD = 64
RWKV_HEADS = 16
GN_EPS = 64e-5
DECAY_SCALE = 0.6065306597126334
DECAY_LORA = 64
AAA_LORA = 64
GATE_LORA = 160
LORA_PAD = 512
LORA_W_OFF, LORA_A_OFF, LORA_G_OFF = 0, 128, 256
MAIN_COLS = POOL_WIDTH + 3 * RWKV_WIDTH
PROJ_COLS = MAIN_COLS + LORA_PAD

RWKV_CHUNK = 64
HEADS_PER_STEP = 4
HG_LANES = HEADS_PER_STEP * RWKV_HEAD

XATTN_HEADS = 4
XATTN_HEAD_DIM = 512

MOE_GROUPS = 4
MOE_EXPERTS_PER_GROUP = 4
MOE_EXPERTS = 16
MOE_FF = 512
ROUTER_LANES = 128

VMEM_LIMIT = 56 * 1024 * 1024


def _cparams(sem):
    return pltpu.CompilerParams(dimension_semantics=sem, vmem_limit_bytes=VMEM_LIMIT)


def _split_dot(x, w_bf16):
    hi = x.astype(bf16)
    lo = (x - hi.astype(f32)).astype(bf16)
    return (jnp.dot(hi, w_bf16, preferred_element_type=f32)
            + jnp.dot(lo, w_bf16, preferred_element_type=f32))


def _resident(shape):
    return pl.BlockSpec(shape, lambda i: (0,) * len(shape), pipeline_mode=pl.Buffered(1))


def _rmsnorm_rows(x, g):
    ms = jnp.mean(x * x, axis=-1, keepdims=True)
    return x * lax.rsqrt(ms + NORM_EPS) * g


def _norm_mm_res_kernel(x_ref, g_ref, w_ref, o_ref, *, tn):
    xn = _rmsnorm_rows(x_ref[...], g_ref[...]).astype(bf16)
    for n0 in range(0, o_ref.shape[1], tn):
        o_ref[:, n0:n0 + tn] = jnp.dot(xn, w_ref[:, n0:n0 + tn],
                                       preferred_element_type=f32).astype(o_ref.dtype)


def norm_matmul_resident(x, g, w, *, tm, tn, name):
    M, K = x.shape
    N = w.shape[1]
    return pl.pallas_call(
        functools.partial(_norm_mm_res_kernel, tn=tn),
        out_shape=jax.ShapeDtypeStruct((M, N), bf16),
        grid=(M // tm,),
        in_specs=[pl.BlockSpec((tm, K), lambda i: (i, 0)), _resident((1, K)), _resident((K, N))],
        out_specs=pl.BlockSpec((tm, N), lambda i: (i, 0)),
        compiler_params=_cparams(("parallel",)),
        name=name,
    )(x, g.reshape(1, K), w)


def _in_proj_kernel(x_ref, xh_ref, g_ref, w_ref, wl_ref, mu_ref, pw_ref, ps_ref, o_ref, pool_ref,
                    *, tn, tiles_per_seq):
    tm = x_ref.shape[0]
    tile_in_seq = pl.program_id(0) % tiles_per_seq
    first = tile_in_seq == 0
    xn = _rmsnorm_rows(x_ref[...], g_ref[...]).astype(bf16)
    xh = jnp.where(first, 0.0, _rmsnorm_rows(xh_ref[...], g_ref[...])).astype(bf16)
    lhs = jnp.concatenate([xh, xn], axis=0)
    pos = tile_in_seq * tm + lax.broadcasted_iota(jnp.int32, (tm, 1), 0)
    n_cols = POOL_WIDTH + o_ref.shape[1]
    def slab(n0):
        w = w_ref[:, n0:n0 + tn] if n0 < MAIN_COLS else wl_ref[:, n0 - MAIN_COLS:n0 - MAIN_COLS + tn]
        return jnp.dot(lhs, w, preferred_element_type=f32)

    def pool_slab(n0, res):
        p = res[POOL_HALO:]
        acc = res
        shift = 1
        for gi in range(n0 // POOL_GROUP, (n0 + tn) // POOL_GROUP):
            win = POOL_WINDOWS[gi]
            while shift < win:
                acc = acc + pltpu.roll(acc, shift, 0)
                shift *= 2
            c0 = gi * POOL_GROUP - n0
            cnt = jnp.minimum(pos + 1, win).astype(f32)
            pooled = acc[POOL_HALO:, c0:c0 + POOL_GROUP] / cnt - p[:, c0:c0 + POOL_GROUP]
            mixed = jnp.dot(pooled.astype(bf16), pw_ref[gi], preferred_element_type=f32)
            lo, hi = gi * POOL_GROUP, (gi + 1) * POOL_GROUP
            pool_ref[:, lo:hi] = (mixed * ps_ref[:, lo:hi]).astype(pool_ref.dtype)

    def shift_slab(n0, res):
        p = res[POOL_HALO:]
        prev = pltpu.roll(res, 1, 0)[POOL_HALO:]
        z = p + (prev - p) * mu_ref[:, n0:n0 + tn]
        o_ref[:, n0 - POOL_WIDTH:n0 - POOL_WIDTH + tn] = z.astype(o_ref.dtype)

    pending = None
    for n0 in range(0, n_cols, tn):
        res = slab(n0)
        if pending is not None:
            pending()
        pending = functools.partial(pool_slab if n0 < POOL_WIDTH else shift_slab, n0, res)
    pending()


def in_proj_mixers(x, g, w_main, w_lora, mu_cols, pool_w, pool_scale, *, tm, tn, seq):
    M, K = x.shape
    n_rwkv = MAIN_COLS - POOL_WIDTH + w_lora.shape[1]
    hb = tm // POOL_HALO
    return pl.pallas_call(
        functools.partial(_in_proj_kernel, tn=tn, tiles_per_seq=seq // tm),
        out_shape=[jax.ShapeDtypeStruct((M, n_rwkv), bf16),
                   jax.ShapeDtypeStruct((M, POOL_WIDTH), bf16)],
        grid=(M // tm,),
        in_specs=[pl.BlockSpec((tm, K), lambda i: (i, 0)),
                  pl.BlockSpec((POOL_HALO, K), lambda i: (jnp.maximum(i * hb - 1, 0), 0)),
                  _resident((1, K)), _resident(w_main.shape), _resident(w_lora.shape),
                  _resident((1, POOL_WIDTH + n_rwkv)), _resident(pool_w.shape),
                  _resident((1, POOL_WIDTH))],
        out_specs=[pl.BlockSpec((tm, n_rwkv), lambda i: (i, 0)),
                   pl.BlockSpec((tm, POOL_WIDTH), lambda i: (i, 0))],
        compiler_params=_cparams(("parallel",)),
        name="in_proj",
    )(x, x, g.reshape(1, K), w_main, w_lora, mu_cols.reshape(1, -1), pool_w,
      pool_scale.reshape(1, POOL_WIDTH))


def _mm_res_resident_kernel(*refs, n, tn):
    a_refs, w_refs, res_ref, o_ref = refs[:n], refs[n:2 * n], refs[2 * n], refs[2 * n + 1]
    for n0 in range(0, o_ref.shape[1], tn):
        acc = res_ref[:, n0:n0 + tn]
        for a_ref, w_ref in zip(a_refs, w_refs):
            acc = acc + jnp.dot(a_ref[...], w_ref[:, n0:n0 + tn], preferred_element_type=f32)
        o_ref[:, n0:n0 + tn] = acc


def matmul_residual_resident(a_list, w_list, res, *, tm, tn, name):
    M, N = res.shape
    n = len(a_list)
    return pl.pallas_call(
        functools.partial(_mm_res_resident_kernel, n=n, tn=tn),
        out_shape=jax.ShapeDtypeStruct((M, N), f32),
        grid=(M // tm,),
        in_specs=([pl.BlockSpec((tm, a.shape[1]), lambda i: (i, 0)) for a in a_list]
                  + [_resident(w.shape) for w in w_list]
                  + [pl.BlockSpec((tm, N), lambda i: (i, 0))]),
        out_specs=pl.BlockSpec((tm, N), lambda i: (i, 0)),
        compiler_params=_cparams(("parallel",)),
        name=name,
    )(*a_list, *w_list, res)


def _head_ones():
    r = lax.broadcasted_iota(jnp.int32, (HG_LANES, HG_LANES), 0) // RWKV_HEAD
    c = lax.broadcasted_iota(jnp.int32, (HG_LANES, HG_LANES), 1) // RWKV_HEAD
    return jnp.where(r == c, 1.0, 0.0).astype(bf16)


def _head_sum(x, ones_bd):
    parts = [_split_dot(x[:, c:c + HG_LANES], ones_bd) for c in range(0, x.shape[1], HG_LANES)]
    return parts[0] if len(parts) == 1 else jnp.concatenate(parts, axis=1)


def _sigmoid(x):
    return 1.0 / (1.0 + jnp.exp(-x))


def _rwkv_prep_kernel(k_ref, lo_ref, w0_ref, w2_ref, a0_ref, a2_ref, g2_ref, kk_ref, ka_ref,
                      lw_out, k_out, kk_out, kka_out, g_out):
    k = k_ref[...].astype(f32)
    lo = lo_ref[...].astype(f32)
    w_lo = lo[:, LORA_W_OFF:LORA_W_OFF + 128]
    a_lo = lo[:, LORA_A_OFF:LORA_A_OFF + 128]
    g_lo = lo[:, LORA_G_OFF:LORA_G_OFF + 256]

    wx = w0_ref[...] + jnp.dot(jnp.tanh(w_lo).astype(bf16), w2_ref[...], preferred_element_type=f32)
    lw_out[...] = -DECAY_SCALE * _sigmoid(wx)
    a = _sigmoid(a0_ref[...] + jnp.dot(a_lo.astype(bf16), a2_ref[...], preferred_element_type=f32))
    g = jnp.dot(_sigmoid(g_lo).astype(bf16), g2_ref[...], preferred_element_type=f32)

    ones_bd = _head_ones()
    kk = k * kk_ref[...]
    kk = kk * lax.rsqrt(jnp.maximum(_head_sum(kk * kk, ones_bd), 1e-24))
    k = k * (1.0 + (a - 1.0) * ka_ref[...])

    k_out[...] = k.astype(k_out.dtype)
    kk_out[...] = kk.astype(kk_out.dtype)
    kka_out[...] = (kk * a).astype(kka_out.dtype)
    g_out[...] = g.astype(g_out.dtype)


def rwkv_prep(proj, w0, w2p, a0, a2p, g2p, k_k, k_a, *, ts):
    T = proj.shape[0]
    C = RWKV_WIDTH
    k_blk = 1
    lora_blk = 3 * C // LORA_PAD

    def full(shape):
        return pl.BlockSpec(shape, lambda i: (0,) * len(shape))

    row = lambda t: t.reshape(1, -1)
    out_sd = lambda dt: jax.ShapeDtypeStruct((T, C), dt)
    out_spec = pl.BlockSpec((ts, C), lambda i: (i, 0))
    return pl.pallas_call(
        _rwkv_prep_kernel,
        out_shape=[out_sd(f32), out_sd(bf16), out_sd(bf16), out_sd(bf16), out_sd(bf16)],
        grid=(T // ts,),
        in_specs=[
            pl.BlockSpec((ts, C), lambda i: (i, k_blk)),
            pl.BlockSpec((ts, LORA_PAD), lambda i: (i, lora_blk)),
            full((1, C)), full((128, C)), full((1, C)), full((128, C)), full((256, C)),
            full((1, C)), full((1, C)),
        ],
        out_specs=[out_spec] * 5,
        compiler_params=_cparams(("parallel",)),
        name="rwkv_prep",
    )(proj, proj, row(w0), w2p, row(a0), a2p, g2p, row(k_k), row(k_a))


def _bd_mask():
    r = lax.broadcasted_iota(jnp.int32, (HG_LANES, HG_LANES), 0) // RWKV_CHUNK
    c = lax.broadcasted_iota(jnp.int32, (HG_LANES, HG_LANES), 1) // RWKV_HEAD
    return r == c


def _rwkv_scan_kernel(r_ref, v_ref, lw_ref, k_ref, kk_ref, kka_ref, g_ref,
                      lnw_ref, lnb_ref, rk_ref, *rest, n_chunks, n_hg, n_cast):
    L = RWKV_CHUNK
    W = HG_LANES
    cast_in, o_ref, cast_out, state_ref = rest[:n_cast], rest[n_cast], rest[n_cast + 1:-1], rest[-1]
    for w_ref, wb_ref in zip(cast_in, cast_out):
        wb_ref[...] = w_ref[...].astype(bf16)

    @pl.when(pl.program_id(2) == 0)
    def _():
        state_ref[...] = jnp.zeros_like(state_ref)

    bd_mask = _bd_mask()
    ones_bd = _head_ones()

    def bd(x):
        tiled = jnp.concatenate([x] * HEADS_PER_STEP, axis=0)
        return jnp.where(bd_mask, tiled, 0.0).astype(bf16)

    def bd_t(x):
        tiled = jnp.concatenate([x] * HEADS_PER_STEP, axis=0)
        return jnp.where(bd_mask, tiled, 0.0).T.astype(bf16)

    def mm(a, b_bf16):
        return jnp.dot(a.astype(bf16), b_bf16, preferred_element_type=f32)

    def mm_nt(a, b_bf16):
        return lax.dot_general(a.astype(bf16), b_bf16, (((1,), (1,)), ((), ())),
                               preferred_element_type=f32)

    def mm_tn(a_bf16, b_bf16):
        return lax.dot_general(a_bf16, b_bf16, (((0,), (0,)), ((), ())),
                               preferred_element_type=f32)

    def split(x):
        hi = x.astype(bf16)
        return hi, (x - hi.astype(f32)).astype(bf16)

    def head_sums(xs):
        tot = mm(jnp.concatenate(xs, axis=0), ones_bd)
        return [tot[j * L:(j + 1) * L] for j in range(len(xs))]

    t_idx = lax.broadcasted_iota(jnp.int32, (L, W), 0)
    s_idx = lax.broadcasted_iota(jnp.int32, (L, W), 1) % L
    strict = t_idx > s_idx
    incl = t_idx >= s_idx
    eye_all = jnp.where(t_idx == s_idx, 1.0, 0.0)
    tri = jnp.where(lax.broadcasted_iota(jnp.int32, (L, L), 0)
                    >= lax.broadcasted_iota(jnp.int32, (L, L), 1), 1.0, 0.0).astype(bf16)
    blockdiag = (lax.broadcasted_iota(jnp.int32, (W, W), 0) // RWKV_HEAD
                 == lax.broadcasted_iota(jnp.int32, (W, W), 1) // RWKV_HEAD)

    def phase1(items):
        I = range(len(items))
        rows = [pl.ds(c * L, L) for _, c in items]
        cols = [pl.ds(h * W, W) for h, _ in items]
        ld = lambda ref, i: ref[rows[i], cols[i]]
        r = [ld(r_ref, i).astype(f32) for i in I]
        k = [ld(k_ref, i).astype(f32) for i in I]
        v = [ld(v_ref, i).astype(f32) for i in I]
        lw = [ld(lw_ref, i) for i in I]
        lw_s = [split(x) for x in lw]
        cs = [jnp.dot(tri, hi, preferred_element_type=f32) + jnp.dot(tri, lo, preferred_element_type=f32)
              for hi, lo in lw_s]
        g_inv = [jnp.exp(-x) for x in cs]
        to_end = [jnp.exp(x[L - 1:L, :] - x) for x in cs]
        a_t = [-ld(kk_ref, i).astype(f32) * jnp.exp(cs[i] - lw[i]) for i in I]
        r_t = [r[i] * jnp.exp(cs[i]) for i in I]
        kka = [ld(kka_ref, i).astype(f32) for i in I]
        b_t = [kka[i] * g_inv[i] for i in I]
        k_t = [k[i] * g_inv[i] for i in I]
        b_e = [kka[i] * to_end[i] for i in I]
        k_e = [k[i] * to_end[i] for i in I]
        v_bd = [bd(x) for x in v]
        ar = [jnp.concatenate([a_t[i], r_t[i]], axis=0) for i in I]
        m_b = [mm(ar[i], bd_t(b_t[i])) for i in I]
        m_k = [mm(ar[i], bd_t(k_t[i])) for i in I]
        m_ab = [jnp.where(strict, x[:L], 0.0) for x in m_b]
        m_rb = [jnp.where(incl, x[L:], 0.0) for x in m_b]
        m_ak = [jnp.where(strict, x[:L], 0.0) for x in m_k]
        m_rk = [jnp.where(incl, x[L:], 0.0) for x in m_k]
        rk_sum = head_sums([r[i] * k[i] * rk_ref[:, cols[i]] for i in I])
        bonus = [rk_sum[i] * v[i] for i in I]
        tinv = [eye_all + x for x in m_ab]
        p = [mm(x, bd(x)) for x in m_ab]
        m = 2
        while m < L:
            w_p = [bd(x) for x in p]
            m *= 2
            if m < L:
                both = [mm(jnp.concatenate([tinv[i], p[i]], axis=0), w_p[i]) for i in I]
                tinv = [tinv[i] + both[i][:L] for i in I]
                p = [x[L:] for x in both]
            else:
                tinv = [tinv[i] + mm(tinv[i], w_p[i]) for i in I]
        xm = [mm(jnp.concatenate([m_ak[i], m_rk[i]], axis=0), v_bd[i]) for i in I]
        u_v = [mm(tinv[i], bd(xm[i][:L])) for i in I]
        a_h = [mm(tinv[i], bd(a_t[i])) for i in I]
        r_h = [r_t[i] + mm(m_rb[i], bd(a_h[i])) for i in I]
        y_v = [mm(m_rb[i], bd(u_v[i])) + xm[i][L:] for i in I]
        p_t = [jnp.where(blockdiag, mm_tn(a_h[i].astype(bf16), b_e[i].astype(bf16)), 0.0).astype(bf16)
               for i in I]
        g_t = [jnp.where(blockdiag,
                         mm_tn(jnp.concatenate([u_v[i], v[i]], axis=0).astype(bf16),
                               jnp.concatenate([b_e[i], k_e[i]], axis=0).astype(bf16)), 0.0) for i in I]
        decay = [jnp.exp(x[L - 1:L, :]) for x in cs]
        for i in I:
            done.append(dict(hg=items[i][0], rows=rows[i], cols=cols[i], r_h=r_h[i], y_v=y_v[i],
                             p_t=p_t[i], g_t=g_t[i], decay=decay[i], bonus=bonus[i]))

    def chain_step(chs):
        ys = []
        for ch in chs:
            state = states[ch['hg']]
            state_b = state.astype(bf16)
            ys.append(mm_nt(ch['r_h'], state_b) + ch['y_v'])
            states[ch['hg']] = (state * ch['decay']
                                + jnp.dot(state_b, ch['p_t'], preferred_element_type=f32) + ch['g_t'])
        means = head_sums(ys)
        ycs = [y - m * (1.0 / RWKV_HEAD) for y, m in zip(ys, means)]
        vrs = head_sums([yc * yc for yc in ycs])
        for ch, yc, vr in zip(chs, ycs, vrs):
            rw, cl = ch['rows'], ch['cols']
            yn = yc * lax.rsqrt(vr * (1.0 / RWKV_HEAD) + GN_EPS) * lnw_ref[:, cl] + lnb_ref[:, cl]
            o_ref[rw, cl] = ((yn + ch['bonus']) * g_ref[rw, cl].astype(f32)).astype(o_ref.dtype)

    states = [state_ref[h] for h in range(n_hg)]
    done = []
    phase1([(h, c) for c in range(n_chunks) for h in range(n_hg)])
    for j in range(0, len(done), n_hg):
        chain_step(done[j:j + n_hg])
    for h in range(n_hg):
        state_ref[h] = states[h]


def rwkv_scan(proj, lw, k, kk, kka, g, ln_w, ln_b, r_k, cast_weights, *, batch, seq, rows, n_hg):
    nb = seq // rows
    T = batch * seq
    width = n_hg * HG_LANES
    n_col = RWKV_WIDTH // width
    n_steps = batch * n_col * nb
    r_blk = 0
    v_blk = 2 * RWKV_WIDTH // width
    blk = pl.BlockSpec((rows, width), lambda b, h, c: (b * nb + c, h))
    par = pl.BlockSpec((1, width), lambda b, h, c: (0, h))
    slabs = [w.reshape(n_steps, w.size // (n_steps * w.shape[-1]), w.shape[-1]) for w in cast_weights]
    slab_spec = lambda w: pl.BlockSpec((1,) + w.shape[1:], lambda b, h, c: ((b * n_col + h) * nb + c, 0, 0))
    outs = pl.pallas_call(
        functools.partial(_rwkv_scan_kernel, n_chunks=rows // RWKV_CHUNK, n_hg=n_hg,
                          n_cast=len(slabs)),
        out_shape=[jax.ShapeDtypeStruct((T, RWKV_WIDTH), bf16)]
                  + [jax.ShapeDtypeStruct(w.shape, bf16) for w in slabs],
        grid=(batch, n_col, nb),
        in_specs=[pl.BlockSpec((rows, width), lambda b, h, c: (b * nb + c, r_blk + h)),
                  pl.BlockSpec((rows, width), lambda b, h, c: (b * nb + c, v_blk + h))]
                 + [blk] * 5 + [par] * 3 + [slab_spec(w) for w in slabs],
        out_specs=[blk] + [slab_spec(w) for w in slabs],
        scratch_shapes=[pltpu.VMEM((n_hg, HG_LANES, HG_LANES), f32)],
        compiler_params=_cparams(("parallel", "parallel", "arbitrary")),
        name="rwkv_scan",
    )(proj, proj, lw, k, kk, kka, g, ln_w.reshape(1, -1), ln_b.reshape(1, -1), r_k.reshape(1, -1),
      *slabs)
    return outs[0], [o.reshape(w.shape) for o, w in zip(outs[1:], cast_weights)]


def _route(hn, w, b):
    w_hi = w.astype(bf16)
    w_lo = (w - w_hi.astype(f32)).astype(bf16)
    h_hi = hn.astype(bf16)
    h_lo = (hn - h_hi.astype(f32)).astype(bf16)
    hh = jnp.dot(h_hi, jnp.concatenate([w_hi, w_lo], axis=1), preferred_element_type=f32)
    logits = (hh[:, :ROUTER_LANES] + hh[:, ROUTER_LANES:]
              + jnp.dot(h_lo, w_hi, preferred_element_type=f32)) + b

    lane_i = lax.broadcasted_iota(jnp.int32, logits.shape, 1)
    lane = lane_i.astype(f32)
    neg = -jnp.inf
    big = float(ROUTER_LANES)
    is_g = lane_i < MOE_GROUPS
    gl = jnp.where(is_g, logits, neg)
    gmax = jnp.max(gl, axis=-1, keepdims=True)
    g_idx = jnp.min(jnp.where(gl == gmax, lane, big), axis=-1, keepdims=True)
    g_w = 1.0 / jnp.sum(jnp.exp(gl - gmax), axis=-1, keepdims=True)

    lo_lane = MOE_GROUPS + g_idx * MOE_EXPERTS_PER_GROUP
    sel = (lane >= lo_lane) & (lane < lo_lane + MOE_EXPERTS_PER_GROUP)
    el = jnp.where(sel, logits, neg)
    v1 = jnp.max(el, axis=-1, keepdims=True)
    i1 = jnp.min(jnp.where(el == v1, lane, big), axis=-1, keepdims=True)
    el2 = jnp.where(lane == i1, neg, el)
    v2 = jnp.max(el2, axis=-1, keepdims=True)
    i2 = jnp.min(jnp.where(el2 == v2, lane, big), axis=-1, keepdims=True)
    e2 = jnp.exp(v2 - v1)
    p1 = 1.0 / (1.0 + e2)
    p2 = e2 / (1.0 + e2)
    return (jnp.where(lane_i == 0, i1 - MOE_GROUPS, 0.0)
            + jnp.where(lane_i == 1, i2 - MOE_GROUPS, 0.0)
            + jnp.where(lane_i == 2, p1 * g_w, 0.0)
            + jnp.where(lane_i == 3, p2 * g_w, 0.0))


def _attn_oproj_router_kernel(q_ref, kv_ref, wo_ref, res_ref, g_ref, wr_ref, br_ref, h_ref, xr_ref,
                              rec_ref, *, tn):
    hd = XATTN_HEAD_DIM
    H = range(XATTN_HEADS)
    s = [lax.dot_general(q_ref[:, h * hd:(h + 1) * hd], kv_ref[:, h * hd:(h + 1) * hd],
                         (((1,), (1,)), ((), ())), preferred_element_type=f32) * (hd ** -0.5) for h in H]
    e = [jnp.exp(x - jnp.max(x, axis=-1, keepdims=True)) for x in s]
    l = [jnp.sum(x, axis=-1, keepdims=True) for x in e]
    pv = [jnp.dot(e[h].astype(bf16), kv_ref[:, (XATTN_HEADS + h) * hd:(XATTN_HEADS + h + 1) * hd],
                  preferred_element_type=f32) for h in H]
    o = jnp.concatenate([(pv[h] / l[h]).astype(bf16) for h in H], axis=1)
    for n0 in range(0, h_ref.shape[1], tn):
        h_ref[:, n0:n0 + tn] = res_ref[:, n0:n0 + tn] + jnp.dot(
            o, wo_ref[:, n0:n0 + tn], preferred_element_type=f32)
    hn = _rmsnorm_rows(h_ref[...], g_ref[...])
    D = hn.shape[1]
    rec = _route(hn, wr_ref[...], br_ref[...])
    xr_ref[:, :D] = hn
    xr_ref[:, D:] = rec
    rec_ref[...] = rec


def attn_oproj_router(q, kv, w_o, res, g, w_router, b_router, *, tm, tn, seq):
    T, D = res.shape
    tiles_per_seq = seq // tm
    row = lambda width: pl.BlockSpec((tm, width), lambda i: (i, 0))
    return pl.pallas_call(
        functools.partial(_attn_oproj_router_kernel, tn=tn),
        out_shape=[jax.ShapeDtypeStruct((T, D), f32),
                   jax.ShapeDtypeStruct((T, D + ROUTER_LANES), f32),
                   jax.ShapeDtypeStruct((T, ROUTER_LANES), f32)],
        grid=(T // tm,),
        in_specs=[row(D), pl.BlockSpec((MEM_LEN, 2 * D), lambda i: (i // tiles_per_seq, 0)),
                  _resident((D, D)), row(D), _resident((1, D)),
                  _resident((D, ROUTER_LANES)), _resident((1, ROUTER_LANES))],
        out_specs=[row(D), row(D + ROUTER_LANES), row(ROUTER_LANES)],
        compiler_params=_cparams(("parallel",)),
        name="attn_oproj_router",
    )(q, kv, w_o, res, g.reshape(1, D), w_router, b_router)


MOE_PAIRS = ((0, 1), (0, 2), (0, 3), (1, 3), (1, 2), (3, 2))
MOE_CLASSES = MOE_GROUPS * len(MOE_PAIRS)


def _route_metadata(rec, *, tm, n_tiles):
    ids = rec[:, 0:2].astype(jnp.int32)
    lo = jnp.minimum(ids[:, 0], ids[:, 1])
    hi = jnp.maximum(ids[:, 0], ids[:, 1])
    a, b = lo % MOE_EXPERTS_PER_GROUP, hi % MOE_EXPERTS_PER_GROUP
    pair_id = (a * (7 - a)) // 2 + b - a - 1
    pair_id = pair_id + (pair_id == 3).astype(jnp.int32) - (pair_id == 4).astype(jnp.int32)
    cls = (lo // MOE_EXPERTS_PER_GROUP) * len(MOE_PAIRS) + pair_id
    onehot = (cls[:, None] == jnp.arange(MOE_CLASSES, dtype=jnp.int32)[None, :]).astype(jnp.int32)
    csum = jnp.cumsum(onehot, axis=0)
    counts = csum[-1]
    rank = jnp.sum(csum * onehot, axis=1) - 1
    padded = ((counts + tm - 1) // tm) * tm
    ends = jnp.cumsum(padded)
    pos = jnp.sum(onehot * (ends - padded)[None, :], axis=1) + rank
    tile_start = jnp.arange(n_tiles, dtype=jnp.int32) * tm
    n_valid = ends[-1] // tm
    tile_cls = jnp.sum((tile_start[:, None] >= ends[None, :]).astype(jnp.int32), axis=1)
    last_cls = jnp.sum((ends[-1] - 1 >= ends).astype(jnp.int32))
    tile_cls = jnp.where(tile_start < ends[-1], tile_cls, last_cls)
    pair = jnp.asarray(MOE_PAIRS, jnp.int32)[tile_cls % len(MOE_PAIRS)]
    base = (tile_cls // len(MOE_PAIRS)) * MOE_EXPERTS_PER_GROUP
    idle = n_valid + jnp.arange(MOE_CLASSES, dtype=jnp.int32)
    fill_start = jnp.concatenate([jnp.maximum(ends - tm, 0), jnp.minimum(idle, n_tiles - 1) * tm])
    fill_on = jnp.concatenate([counts > 0, idle < n_tiles]).astype(jnp.int32)
    return pos, base + pair[:, 0], base + pair[:, 1], n_valid.reshape(1), fill_start, fill_on


def _moe_dispatch_kernel(pos_ref, fs_ref, fo_ref, x_ref, xs_hbm, stage, zbuf, sem, zsem, *, tm, tz):
    i = pl.program_id(0)
    n = pl.num_programs(0)

    def fill_copy(c):
        return pltpu.make_async_copy(zbuf, xs_hbm.at[pl.ds(pl.multiple_of(fs_ref[c], tz), tz), :], zsem)

    @pl.when(i == 0)
    def _():
        zbuf[...] = jnp.zeros_like(zbuf)
        for c in range(2 * MOE_CLASSES):
            @pl.when(fo_ref[c] > 0)
            def _():
                fill_copy(c).start()
        for c in range(2 * MOE_CLASSES):
            @pl.when(fo_ref[c] > 0)
            def _():
                fill_copy(c).wait()

    def tile_wait(slot):
        pltpu.make_async_copy(stage.at[slot], xs_hbm.at[pl.ds(0, tm), :], sem.at[slot]).wait()

    slot = i % 2

    @pl.when(i >= 2)
    def _():
        tile_wait(slot)

    stage[slot] = x_ref[...]

    def body(r, carry):
        p = pos_ref[i * tm + r]
        pltpu.make_async_copy(stage.at[slot, pl.ds(r, 1), :], xs_hbm.at[pl.ds(p, 1), :],
                              sem.at[slot]).start()
        return carry
    lax.fori_loop(0, tm, body, 0, unroll=8)

    @pl.when(i == n - 1)
    def _():
        tile_wait(slot)

        @pl.when(n >= 2)
        def _():
            tile_wait(1 - slot)


def moe_dispatch(xr, pos, fill_start, fill_on, *, tm, tz, n_slots):
    T, width = xr.shape
    grid_spec = pltpu.PrefetchScalarGridSpec(
        num_scalar_prefetch=3,
        grid=(T // tm,),
        in_specs=[pl.BlockSpec((tm, width), lambda i, pos, fs, fo: (i, 0))],
        out_specs=pl.BlockSpec(memory_space=pl.ANY),
        scratch_shapes=[pltpu.VMEM((2, tm, width), f32), pltpu.VMEM((tz, width), f32),
                        pltpu.SemaphoreType.DMA((2,)), pltpu.SemaphoreType.DMA(())],
    )
    return pl.pallas_call(
        functools.partial(_moe_dispatch_kernel, tm=tm, tz=tz),
        out_shape=jax.ShapeDtypeStruct((n_slots, width), f32),
        grid_spec=grid_spec,
        compiler_params=_cparams(("arbitrary",)),
        name="moe_dispatch",
    )(pos, fill_start, fill_on, xr)


def _moe_pair_kernel(ea_ref, eb_ref, nv_ref, xs_ref, wga, wua, wda, wgb, wub, wdb, o_ref):
    @pl.when(pl.program_id(0) >= nv_ref[0])
    def _():
        o_ref[...] = jnp.zeros_like(o_ref)

    @pl.when(pl.program_id(0) < nv_ref[0])
    def _():
        D = o_ref.shape[1]
        x = xs_ref[:, :D].astype(bf16)
        rec = xs_ref[:, D:]
        first = rec[:, 0:1] == ea_ref[pl.program_id(0)].astype(f32)
        w_a = jnp.where(first, rec[:, 2:3], rec[:, 3:4])
        w_b = jnp.where(first, rec[:, 3:4], rec[:, 2:3])

        gates = [jnp.dot(x, wg[0], preferred_element_type=f32) for wg in (wga, wgb)]
        ups = [jnp.dot(x, wu[0], preferred_element_type=f32) for wu in (wua, wub)]
        hids = [(g * _sigmoid(g) * u).astype(bf16) for g, u in zip(gates, ups)]
        y_a, y_b = [jnp.dot(h, wd[0], preferred_element_type=f32) for h, wd in zip(hids, (wda, wdb))]
        o_ref[...] = w_a * y_a + w_b * y_b


def moe_pair_experts(xs, e_a, e_b, n_valid, wg, wu, wd, *, tm, n_tiles):
    width = xs.shape[1]
    D = wg.shape[1]
    held = lambda i, ea, eb, nv: (jnp.minimum(i, nv[0] - 1), 0)
    tile = lambda i, ea, eb, nv: (i, 0)
    wa = lambda shape: pl.BlockSpec(shape, lambda i, ea, eb, nv: (ea[i], 0, 0))
    wb = lambda shape: pl.BlockSpec(shape, lambda i, ea, eb, nv: (eb[i], 0, 0))
    grid_spec = pltpu.PrefetchScalarGridSpec(
        num_scalar_prefetch=3,
        grid=(n_tiles,),
        in_specs=[pl.BlockSpec((tm, width), held),
                  wa((1, D, MOE_FF)), wa((1, D, MOE_FF)), wa((1, MOE_FF, D)),
                  wb((1, D, MOE_FF)), wb((1, D, MOE_FF)), wb((1, MOE_FF, D))],
        out_specs=pl.BlockSpec((tm, D), tile),
    )
    return pl.pallas_call(
        _moe_pair_kernel,
        out_shape=jax.ShapeDtypeStruct((n_tiles * tm, D), f32),
        grid_spec=grid_spec,
        compiler_params=_cparams(("arbitrary",)),
        name="moe_experts",
    )(e_a, e_b, n_valid, xs, wg, wu, wd, wg, wu, wd)


def _moe_combine_kernel(pos_ref, h_ref, y_hbm, gf_ref, o_ref, ybuf, sem, *, tm):
    i = pl.program_id(0)
    n = pl.num_programs(0)

    def gather(tile, slot):
        def body(r, carry):
            p = pos_ref[tile * tm + r]
            pltpu.make_async_copy(y_hbm.at[pl.ds(p, 1), :],
                                  ybuf.at[slot, pl.ds(r, 1), :], sem.at[slot]).start()
            return carry
        lax.fori_loop(0, tm, body, 0, unroll=8)

    @pl.when(i == 0)
    def _():
        gather(0, 0)

    @pl.when(i + 1 < n)
    def _():
        gather(i + 1, (i + 1) % 2)

    slot = i % 2
    pltpu.make_async_copy(y_hbm.at[pl.ds(0, tm), :], ybuf.at[slot], sem.at[slot]).wait()
    o_ref[...] = _rmsnorm_rows(h_ref[...] + ybuf[slot], gf_ref[...])


def moe_combine(h, y_sorted, pos, g_final, *, tm):
    T, D = h.shape
    grid_spec = pltpu.PrefetchScalarGridSpec(
        num_scalar_prefetch=1,
        grid=(T // tm,),
        in_specs=[
            pl.BlockSpec((tm, D), lambda i, pos: (i, 0)),
            pl.BlockSpec(memory_space=pl.ANY),
            pl.BlockSpec((1, D), lambda i, pos: (0, 0)),
        ],
        out_specs=pl.BlockSpec((tm, D), lambda i, pos: (i, 0)),
        scratch_shapes=[pltpu.VMEM((2, tm, D), f32), pltpu.SemaphoreType.DMA((2,))],
    )
    return pl.pallas_call(
        functools.partial(_moe_combine_kernel, tm=tm),
        out_shape=jax.ShapeDtypeStruct((T, D), f32),
        grid_spec=grid_spec,
        compiler_params=_cparams(("arbitrary",)),
        name="moe_combine",
    )(pos, h, y_sorted, g_final.reshape(1, D))


def _pad_rows(w, rows):
    return jnp.pad(w, ((0, rows - w.shape[0]), (0, 0)))


def _pack_lora_cols(t):
    o1, o2 = DECAY_LORA, DECAY_LORA + AAA_LORA
    pad = lambda a, n: jnp.pad(a, [(0, 0)] * (a.ndim - 1) + [(0, n - a.shape[-1])])
    return jnp.concatenate([pad(t[..., :o1], 128), pad(t[..., o1:o2], 128),
                            pad(t[..., o2:], 256)], axis=-1)


def _layer(x2, mem2, p, *, batch, seq, cfg):
    D = D_MODEL
    w_in = p['w_in']
    w_main = w_in.astype(bf16)
    w_lora = _pack_lora_cols(w_in[:, MAIN_COLS:]).astype(bf16)
    mu_cols = jnp.concatenate([jnp.zeros((POOL_WIDTH,), f32), p['rwkv_mu'][:3 * RWKV_WIDTH],
                               _pack_lora_cols(p['rwkv_mu'][3 * RWKV_WIDTH:])])

    proj, pool_out = in_proj_mixers(x2, p['norm_mix_g'], w_main, w_lora, mu_cols,
                                    p['pool_w'].astype(bf16), p['pool_scale'],
                                    tm=cfg['tm'], tn=cfg['tn_in'], seq=seq)
    lw, k, kk, kka, g = rwkv_prep(
        proj, p['rwkv_w0'], _pad_rows(p['rwkv_w2'], 128).astype(bf16),
        p['rwkv_a0'], _pad_rows(p['rwkv_a2'], 128).astype(bf16),
        _pad_rows(p['rwkv_g2'], 256).astype(bf16), p['rwkv_k_k'], p['rwkv_k_a'], ts=cfg['ts_prep'])
    rwkv_out, (w_out, w_q, w_kv, w_o, wg, wu, wd) = rwkv_scan(
        proj, lw, k, kk, kka, g, p['rwkv_ln_w'], p['rwkv_ln_b'], p['rwkv_r_k'],
        [p['w_out'], p['xattn_w_q'], p['xattn_w_kv'], p['xattn_w_o'],
         p['moe_w_gate'], p['moe_w_up'], p['moe_w_down']],
        batch=batch, seq=seq, rows=cfg['scan_rows'], n_hg=cfg['scan_hg'])
    h1 = matmul_residual_resident([pool_out, rwkv_out], [w_out[:POOL_WIDTH], w_out[POOL_WIDTH:]],
                                  x2, tm=cfg['tm'], tn=cfg['tn'], name="out_proj")

    q = norm_matmul_resident(h1, p['norm_xattn_g'], w_q,
                             tm=cfg['tm_q'], tn=cfg['tn'], name="q_proj")
    kv = norm_matmul_resident(mem2, p['norm_mem_g'], w_kv,
                              tm=min(cfg['tm'], mem2.shape[0]), tn=cfg['tn'], name="kv_proj")

    w_router = jnp.pad(jnp.concatenate([p['moe_w_group'], p['moe_w_expert']], axis=1),
                       ((0, 0), (0, ROUTER_LANES - MOE_GROUPS - MOE_EXPERTS)))
    b_router = jnp.pad(jnp.concatenate([p['moe_b_group'], p['moe_b_expert']]),
                       (0, ROUTER_LANES - MOE_GROUPS - MOE_EXPERTS)).reshape(1, ROUTER_LANES)
    h2, xr, rec = attn_oproj_router(q, kv, w_o, h1, p['norm_ffn_g'],
                               w_router, b_router, tm=cfg['tm'], tn=cfg['tn'], seq=seq)
    tm_e = cfg['tm_moe']
    n_tiles = xr.shape[0] // tm_e + MOE_CLASSES
    pos, e_a, e_b, n_valid, fill_start, fill_on = _route_metadata(
        rec, tm=tm_e, n_tiles=n_tiles)
    xs = moe_dispatch(xr, pos, fill_start, fill_on, tm=cfg['tm_disp'], tz=tm_e,
                      n_slots=n_tiles * tm_e)
    y_sorted = moe_pair_experts(xs, e_a, e_b, n_valid, wg, wu, wd, tm=tm_e, n_tiles=n_tiles)
    return moe_combine(h2, y_sorted, pos, p['norm_final_g'], tm=cfg['tm_comb'])


_CFG = dict(tm=512, tm_q=1024, tn=1024, tn_in=512, ts_prep=512, scan_rows=256, scan_hg=4,
            tm_moe=256, tm_disp=512, tm_comb=512)


def kernel(x, mem, norm_mix_g, w_in, pool_w, pool_scale, rwkv_mu, rwkv_w0, rwkv_w2, rwkv_a0,
           rwkv_a2, rwkv_g2, rwkv_k_k, rwkv_k_a, rwkv_r_k, rwkv_ln_w, rwkv_ln_b, w_out,
           norm_xattn_g, norm_mem_g, xattn_w_q, xattn_w_kv, xattn_w_o, norm_ffn_g,
           moe_w_group, moe_b_group, moe_w_expert, moe_b_expert, moe_w_gate, moe_w_up,
           moe_w_down, norm_final_g):
    batch, seq, D = x.shape
    p = dict(norm_mix_g=norm_mix_g[0], w_in=w_in[0], pool_w=pool_w[0], pool_scale=pool_scale[0],
             rwkv_mu=rwkv_mu[0], rwkv_w0=rwkv_w0[0], rwkv_w2=rwkv_w2[0], rwkv_a0=rwkv_a0[0],
             rwkv_a2=rwkv_a2[0], rwkv_g2=rwkv_g2[0], rwkv_k_k=rwkv_k_k[0], rwkv_k_a=rwkv_k_a[0],
             rwkv_r_k=rwkv_r_k[0], rwkv_ln_w=rwkv_ln_w[0], rwkv_ln_b=rwkv_ln_b[0], w_out=w_out[0],
             norm_xattn_g=norm_xattn_g[0], norm_mem_g=norm_mem_g[0], xattn_w_q=xattn_w_q[0],
             xattn_w_kv=xattn_w_kv[0], xattn_w_o=xattn_w_o[0], norm_ffn_g=norm_ffn_g[0],
             moe_w_group=moe_w_group[0], moe_b_group=moe_b_group[0], moe_w_expert=moe_w_expert[0],
             moe_b_expert=moe_b_expert[0], moe_w_gate=moe_w_gate[0], moe_w_up=moe_w_up[0],
             moe_w_down=moe_w_down[0], norm_final_g=norm_final_g)
    out = _layer(x.reshape(batch * seq, D), mem.reshape(batch * MEM_LEN, D), p,
                 batch=batch, seq=seq, cfg=_CFG)
    return out.reshape(batch, seq, D)
```

```python
import functools

import jax
import jax.numpy as jnp
from jax import lax
from jax.experimental import pallas as pl
from jax.experimental.pallas import tpu as pltpu

f32 = jnp.float32
bf16 = jnp.bfloat16

D_MODEL = 2048
MEM_LEN = 256
NORM_EPS = 1e-6

POOL_WIDTH = 1024
POOL_WINDOWS = (2, 4, 8, 16)
POOL_GROUP = 256
POOL_HALO = 16
RWKV_WIDTH = 1024
RWKV_HEAD = 64
RWKV_HEADS = 16
GN_EPS = 64e-5
DECAY_SCALE = 0.6065306597126334
DECAY_LORA = 64
AAA_LORA = 64
GATE_LORA = 160
LORA_PAD = 512
LORA_W_OFF, LORA_A_OFF, LORA_G_OFF = 0, 128, 256
MAIN_COLS = POOL_WIDTH + 3 * RWKV_WIDTH
PROJ_COLS = MAIN_COLS + LORA_PAD

RWKV_CHUNK = 64
HEADS_PER_STEP = 4
HG_LANES = HEADS_PER_STEP * RWKV_HEAD

XATTN_HEADS = 4
XATTN_HEAD_DIM = 512

MOE_GROUPS = 4
MOE_EXPERTS_PER_GROUP = 4
MOE_EXPERTS = 16
MOE_FF = 512
ROUTER_LANES = 128

VMEM_LIMIT = 56 * 1024 * 1024


def _cparams(sem):
    return pltpu.CompilerParams(dimension_semantics=sem, vmem_limit_bytes=VMEM_LIMIT)


def _split_dot(x, w_bf16):
    hi = x.astype(bf16)
    lo = (x - hi.astype(f32)).astype(bf16)
    return (jnp.dot(hi, w_bf16, preferred_element_type=f32)
            + jnp.dot(lo, w_bf16, preferred_element_type=f32))


def _resident(shape):
    return pl.BlockSpec(shape, lambda i: (0,) * len(shape), pipeline_mode=pl.Buffered(1))


def _rmsnorm_rows(x, g):
    ms = jnp.mean(x * x, axis=-1, keepdims=True)
    return x * lax.rsqrt(ms + NORM_EPS) * g


def _norm_mm_res_kernel(x_ref, g_ref, w_ref, o_ref, *, tn):
    xn = _rmsnorm_rows(x_ref[...], g_ref[...]).astype(bf16)
    for n0 in range(0, o_ref.shape[1], tn):
        o_ref[:, n0:n0 + tn] = jnp.dot(xn, w_ref[:, n0:n0 + tn],
                                       preferred_element_type=f32).astype(o_ref.dtype)


def norm_matmul_resident(x, g, w, *, tm, tn, name):
    M, K = x.shape
    N = w.shape[1]
    return pl.pallas_call(
        functools.partial(_norm_mm_res_kernel, tn=tn),
        out_shape=jax.ShapeDtypeStruct((M, N), bf16),
        grid=(M // tm,),
        in_specs=[pl.BlockSpec((tm, K), lambda i: (i, 0)), _resident((1, K)), _resident((K, N))],
        out_specs=pl.BlockSpec((tm, N), lambda i: (i, 0)),
        compiler_params=_cparams(("parallel",)),
        name=name,
    )(x, g.reshape(1, K), w)


def _head_ones():
    r = lax.broadcasted_iota(jnp.int32, (HG_LANES, HG_LANES), 0) // RWKV_HEAD
    c = lax.broadcasted_iota(jnp.int32, (HG_LANES, HG_LANES), 1) // RWKV_HEAD
    return jnp.where(r == c, 1.0, 0.0).astype(bf16)


def _head_sum(x, ones_bd):
    parts = [_split_dot(x[:, c:c + HG_LANES], ones_bd) for c in range(0, x.shape[1], HG_LANES)]
    return parts[0] if len(parts) == 1 else jnp.concatenate(parts, axis=1)


def _sigmoid(x):
    return 1.0 / (1.0 + jnp.exp(-x))


def _in_proj_kernel(x_ref, xh_ref, g_ref, w_ref, wl_ref, mu_ref, pw_ref, ps_ref,
                    w0_ref, w2_ref, a0_ref, a2_ref, g2_ref, kk_ref, ka_ref,
                    rv_ref, pool_ref, lw_out, k_out, kk_out, kka_out, g_out, *, tn, tiles_per_seq):
    tm = x_ref.shape[0]
    C = RWKV_WIDTH
    tile_in_seq = pl.program_id(0) % tiles_per_seq
    first = tile_in_seq == 0
    xn = _rmsnorm_rows(x_ref[...], g_ref[...]).astype(bf16)
    xh = jnp.where(first, 0.0, _rmsnorm_rows(xh_ref[...], g_ref[...])).astype(bf16)
    lhs = jnp.concatenate([xh, xn], axis=0)
    pos = tile_in_seq * tm + lax.broadcasted_iota(jnp.int32, (tm, 1), 0)
    ones_bd = _head_ones()
    lora = {}

    def slab(n0):
        w = w_ref[:, n0:n0 + tn] if n0 < MAIN_COLS else wl_ref[:, n0 - MAIN_COLS:n0 - MAIN_COLS + tn]
        return jnp.dot(lhs, w, preferred_element_type=f32)

    def shifted(n0, res):
        p = res[POOL_HALO:]
        prev = pltpu.roll(res, 1, 0)[POOL_HALO:]
        return p + (prev - p) * mu_ref[:, n0:n0 + tn]

    def pool_slab(n0, res):
        p = res[POOL_HALO:]
        acc = res
        shift = 1
        for gi in range(n0 // POOL_GROUP, (n0 + tn) // POOL_GROUP):
            win = POOL_WINDOWS[gi]
            while shift < win:
                acc = acc + pltpu.roll(acc, shift, 0)
                shift *= 2
            c0 = gi * POOL_GROUP - n0
            cnt = jnp.minimum(pos + 1, win).astype(f32)
            pooled = acc[POOL_HALO:, c0:c0 + POOL_GROUP] / cnt - p[:, c0:c0 + POOL_GROUP]
            mixed = jnp.dot(pooled.astype(bf16), pw_ref[gi], preferred_element_type=f32)
            lo, hi = gi * POOL_GROUP, (gi + 1) * POOL_GROUP
            pool_ref[:, lo:hi] = (mixed * ps_ref[:, lo:hi]).astype(pool_ref.dtype)

    def lora_slab(n0, res):
        lo = shifted(n0, res)
        lora['tanh_w'] = jnp.tanh(lo[:, LORA_W_OFF:LORA_W_OFF + 128]).astype(bf16)
        lora['a_lo'] = lo[:, LORA_A_OFF:LORA_A_OFF + 128].astype(bf16)
        lora['sig_g'] = _sigmoid(lo[:, LORA_G_OFF:LORA_G_OFF + 256]).astype(bf16)

    def decay_gate(c0):
        cs = slice(c0, c0 + tn)
        wx = w0_ref[:, cs] + jnp.dot(lora['tanh_w'], w2_ref[:, cs], preferred_element_type=f32)
        lw_out[:, cs] = -DECAY_SCALE * _sigmoid(wx)
        g_out[:, cs] = jnp.dot(lora['sig_g'], g2_ref[:, cs], preferred_element_type=f32).astype(g_out.dtype)

    def key_slab(n0, res):
        cs = slice(n0 - POOL_WIDTH - C, n0 - POOL_WIDTH - C + tn)
        k = shifted(n0, res)
        a = _sigmoid(a0_ref[:, cs] + jnp.dot(lora['a_lo'], a2_ref[:, cs], preferred_element_type=f32))
        kk = k * kk_ref[:, cs]
        kk = kk * lax.rsqrt(jnp.maximum(_head_sum(kk * kk, ones_bd), 1e-24))
        k_out[:, cs] = (k * (1.0 + (a - 1.0) * ka_ref[:, cs])).astype(k_out.dtype)
        kk_out[:, cs] = kk.astype(kk_out.dtype)
        kka_out[:, cs] = (kk * a).astype(kka_out.dtype)

    def store_slab(n0, res, c0):
        rv_ref[:, c0:c0 + tn] = shifted(n0, res).astype(rv_ref.dtype)

    r0, k0, v0 = POOL_WIDTH, POOL_WIDTH + C, POOL_WIDTH + 2 * C
    plan = [(MAIN_COLS, [lora_slab])]
    plan += [(n0, [pool_slab]) for n0 in range(0, POOL_WIDTH, tn)]
    plan += [(n0, [functools.partial(store_slab, c0=n0 - r0)]) for n0 in range(r0, k0, tn)]
    plan += [(n0, [key_slab]) for n0 in range(k0, v0, tn)]
    plan += [(n0, [functools.partial(store_slab, c0=C + n0 - v0)]) for n0 in range(v0, MAIN_COLS, tn)]
    for j, c0 in enumerate(range(0, C, tn)):
        plan[1 + j][1].append(lambda n0, res, c0=c0: decay_gate(c0))

    pending = []
    for n0, epilogues in plan:
        res = slab(n0)
        for fn in pending:
            fn()
        pending = [functools.partial(fn, n0, res) for fn in epilogues]
    for fn in pending:
        fn()


def in_proj_mixers(x, g, w_main, w_lora, mu_cols, pool_w, pool_scale,
                   w0, w2p, a0, a2p, g2p, k_k, k_a, *, tm, tn, seq):
    M, K = x.shape
    C = RWKV_WIDTH
    hb = tm // POOL_HALO
    row = lambda t: t.reshape(1, -1)
    out_sd = lambda width, dt: jax.ShapeDtypeStruct((M, width), dt)
    out_spec = lambda width: pl.BlockSpec((tm, width), lambda i: (i, 0))
    return pl.pallas_call(
        functools.partial(_in_proj_kernel, tn=tn, tiles_per_seq=seq // tm),
        out_shape=[out_sd(2 * C, bf16), out_sd(POOL_WIDTH, bf16), out_sd(C, f32),
                   out_sd(C, bf16), out_sd(C, bf16), out_sd(C, bf16), out_sd(C, bf16)],
        grid=(M // tm,),
        in_specs=[pl.BlockSpec((tm, K), lambda i: (i, 0)),
                  pl.BlockSpec((POOL_HALO, K), lambda i: (jnp.maximum(i * hb - 1, 0), 0)),
                  _resident((1, K)), _resident(w_main.shape), _resident(w_lora.shape),
                  _resident((1, mu_cols.shape[0])), _resident(pool_w.shape),
                  _resident((1, POOL_WIDTH)),
                  _resident((1, C)), _resident(w2p.shape), _resident((1, C)), _resident(a2p.shape),
                  _resident(g2p.shape), _resident((1, C)), _resident((1, C))],
        out_specs=[out_spec(2 * C), out_spec(POOL_WIDTH)] + [out_spec(C)] * 5,
        compiler_params=_cparams(("parallel",)),
        name="in_proj",
    )(x, x, row(g), w_main, w_lora, row(mu_cols), pool_w, row(pool_scale),
      row(w0), w2p, row(a0), a2p, g2p, row(k_k), row(k_a))


def _mm_res_resident_kernel(*refs, n, tn):
    a_refs, w_refs, res_ref, o_ref = refs[:n], refs[n:2 * n], refs[2 * n], refs[2 * n + 1]
    for n0 in range(0, o_ref.shape[1], tn):
        acc = res_ref[:, n0:n0 + tn]
        for a_ref, w_ref in zip(a_refs, w_refs):
            acc = acc + jnp.dot(a_ref[...], w_ref[:, n0:n0 + tn], preferred_element_type=f32)
        o_ref[:, n0:n0 + tn] = acc


def matmul_residual_resident(a_list, w_list, res, *, tm, tn, name):
    M, N = res.shape
    n = len(a_list)
    return pl.pallas_call(
        functools.partial(_mm_res_resident_kernel, n=n, tn=tn),
        out_shape=jax.ShapeDtypeStruct((M, N), f32),
        grid=(M // tm,),
        in_specs=([pl.BlockSpec((tm, a.shape[1]), lambda i: (i, 0)) for a in a_list]
                  + [_resident(w.shape) for w in w_list]
                  + [pl.BlockSpec((tm, N), lambda i: (i, 0))]),
        out_specs=pl.BlockSpec((tm, N), lambda i: (i, 0)),
        compiler_params=_cparams(("parallel",)),
        name=name,
    )(*a_list, *w_list, res)


def _bd_mask():
    r = lax.broadcasted_iota(jnp.int32, (HG_LANES, HG_LANES), 0) // RWKV_CHUNK
    c = lax.broadcasted_iota(jnp.int32, (HG_LANES, HG_LANES), 1) // RWKV_HEAD
    return r == c


def _rwkv_scan_kernel(r_ref, v_ref, lw_ref, k_ref, kk_ref, kka_ref, g_ref,
                      lnw_ref, lnb_ref, rk_ref, *rest, n_chunks, n_hg, n_cast):
    L = RWKV_CHUNK
    W = HG_LANES
    cast_in, o_ref, cast_out, state_ref = rest[:n_cast], rest[n_cast], rest[n_cast + 1:-1], rest[-1]
    for w_ref, wb_ref in zip(cast_in, cast_out):
        wb_ref[...] = w_ref[...].astype(bf16)

    @pl.when(pl.program_id(2) == 0)
    def _():
        state_ref[...] = jnp.zeros_like(state_ref)

    bd_mask = _bd_mask()
    ones_bd = _head_ones()

    def bd(x):
        tiled = jnp.concatenate([x] * HEADS_PER_STEP, axis=0)
        return jnp.where(bd_mask, tiled, 0.0).astype(bf16)

    def bd_t(x):
        tiled = jnp.concatenate([x] * HEADS_PER_STEP, axis=0)
        return jnp.where(bd_mask, tiled, 0.0).T.astype(bf16)

    def mm(a, b_bf16):
        return jnp.dot(a.astype(bf16), b_bf16, preferred_element_type=f32)

    def mm_nt(a, b_bf16):
        return lax.dot_general(a.astype(bf16), b_bf16, (((1,), (1,)), ((), ())),
                               preferred_element_type=f32)

    def mm_tn(a_bf16, b_bf16):
        return lax.dot_general(a_bf16, b_bf16, (((0,), (0,)), ((), ())),
                               preferred_element_type=f32)

    def split(x):
        hi = x.astype(bf16)
        return hi, (x - hi.astype(f32)).astype(bf16)

    def head_sums(xs):
        tot = mm(jnp.concatenate(xs, axis=0), ones_bd)
        return [tot[j * L:(j + 1) * L] for j in range(len(xs))]

    t_idx = lax.broadcasted_iota(jnp.int32, (L, W), 0)
    s_idx = lax.broadcasted_iota(jnp.int32, (L, W), 1) % L
    strict = t_idx > s_idx
    incl = t_idx >= s_idx
    eye_all = jnp.where(t_idx == s_idx, 1.0, 0.0)
    tri = jnp.where(lax.broadcasted_iota(jnp.int32, (L, L), 0)
                    >= lax.broadcasted_iota(jnp.int32, (L, L), 1), 1.0, 0.0).astype(bf16)
    blockdiag = (lax.broadcasted_iota(jnp.int32, (W, W), 0) // RWKV_HEAD
                 == lax.broadcasted_iota(jnp.int32, (W, W), 1) // RWKV_HEAD)

    def phase1(items):
        I = range(len(items))
        rows = [pl.ds(c * L, L) for _, c in items]
        cols = [pl.ds(h * W, W) for h, _ in items]
        ld = lambda ref, i: ref[rows[i], cols[i]]
        r = [ld(r_ref, i).astype(f32) for i in I]
        k = [ld(k_ref, i).astype(f32) for i in I]
        v = [ld(v_ref, i).astype(f32) for i in I]
        lw = [ld(lw_ref, i) for i in I]
        lw_s = [split(x) for x in lw]
        cs = [jnp.dot(tri, hi, preferred_element_type=f32) + jnp.dot(tri, lo, preferred_element_type=f32)
              for hi, lo in lw_s]
        g_inv = [jnp.exp(-x) for x in cs]
        to_end = [jnp.exp(x[L - 1:L, :] - x) for x in cs]
        a_t = [-ld(kk_ref, i).astype(f32) * jnp.exp(cs[i] - lw[i]) for i in I]
        r_t = [r[i] * jnp.exp(cs[i]) for i in I]
        kka = [ld(kka_ref, i).astype(f32) for i in I]
        b_t = [kka[i] * g_inv[i] for i in I]
        k_t = [k[i] * g_inv[i] for i in I]
        b_e = [kka[i] * to_end[i] for i in I]
        k_e = [k[i] * to_end[i] for i in I]
        v_bd = [bd(x) for x in v]
        ar = [jnp.concatenate([a_t[i], r_t[i]], axis=0) for i in I]
        m_b = [mm(ar[i], bd_t(b_t[i])) for i in I]
        m_k = [mm(ar[i], bd_t(k_t[i])) for i in I]
        m_ab = [jnp.where(strict, x[:L], 0.0) for x in m_b]
        m_rb = [jnp.where(incl, x[L:], 0.0) for x in m_b]
        m_ak = [jnp.where(strict, x[:L], 0.0) for x in m_k]
        m_rk = [jnp.where(incl, x[L:], 0.0) for x in m_k]
        rk_sum = head_sums([r[i] * k[i] * rk_ref[:, cols[i]] for i in I])
        bonus = [rk_sum[i] * v[i] for i in I]
        tinv = [eye_all + x for x in m_ab]
        p = [mm(x, bd(x)) for x in m_ab]
        m = 2
        while m < L:
            w_p = [bd(x) for x in p]
            m *= 2
            if m < L:
                both = [mm(jnp.concatenate([tinv[i], p[i]], axis=0), w_p[i]) for i in I]
                tinv = [tinv[i] + both[i][:L] for i in I]
                p = [x[L:] for x in both]
            else:
                tinv = [tinv[i] + mm(tinv[i], w_p[i]) for i in I]
        xm = [mm(jnp.concatenate([m_ak[i], m_rk[i]], axis=0), v_bd[i]) for i in I]
        u_v = [mm(tinv[i], bd(xm[i][:L])) for i in I]
        a_h = [mm(tinv[i], bd(a_t[i])) for i in I]
        r_h = [r_t[i] + mm(m_rb[i], bd(a_h[i])) for i in I]
        y_v = [mm(m_rb[i], bd(u_v[i])) + xm[i][L:] for i in I]
        p_t = [jnp.where(blockdiag, mm_tn(a_h[i].astype(bf16), b_e[i].astype(bf16)), 0.0).astype(bf16)
               for i in I]
        g_t = [jnp.where(blockdiag,
                         mm_tn(jnp.concatenate([u_v[i], v[i]], axis=0).astype(bf16),
                               jnp.concatenate([b_e[i], k_e[i]], axis=0).astype(bf16)), 0.0) for i in I]
        decay = [jnp.exp(x[L - 1:L, :]) for x in cs]
        for i in I:
            done.append(dict(hg=items[i][0], rows=rows[i], cols=cols[i], r_h=r_h[i], y_v=y_v[i],
                             p_t=p_t[i], g_t=g_t[i], decay=decay[i], bonus=bonus[i]))

    def chain_step(chs):
        ys = []
        for ch in chs:
            state = states[ch['hg']]
            state_b = state.astype(bf16)
            ys.append(mm_nt(ch['r_h'], state_b) + ch['y_v'])
            states[ch['hg']] = (state * ch['decay']
                                + jnp.dot(state_b, ch['p_t'], preferred_element_type=f32) + ch['g_t'])
        means = head_sums(ys)
        ycs = [y - m * (1.0 / RWKV_HEAD) for y, m in zip(ys, means)]
        vrs = head_sums([yc * yc for yc in ycs])
        for ch, yc, vr in zip(chs, ycs, vrs):
            rw, cl = ch['rows'], ch['cols']
            yn = yc * lax.rsqrt(vr * (1.0 / RWKV_HEAD) + GN_EPS) * lnw_ref[:, cl] + lnb_ref[:, cl]
            o_ref[rw, cl] = ((yn + ch['bonus']) * g_ref[rw, cl].astype(f32)).astype(o_ref.dtype)

    states = [state_ref[h] for h in range(n_hg)]
    done = []
    phase1([(h, c) for c in range(n_chunks) for h in range(n_hg)])
    for j in range(0, len(done), n_hg):
        chain_step(done[j:j + n_hg])
    for h in range(n_hg):
        state_ref[h] = states[h]


def rwkv_scan(proj, lw, k, kk, kka, g, ln_w, ln_b, r_k, cast_weights, *, batch, seq, rows, n_hg):
    nb = seq // rows
    T = batch * seq
    width = n_hg * HG_LANES
    n_col = RWKV_WIDTH // width
    n_steps = batch * n_col * nb
    r_blk = 0
    v_blk = RWKV_WIDTH // width
    blk = pl.BlockSpec((rows, width), lambda b, h, c: (b * nb + c, h))
    par = pl.BlockSpec((1, width), lambda b, h, c: (0, h))
    slabs = [w.reshape(n_steps, w.size // (n_steps * w.shape[-1]), w.shape[-1]) for w in cast_weights]
    slab_spec = lambda w: pl.BlockSpec((1,) + w.shape[1:], lambda b, h, c: ((b * n_col + h) * nb + c, 0, 0))
    outs = pl.pallas_call(
        functools.partial(_rwkv_scan_kernel, n_chunks=rows // RWKV_CHUNK, n_hg=n_hg,
                          n_cast=len(slabs)),
        out_shape=[jax.ShapeDtypeStruct((T, RWKV_WIDTH), bf16)]
                  + [jax.ShapeDtypeStruct(w.shape, bf16) for w in slabs],
        grid=(batch, n_col, nb),
        in_specs=[pl.BlockSpec((rows, width), lambda b, h, c: (b * nb + c, r_blk + h)),
                  pl.BlockSpec((rows, width), lambda b, h, c: (b * nb + c, v_blk + h))]
                 + [blk] * 5 + [par] * 3 + [slab_spec(w) for w in slabs],
        out_specs=[blk] + [slab_spec(w) for w in slabs],
        scratch_shapes=[pltpu.VMEM((n_hg, HG_LANES, HG_LANES), f32)],
        compiler_params=_cparams(("parallel", "parallel", "arbitrary")),
        name="rwkv_scan",
    )(proj, proj, lw, k, kk, kka, g, ln_w.reshape(1, -1), ln_b.reshape(1, -1), r_k.reshape(1, -1),
      *slabs)
    return outs[0], [o.reshape(w.shape) for o, w in zip(outs[1:], cast_weights)]


def _route(hn, w, b):
    w_hi = w.astype(bf16)
    w_lo = (w - w_hi.astype(f32)).astype(bf16)
    h_hi = hn.astype(bf16)
    h_lo = (hn - h_hi.astype(f32)).astype(bf16)
    hh = jnp.dot(h_hi, jnp.concatenate([w_hi, w_lo], axis=1), preferred_element_type=f32)
    logits = (hh[:, :ROUTER_LANES] + hh[:, ROUTER_LANES:]
              + jnp.dot(h_lo, w_hi, preferred_element_type=f32)) + b

    lane_i = lax.broadcasted_iota(jnp.int32, logits.shape, 1)
    lane = lane_i.astype(f32)
    neg = -jnp.inf
    big = float(ROUTER_LANES)
    is_g = lane_i < MOE_GROUPS
    gl = jnp.where(is_g, logits, neg)
    gmax = jnp.max(gl, axis=-1, keepdims=True)
    g_idx = jnp.min(jnp.where(gl == gmax, lane, big), axis=-1, keepdims=True)
    g_w = 1.0 / jnp.sum(jnp.exp(gl - gmax), axis=-1, keepdims=True)

    lo_lane = MOE_GROUPS + g_idx * MOE_EXPERTS_PER_GROUP
    sel = (lane >= lo_lane) & (lane < lo_lane + MOE_EXPERTS_PER_GROUP)
    el = jnp.where(sel, logits, neg)
    v1 = jnp.max(el, axis=-1, keepdims=True)
    i1 = jnp.min(jnp.where(el == v1, lane, big), axis=-1, keepdims=True)
    el2 = jnp.where(lane == i1, neg, el)
    v2 = jnp.max(el2, axis=-1, keepdims=True)
    i2 = jnp.min(jnp.where(el2 == v2, lane, big), axis=-1, keepdims=True)
    e2 = jnp.exp(v2 - v1)
    p1 = 1.0 / (1.0 + e2)
    p2 = e2 / (1.0 + e2)
    return (jnp.where(lane_i == 0, i1 - MOE_GROUPS, 0.0)
            + jnp.where(lane_i == 1, i2 - MOE_GROUPS, 0.0)
            + jnp.where(lane_i == 2, p1 * g_w, 0.0)
            + jnp.where(lane_i == 3, p2 * g_w, 0.0))


def _attn_oproj_router_kernel(q_ref, kv_ref, wo_ref, res_ref, g_ref, wr_ref, br_ref, h_ref, xr_ref,
                              rec_ref, *, tn):
    hd = XATTN_HEAD_DIM
    H = range(XATTN_HEADS)
    s = [lax.dot_general(q_ref[:, h * hd:(h + 1) * hd], kv_ref[:, h * hd:(h + 1) * hd],
                         (((1,), (1,)), ((), ())), preferred_element_type=f32) * (hd ** -0.5) for h in H]
    e = [jnp.exp(x - jnp.max(x, axis=-1, keepdims=True)) for x in s]
    l = [jnp.sum(x, axis=-1, keepdims=True) for x in e]
    pv = [jnp.dot(e[h].astype(bf16), kv_ref[:, (XATTN_HEADS + h) * hd:(XATTN_HEADS + h + 1) * hd],
                  preferred_element_type=f32) for h in H]
    o = jnp.concatenate([(pv[h] / l[h]).astype(bf16) for h in H], axis=1)
    for n0 in range(0, h_ref.shape[1], tn):
        h_ref[:, n0:n0 + tn] = res_ref[:, n0:n0 + tn] + jnp.dot(
            o, wo_ref[:, n0:n0 + tn], preferred_element_type=f32)
    hn = _rmsnorm_rows(h_ref[...], g_ref[...])
    D = hn.shape[1]
    rec = _route(hn, wr_ref[...], br_ref[...])
    xr_ref[:, :D] = hn
    xr_ref[:, D:] = rec
    rec_ref[...] = rec


def attn_oproj_router(q, kv, w_o, res, g, w_router, b_router, *, tm, tn, seq):
    T, D = res.shape
    tiles_per_seq = seq // tm
    row = lambda width: pl.BlockSpec((tm, width), lambda i: (i, 0))
    return pl.pallas_call(
        functools.partial(_attn_oproj_router_kernel, tn=tn),
        out_shape=[jax.ShapeDtypeStruct((T, D), f32),
                   jax.ShapeDtypeStruct((T, D + ROUTER_LANES), f32),
                   jax.ShapeDtypeStruct((T, ROUTER_LANES), f32)],
        grid=(T // tm,),
        in_specs=[row(D), pl.BlockSpec((MEM_LEN, 2 * D), lambda i: (i // tiles_per_seq, 0)),
                  _resident((D, D)), row(D), _resident((1, D)),
                  _resident((D, ROUTER_LANES)), _resident((1, ROUTER_LANES))],
        out_specs=[row(D), row(D + ROUTER_LANES), row(ROUTER_LANES)],
        compiler_params=_cparams(("parallel",)),
        name="attn_oproj_router",
    )(q, kv, w_o, res, g.reshape(1, D), w_router, b_router)


MOE_PAIRS = ((0, 1), (0, 2), (0, 3), (1, 3), (1, 2), (3, 2))
MOE_CLASSES = MOE_GROUPS * len(MOE_PAIRS)


def _route_metadata(rec, *, tm, n_tiles):
    ids = rec[:, 0:2].astype(jnp.int32)
    lo = jnp.minimum(ids[:, 0], ids[:, 1])
    hi = jnp.maximum(ids[:, 0], ids[:, 1])
    a, b = lo % MOE_EXPERTS_PER_GROUP, hi % MOE_EXPERTS_PER_GROUP
    pair_id = (a * (7 - a)) // 2 + b - a - 1
    pair_id = pair_id + (pair_id == 3).astype(jnp.int32) - (pair_id == 4).astype(jnp.int32)
    cls = (lo // MOE_EXPERTS_PER_GROUP) * len(MOE_PAIRS) + pair_id
    onehot = (cls[:, None] == jnp.arange(MOE_CLASSES, dtype=jnp.int32)[None, :]).astype(jnp.int32)
    csum = jnp.cumsum(onehot, axis=0)
    counts = csum[-1]
    rank = jnp.sum(csum * onehot, axis=1) - 1
    padded = ((counts + tm - 1) // tm) * tm
    ends = jnp.cumsum(padded)
    pos = jnp.sum(onehot * (ends - padded)[None, :], axis=1) + rank
    tile_start = jnp.arange(n_tiles, dtype=jnp.int32) * tm
    n_valid = ends[-1] // tm
    tile_cls = jnp.sum((tile_start[:, None] >= ends[None, :]).astype(jnp.int32), axis=1)
    last_cls = jnp.sum((ends[-1] - 1 >= ends).astype(jnp.int32))
    tile_cls = jnp.where(tile_start < ends[-1], tile_cls, last_cls)
    pair = jnp.asarray(MOE_PAIRS, jnp.int32)[tile_cls % len(MOE_PAIRS)]
    base = (tile_cls // len(MOE_PAIRS)) * MOE_EXPERTS_PER_GROUP
    idle = n_valid + jnp.arange(MOE_CLASSES, dtype=jnp.int32)
    fill_start = jnp.concatenate([jnp.maximum(ends - tm, 0), jnp.minimum(idle, n_tiles - 1) * tm])
    fill_on = jnp.concatenate([counts > 0, idle < n_tiles]).astype(jnp.int32)
    return pos, base + pair[:, 0], base + pair[:, 1], n_valid.reshape(1), fill_start, fill_on


def _moe_dispatch_kernel(pos_ref, fs_ref, fo_ref, x_ref, xs_hbm, stage, zbuf, sem, zsem, *, tm, tz):
    i = pl.program_id(0)
    n = pl.num_programs(0)

    def fill_copy(c):
        return pltpu.make_async_copy(zbuf, xs_hbm.at[pl.ds(pl.multiple_of(fs_ref[c], tz), tz), :], zsem)

    @pl.when(i == 0)
    def _():
        zbuf[...] = jnp.zeros_like(zbuf)
        for c in range(2 * MOE_CLASSES):
            @pl.when(fo_ref[c] > 0)
            def _():
                fill_copy(c).start()
        for c in range(2 * MOE_CLASSES):
            @pl.when(fo_ref[c] > 0)
            def _():
                fill_copy(c).wait()

    def tile_wait(slot):
        pltpu.make_async_copy(stage.at[slot], xs_hbm.at[pl.ds(0, tm), :], sem.at[slot]).wait()

    slot = i % 2

    @pl.when(i >= 2)
    def _():
        tile_wait(slot)

    stage[slot] = x_ref[...]

    def body(r, carry):
        p = pos_ref[i * tm + r]
        pltpu.make_async_copy(stage.at[slot, pl.ds(r, 1), :], xs_hbm.at[pl.ds(p, 1), :],
                              sem.at[slot]).start()
        return carry
    lax.fori_loop(0, tm, body, 0, unroll=8)

    @pl.when(i == n - 1)
    def _():
        tile_wait(slot)

        @pl.when(n >= 2)
        def _():
            tile_wait(1 - slot)


def moe_dispatch(xr, pos, fill_start, fill_on, *, tm, tz, n_slots):
    T, width = xr.shape
    grid_spec = pltpu.PrefetchScalarGridSpec(
        num_scalar_prefetch=3,
        grid=(T // tm,),
        in_specs=[pl.BlockSpec((tm, width), lambda i, pos, fs, fo: (i, 0))],
        out_specs=pl.BlockSpec(memory_space=pl.ANY),
        scratch_shapes=[pltpu.VMEM((2, tm, width), f32), pltpu.VMEM((tz, width), f32),
                        pltpu.SemaphoreType.DMA((2,)), pltpu.SemaphoreType.DMA(())],
    )
    return pl.pallas_call(
        functools.partial(_moe_dispatch_kernel, tm=tm, tz=tz),
        out_shape=jax.ShapeDtypeStruct((n_slots, width), f32),
        grid_spec=grid_spec,
        compiler_params=_cparams(("arbitrary",)),
        name="moe_dispatch",
    )(pos, fill_start, fill_on, xr)


def _moe_pair_kernel(ea_ref, eb_ref, nv_ref, xs_ref, wga, wua, wda, wgb, wub, wdb, o_ref):
    @pl.when(pl.program_id(0) >= nv_ref[0])
    def _():
        o_ref[...] = jnp.zeros_like(o_ref)

    @pl.when(pl.program_id(0) < nv_ref[0])
    def _():
        D = o_ref.shape[1]
        x = xs_ref[:, :D].astype(bf16)
        rec = xs_ref[:, D:]
        first = rec[:, 0:1] == ea_ref[pl.program_id(0)].astype(f32)
        w_a = jnp.where(first, rec[:, 2:3], rec[:, 3:4])
        w_b = jnp.where(first, rec[:, 3:4], rec[:, 2:3])

        gates = [jnp.dot(x, wg[0], preferred_element_type=f32) for wg in (wga, wgb)]
        ups = [jnp.dot(x, wu[0], preferred_element_type=f32) for wu in (wua, wub)]
        hids = [(g * _sigmoid(g) * u).astype(bf16) for g, u in zip(gates, ups)]
        y_a, y_b = [jnp.dot(h, wd[0], preferred_element_type=f32) for h, wd in zip(hids, (wda, wdb))]
        o_ref[...] = w_a * y_a + w_b * y_b


def moe_pair_experts(xs, e_a, e_b, n_valid, wg, wu, wd, *, tm, n_tiles):
    width = xs.shape[1]
    D = wg.shape[1]
    held = lambda i, ea, eb, nv: (jnp.minimum(i, nv[0] - 1), 0)
    tile = lambda i, ea, eb, nv: (i, 0)
    wa = lambda shape: pl.BlockSpec(shape, lambda i, ea, eb, nv: (ea[i], 0, 0))
    wb = lambda shape: pl.BlockSpec(shape, lambda i, ea, eb, nv: (eb[i], 0, 0))
    grid_spec = pltpu.PrefetchScalarGridSpec(
        num_scalar_prefetch=3,
        grid=(n_tiles,),
        in_specs=[pl.BlockSpec((tm, width), held),
                  wa((1, D, MOE_FF)), wa((1, D, MOE_FF)), wa((1, MOE_FF, D)),
                  wb((1, D, MOE_FF)), wb((1, D, MOE_FF)), wb((1, MOE_FF, D))],
        out_specs=pl.BlockSpec((tm, D), tile),
    )
    return pl.pallas_call(
        _moe_pair_kernel,
        out_shape=jax.ShapeDtypeStruct((n_tiles * tm, D), f32),
        grid_spec=grid_spec,
        compiler_params=_cparams(("arbitrary",)),
        name="moe_experts",
    )(e_a, e_b, n_valid, xs, wg, wu, wd, wg, wu, wd)


def _moe_combine_kernel(pos_ref, h_ref, y_hbm, gf_ref, o_ref, ybuf, sem, *, tm):
    i = pl.program_id(0)
    n = pl.num_programs(0)

    def gather(tile, slot):
        def body(r, carry):
            p = pos_ref[tile * tm + r]
            pltpu.make_async_copy(y_hbm.at[pl.ds(p, 1), :],
                                  ybuf.at[slot, pl.ds(r, 1), :], sem.at[slot]).start()
            return carry
        lax.fori_loop(0, tm, body, 0, unroll=8)

    @pl.when(i == 0)
    def _():
        gather(0, 0)

    @pl.when(i + 1 < n)
    def _():
        gather(i + 1, (i + 1) % 2)

    slot = i % 2
    pltpu.make_async_copy(y_hbm.at[pl.ds(0, tm), :], ybuf.at[slot], sem.at[slot]).wait()
    o_ref[...] = _rmsnorm_rows(h_ref[...] + ybuf[slot], gf_ref[...])


def moe_combine(h, y_sorted, pos, g_final, *, tm):
    T, D = h.shape
    grid_spec = pltpu.PrefetchScalarGridSpec(
        num_scalar_prefetch=1,
        grid=(T // tm,),
        in_specs=[
            pl.BlockSpec((tm, D), lambda i, pos: (i, 0)),
            pl.BlockSpec(memory_space=pl.ANY),
            pl.BlockSpec((1, D), lambda i, pos: (0, 0)),
        ],
        out_specs=pl.BlockSpec((tm, D), lambda i, pos: (i, 0)),
        scratch_shapes=[pltpu.VMEM((2, tm, D), f32), pltpu.SemaphoreType.DMA((2,))],
    )
    return pl.pallas_call(
        functools.partial(_moe_combine_kernel, tm=tm),
        out_shape=jax.ShapeDtypeStruct((T, D), f32),
        grid_spec=grid_spec,
        compiler_params=_cparams(("arbitrary",)),
        name="moe_combine",
    )(pos, h, y_sorted, g_final.reshape(1, D))


def _pad_rows(w, rows):
    return jnp.pad(w, ((0, rows - w.shape[0]), (0, 0)))


def _pack_lora_cols(t):
    o1, o2 = DECAY_LORA, DECAY_LORA + AAA_LORA
    pad = lambda a, n: jnp.pad(a, [(0, 0)] * (a.ndim - 1) + [(0, n - a.shape[-1])])
    return jnp.concatenate([pad(t[..., :o1], 128), pad(t[..., o1:o2], 128),
                            pad(t[..., o2:], 256)], axis=-1)


def _layer(x2, mem2, p, *, batch, seq, cfg):
    D = D_MODEL
    w_in = p['w_in']
    w_main = w_in.astype(bf16)
    w_lora = _pack_lora_cols(w_in[:, MAIN_COLS:]).astype(bf16)
    mu_cols = jnp.concatenate([jnp.zeros((POOL_WIDTH,), f32), p['rwkv_mu'][:3 * RWKV_WIDTH],
                               _pack_lora_cols(p['rwkv_mu'][3 * RWKV_WIDTH:])])

    proj, pool_out, lw, k, kk, kka, g = in_proj_mixers(
        x2, p['norm_mix_g'], w_main, w_lora, mu_cols, p['pool_w'].astype(bf16), p['pool_scale'],
        p['rwkv_w0'], _pad_rows(p['rwkv_w2'], 128).astype(bf16), p['rwkv_a0'],
        _pad_rows(p['rwkv_a2'], 128).astype(bf16), _pad_rows(p['rwkv_g2'], 256).astype(bf16),
        p['rwkv_k_k'], p['rwkv_k_a'], tm=cfg['tm'], tn=cfg['tn_in'], seq=seq)
    rwkv_out, (w_out, w_q, w_kv, w_o, wg, wu, wd) = rwkv_scan(
        proj, lw, k, kk, kka, g, p['rwkv_ln_w'], p['rwkv_ln_b'], p['rwkv_r_k'],
        [p['w_out'], p['xattn_w_q'], p['xattn_w_kv'], p['xattn_w_o'],
         p['moe_w_gate'], p['moe_w_up'], p['moe_w_down']],
        batch=batch, seq=seq, rows=cfg['scan_rows'], n_hg=cfg['scan_hg'])
    h1 = matmul_residual_resident([pool_out, rwkv_out], [w_out[:POOL_WIDTH], w_out[POOL_WIDTH:]],
                                  x2, tm=cfg['tm'], tn=cfg['tn'], name="out_proj")

    q = norm_matmul_resident(h1, p['norm_xattn_g'], w_q,
                             tm=cfg['tm_q'], tn=cfg['tn'], name="q_proj")
    kv = norm_matmul_resident(mem2, p['norm_mem_g'], w_kv,
                              tm=min(cfg['tm'], mem2.shape[0]), tn=cfg['tn'], name="kv_proj")

    w_router = jnp.pad(jnp.concatenate([p['moe_w_group'], p['moe_w_expert']], axis=1),
                       ((0, 0), (0, ROUTER_LANES - MOE_GROUPS - MOE_EXPERTS)))
    b_router = jnp.pad(jnp.concatenate([p['moe_b_group'], p['moe_b_expert']]),
                       (0, ROUTER_LANES - MOE_GROUPS - MOE_EXPERTS)).reshape(1, ROUTER_LANES)
    h2, xr, rec = attn_oproj_router(q, kv, w_o, h1, p['norm_ffn_g'],
                               w_router, b_router, tm=cfg['tm'], tn=cfg['tn'], seq=seq)
    tm_e = cfg['tm_moe']
    n_tiles = xr.shape[0] // tm_e + MOE_CLASSES
    pos, e_a, e_b, n_valid, fill_start, fill_on = _route_metadata(
        rec, tm=tm_e, n_tiles=n_tiles)
    xs = moe_dispatch(xr, pos, fill_start, fill_on, tm=cfg['tm_disp'], tz=tm_e,
                      n_slots=n_tiles * tm_e)
    y_sorted = moe_pair_experts(xs, e_a, e_b, n_valid, wg, wu, wd, tm=tm_e, n_tiles=n_tiles)
    return moe_combine(h2, y_sorted, pos, p['norm_final_g'], tm=cfg['tm_comb'])


_CFG = dict(tm=512, tm_q=1024, tn=1024, tn_in=512, scan_rows=256, scan_hg=4,
            tm_moe=256, tm_disp=1024, tm_comb=512)


def kernel(x, mem, norm_mix_g, w_in, pool_w, pool_scale, rwkv_mu, rwkv_w0, rwkv_w2, rwkv_a0,
           rwkv_a2, rwkv_g2, rwkv_k_k, rwkv_k_a, rwkv_r_k, rwkv_ln_w, rwkv_ln_b, w_out,
           norm_xattn_g, norm_mem_g, xattn_w_q, xattn_w_kv, xattn_w_o, norm_ffn_g,
           moe_w_group, moe_b_group, moe_w_expert, moe_b_expert, moe_w_gate, moe_w_up,
           moe_w_down, norm_final_g):
    batch, seq, D = x.shape
    p = dict(norm_mix_g=norm_mix_g[0], w_in=w_in[0], pool_w=pool_w[0], pool_scale=pool_scale[0],
             rwkv_mu=rwkv_mu[0], rwkv_w0=rwkv_w0[0], rwkv_w2=rwkv_w2[0], rwkv_a0=rwkv_a0[0],
             rwkv_a2=rwkv_a2[0], rwkv_g2=rwkv_g2[0], rwkv_k_k=rwkv_k_k[0], rwkv_k_a=rwkv_k_a[0],
             rwkv_r_k=rwkv_r_k[0], rwkv_ln_w=rwkv_ln_w[0], rwkv_ln_b=rwkv_ln_b[0], w_out=w_out[0],
             norm_xattn_g=norm_xattn_g[0], norm_mem_g=norm_mem_g[0], xattn_w_q=xattn_w_q[0],
             xattn_w_kv=xattn_w_kv[0], xattn_w_o=xattn_w_o[0], norm_ffn_g=norm_ffn_g[0],
             moe_w_group=moe_w_group[0], moe_b_group=moe_b_group[0], moe_w_expert=moe_w_expert[0],
             moe_b_expert=moe_b_expert[0], moe_w_gate=moe_w_gate[0], moe_w_up=moe_w_up[0],
             moe_w_down=moe_w_down[0], norm_final_g=norm_final_g)
    out = _layer(x.reshape(batch * seq, D), mem.reshape(batch * MEM_LEN, D), p,
                 batch=batch, seq=seq, cfg=_CFG)
    return out.reshape(batch, seq, D)
```

```python
import functools

import jax
import jax.numpy as jnp
from jax import lax
from jax.experimental import pallas as pl
from jax.experimental.pallas import tpu as pltpu

f32 = jnp.float32
bf16 = jnp.bfloat16

D_MODEL = 2048
MEM_LEN = 256
NORM_EPS = 1e-6

POOL_WIDTH = 1024
POOL_WINDOWS = (2, 4, 8, 16)
POOL_GROUP = 256
POOL_HALO = 16
RWKV_WIDTH = 1024
RWKV_HEAD = 64
RWKV_HEADS = 16
GN_EPS = 64e-5
DECAY_SCALE = 0.6065306597126334
DECAY_LORA = 64
AAA_LORA = 64
GATE_LORA = 160
LORA_PAD = 512
LORA_W_OFF, LORA_A_OFF, LORA_G_OFF = 0, 128, 256
MAIN_COLS = POOL_WIDTH + 3 * RWKV_WIDTH
PROJ_COLS = MAIN_COLS + LORA_PAD

RWKV_CHUNK = 64
HEADS_PER_STEP = 4
HG_LANES = HEADS_PER_STEP * RWKV_HEAD

XATTN_HEADS = 4
XATTN_HEAD_DIM = 512

MOE_GROUPS = 4
MOE_EXPERTS_PER_GROUP = 4
MOE_EXPERTS = 16
MOE_FF = 512
ROUTER_LANES = 128

VMEM_LIMIT = 56 * 1024 * 1024


def _cparams(sem):
    return pltpu.CompilerParams(dimension_semantics=sem, vmem_limit_bytes=VMEM_LIMIT)


def _split_dot(x, w_bf16):
    hi = x.astype(bf16)
    lo = (x - hi.astype(f32)).astype(bf16)
    return (jnp.dot(hi, w_bf16, preferred_element_type=f32)
            + jnp.dot(lo, w_bf16, preferred_element_type=f32))


def _resident(shape):
    return pl.BlockSpec(shape, lambda i: (0,) * len(shape), pipeline_mode=pl.Buffered(1))


def _rmsnorm_rows(x, g):
    ms = jnp.mean(x * x, axis=-1, keepdims=True)
    return x * lax.rsqrt(ms + NORM_EPS) * g


def _norm_mm_res_kernel(x_ref, g_ref, w_ref, o_ref, *, tn):
    xn = _rmsnorm_rows(x_ref[...], g_ref[...]).astype(bf16)
    for n0 in range(0, o_ref.shape[1], tn):
        o_ref[:, n0:n0 + tn] = jnp.dot(xn, w_ref[:, n0:n0 + tn],
                                       preferred_element_type=f32).astype(o_ref.dtype)


def norm_matmul_resident(x, g, w, *, tm, tn, name):
    M, K = x.shape
    N = w.shape[1]
    return pl.pallas_call(
        functools.partial(_norm_mm_res_kernel, tn=tn),
        out_shape=jax.ShapeDtypeStruct((M, N), bf16),
        grid=(M // tm,),
        in_specs=[pl.BlockSpec((tm, K), lambda i: (i, 0)), _resident((1, K)), _resident((K, N))],
        out_specs=pl.BlockSpec((tm, N), lambda i: (i, 0)),
        compiler_params=_cparams(("parallel",)),
        name=name,
    )(x, g.reshape(1, K), w)


def _in_proj_kernel(x_ref, xh_ref, g_ref, w_ref, wl_ref, mu_ref, pw_ref, ps_ref, o_ref, pool_ref,
                    *, tn, tiles_per_seq):
    tm = x_ref.shape[0]
    tile_in_seq = pl.program_id(0) % tiles_per_seq
    first = tile_in_seq == 0
    xn = _rmsnorm_rows(x_ref[...], g_ref[...]).astype(bf16)
    xh = jnp.where(first, 0.0, _rmsnorm_rows(xh_ref[...], g_ref[...])).astype(bf16)
    lhs = jnp.concatenate([xh, xn], axis=0)
    pos = tile_in_seq * tm + lax.broadcasted_iota(jnp.int32, (tm, 1), 0)
    n_cols = POOL_WIDTH + o_ref.shape[1]
    def slab(n0):
        w = w_ref[:, n0:n0 + tn] if n0 < MAIN_COLS else wl_ref[:, n0 - MAIN_COLS:n0 - MAIN_COLS + tn]
        return jnp.dot(lhs, w, preferred_element_type=f32)

    def pool_slab(n0, res):
        p = res[POOL_HALO:]
        acc = res
        shift = 1
        for gi in range(n0 // POOL_GROUP, (n0 + tn) // POOL_GROUP):
            win = POOL_WINDOWS[gi]
            while shift < win:
                acc = acc + pltpu.roll(acc, shift, 0)
                shift *= 2
            c0 = gi * POOL_GROUP - n0
            cnt = jnp.minimum(pos + 1, win).astype(f32)
            pooled = acc[POOL_HALO:, c0:c0 + POOL_GROUP] / cnt - p[:, c0:c0 + POOL_GROUP]
            mixed = jnp.dot(pooled.astype(bf16), pw_ref[gi], preferred_element_type=f32)
            lo, hi = gi * POOL_GROUP, (gi + 1) * POOL_GROUP
            pool_ref[:, lo:hi] = (mixed * ps_ref[:, lo:hi]).astype(pool_ref.dtype)

    def shift_slab(n0, res):
        p = res[POOL_HALO:]
        prev = pltpu.roll(res, 1, 0)[POOL_HALO:]
        z = p + (prev - p) * mu_ref[:, n0:n0 + tn]
        o_ref[:, n0 - POOL_WIDTH:n0 - POOL_WIDTH + tn] = z.astype(o_ref.dtype)

    pending = None
    for n0 in range(0, n_cols, tn):
        res = slab(n0)
        if pending is not None:
            pending()
        pending = functools.partial(pool_slab if n0 < POOL_WIDTH else shift_slab, n0, res)
    pending()


def in_proj_mixers(x, g, w_main, w_lora, mu_cols, pool_w, pool_scale, *, tm, tn, seq):
    M, K = x.shape
    n_rwkv = MAIN_COLS - POOL_WIDTH + w_lora.shape[1]
    hb = tm // POOL_HALO
    return pl.pallas_call(
        functools.partial(_in_proj_kernel, tn=tn, tiles_per_seq=seq // tm),
        out_shape=[jax.ShapeDtypeStruct((M, n_rwkv), bf16),
                   jax.ShapeDtypeStruct((M, POOL_WIDTH), bf16)],
        grid=(M // tm,),
        in_specs=[pl.BlockSpec((tm, K), lambda i: (i, 0)),
                  pl.BlockSpec((POOL_HALO, K), lambda i: (jnp.maximum(i * hb - 1, 0), 0)),
                  _resident((1, K)), _resident(w_main.shape), _resident(w_lora.shape),
                  _resident((1, POOL_WIDTH + n_rwkv)), _resident(pool_w.shape),
                  _resident((1, POOL_WIDTH))],
        out_specs=[pl.BlockSpec((tm, n_rwkv), lambda i: (i, 0)),
                   pl.BlockSpec((tm, POOL_WIDTH), lambda i: (i, 0))],
        compiler_params=_cparams(("parallel",)),
        name="in_proj",
    )(x, x, g.reshape(1, K), w_main, w_lora, mu_cols.reshape(1, -1), pool_w,
      pool_scale.reshape(1, POOL_WIDTH))


def _out_q_kernel(*refs, n, tn):
    a_refs, w_refs = refs[:n], refs[n:2 * n]
    res_ref, g_ref, wq_ref, h_ref, q_ref = refs[2 * n:]
    for n0 in range(0, h_ref.shape[1], tn):
        acc = res_ref[:, n0:n0 + tn]
        for a_ref, w_ref in zip(a_refs, w_refs):
            acc = acc + jnp.dot(a_ref[...], w_ref[:, n0:n0 + tn], preferred_element_type=f32)
        h_ref[:, n0:n0 + tn] = acc
    xn = _rmsnorm_rows(h_ref[...], g_ref[...]).astype(bf16)
    for n0 in range(0, q_ref.shape[1], tn):
        q_ref[:, n0:n0 + tn] = jnp.dot(xn, wq_ref[:, n0:n0 + tn],
                                       preferred_element_type=f32).astype(q_ref.dtype)


def out_q_proj(a_list, w_list, res, g, w_q, *, tm, tn):
    M, N = res.shape
    n = len(a_list)
    row = lambda width: pl.BlockSpec((tm, width), lambda i: (i, 0))
    return pl.pallas_call(
        functools.partial(_out_q_kernel, n=n, tn=tn),
        out_shape=[jax.ShapeDtypeStruct((M, N), f32), jax.ShapeDtypeStruct((M, w_q.shape[1]), bf16)],
        grid=(M // tm,),
        in_specs=([row(a.shape[1]) for a in a_list] + [_resident(w.shape) for w in w_list]
                  + [row(N), _resident((1, N)), _resident(w_q.shape)]),
        out_specs=[row(N), row(w_q.shape[1])],
        compiler_params=_cparams(("parallel",)),
        name="out_q_proj",
    )(*a_list, *w_list, res, g.reshape(1, N), w_q)


def _head_ones():
    r = lax.broadcasted_iota(jnp.int32, (HG_LANES, HG_LANES), 0) // RWKV_HEAD
    c = lax.broadcasted_iota(jnp.int32, (HG_LANES, HG_LANES), 1) // RWKV_HEAD
    return jnp.where(r == c, 1.0, 0.0).astype(bf16)


def _head_sum(x, ones_bd):
    parts = [_split_dot(x[:, c:c + HG_LANES], ones_bd) for c in range(0, x.shape[1], HG_LANES)]
    return parts[0] if len(parts) == 1 else jnp.concatenate(parts, axis=1)


def _sigmoid(x):
    return 1.0 / (1.0 + jnp.exp(-x))


def _rwkv_prep_kernel(k_ref, lo_ref, w0_ref, w2_ref, a0_ref, a2_ref, g2_ref, kk_ref, ka_ref,
                      lw_out, k_out, kk_out, kka_out, g_out):
    k = k_ref[...].astype(f32)
    lo = lo_ref[...].astype(f32)
    w_lo = lo[:, LORA_W_OFF:LORA_W_OFF + 128]
    a_lo = lo[:, LORA_A_OFF:LORA_A_OFF + 128]
    g_lo = lo[:, LORA_G_OFF:LORA_G_OFF + 256]

    wx = w0_ref[...] + jnp.dot(jnp.tanh(w_lo).astype(bf16), w2_ref[...], preferred_element_type=f32)
    lw_out[...] = -DECAY_SCALE * _sigmoid(wx)
    a = _sigmoid(a0_ref[...] + jnp.dot(a_lo.astype(bf16), a2_ref[...], preferred_element_type=f32))
    g = jnp.dot(_sigmoid(g_lo).astype(bf16), g2_ref[...], preferred_element_type=f32)

    ones_bd = _head_ones()
    kk = k * kk_ref[...]
    kk = kk * lax.rsqrt(jnp.maximum(_head_sum(kk * kk, ones_bd), 1e-24))
    k = k * (1.0 + (a - 1.0) * ka_ref[...])

    k_out[...] = k.astype(k_out.dtype)
    kk_out[...] = kk.astype(kk_out.dtype)
    kka_out[...] = (kk * a).astype(kka_out.dtype)
    g_out[...] = g.astype(g_out.dtype)


def rwkv_prep(proj, w0, w2p, a0, a2p, g2p, k_k, k_a, *, ts):
    T = proj.shape[0]
    C = RWKV_WIDTH
    k_blk = 1
    lora_blk = 3 * C // LORA_PAD

    def full(shape):
        return pl.BlockSpec(shape, lambda i: (0,) * len(shape))

    row = lambda t: t.reshape(1, -1)
    out_sd = lambda dt: jax.ShapeDtypeStruct((T, C), dt)
    out_spec = pl.BlockSpec((ts, C), lambda i: (i, 0))
    return pl.pallas_call(
        _rwkv_prep_kernel,
        out_shape=[out_sd(f32), out_sd(bf16), out_sd(bf16), out_sd(bf16), out_sd(bf16)],
        grid=(T // ts,),
        in_specs=[
            pl.BlockSpec((ts, C), lambda i: (i, k_blk)),
            pl.BlockSpec((ts, LORA_PAD), lambda i: (i, lora_blk)),
            full((1, C)), full((128, C)), full((1, C)), full((128, C)), full((256, C)),
            full((1, C)), full((1, C)),
        ],
        out_specs=[out_spec] * 5,
        compiler_params=_cparams(("parallel",)),
        name="rwkv_prep",
    )(proj, proj, row(w0), w2p, row(a0), a2p, g2p, row(k_k), row(k_a))


def _bd_mask():
    r = lax.broadcasted_iota(jnp.int32, (HG_LANES, HG_LANES), 0) // RWKV_CHUNK
    c = lax.broadcasted_iota(jnp.int32, (HG_LANES, HG_LANES), 1) // RWKV_HEAD
    return r == c


def _rwkv_scan_kernel(r_ref, v_ref, lw_ref, k_ref, kk_ref, kka_ref, g_ref,
                      lnw_ref, lnb_ref, rk_ref, *rest, n_chunks, n_hg, n_cast):
    L = RWKV_CHUNK
    W = HG_LANES
    cast_in, o_ref, cast_out, state_ref = rest[:n_cast], rest[n_cast], rest[n_cast + 1:-1], rest[-1]
    for w_ref, wb_ref in zip(cast_in, cast_out):
        wb_ref[...] = w_ref[...].astype(bf16)

    @pl.when(pl.program_id(2) == 0)
    def _():
        state_ref[...] = jnp.zeros_like(state_ref)

    bd_mask = _bd_mask()
    ones_bd = _head_ones()

    def bd(x):
        tiled = jnp.concatenate([x] * HEADS_PER_STEP, axis=0)
        return jnp.where(bd_mask, tiled, 0.0).astype(bf16)

    def bd_t(x):
        tiled = jnp.concatenate([x] * HEADS_PER_STEP, axis=0)
        return jnp.where(bd_mask, tiled, 0.0).T.astype(bf16)

    def mm(a, b_bf16):
        return jnp.dot(a.astype(bf16), b_bf16, preferred_element_type=f32)

    def mm_nt(a, b_bf16):
        return lax.dot_general(a.astype(bf16), b_bf16, (((1,), (1,)), ((), ())),
                               preferred_element_type=f32)

    def mm_tn(a_bf16, b_bf16):
        return lax.dot_general(a_bf16, b_bf16, (((0,), (0,)), ((), ())),
                               preferred_element_type=f32)

    def split(x):
        hi = x.astype(bf16)
        return hi, (x - hi.astype(f32)).astype(bf16)

    def head_sums(xs):
        tot = mm(jnp.concatenate(xs, axis=0), ones_bd)
        return [tot[j * L:(j + 1) * L] for j in range(len(xs))]

    t_idx = lax.broadcasted_iota(jnp.int32, (L, W), 0)
    s_idx = lax.broadcasted_iota(jnp.int32, (L, W), 1) % L
    strict = t_idx > s_idx
    incl = t_idx >= s_idx
    eye_all = jnp.where(t_idx == s_idx, 1.0, 0.0)
    tri = jnp.where(lax.broadcasted_iota(jnp.int32, (L, L), 0)
                    >= lax.broadcasted_iota(jnp.int32, (L, L), 1), 1.0, 0.0).astype(bf16)
    blockdiag = (lax.broadcasted_iota(jnp.int32, (W, W), 0) // RWKV_HEAD
                 == lax.broadcasted_iota(jnp.int32, (W, W), 1) // RWKV_HEAD)

    def phase1(items):
        I = range(len(items))
        rows = [pl.ds(c * L, L) for _, c in items]
        cols = [pl.ds(h * W, W) for h, _ in items]
        ld = lambda ref, i: ref[rows[i], cols[i]]
        r = [ld(r_ref, i).astype(f32) for i in I]
        k = [ld(k_ref, i).astype(f32) for i in I]
        v = [ld(v_ref, i).astype(f32) for i in I]
        lw = [ld(lw_ref, i) for i in I]
        lw_s = [split(x) for x in lw]
        cs = [jnp.dot(tri, hi, preferred_element_type=f32) + jnp.dot(tri, lo, preferred_element_type=f32)
              for hi, lo in lw_s]
        g_inv = [jnp.exp(-x) for x in cs]
        to_end = [jnp.exp(x[L - 1:L, :] - x) for x in cs]
        a_t = [-ld(kk_ref, i).astype(f32) * jnp.exp(cs[i] - lw[i]) for i in I]
        r_t = [r[i] * jnp.exp(cs[i]) for i in I]
        kka = [ld(kka_ref, i).astype(f32) for i in I]
        b_t = [kka[i] * g_inv[i] for i in I]
        k_t = [k[i] * g_inv[i] for i in I]
        b_e = [kka[i] * to_end[i] for i in I]
        k_e = [k[i] * to_end[i] for i in I]
        v_bd = [bd(x) for x in v]
        ar = [jnp.concatenate([a_t[i], r_t[i]], axis=0) for i in I]
        m_b = [mm(ar[i], bd_t(b_t[i])) for i in I]
        m_k = [mm(ar[i], bd_t(k_t[i])) for i in I]
        m_ab = [jnp.where(strict, x[:L], 0.0) for x in m_b]
        m_rb = [jnp.where(incl, x[L:], 0.0) for x in m_b]
        m_ak = [jnp.where(strict, x[:L], 0.0) for x in m_k]
        m_rk = [jnp.where(incl, x[L:], 0.0) for x in m_k]
        rk_sum = head_sums([r[i] * k[i] * rk_ref[:, cols[i]] for i in I])
        bonus = [rk_sum[i] * v[i] for i in I]
        tinv = [eye_all + x for x in m_ab]
        p = [mm(x, bd(x)) for x in m_ab]
        m = 2
        while m < L:
            w_p = [bd(x) for x in p]
            m *= 2
            if m < L:
                both = [mm(jnp.concatenate([tinv[i], p[i]], axis=0), w_p[i]) for i in I]
                tinv = [tinv[i] + both[i][:L] for i in I]
                p = [x[L:] for x in both]
            else:
                tinv = [tinv[i] + mm(tinv[i], w_p[i]) for i in I]
        xm = [mm(jnp.concatenate([m_ak[i], m_rk[i]], axis=0), v_bd[i]) for i in I]
        u_v = [mm(tinv[i], bd(xm[i][:L])) for i in I]
        a_h = [mm(tinv[i], bd(a_t[i])) for i in I]
        r_h = [r_t[i] + mm(m_rb[i], bd(a_h[i])) for i in I]
        y_v = [mm(m_rb[i], bd(u_v[i])) + xm[i][L:] for i in I]
        p_t = [jnp.where(blockdiag, mm_tn(a_h[i].astype(bf16), b_e[i].astype(bf16)), 0.0).astype(bf16)
               for i in I]
        g_t = [jnp.where(blockdiag,
                         mm_tn(jnp.concatenate([u_v[i], v[i]], axis=0).astype(bf16),
                               jnp.concatenate([b_e[i], k_e[i]], axis=0).astype(bf16)), 0.0) for i in I]
        decay = [jnp.exp(x[L - 1:L, :]) for x in cs]
        for i in I:
            done.append(dict(hg=items[i][0], rows=rows[i], cols=cols[i], r_h=r_h[i], y_v=y_v[i],
                             p_t=p_t[i], g_t=g_t[i], decay=decay[i], bonus=bonus[i]))

    def chain_step(chs):
        ys = []
        for ch in chs:
            state = states[ch['hg']]
            state_b = state.astype(bf16)
            ys.append(mm_nt(ch['r_h'], state_b) + ch['y_v'])
            states[ch['hg']] = (state * ch['decay']
                                + jnp.dot(state_b, ch['p_t'], preferred_element_type=f32) + ch['g_t'])
        means = head_sums(ys)
        ycs = [y - m * (1.0 / RWKV_HEAD) for y, m in zip(ys, means)]
        vrs = head_sums([yc * yc for yc in ycs])
        for ch, yc, vr in zip(chs, ycs, vrs):
            rw, cl = ch['rows'], ch['cols']
            yn = yc * lax.rsqrt(vr * (1.0 / RWKV_HEAD) + GN_EPS) * lnw_ref[:, cl] + lnb_ref[:, cl]
            o_ref[rw, cl] = ((yn + ch['bonus']) * g_ref[rw, cl].astype(f32)).astype(o_ref.dtype)

    states = [state_ref[h] for h in range(n_hg)]
    done = []
    phase1([(h, c) for c in range(n_chunks) for h in range(n_hg)])
    for j in range(0, len(done), n_hg):
        chain_step(done[j:j + n_hg])
    for h in range(n_hg):
        state_ref[h] = states[h]


def rwkv_scan(proj, lw, k, kk, kka, g, ln_w, ln_b, r_k, cast_weights, *, batch, seq, rows, n_hg):
    nb = seq // rows
    T = batch * seq
    width = n_hg * HG_LANES
    n_col = RWKV_WIDTH // width
    n_steps = batch * n_col * nb
    r_blk = 0
    v_blk = 2 * RWKV_WIDTH // width
    blk = pl.BlockSpec((rows, width), lambda b, h, c: (b * nb + c, h))
    par = pl.BlockSpec((1, width), lambda b, h, c: (0, h))
    slabs = [w.reshape(n_steps, w.size // (n_steps * w.shape[-1]), w.shape[-1]) for w in cast_weights]
    slab_spec = lambda w: pl.BlockSpec((1,) + w.shape[1:], lambda b, h, c: ((b * n_col + h) * nb + c, 0, 0))
    outs = pl.pallas_call(
        functools.partial(_rwkv_scan_kernel, n_chunks=rows // RWKV_CHUNK, n_hg=n_hg,
                          n_cast=len(slabs)),
        out_shape=[jax.ShapeDtypeStruct((T, RWKV_WIDTH), bf16)]
                  + [jax.ShapeDtypeStruct(w.shape, bf16) for w in slabs],
        grid=(batch, n_col, nb),
        in_specs=[pl.BlockSpec((rows, width), lambda b, h, c: (b * nb + c, r_blk + h)),
                  pl.BlockSpec((rows, width), lambda b, h, c: (b * nb + c, v_blk + h))]
                 + [blk] * 5 + [par] * 3 + [slab_spec(w) for w in slabs],
        out_specs=[blk] + [slab_spec(w) for w in slabs],
        scratch_shapes=[pltpu.VMEM((n_hg, HG_LANES, HG_LANES), f32)],
        compiler_params=_cparams(("parallel", "parallel", "arbitrary")),
        name="rwkv_scan",
    )(proj, proj, lw, k, kk, kka, g, ln_w.reshape(1, -1), ln_b.reshape(1, -1), r_k.reshape(1, -1),
      *slabs)
    return outs[0], [o.reshape(w.shape) for o, w in zip(outs[1:], cast_weights)]


def _route(hn, w, b):
    w_hi = w.astype(bf16)
    w_lo = (w - w_hi.astype(f32)).astype(bf16)
    h_hi = hn.astype(bf16)
    h_lo = (hn - h_hi.astype(f32)).astype(bf16)
    hh = jnp.dot(h_hi, jnp.concatenate([w_hi, w_lo], axis=1), preferred_element_type=f32)
    logits = (hh[:, :ROUTER_LANES] + hh[:, ROUTER_LANES:]
              + jnp.dot(h_lo, w_hi, preferred_element_type=f32)) + b

    lane_i = lax.broadcasted_iota(jnp.int32, logits.shape, 1)
    lane = lane_i.astype(f32)
    neg = -jnp.inf
    big = float(ROUTER_LANES)
    is_g = lane_i < MOE_GROUPS
    gl = jnp.where(is_g, logits, neg)
    gmax = jnp.max(gl, axis=-1, keepdims=True)
    g_idx = jnp.min(jnp.where(gl == gmax, lane, big), axis=-1, keepdims=True)
    g_w = 1.0 / jnp.sum(jnp.exp(gl - gmax), axis=-1, keepdims=True)

    lo_lane = MOE_GROUPS + g_idx * MOE_EXPERTS_PER_GROUP
    sel = (lane >= lo_lane) & (lane < lo_lane + MOE_EXPERTS_PER_GROUP)
    el = jnp.where(sel, logits, neg)
    v1 = jnp.max(el, axis=-1, keepdims=True)
    i1 = jnp.min(jnp.where(el == v1, lane, big), axis=-1, keepdims=True)
    el2 = jnp.where(lane == i1, neg, el)
    v2 = jnp.max(el2, axis=-1, keepdims=True)
    i2 = jnp.min(jnp.where(el2 == v2, lane, big), axis=-1, keepdims=True)
    e2 = jnp.exp(v2 - v1)
    p1 = 1.0 / (1.0 + e2)
    p2 = e2 / (1.0 + e2)
    return (jnp.where(lane_i == 0, i1 - MOE_GROUPS, 0.0)
            + jnp.where(lane_i == 1, i2 - MOE_GROUPS, 0.0)
            + jnp.where(lane_i == 2, p1 * g_w, 0.0)
            + jnp.where(lane_i == 3, p2 * g_w, 0.0))


def _attn_oproj_router_kernel(q_ref, kv_ref, wo_ref, res_ref, g_ref, wr_ref, br_ref, h_ref, xr_ref,
                              rec_ref, *, tn):
    hd = XATTN_HEAD_DIM
    H = range(XATTN_HEADS)
    s = [lax.dot_general(q_ref[:, h * hd:(h + 1) * hd], kv_ref[:, h * hd:(h + 1) * hd],
                         (((1,), (1,)), ((), ())), preferred_element_type=f32) * (hd ** -0.5) for h in H]
    e = [jnp.exp(x - jnp.max(x, axis=-1, keepdims=True)) for x in s]
    l = [jnp.sum(x, axis=-1, keepdims=True) for x in e]
    pv = [jnp.dot(e[h].astype(bf16), kv_ref[:, (XATTN_HEADS + h) * hd:(XATTN_HEADS + h + 1) * hd],
                  preferred_element_type=f32) for h in H]
    o = jnp.concatenate([(pv[h] / l[h]).astype(bf16) for h in H], axis=1)
    for n0 in range(0, h_ref.shape[1], tn):
        h_ref[:, n0:n0 + tn] = res_ref[:, n0:n0 + tn] + jnp.dot(
            o, wo_ref[:, n0:n0 + tn], preferred_element_type=f32)
    hn = _rmsnorm_rows(h_ref[...], g_ref[...])
    D = hn.shape[1]
    rec = _route(hn, wr_ref[...], br_ref[...])
    xr_ref[:, :D] = hn
    xr_ref[:, D:] = rec
    rec_ref[...] = rec


def attn_oproj_router(q, kv, w_o, res, g, w_router, b_router, *, tm, tn, seq):
    T, D = res.shape
    tiles_per_seq = seq // tm
    row = lambda width: pl.BlockSpec((tm, width), lambda i: (i, 0))
    return pl.pallas_call(
        functools.partial(_attn_oproj_router_kernel, tn=tn),
        out_shape=[jax.ShapeDtypeStruct((T, D), f32),
                   jax.ShapeDtypeStruct((T, D + ROUTER_LANES), f32),
                   jax.ShapeDtypeStruct((T, ROUTER_LANES), f32)],
        grid=(T // tm,),
        in_specs=[row(D), pl.BlockSpec((MEM_LEN, 2 * D), lambda i: (i // tiles_per_seq, 0)),
                  _resident((D, D)), row(D), _resident((1, D)),
                  _resident((D, ROUTER_LANES)), _resident((1, ROUTER_LANES))],
        out_specs=[row(D), row(D + ROUTER_LANES), row(ROUTER_LANES)],
        compiler_params=_cparams(("parallel",)),
        name="attn_oproj_router",
    )(q, kv, w_o, res, g.reshape(1, D), w_router, b_router)


MOE_PAIRS = ((0, 1), (0, 2), (0, 3), (1, 3), (1, 2), (3, 2))
MOE_CLASSES = MOE_GROUPS * len(MOE_PAIRS)


def _route_metadata(rec, *, tm, n_tiles):
    ids = rec[:, 0:2].astype(jnp.int32)
    lo = jnp.minimum(ids[:, 0], ids[:, 1])
    hi = jnp.maximum(ids[:, 0], ids[:, 1])
    a, b = lo % MOE_EXPERTS_PER_GROUP, hi % MOE_EXPERTS_PER_GROUP
    pair_id = (a * (7 - a)) // 2 + b - a - 1
    pair_id = pair_id + (pair_id == 3).astype(jnp.int32) - (pair_id == 4).astype(jnp.int32)
    cls = (lo // MOE_EXPERTS_PER_GROUP) * len(MOE_PAIRS) + pair_id
    onehot = (cls[:, None] == jnp.arange(MOE_CLASSES, dtype=jnp.int32)[None, :]).astype(jnp.int32)
    csum = jnp.cumsum(onehot, axis=0)
    counts = csum[-1]
    rank = jnp.sum(csum * onehot, axis=1) - 1
    padded = ((counts + tm - 1) // tm) * tm
    ends = jnp.cumsum(padded)
    pos = jnp.sum(onehot * (ends - padded)[None, :], axis=1) + rank
    tile_start = jnp.arange(n_tiles, dtype=jnp.int32) * tm
    n_valid = ends[-1] // tm
    tile_cls = jnp.sum((tile_start[:, None] >= ends[None, :]).astype(jnp.int32), axis=1)
    last_cls = jnp.sum((ends[-1] - 1 >= ends).astype(jnp.int32))
    tile_cls = jnp.where(tile_start < ends[-1], tile_cls, last_cls)
    pair = jnp.asarray(MOE_PAIRS, jnp.int32)[tile_cls % len(MOE_PAIRS)]
    base = (tile_cls // len(MOE_PAIRS)) * MOE_EXPERTS_PER_GROUP
    idle = n_valid + jnp.arange(MOE_CLASSES, dtype=jnp.int32)
    fill_start = jnp.concatenate([jnp.maximum(ends - tm, 0), jnp.minimum(idle, n_tiles - 1) * tm])
    fill_on = jnp.concatenate([counts > 0, idle < n_tiles]).astype(jnp.int32)
    return pos, base + pair[:, 0], base + pair[:, 1], n_valid.reshape(1), fill_start, fill_on


def _moe_dispatch_kernel(pos_ref, fs_ref, fo_ref, x_ref, xs_hbm, stage, zbuf, sem, zsem, *, tm, tz):
    i = pl.program_id(0)
    n = pl.num_programs(0)

    def fill_copy(c):
        return pltpu.make_async_copy(zbuf, xs_hbm.at[pl.ds(pl.multiple_of(fs_ref[c], tz), tz), :], zsem)

    @pl.when(i == 0)
    def _():
        zbuf[...] = jnp.zeros_like(zbuf)
        for c in range(2 * MOE_CLASSES):
            @pl.when(fo_ref[c] > 0)
            def _():
                fill_copy(c).start()
        for c in range(2 * MOE_CLASSES):
            @pl.when(fo_ref[c] > 0)
            def _():
                fill_copy(c).wait()

    def tile_wait(slot):
        pltpu.make_async_copy(stage.at[slot], xs_hbm.at[pl.ds(0, tm), :], sem.at[slot]).wait()

    slot = i % 2

    @pl.when(i >= 2)
    def _():
        tile_wait(slot)

    stage[slot] = x_ref[...]

    def body(r, carry):
        p = pos_ref[i * tm + r]
        pltpu.make_async_copy(stage.at[slot, pl.ds(r, 1), :], xs_hbm.at[pl.ds(p, 1), :],
                              sem.at[slot]).start()
        return carry
    lax.fori_loop(0, tm, body, 0, unroll=8)

    @pl.when(i == n - 1)
    def _():
        tile_wait(slot)

        @pl.when(n >= 2)
        def _():
            tile_wait(1 - slot)


def moe_dispatch(xr, pos, fill_start, fill_on, *, tm, tz, n_slots):
    T, width = xr.shape
    grid_spec = pltpu.PrefetchScalarGridSpec(
        num_scalar_prefetch=3,
        grid=(T // tm,),
        in_specs=[pl.BlockSpec((tm, width), lambda i, pos, fs, fo: (i, 0))],
        out_specs=pl.BlockSpec(memory_space=pl.ANY),
        scratch_shapes=[pltpu.VMEM((2, tm, width), f32), pltpu.VMEM((tz, width), f32),
                        pltpu.SemaphoreType.DMA((2,)), pltpu.SemaphoreType.DMA(())],
    )
    return pl.pallas_call(
        functools.partial(_moe_dispatch_kernel, tm=tm, tz=tz),
        out_shape=jax.ShapeDtypeStruct((n_slots, width), f32),
        grid_spec=grid_spec,
        compiler_params=_cparams(("arbitrary",)),
        name="moe_dispatch",
    )(pos, fill_start, fill_on, xr)


def _moe_pair_kernel(ea_ref, eb_ref, nv_ref, xs_ref, wga, wua, wda, wgb, wub, wdb, o_ref):
    @pl.when(pl.program_id(0) >= nv_ref[0])
    def _():
        o_ref[...] = jnp.zeros_like(o_ref)

    @pl.when(pl.program_id(0) < nv_ref[0])
    def _():
        D = o_ref.shape[1]
        x = xs_ref[:, :D].astype(bf16)
        rec = xs_ref[:, D:]
        first = rec[:, 0:1] == ea_ref[pl.program_id(0)].astype(f32)
        w_a = jnp.where(first, rec[:, 2:3], rec[:, 3:4])
        w_b = jnp.where(first, rec[:, 3:4], rec[:, 2:3])

        gates = [jnp.dot(x, wg[0], preferred_element_type=f32) for wg in (wga, wgb)]
        ups = [jnp.dot(x, wu[0], preferred_element_type=f32) for wu in (wua, wub)]
        hids = [(g * _sigmoid(g) * u).astype(bf16) for g, u in zip(gates, ups)]
        y_a, y_b = [jnp.dot(h, wd[0], preferred_element_type=f32) for h, wd in zip(hids, (wda, wdb))]
        o_ref[...] = w_a * y_a + w_b * y_b


def moe_pair_experts(xs, e_a, e_b, n_valid, wg, wu, wd, *, tm, n_tiles):
    width = xs.shape[1]
    D = wg.shape[1]
    held = lambda i, ea, eb, nv: (jnp.minimum(i, nv[0] - 1), 0)
    tile = lambda i, ea, eb, nv: (i, 0)
    wa = lambda shape: pl.BlockSpec(shape, lambda i, ea, eb, nv: (ea[i], 0, 0))
    wb = lambda shape: pl.BlockSpec(shape, lambda i, ea, eb, nv: (eb[i], 0, 0))
    grid_spec = pltpu.PrefetchScalarGridSpec(
        num_scalar_prefetch=3,
        grid=(n_tiles,),
        in_specs=[pl.BlockSpec((tm, width), held),
                  wa((1, D, MOE_FF)), wa((1, D, MOE_FF)), wa((1, MOE_FF, D)),
                  wb((1, D, MOE_FF)), wb((1, D, MOE_FF)), wb((1, MOE_FF, D))],
        out_specs=pl.BlockSpec((tm, D), tile),
    )
    return pl.pallas_call(
        _moe_pair_kernel,
        out_shape=jax.ShapeDtypeStruct((n_tiles * tm, D), f32),
        grid_spec=grid_spec,
        compiler_params=_cparams(("arbitrary",)),
        name="moe_experts",
    )(e_a, e_b, n_valid, xs, wg, wu, wd, wg, wu, wd)


def _moe_combine_kernel(pos_ref, h_ref, y_hbm, gf_ref, o_ref, ybuf, sem, *, tm):
    i = pl.program_id(0)
    n = pl.num_programs(0)

    def gather(tile, slot):
        def body(r, carry):
            p = pos_ref[tile * tm + r]
            pltpu.make_async_copy(y_hbm.at[pl.ds(p, 1), :],
                                  ybuf.at[slot, pl.ds(r, 1), :], sem.at[slot]).start()
            return carry
        lax.fori_loop(0, tm, body, 0, unroll=8)

    @pl.when(i == 0)
    def _():
        gather(0, 0)

    @pl.when(i + 1 < n)
    def _():
        gather(i + 1, (i + 1) % 2)

    slot = i % 2
    pltpu.make_async_copy(y_hbm.at[pl.ds(0, tm), :], ybuf.at[slot], sem.at[slot]).wait()
    o_ref[...] = _rmsnorm_rows(h_ref[...] + ybuf[slot], gf_ref[...])


def moe_combine(h, y_sorted, pos, g_final, *, tm):
    T, D = h.shape
    grid_spec = pltpu.PrefetchScalarGridSpec(
        num_scalar_prefetch=1,
        grid=(T // tm,),
        in_specs=[
            pl.BlockSpec((tm, D), lambda i, pos: (i, 0)),
            pl.BlockSpec(memory_space=pl.ANY),
            pl.BlockSpec((1, D), lambda i, pos: (0, 0)),
        ],
        out_specs=pl.BlockSpec((tm, D), lambda i, pos: (i, 0)),
        scratch_shapes=[pltpu.VMEM((2, tm, D), f32), pltpu.SemaphoreType.DMA((2,))],
    )
    return pl.pallas_call(
        functools.partial(_moe_combine_kernel, tm=tm),
        out_shape=jax.ShapeDtypeStruct((T, D), f32),
        grid_spec=grid_spec,
        compiler_params=_cparams(("arbitrary",)),
        name="moe_combine",
    )(pos, h, y_sorted, g_final.reshape(1, D))


def _pad_rows(w, rows):
    return jnp.pad(w, ((0, rows - w.shape[0]), (0, 0)))


def _pack_lora_cols(t):
    o1, o2 = DECAY_LORA, DECAY_LORA + AAA_LORA
    pad = lambda a, n: jnp.pad(a, [(0, 0)] * (a.ndim - 1) + [(0, n - a.shape[-1])])
    return jnp.concatenate([pad(t[..., :o1], 128), pad(t[..., o1:o2], 128),
                            pad(t[..., o2:], 256)], axis=-1)


def _layer(x2, mem2, p, *, batch, seq, cfg):
    D = D_MODEL
    w_in = p['w_in']
    w_main = w_in.astype(bf16)
    w_lora = _pack_lora_cols(w_in[:, MAIN_COLS:]).astype(bf16)
    mu_cols = jnp.concatenate([jnp.zeros((POOL_WIDTH,), f32), p['rwkv_mu'][:3 * RWKV_WIDTH],
                               _pack_lora_cols(p['rwkv_mu'][3 * RWKV_WIDTH:])])

    proj, pool_out = in_proj_mixers(x2, p['norm_mix_g'], w_main, w_lora, mu_cols,
                                    p['pool_w'].astype(bf16), p['pool_scale'],
                                    tm=cfg['tm'], tn=cfg['tn_in'], seq=seq)
    lw, k, kk, kka, g = rwkv_prep(
        proj, p['rwkv_w0'], _pad_rows(p['rwkv_w2'], 128).astype(bf16),
        p['rwkv_a0'], _pad_rows(p['rwkv_a2'], 128).astype(bf16),
        _pad_rows(p['rwkv_g2'], 256).astype(bf16), p['rwkv_k_k'], p['rwkv_k_a'], ts=cfg['ts_prep'])
    rwkv_out, (w_out, w_q, w_kv, w_o, wg, wu, wd) = rwkv_scan(
        proj, lw, k, kk, kka, g, p['rwkv_ln_w'], p['rwkv_ln_b'], p['rwkv_r_k'],
        [p['w_out'], p['xattn_w_q'], p['xattn_w_kv'], p['xattn_w_o'],
         p['moe_w_gate'], p['moe_w_up'], p['moe_w_down']],
        batch=batch, seq=seq, rows=cfg['scan_rows'], n_hg=cfg['scan_hg'])
    h1, q = out_q_proj([pool_out, rwkv_out], [w_out[:POOL_WIDTH], w_out[POOL_WIDTH:]], x2,
                       p['norm_xattn_g'], w_q, tm=cfg['tm'], tn=cfg['tn'])
    kv = norm_matmul_resident(mem2, p['norm_mem_g'], w_kv,
                              tm=min(cfg['tm'], mem2.shape[0]), tn=cfg['tn'], name="kv_proj")

    w_router = jnp.pad(jnp.concatenate([p['moe_w_group'], p['moe_w_expert']], axis=1),
                       ((0, 0), (0, ROUTER_LANES - MOE_GROUPS - MOE_EXPERTS)))
    b_router = jnp.pad(jnp.concatenate([p['moe_b_group'], p['moe_b_expert']]),
                       (0, ROUTER_LANES - MOE_GROUPS - MOE_EXPERTS)).reshape(1, ROUTER_LANES)
    h2, xr, rec = attn_oproj_router(q, kv, w_o, h1, p['norm_ffn_g'],
                               w_router, b_router, tm=cfg['tm'], tn=cfg['tn'], seq=seq)
    tm_e = cfg['tm_moe']
    n_tiles = xr.shape[0] // tm_e + MOE_CLASSES
    pos, e_a, e_b, n_valid, fill_start, fill_on = _route_metadata(
        rec, tm=tm_e, n_tiles=n_tiles)
    xs = moe_dispatch(xr, pos, fill_start, fill_on, tm=cfg['tm_disp'], tz=tm_e,
                      n_slots=n_tiles * tm_e)
    y_sorted = moe_pair_experts(xs, e_a, e_b, n_valid, wg, wu, wd, tm=tm_e, n_tiles=n_tiles)
    return moe_combine(h2, y_sorted, pos, p['norm_final_g'], tm=cfg['tm_comb'])


_CFG = dict(tm=512, tn=1024, tn_in=512, ts_prep=512, scan_rows=256, scan_hg=4,
            tm_moe=256, tm_disp=512, tm_comb=512)


def kernel(x, mem, norm_mix_g, w_in, pool_w, pool_scale, rwkv_mu, rwkv_w0, rwkv_w2, rwkv_a0,
           rwkv_a2, rwkv_g2, rwkv_k_k, rwkv_k_a, rwkv_r_k, rwkv_ln_w, rwkv_ln_b, w_out,
           norm_xattn_g, norm_mem_g, xattn_w_q, xattn_w_kv, xattn_w_o, norm_ffn_g,
           moe_w_group, moe_b_group, moe_w_expert, moe_b_expert, moe_w_gate, moe_w_up,
           moe_w_down, norm_final_g):
    batch, seq, D = x.shape
    p = dict(norm_mix_g=norm_mix_g[0], w_in=w_in[0], pool_w=pool_w[0], pool_scale=pool_scale[0],
             rwkv_mu=rwkv_mu[0], rwkv_w0=rwkv_w0[0], rwkv_w2=rwkv_w2[0], rwkv_a0=rwkv_a0[0],
             rwkv_a2=rwkv_a2[0], rwkv_g2=rwkv_g2[0], rwkv_k_k=rwkv_k_k[0], rwkv_k_a=rwkv_k_a[0],
             rwkv_r_k=rwkv_r_k[0], rwkv_ln_w=rwkv_ln_w[0], rwkv_ln_b=rwkv_ln_b[0], w_out=w_out[0],
             norm_xattn_g=norm_xattn_g[0], norm_mem_g=norm_mem_g[0], xattn_w_q=xattn_w_q[0],
             xattn_w_kv=xattn_w_kv[0], xattn_w_o=xattn_w_o[0], norm_ffn_g=norm_ffn_g[0],
             moe_w_group=moe_w_group[0], moe_b_group=moe_b_group[0], moe_w_expert=moe_w_expert[0],
             moe_b_expert=moe_b_expert[0], moe_w_gate=moe_w_gate[0], moe_w_up=moe_w_up[0],
             moe_w_down=moe_w_down[0], norm_final_g=norm_final_g)
    out = _layer(x.reshape(batch * seq, D), mem.reshape(batch * MEM_LEN, D), p,
                 batch=batch, seq=seq, cfg=_CFG)
    return out.reshape(batch, seq, D)
```

```python
import functools

import jax
import jax.numpy as jnp
from jax import lax
from jax.experimental import pallas as pl
from jax.experimental.pallas import tpu as pltpu

f32 = jnp.float32
bf16 = jnp.bfloat16

D_MODEL = 2048
MEM_LEN = 256
NORM_EPS = 1e-6

POOL_WIDTH = 1024
POOL_WINDOWS = (2, 4, 8, 16)
POOL_GROUP = 256
POOL_HALO = 16
RWKV_WIDTH = 1024
RWKV_HEAD = 64
RWKV_HEADS = 16
GN_EPS = 64e-5
DECAY_SCALE = 0.6065306597126334
DECAY_LORA = 64
AAA_LORA = 64
GATE_LORA = 160
LORA_PAD = 512
LORA_W_OFF, LORA_A_OFF, LORA_G_OFF = 0, 128, 256
MAIN_COLS = POOL_WIDTH + 3 * RWKV_WIDTH
PROJ_COLS = MAIN_COLS + LORA_PAD

RWKV_CHUNK = 64
HEADS_PER_STEP = 4
HG_LANES = HEADS_PER_STEP * RWKV_HEAD

XATTN_HEADS = 4
XATTN_HEAD_DIM = 512

MOE_GROUPS = 4
MOE_EXPERTS_PER_GROUP = 4
MOE_EXPERTS = 16
MOE_FF = 512
ROUTER_LANES = 128

VMEM_LIMIT = 56 * 1024 * 1024
ROW_DMA_UNROLL = 8


def _cparams(sem):
    return pltpu.CompilerParams(dimension_semantics=sem, vmem_limit_bytes=VMEM_LIMIT)


def _split_dot(x, w_bf16):
    hi = x.astype(bf16)
    lo = (x - hi.astype(f32)).astype(bf16)
    return (jnp.dot(hi, w_bf16, preferred_element_type=f32)
            + jnp.dot(lo, w_bf16, preferred_element_type=f32))


def _resident(shape):
    return pl.BlockSpec(shape, lambda i: (0,) * len(shape), pipeline_mode=pl.Buffered(1))


def _rmsnorm_rows(x, g):
    ms = jnp.mean(x * x, axis=-1, keepdims=True)
    return x * lax.rsqrt(ms + NORM_EPS) * g


def _norm_mm_res_kernel(x_ref, g_ref, w_ref, o_ref, *, tn):
    xn = _rmsnorm_rows(x_ref[...], g_ref[...]).astype(bf16)
    for n0 in range(0, o_ref.shape[1], tn):
        o_ref[:, n0:n0 + tn] = jnp.dot(xn, w_ref[:, n0:n0 + tn],
                                       preferred_element_type=f32).astype(o_ref.dtype)


def norm_matmul_resident(x, g, w, *, tm, tn, name):
    M, K = x.shape
    N = w.shape[1]
    return pl.pallas_call(
        functools.partial(_norm_mm_res_kernel, tn=tn),
        out_shape=jax.ShapeDtypeStruct((M, N), bf16),
        grid=(M // tm,),
        in_specs=[pl.BlockSpec((tm, K), lambda i: (i, 0)), _resident((1, K)), _resident((K, N))],
        out_specs=pl.BlockSpec((tm, N), lambda i: (i, 0)),
        compiler_params=_cparams(("parallel",)),
        name=name,
    )(x, g.reshape(1, K), w)


def _in_proj_kernel(x_ref, xh_ref, g_ref, w_ref, wl_ref, mu_ref, pw_ref, ps_ref, o_ref, pool_ref,
                    *, tn, tiles_per_seq):
    tm = x_ref.shape[0]
    tile_in_seq = pl.program_id(0) % tiles_per_seq
    first = tile_in_seq == 0
    xn = _rmsnorm_rows(x_ref[...], g_ref[...]).astype(bf16)
    xh = jnp.where(first, 0.0, _rmsnorm_rows(xh_ref[...], g_ref[...])).astype(bf16)
    lhs = jnp.concatenate([xh, xn], axis=0)
    pos = tile_in_seq * tm + lax.broadcasted_iota(jnp.int32, (tm, 1), 0)
    n_cols = POOL_WIDTH + o_ref.shape[1]
    def slab(n0):
        w = w_ref[:, n0:n0 + tn] if n0 < MAIN_COLS else wl_ref[:, n0 - MAIN_COLS:n0 - MAIN_COLS + tn]
        return jnp.dot(lhs, w, preferred_element_type=f32)

    def pool_slab(n0, res):
        p = res[POOL_HALO:]
        acc = res
        shift = 1
        for gi in range(n0 // POOL_GROUP, (n0 + tn) // POOL_GROUP):
            win = POOL_WINDOWS[gi]
            while shift < win:
                acc = acc + pltpu.roll(acc, shift, 0)
                shift *= 2
            c0 = gi * POOL_GROUP - n0
            cnt = jnp.minimum(pos + 1, win).astype(f32)
            pooled = acc[POOL_HALO:, c0:c0 + POOL_GROUP] / cnt - p[:, c0:c0 + POOL_GROUP]
            mixed = jnp.dot(pooled.astype(bf16), pw_ref[gi], preferred_element_type=f32)
            lo, hi = gi * POOL_GROUP, (gi + 1) * POOL_GROUP
            pool_ref[:, lo:hi] = (mixed * ps_ref[:, lo:hi]).astype(pool_ref.dtype)

    def shift_slab(n0, res):
        p = res[POOL_HALO:]
        prev = pltpu.roll(res, 1, 0)[POOL_HALO:]
        z = p + (prev - p) * mu_ref[:, n0:n0 + tn]
        o_ref[:, n0 - POOL_WIDTH:n0 - POOL_WIDTH + tn] = z.astype(o_ref.dtype)

    pending = None
    for n0 in range(0, n_cols, tn):
        res = slab(n0)
        if pending is not None:
            pending()
        pending = functools.partial(pool_slab if n0 < POOL_WIDTH else shift_slab, n0, res)
    pending()


def in_proj_mixers(x, g, w_main, w_lora, mu_cols, pool_w, pool_scale, *, tm, tn, seq):
    M, K = x.shape
    n_rwkv = MAIN_COLS - POOL_WIDTH + w_lora.shape[1]
    hb = tm // POOL_HALO
    return pl.pallas_call(
        functools.partial(_in_proj_kernel, tn=tn, tiles_per_seq=seq // tm),
        out_shape=[jax.ShapeDtypeStruct((M, n_rwkv), bf16),
                   jax.ShapeDtypeStruct((M, POOL_WIDTH), bf16)],
        grid=(M // tm,),
        in_specs=[pl.BlockSpec((tm, K), lambda i: (i, 0)),
                  pl.BlockSpec((POOL_HALO, K), lambda i: (jnp.maximum(i * hb - 1, 0), 0)),
                  _resident((1, K)), _resident(w_main.shape), _resident(w_lora.shape),
                  _resident((1, POOL_WIDTH + n_rwkv)), _resident(pool_w.shape),
                  _resident((1, POOL_WIDTH))],
        out_specs=[pl.BlockSpec((tm, n_rwkv), lambda i: (i, 0)),
                   pl.BlockSpec((tm, POOL_WIDTH), lambda i: (i, 0))],
        compiler_params=_cparams(("parallel",)),
        name="in_proj",
    )(x, x, g.reshape(1, K), w_main, w_lora, mu_cols.reshape(1, -1), pool_w,
      pool_scale.reshape(1, POOL_WIDTH))


def _out_q_kernel(*refs, n, tn):
    a_refs, w_refs = refs[:n], refs[n:2 * n]
    res_ref, g_ref, wq_ref, h_ref, q_ref = refs[2 * n:]
    for n0 in range(0, h_ref.shape[1], tn):
        acc = res_ref[:, n0:n0 + tn]
        for a_ref, w_ref in zip(a_refs, w_refs):
            acc = acc + jnp.dot(a_ref[...], w_ref[:, n0:n0 + tn], preferred_element_type=f32)
        h_ref[:, n0:n0 + tn] = acc
    xn = _rmsnorm_rows(h_ref[...], g_ref[...]).astype(bf16)
    for n0 in range(0, q_ref.shape[1], tn):
        q_ref[:, n0:n0 + tn] = jnp.dot(xn, wq_ref[:, n0:n0 + tn],
                                       preferred_element_type=f32).astype(q_ref.dtype)


def out_q_proj(a_list, w_list, res, g, w_q, *, tm, tn):
    M, N = res.shape
    n = len(a_list)
    row = lambda width: pl.BlockSpec((tm, width), lambda i: (i, 0))
    return pl.pallas_call(
        functools.partial(_out_q_kernel, n=n, tn=tn),
        out_shape=[jax.ShapeDtypeStruct((M, N), f32), jax.ShapeDtypeStruct((M, w_q.shape[1]), bf16)],
        grid=(M // tm,),
        in_specs=([row(a.shape[1]) for a in a_list] + [_resident(w.shape) for w in w_list]
                  + [row(N), _resident((1, N)), _resident(w_q.shape)]),
        out_specs=[row(N), row(w_q.shape[1])],
        compiler_params=_cparams(("parallel",)),
        name="out_q_proj",
    )(*a_list, *w_list, res, g.reshape(1, N), w_q)


def _head_ones():
    r = lax.broadcasted_iota(jnp.int32, (HG_LANES, HG_LANES), 0) // RWKV_HEAD
    c = lax.broadcasted_iota(jnp.int32, (HG_LANES, HG_LANES), 1) // RWKV_HEAD
    return jnp.where(r == c, 1.0, 0.0).astype(bf16)


def _head_sum(x, ones_bd):
    parts = [_split_dot(x[:, c:c + HG_LANES], ones_bd) for c in range(0, x.shape[1], HG_LANES)]
    return parts[0] if len(parts) == 1 else jnp.concatenate(parts, axis=1)


def _sigmoid(x):
    return 1.0 / (1.0 + jnp.exp(-x))


def _rwkv_prep_kernel(k_ref, lo_ref, w0_ref, w2_ref, a0_ref, a2_ref, g2_ref, kk_ref, ka_ref,
                      lw_out, k_out, kk_out, kka_out, g_out):
    k = k_ref[...].astype(f32)
    lo = lo_ref[...].astype(f32)
    w_lo = lo[:, LORA_W_OFF:LORA_W_OFF + 128]
    a_lo = lo[:, LORA_A_OFF:LORA_A_OFF + 128]
    g_lo = lo[:, LORA_G_OFF:LORA_G_OFF + 256]

    wx = w0_ref[...] + jnp.dot(jnp.tanh(w_lo).astype(bf16), w2_ref[...], preferred_element_type=f32)
    lw_out[...] = -DECAY_SCALE * _sigmoid(wx)
    a = _sigmoid(a0_ref[...] + jnp.dot(a_lo.astype(bf16), a2_ref[...], preferred_element_type=f32))
    g = jnp.dot(_sigmoid(g_lo).astype(bf16), g2_ref[...], preferred_element_type=f32)

    ones_bd = _head_ones()
    kk = k * kk_ref[...]
    kk = kk * lax.rsqrt(jnp.maximum(_head_sum(kk * kk, ones_bd), 1e-24))
    k = k * (1.0 + (a - 1.0) * ka_ref[...])

    k_out[...] = k.astype(k_out.dtype)
    kk_out[...] = kk.astype(kk_out.dtype)
    kka_out[...] = (kk * a).astype(kka_out.dtype)
    g_out[...] = g.astype(g_out.dtype)


def rwkv_prep(proj, w0, w2p, a0, a2p, g2p, k_k, k_a, *, ts):
    T = proj.shape[0]
    C = RWKV_WIDTH
    k_blk = 1
    lora_blk = 3 * C // LORA_PAD

    def full(shape):
        return pl.BlockSpec(shape, lambda i: (0,) * len(shape))

    row = lambda t: t.reshape(1, -1)
    out_sd = lambda dt: jax.ShapeDtypeStruct((T, C), dt)
    out_spec = pl.BlockSpec((ts, C), lambda i: (i, 0))
    return pl.pallas_call(
        _rwkv_prep_kernel,
        out_shape=[out_sd(f32), out_sd(bf16), out_sd(bf16), out_sd(bf16), out_sd(bf16)],
        grid=(T // ts,),
        in_specs=[
            pl.BlockSpec((ts, C), lambda i: (i, k_blk)),
            pl.BlockSpec((ts, LORA_PAD), lambda i: (i, lora_blk)),
            full((1, C)), full((128, C)), full((1, C)), full((128, C)), full((256, C)),
            full((1, C)), full((1, C)),
        ],
        out_specs=[out_spec] * 5,
        compiler_params=_cparams(("parallel",)),
        name="rwkv_prep",
    )(proj, proj, row(w0), w2p, row(a0), a2p, g2p, row(k_k), row(k_a))


def _bd_mask():
    r = lax.broadcasted_iota(jnp.int32, (HG_LANES, HG_LANES), 0) // RWKV_CHUNK
    c = lax.broadcasted_iota(jnp.int32, (HG_LANES, HG_LANES), 1) // RWKV_HEAD
    return r == c


def _rwkv_scan_kernel(r_ref, v_ref, lw_ref, k_ref, kk_ref, kka_ref, g_ref,
                      lnw_ref, lnb_ref, rk_ref, *rest, n_chunks, n_hg, n_cast):
    L = RWKV_CHUNK
    W = HG_LANES
    cast_in, o_ref, cast_out, state_ref = rest[:n_cast], rest[n_cast], rest[n_cast + 1:-1], rest[-1]
    for w_ref, wb_ref in zip(cast_in, cast_out):
        wb_ref[...] = w_ref[...].astype(bf16)

    @pl.when(pl.program_id(2) == 0)
    def _():
        state_ref[...] = jnp.zeros_like(state_ref)

    bd_mask = _bd_mask()
    ones_bd = _head_ones()

    def bd(x):
        tiled = jnp.concatenate([x] * HEADS_PER_STEP, axis=0)
        return jnp.where(bd_mask, tiled, 0.0).astype(bf16)

    def bd_t(x):
        tiled = jnp.concatenate([x] * HEADS_PER_STEP, axis=0)
        return jnp.where(bd_mask, tiled, 0.0).T.astype(bf16)

    def mm(a, b_bf16):
        return jnp.dot(a.astype(bf16), b_bf16, preferred_element_type=f32)

    def mm_nt(a, b_bf16):
        return lax.dot_general(a.astype(bf16), b_bf16, (((1,), (1,)), ((), ())),
                               preferred_element_type=f32)

    def mm_tn(a_bf16, b_bf16):
        return lax.dot_general(a_bf16, b_bf16, (((0,), (0,)), ((), ())),
                               preferred_element_type=f32)

    def split(x):
        hi = x.astype(bf16)
        return hi, (x - hi.astype(f32)).astype(bf16)

    def head_sums(xs):
        tot = mm(jnp.concatenate(xs, axis=0), ones_bd)
        return [tot[j * L:(j + 1) * L] for j in range(len(xs))]

    t_idx = lax.broadcasted_iota(jnp.int32, (L, W), 0)
    s_idx = lax.broadcasted_iota(jnp.int32, (L, W), 1) % L
    strict = t_idx > s_idx
    incl = t_idx >= s_idx
    eye_all = jnp.where(t_idx == s_idx, 1.0, 0.0)
    tri = jnp.where(lax.broadcasted_iota(jnp.int32, (L, L), 0)
                    >= lax.broadcasted_iota(jnp.int32, (L, L), 1), 1.0, 0.0).astype(bf16)
    blockdiag = (lax.broadcasted_iota(jnp.int32, (W, W), 0) // RWKV_HEAD
                 == lax.broadcasted_iota(jnp.int32, (W, W), 1) // RWKV_HEAD)

    def phase1(items):
        I = range(len(items))
        rows = [pl.ds(c * L, L) for _, c in items]
        cols = [pl.ds(h * W, W) for h, _ in items]
        ld = lambda ref, i: ref[rows[i], cols[i]]
        r = [ld(r_ref, i).astype(f32) for i in I]
        k = [ld(k_ref, i).astype(f32) for i in I]
        v = [ld(v_ref, i).astype(f32) for i in I]
        lw = [ld(lw_ref, i) for i in I]
        lw_s = [split(x) for x in lw]
        cs = [jnp.dot(tri, hi, preferred_element_type=f32) + jnp.dot(tri, lo, preferred_element_type=f32)
              for hi, lo in lw_s]
        g_inv = [jnp.exp(-x) for x in cs]
        to_end = [jnp.exp(x[L - 1:L, :] - x) for x in cs]
        a_t = [-ld(kk_ref, i).astype(f32) * jnp.exp(cs[i] - lw[i]) for i in I]
        r_t = [r[i] * jnp.exp(cs[i]) for i in I]
        kka = [ld(kka_ref, i).astype(f32) for i in I]
        b_t = [kka[i] * g_inv[i] for i in I]
        k_t = [k[i] * g_inv[i] for i in I]
        b_e = [kka[i] * to_end[i] for i in I]
        k_e = [k[i] * to_end[i] for i in I]
        v_bd = [bd(x) for x in v]
        ar = [jnp.concatenate([a_t[i], r_t[i]], axis=0) for i in I]
        m_b = [mm(ar[i], bd_t(b_t[i])) for i in I]
        m_k = [mm(ar[i], bd_t(k_t[i])) for i in I]
        m_ab = [jnp.where(strict, x[:L], 0.0) for x in m_b]
        m_rb = [jnp.where(incl, x[L:], 0.0) for x in m_b]
        m_ak = [jnp.where(strict, x[:L], 0.0) for x in m_k]
        m_rk = [jnp.where(incl, x[L:], 0.0) for x in m_k]
        rk_sum = head_sums([r[i] * k[i] * rk_ref[:, cols[i]] for i in I])
        bonus = [rk_sum[i] * v[i] for i in I]
        tinv = [eye_all + x for x in m_ab]
        p = [mm(x, bd(x)) for x in m_ab]
        m = 2
        while m < L:
            w_p = [bd(x) for x in p]
            m *= 2
            if m < L:
                both = [mm(jnp.concatenate([tinv[i], p[i]], axis=0), w_p[i]) for i in I]
                tinv = [tinv[i] + both[i][:L] for i in I]
                p = [x[L:] for x in both]
            else:
                tinv = [tinv[i] + mm(tinv[i], w_p[i]) for i in I]
        xm = [mm(jnp.concatenate([m_ak[i], m_rk[i]], axis=0), v_bd[i]) for i in I]
        u_v = [mm(tinv[i], bd(xm[i][:L])) for i in I]
        a_h = [mm(tinv[i], bd(a_t[i])) for i in I]
        r_h = [r_t[i] + mm(m_rb[i], bd(a_h[i])) for i in I]
        y_v = [mm(m_rb[i], bd(u_v[i])) + xm[i][L:] for i in I]
        p_t = [jnp.where(blockdiag, mm_tn(a_h[i].astype(bf16), b_e[i].astype(bf16)), 0.0).astype(bf16)
               for i in I]
        g_t = [jnp.where(blockdiag,
                         mm_tn(jnp.concatenate([u_v[i], v[i]], axis=0).astype(bf16),
                               jnp.concatenate([b_e[i], k_e[i]], axis=0).astype(bf16)), 0.0) for i in I]
        decay = [jnp.exp(x[L - 1:L, :]) for x in cs]
        for i in I:
            done.append(dict(hg=items[i][0], rows=rows[i], cols=cols[i], r_h=r_h[i], y_v=y_v[i],
                             p_t=p_t[i], g_t=g_t[i], decay=decay[i], bonus=bonus[i]))

    def chain_step(chs):
        ys = []
        for ch in chs:
            state = states[ch['hg']]
            state_b = state.astype(bf16)
            ys.append(mm_nt(ch['r_h'], state_b) + ch['y_v'])
            states[ch['hg']] = (state * ch['decay']
                                + jnp.dot(state_b, ch['p_t'], preferred_element_type=f32) + ch['g_t'])
        means = head_sums(ys)
        ycs = [y - m * (1.0 / RWKV_HEAD) for y, m in zip(ys, means)]
        vrs = head_sums([yc * yc for yc in ycs])
        for ch, yc, vr in zip(chs, ycs, vrs):
            rw, cl = ch['rows'], ch['cols']
            yn = yc * lax.rsqrt(vr * (1.0 / RWKV_HEAD) + GN_EPS) * lnw_ref[:, cl] + lnb_ref[:, cl]
            o_ref[rw, cl] = ((yn + ch['bonus']) * g_ref[rw, cl].astype(f32)).astype(o_ref.dtype)

    states = [state_ref[h] for h in range(n_hg)]
    done = []
    phase1([(h, c) for c in range(n_chunks) for h in range(n_hg)])
    for j in range(0, len(done), n_hg):
        chain_step(done[j:j + n_hg])
    for h in range(n_hg):
        state_ref[h] = states[h]


def rwkv_scan(proj, lw, k, kk, kka, g, ln_w, ln_b, r_k, cast_weights, *, batch, seq, rows, n_hg):
    nb = seq // rows
    T = batch * seq
    width = n_hg * HG_LANES
    n_col = RWKV_WIDTH // width
    n_steps = batch * n_col * nb
    r_blk = 0
    v_blk = 2 * RWKV_WIDTH // width
    blk = pl.BlockSpec((rows, width), lambda b, h, c: (b * nb + c, h))
    par = pl.BlockSpec((1, width), lambda b, h, c: (0, h))
    slabs = [w.reshape(n_steps, w.size // (n_steps * w.shape[-1]), w.shape[-1]) for w in cast_weights]
    slab_spec = lambda w: pl.BlockSpec((1,) + w.shape[1:], lambda b, h, c: ((b * n_col + h) * nb + c, 0, 0))
    outs = pl.pallas_call(
        functools.partial(_rwkv_scan_kernel, n_chunks=rows // RWKV_CHUNK, n_hg=n_hg,
                          n_cast=len(slabs)),
        out_shape=[jax.ShapeDtypeStruct((T, RWKV_WIDTH), bf16)]
                  + [jax.ShapeDtypeStruct(w.shape, bf16) for w in slabs],
        grid=(batch, n_col, nb),
        in_specs=[pl.BlockSpec((rows, width), lambda b, h, c: (b * nb + c, r_blk + h)),
                  pl.BlockSpec((rows, width), lambda b, h, c: (b * nb + c, v_blk + h))]
                 + [blk] * 5 + [par] * 3 + [slab_spec(w) for w in slabs],
        out_specs=[blk] + [slab_spec(w) for w in slabs],
        scratch_shapes=[pltpu.VMEM((n_hg, HG_LANES, HG_LANES), f32)],
        compiler_params=_cparams(("parallel", "parallel", "arbitrary")),
        name="rwkv_scan",
    )(proj, proj, lw, k, kk, kka, g, ln_w.reshape(1, -1), ln_b.reshape(1, -1), r_k.reshape(1, -1),
      *slabs)
    return outs[0], [o.reshape(w.shape) for o, w in zip(outs[1:], cast_weights)]


def _route(hn, w, b):
    w_hi = w.astype(bf16)
    w_lo = (w - w_hi.astype(f32)).astype(bf16)
    h_hi = hn.astype(bf16)
    h_lo = (hn - h_hi.astype(f32)).astype(bf16)
    hh = jnp.dot(h_hi, jnp.concatenate([w_hi, w_lo], axis=1), preferred_element_type=f32)
    logits = (hh[:, :ROUTER_LANES] + hh[:, ROUTER_LANES:]
              + jnp.dot(h_lo, w_hi, preferred_element_type=f32)) + b

    lane_i = lax.broadcasted_iota(jnp.int32, logits.shape, 1)
    lane = lane_i.astype(f32)
    neg = -jnp.inf
    big = float(ROUTER_LANES)
    is_g = lane_i < MOE_GROUPS
    gl = jnp.where(is_g, logits, neg)
    gmax = jnp.max(gl, axis=-1, keepdims=True)
    g_idx = jnp.min(jnp.where(gl == gmax, lane, big), axis=-1, keepdims=True)
    g_w = 1.0 / jnp.sum(jnp.exp(gl - gmax), axis=-1, keepdims=True)

    lo_lane = MOE_GROUPS + g_idx * MOE_EXPERTS_PER_GROUP
    sel = (lane >= lo_lane) & (lane < lo_lane + MOE_EXPERTS_PER_GROUP)
    el = jnp.where(sel, logits, neg)
    v1 = jnp.max(el, axis=-1, keepdims=True)
    i1 = jnp.min(jnp.where(el == v1, lane, big), axis=-1, keepdims=True)
    el2 = jnp.where(lane == i1, neg, el)
    v2 = jnp.max(el2, axis=-1, keepdims=True)
    i2 = jnp.min(jnp.where(el2 == v2, lane, big), axis=-1, keepdims=True)
    e2 = jnp.exp(v2 - v1)
    p1 = 1.0 / (1.0 + e2)
    p2 = e2 / (1.0 + e2)
    return (jnp.where(lane_i == 0, i1 - MOE_GROUPS, 0.0)
            + jnp.where(lane_i == 1, i2 - MOE_GROUPS, 0.0)
            + jnp.where(lane_i == 2, p1 * g_w, 0.0)
            + jnp.where(lane_i == 3, p2 * g_w, 0.0))


def _attn_oproj_router_kernel(q_ref, kv_ref, wo_ref, res_ref, g_ref, wr_ref, br_ref, h_ref, xr_ref,
                              rec_ref, *, tn):
    hd = XATTN_HEAD_DIM
    H = range(XATTN_HEADS)
    s = [lax.dot_general(q_ref[:, h * hd:(h + 1) * hd], kv_ref[:, h * hd:(h + 1) * hd],
                         (((1,), (1,)), ((), ())), preferred_element_type=f32) * (hd ** -0.5) for h in H]
    e = [jnp.exp(x - jnp.max(x, axis=-1, keepdims=True)) for x in s]
    l = [jnp.sum(x, axis=-1, keepdims=True) for x in e]
    pv = [jnp.dot(e[h].astype(bf16), kv_ref[:, (XATTN_HEADS + h) * hd:(XATTN_HEADS + h + 1) * hd],
                  preferred_element_type=f32) for h in H]
    o = jnp.concatenate([(pv[h] / l[h]).astype(bf16) for h in H], axis=1)
    for n0 in range(0, h_ref.shape[1], tn):
        h_ref[:, n0:n0 + tn] = res_ref[:, n0:n0 + tn] + jnp.dot(
            o, wo_ref[:, n0:n0 + tn], preferred_element_type=f32)
    hn = _rmsnorm_rows(h_ref[...], g_ref[...])
    D = hn.shape[1]
    rec = _route(hn, wr_ref[...], br_ref[...])
    xr_ref[:, :D] = hn
    xr_ref[:, D:] = rec
    rec_ref[...] = rec


def attn_oproj_router(q, kv, w_o, res, g, w_router, b_router, *, tm, tn, seq):
    T, D = res.shape
    tiles_per_seq = seq // tm
    row = lambda width: pl.BlockSpec((tm, width), lambda i: (i, 0))
    return pl.pallas_call(
        functools.partial(_attn_oproj_router_kernel, tn=tn),
        out_shape=[jax.ShapeDtypeStruct((T, D), f32),
                   jax.ShapeDtypeStruct((T, D + ROUTER_LANES), f32),
                   jax.ShapeDtypeStruct((T, ROUTER_LANES), f32)],
        grid=(T // tm,),
        in_specs=[row(D), pl.BlockSpec((MEM_LEN, 2 * D), lambda i: (i // tiles_per_seq, 0)),
                  _resident((D, D)), row(D), _resident((1, D)),
                  _resident((D, ROUTER_LANES)), _resident((1, ROUTER_LANES))],
        out_specs=[row(D), row(D + ROUTER_LANES), row(ROUTER_LANES)],
        compiler_params=_cparams(("parallel",)),
        name="attn_oproj_router",
    )(q, kv, w_o, res, g.reshape(1, D), w_router, b_router)


MOE_PAIRS = ((0, 1), (0, 2), (0, 3), (1, 3), (1, 2), (3, 2))
MOE_CLASSES = MOE_GROUPS * len(MOE_PAIRS)


def _route_metadata(rec, *, tm, n_tiles):
    ids = rec[:, 0:2].astype(jnp.int32)
    lo = jnp.minimum(ids[:, 0], ids[:, 1])
    hi = jnp.maximum(ids[:, 0], ids[:, 1])
    a, b = lo % MOE_EXPERTS_PER_GROUP, hi % MOE_EXPERTS_PER_GROUP
    pair_id = (a * (7 - a)) // 2 + b - a - 1
    pair_id = pair_id + (pair_id == 3).astype(jnp.int32) - (pair_id == 4).astype(jnp.int32)
    cls = (lo // MOE_EXPERTS_PER_GROUP) * len(MOE_PAIRS) + pair_id
    onehot = (cls[:, None] == jnp.arange(MOE_CLASSES, dtype=jnp.int32)[None, :]).astype(jnp.int32)
    csum = jnp.cumsum(onehot, axis=0)
    counts = csum[-1]
    rank = jnp.sum(csum * onehot, axis=1) - 1
    padded = ((counts + tm - 1) // tm) * tm
    ends = jnp.cumsum(padded)
    pos = jnp.sum(onehot * (ends - padded)[None, :], axis=1) + rank
    tile_start = jnp.arange(n_tiles, dtype=jnp.int32) * tm
    n_valid = ends[-1] // tm
    tile_cls = jnp.sum((tile_start[:, None] >= ends[None, :]).astype(jnp.int32), axis=1)
    last_cls = jnp.sum((ends[-1] - 1 >= ends).astype(jnp.int32))
    tile_cls = jnp.where(tile_start < ends[-1], tile_cls, last_cls)
    pair = jnp.asarray(MOE_PAIRS, jnp.int32)[tile_cls % len(MOE_PAIRS)]
    base = (tile_cls // len(MOE_PAIRS)) * MOE_EXPERTS_PER_GROUP
    idle = n_valid + jnp.arange(MOE_CLASSES, dtype=jnp.int32)
    fill_start = jnp.concatenate([jnp.maximum(ends - tm, 0), jnp.minimum(idle, n_tiles - 1) * tm])
    fill_on = jnp.concatenate([counts > 0, idle < n_tiles]).astype(jnp.int32)
    return pos, base + pair[:, 0], base + pair[:, 1], n_valid.reshape(1), fill_start, fill_on


def _moe_dispatch_kernel(pos_ref, fs_ref, fo_ref, x_ref, xs_hbm, stage, zbuf, sem, zsem, *, tm, tz):
    i = pl.program_id(0)
    n = pl.num_programs(0)

    def fill_copy(c):
        return pltpu.make_async_copy(zbuf, xs_hbm.at[pl.ds(pl.multiple_of(fs_ref[c], tz), tz), :], zsem)

    @pl.when(i == 0)
    def _():
        zbuf[...] = jnp.zeros_like(zbuf)
        for c in range(2 * MOE_CLASSES):
            @pl.when(fo_ref[c] > 0)
            def _():
                fill_copy(c).start()
        for c in range(2 * MOE_CLASSES):
            @pl.when(fo_ref[c] > 0)
            def _():
                fill_copy(c).wait()

    def tile_wait(slot):
        pltpu.make_async_copy(stage.at[slot], xs_hbm.at[pl.ds(0, tm), :], sem.at[slot]).wait()

    slot = i % 2

    @pl.when(i >= 2)
    def _():
        tile_wait(slot)

    stage[slot] = x_ref[...]

    def body(j, carry):
        for u in range(ROW_DMA_UNROLL):
            r = j * ROW_DMA_UNROLL + u
            p = pos_ref[i * tm + r]
            pltpu.make_async_copy(stage.at[slot, pl.ds(r, 1), :], xs_hbm.at[pl.ds(p, 1), :],
                                  sem.at[slot]).start(priority=u % 2)
        return carry
    lax.fori_loop(0, tm // ROW_DMA_UNROLL, body, 0)

    @pl.when(i == n - 1)
    def _():
        tile_wait(slot)

        @pl.when(n >= 2)
        def _():
            tile_wait(1 - slot)


def moe_dispatch(xr, pos, fill_start, fill_on, *, tm, tz, n_slots):
    T, width = xr.shape
    grid_spec = pltpu.PrefetchScalarGridSpec(
        num_scalar_prefetch=3,
        grid=(T // tm,),
        in_specs=[pl.BlockSpec((tm, width), lambda i, pos, fs, fo: (i, 0))],
        out_specs=pl.BlockSpec(memory_space=pl.ANY),
        scratch_shapes=[pltpu.VMEM((2, tm, width), f32), pltpu.VMEM((tz, width), f32),
                        pltpu.SemaphoreType.DMA((2,)), pltpu.SemaphoreType.DMA(())],
    )
    return pl.pallas_call(
        functools.partial(_moe_dispatch_kernel, tm=tm, tz=tz),
        out_shape=jax.ShapeDtypeStruct((n_slots, width), f32),
        grid_spec=grid_spec,
        compiler_params=_cparams(("arbitrary",)),
        name="moe_dispatch",
    )(pos, fill_start, fill_on, xr)


def _moe_pair_kernel(ea_ref, eb_ref, nv_ref, xs_ref, wga, wua, wda, wgb, wub, wdb, o_ref):
    @pl.when(pl.program_id(0) >= nv_ref[0])
    def _():
        o_ref[...] = jnp.zeros_like(o_ref)

    @pl.when(pl.program_id(0) < nv_ref[0])
    def _():
        D = o_ref.shape[1]
        x = xs_ref[:, :D].astype(bf16)
        rec = xs_ref[:, D:]
        first = rec[:, 0:1] == ea_ref[pl.program_id(0)].astype(f32)
        w_a = jnp.where(first, rec[:, 2:3], rec[:, 3:4])
        w_b = jnp.where(first, rec[:, 3:4], rec[:, 2:3])

        gates = [jnp.dot(x, wg[0], preferred_element_type=f32) for wg in (wga, wgb)]
        ups = [jnp.dot(x, wu[0], preferred_element_type=f32) for wu in (wua, wub)]
        hids = [(g * _sigmoid(g) * u).astype(bf16) for g, u in zip(gates, ups)]
        y_a, y_b = [jnp.dot(h, wd[0], preferred_element_type=f32) for h, wd in zip(hids, (wda, wdb))]
        o_ref[...] = w_a * y_a + w_b * y_b


def moe_pair_experts(xs, e_a, e_b, n_valid, wg, wu, wd, *, tm, n_tiles):
    width = xs.shape[1]
    D = wg.shape[1]
    held = lambda i, ea, eb, nv: (jnp.minimum(i, nv[0] - 1), 0)
    tile = lambda i, ea, eb, nv: (i, 0)
    wa = lambda shape: pl.BlockSpec(shape, lambda i, ea, eb, nv: (ea[i], 0, 0))
    wb = lambda shape: pl.BlockSpec(shape, lambda i, ea, eb, nv: (eb[i], 0, 0))
    grid_spec = pltpu.PrefetchScalarGridSpec(
        num_scalar_prefetch=3,
        grid=(n_tiles,),
        in_specs=[pl.BlockSpec((tm, width), held),
                  wa((1, D, MOE_FF)), wa((1, D, MOE_FF)), wa((1, MOE_FF, D)),
                  wb((1, D, MOE_FF)), wb((1, D, MOE_FF)), wb((1, MOE_FF, D))],
        out_specs=pl.BlockSpec((tm, D), tile),
    )
    return pl.pallas_call(
        _moe_pair_kernel,
        out_shape=jax.ShapeDtypeStruct((n_tiles * tm, D), f32),
        grid_spec=grid_spec,
        compiler_params=_cparams(("arbitrary",)),
        name="moe_experts",
    )(e_a, e_b, n_valid, xs, wg, wu, wd, wg, wu, wd)


def _moe_combine_kernel(pos_ref, h_ref, y_hbm, gf_ref, o_ref, ybuf, sem, *, tm):
    i = pl.program_id(0)
    n = pl.num_programs(0)

    def gather(tile, slot):
        def body(j, carry):
            for u in range(ROW_DMA_UNROLL):
                r = j * ROW_DMA_UNROLL + u
                p = pos_ref[tile * tm + r]
                pltpu.make_async_copy(y_hbm.at[pl.ds(p, 1), :],
                                      ybuf.at[slot, pl.ds(r, 1), :], sem.at[slot]).start(priority=u % 2)
            return carry
        lax.fori_loop(0, tm // ROW_DMA_UNROLL, body, 0)

    @pl.when(i == 0)
    def _():
        gather(0, 0)

    @pl.when(i + 1 < n)
    def _():
        gather(i + 1, (i + 1) % 2)

    slot = i % 2
    pltpu.make_async_copy(y_hbm.at[pl.ds(0, tm), :], ybuf.at[slot], sem.at[slot]).wait()
    o_ref[...] = _rmsnorm_rows(h_ref[...] + ybuf[slot], gf_ref[...])


def moe_combine(h, y_sorted, pos, g_final, *, tm):
    T, D = h.shape
    grid_spec = pltpu.PrefetchScalarGridSpec(
        num_scalar_prefetch=1,
        grid=(T // tm,),
        in_specs=[
            pl.BlockSpec((tm, D), lambda i, pos: (i, 0)),
            pl.BlockSpec(memory_space=pl.ANY),
            pl.BlockSpec((1, D), lambda i, pos: (0, 0)),
        ],
        out_specs=pl.BlockSpec((tm, D), lambda i, pos: (i, 0)),
        scratch_shapes=[pltpu.VMEM((2, tm, D), f32), pltpu.SemaphoreType.DMA((2,))],
    )
    return pl.pallas_call(
        functools.partial(_moe_combine_kernel, tm=tm),
        out_shape=jax.ShapeDtypeStruct((T, D), f32),
        grid_spec=grid_spec,
        compiler_params=_cparams(("arbitrary",)),
        name="moe_combine",
    )(pos, h, y_sorted, g_final.reshape(1, D))


def _pad_rows(w, rows):
    return jnp.pad(w, ((0, rows - w.shape[0]), (0, 0)))


def _pack_lora_cols(t):
    o1, o2 = DECAY_LORA, DECAY_LORA + AAA_LORA
    pad = lambda a, n: jnp.pad(a, [(0, 0)] * (a.ndim - 1) + [(0, n - a.shape[-1])])
    return jnp.concatenate([pad(t[..., :o1], 128), pad(t[..., o1:o2], 128),
                            pad(t[..., o2:], 256)], axis=-1)


def _layer(x2, mem2, p, *, batch, seq, cfg):
    D = D_MODEL
    w_in = p['w_in']
    w_main = w_in.astype(bf16)
    w_lora = _pack_lora_cols(w_in[:, MAIN_COLS:]).astype(bf16)
    mu_cols = jnp.concatenate([jnp.zeros((POOL_WIDTH,), f32), p['rwkv_mu'][:3 * RWKV_WIDTH],
                               _pack_lora_cols(p['rwkv_mu'][3 * RWKV_WIDTH:])])

    proj, pool_out = in_proj_mixers(x2, p['norm_mix_g'], w_main, w_lora, mu_cols,
                                    p['pool_w'].astype(bf16), p['pool_scale'],
                                    tm=cfg['tm'], tn=cfg['tn_in'], seq=seq)
    lw, k, kk, kka, g = rwkv_prep(
        proj, p['rwkv_w0'], _pad_rows(p['rwkv_w2'], 128).astype(bf16),
        p['rwkv_a0'], _pad_rows(p['rwkv_a2'], 128).astype(bf16),
        _pad_rows(p['rwkv_g2'], 256).astype(bf16), p['rwkv_k_k'], p['rwkv_k_a'], ts=cfg['ts_prep'])
    rwkv_out, (w_out, w_q, w_kv, w_o, wg, wu, wd) = rwkv_scan(
        proj, lw, k, kk, kka, g, p['rwkv_ln_w'], p['rwkv_ln_b'], p['rwkv_r_k'],
        [p['w_out'], p['xattn_w_q'], p['xattn_w_kv'], p['xattn_w_o'],
         p['moe_w_gate'], p['moe_w_up'], p['moe_w_down']],
        batch=batch, seq=seq, rows=cfg['scan_rows'], n_hg=cfg['scan_hg'])
    h1, q = out_q_proj([pool_out, rwkv_out], [w_out[:POOL_WIDTH], w_out[POOL_WIDTH:]], x2,
                       p['norm_xattn_g'], w_q, tm=cfg['tm'], tn=cfg['tn'])
    kv = norm_matmul_resident(mem2, p['norm_mem_g'], w_kv,
                              tm=min(cfg['tm'], mem2.shape[0]), tn=cfg['tn'], name="kv_proj")

    w_router = jnp.pad(jnp.concatenate([p['moe_w_group'], p['moe_w_expert']], axis=1),
                       ((0, 0), (0, ROUTER_LANES - MOE_GROUPS - MOE_EXPERTS)))
    b_router = jnp.pad(jnp.concatenate([p['moe_b_group'], p['moe_b_expert']]),
                       (0, ROUTER_LANES - MOE_GROUPS - MOE_EXPERTS)).reshape(1, ROUTER_LANES)
    h2, xr, rec = attn_oproj_router(q, kv, w_o, h1, p['norm_ffn_g'],
                               w_router, b_router, tm=cfg['tm'], tn=cfg['tn'], seq=seq)
    tm_e = cfg['tm_moe']
    n_tiles = xr.shape[0] // tm_e + MOE_CLASSES
    pos, e_a, e_b, n_valid, fill_start, fill_on = _route_metadata(
        rec, tm=tm_e, n_tiles=n_tiles)
    xs = moe_dispatch(xr, pos, fill_start, fill_on, tm=cfg['tm_disp'], tz=tm_e,
                      n_slots=n_tiles * tm_e)
    y_sorted = moe_pair_experts(xs, e_a, e_b, n_valid, wg, wu, wd, tm=tm_e, n_tiles=n_tiles)
    return moe_combine(h2, y_sorted, pos, p['norm_final_g'], tm=cfg['tm_comb'])


_CFG = dict(tm=512, tn=1024, tn_in=512, ts_prep=512, scan_rows=256, scan_hg=4,
            tm_moe=256, tm_disp=512, tm_comb=512)


def kernel(x, mem, norm_mix_g, w_in, pool_w, pool_scale, rwkv_mu, rwkv_w0, rwkv_w2, rwkv_a0,
           rwkv_a2, rwkv_g2, rwkv_k_k, rwkv_k_a, rwkv_r_k, rwkv_ln_w, rwkv_ln_b, w_out,
           norm_xattn_g, norm_mem_g, xattn_w_q, xattn_w_kv, xattn_w_o, norm_ffn_g,
           moe_w_group, moe_b_group, moe_w_expert, moe_b_expert, moe_w_gate, moe_w_up,
           moe_w_down, norm_final_g):
    batch, seq, D = x.shape
    p = dict(norm_mix_g=norm_mix_g[0], w_in=w_in[0], pool_w=pool_w[0], pool_scale=pool_scale[0],
             rwkv_mu=rwkv_mu[0], rwkv_w0=rwkv_w0[0], rwkv_w2=rwkv_w2[0], rwkv_a0=rwkv_a0[0],
             rwkv_a2=rwkv_a2[0], rwkv_g2=rwkv_g2[0], rwkv_k_k=rwkv_k_k[0], rwkv_k_a=rwkv_k_a[0],
             rwkv_r_k=rwkv_r_k[0], rwkv_ln_w=rwkv_ln_w[0], rwkv_ln_b=rwkv_ln_b[0], w_out=w_out[0],
             norm_xattn_g=norm_xattn_g[0], norm_mem_g=norm_mem_g[0], xattn_w_q=xattn_w_q[0],
             xattn_w_kv=xattn_w_kv[0], xattn_w_o=xattn_w_o[0], norm_ffn_g=norm_ffn_g[0],
             moe_w_group=moe_w_group[0], moe_b_group=moe_b_group[0], moe_w_expert=moe_w_expert[0],
             moe_b_expert=moe_b_expert[0], moe_w_gate=moe_w_gate[0], moe_w_up=moe_w_up[0],
             moe_w_down=moe_w_down[0], norm_final_g=norm_final_g)
    out = _layer(x.reshape(batch * seq, D), mem.reshape(batch * MEM_LEN, D), p,
                 batch=batch, seq=seq, cfg=_CFG)
    return out.reshape(batch, seq, D)
```
